```python
import jax, jax.numpy as jnp
from jax import lax
import numpy as np

D_MODEL = 1024
BATCH = 8
SEQ = 2048
DEPTH = 1
DEC_BATCH = 128
DEC_SEQ = 4
PAST_LEN = 16384
PAGE_SIZE = 128

A_HEAD = 64
A_HEADS = 8
A_WIDTH = A_HEADS * A_HEAD
A_RANK_W = 64
A_RANK_A = 64
A_RANK_G = 128
A_PROJ = 3 * A_WIDTH + A_RANK_W + A_RANK_A + A_RANK_G
A_LNX_EPS = 64e-5
B_HEADS = 4
B_HEAD = 128
B_WIDTH = B_HEADS * B_HEAD
CONV_K = 4
CONV_CH = 3 * B_WIDTH
B_PROJ = CONV_CH + 2 * B_HEADS + B_WIDTH
DELTA_CHUNK = 64
GATE_COLS = 2 * D_MODEL
IN_COLS = A_PROJ + B_PROJ + GATE_COLS
D_FF = 2816
RMS_EPS = 1e-6

kernel_name = 'hybrid_rwkv7_gdn_macaron_step'

F32 = jnp.float32


def _rms(x, w):
    x32 = x.astype(F32)
    y = x32 * lax.rsqrt(jnp.mean(x32 * x32, axis=-1, keepdims=True) + RMS_EPS) * w.astype(F32)
    return y.astype(x.dtype)


def _swiglu(x, wg, wu, wd):
    return (jax.nn.silu(x @ wg) * (x @ wu)) @ wd


def _l2norm(t):
    return t * lax.rsqrt(jnp.sum(t * t, axis=-1, keepdims=True) + 1e-6)


def _rwkv7(pa, s_shift, s_state, mu, w0, w2, a0, a2, g2, k_k, k_a, r_k, lnx_w, lnx_b):
    Bn, T, _ = pa.shape
    pa32 = pa.astype(F32)
    prev = jnp.concatenate([s_shift.astype(F32)[:, None], pa32[:, :-1]], axis=1)
    pm = pa32 + (prev - pa32) * mu.astype(F32)
    splits = (A_WIDTH, A_WIDTH + A_RANK_W, 2 * A_WIDTH + A_RANK_W,
              3 * A_WIDTH + A_RANK_W, 3 * A_WIDTH + A_RANK_W + A_RANK_A)
    r, wd, k, v, ad, gd = jnp.split(pm, splits, axis=-1)
    w_log = -jax.nn.softplus(-(w0.astype(F32) + jnp.tanh(wd) @ w2.astype(F32))) - 0.5
    decay = jnp.exp(-jnp.exp(w_log))
    a = jax.nn.sigmoid(a0.astype(F32) + ad @ a2.astype(F32))
    g = jax.nn.sigmoid(gd) @ g2.astype(F32)
    hs = lambda t: t.reshape(Bn, T, A_HEADS, A_HEAD)
    kk = hs(k * k_k.astype(F32))
    kk = kk / jnp.maximum(jnp.sqrt(jnp.sum(kk * kk, axis=-1, keepdims=True)), 1e-12)
    k = k * (1.0 + (a - 1.0) * k_a.astype(F32))
    r_h, k_h, v_h, a_h, d_h = hs(r), hs(k), hs(v), hs(a), hs(decay)

    def step(S, inp):
        r_t, d_t, k_t, v_t, kk_t, a_t = inp
        sa = jnp.einsum('bhvk,bhk->bhv', S, -kk_t)
        S = (S * d_t[:, :, None, :] + sa[..., None] * (kk_t * a_t)[:, :, None, :]
             + v_t[..., None] * k_t[:, :, None, :])
        return S, jnp.einsum('bhvk,bhk->bhv', S, r_t)

    xs = tuple(jnp.moveaxis(t, 1, 0) for t in (r_h, d_h, k_h, v_h, kk, a_h))
    S_fin, o = lax.scan(step, s_state.astype(F32), xs)
    o = jnp.moveaxis(o, 0, 1)
    mean = jnp.mean(o, axis=-1, keepdims=True)
    var = jnp.mean(jnp.square(o - mean), axis=-1, keepdims=True)
    o = ((o - mean) * lax.rsqrt(var + A_LNX_EPS)).reshape(Bn, T, A_WIDTH) * lnx_w.astype(F32) + lnx_b.astype(F32)
    bonus = jnp.sum(r_h * k_h * r_k.astype(F32), axis=-1, keepdims=True) * v_h
    o = (o + bonus.reshape(Bn, T, A_WIDTH)) * g
    return o, S_fin, pa32[:, -1]


def _chunked_gated_delta(q, k, v, beta, g, S0):
    Bn, T, H, DK = q.shape
    DV = v.shape[-1]
    C = min(DELTA_CHUNK, T)
    n = -(-T // C)
    pad = n * C - T

    def blk(t):
        t = jnp.pad(t, [(0, 0), (0, pad)] + [(0, 0)] * (t.ndim - 2))
        t = t.reshape((Bn, n, C) + t.shape[2:])
        return jnp.moveaxis(t, 3, 1)

    q, k, v, beta, g = blk(q), blk(k), blk(v), blk(beta), blk(g)
    gc = jnp.cumsum(g, axis=-1)
    idx = jnp.arange(C)
    incl = idx[:, None] >= idx[None, :]
    strict = idx[:, None] > idx[None, :]
    diff = gc[..., :, None] - gc[..., None, :]
    decay_mat = jnp.where(incl, jnp.exp(jnp.where(incl, diff, 0.0)), 0.0)
    kb = k * beta[..., None]
    A = jnp.where(strict, jnp.einsum('bhnik,bhnjk->bhnij', kb, k) * decay_mat, 0.0)
    M = A + jnp.eye(C, dtype=A.dtype)
    rhs = jnp.concatenate([v * beta[..., None], kb * jnp.exp(gc)[..., None]], axis=-1)
    sol = lax.linalg.triangular_solve(M, rhs, left_side=True, lower=True, unit_diagonal=True)
    u_c, w_c = sol[..., :DV], sol[..., DV:]
    qk = jnp.einsum('bhnik,bhnjk->bhnij', q, k) * decay_mat
    q_dec = q * jnp.exp(gc)[..., None]
    k_dec = k * jnp.exp(gc[..., -1:] - gc)[..., None]
    g_last = jnp.exp(gc[..., -1])

    def step(S, inp):
        u_t, w_t, qk_t, qd_t, kd_t, gl_t = inp
        v_new = u_t - jnp.einsum('bhck,bhkv->bhcv', w_t, S)
        o = jnp.einsum('bhck,bhkv->bhcv', qd_t, S) + jnp.einsum('bhij,bhjv->bhiv', qk_t, v_new)
        S = S * gl_t[..., None, None] + jnp.einsum('bhck,bhcv->bhkv', kd_t, v_new)
        return S, o

    xs = tuple(jnp.moveaxis(t, 2, 0) for t in (u_c, w_c, qk, q_dec, k_dec, g_last))
    S_fin, o = lax.scan(step, S0, xs)
    o = jnp.transpose(o, (1, 0, 3, 2, 4)).reshape(Bn, n * C, H, DV)[:, :T]
    return o, S_fin


def _gated_delta_branch(pb, s_conv, s_state, conv_w, A_log, dt_bias, norm_w):
    Bn, T, _ = pb.shape
    pb = pb.astype(F32)
    qkv = pb[..., :CONV_CH]
    a_in = pb[..., CONV_CH:CONV_CH + B_HEADS]
    b_in = pb[..., CONV_CH + B_HEADS:CONV_CH + 2 * B_HEADS]
    z = pb[..., CONV_CH + 2 * B_HEADS:]
    xp = jnp.concatenate([s_conv.astype(F32), qkv], axis=1)
    cw = conv_w.astype(F32)
    conv = sum(xp[:, i:i + T] * cw[i] for i in range(CONV_K))
    conv = jax.nn.silu(conv)
    new_conv = xp[:, T:]
    q, k, v = jnp.split(conv, 3, axis=-1)
    hs = lambda t: t.reshape(Bn, T, B_HEADS, B_HEAD)
    q = _l2norm(hs(q)) * (B_HEAD ** -0.5)
    k = _l2norm(hs(k))
    v = hs(v)
    beta = jax.nn.sigmoid(b_in)
    g = -jnp.exp(A_log.astype(F32)) * jax.nn.softplus(a_in + dt_bias.astype(F32))
    o, S_fin = _chunked_gated_delta(q, k, v, beta, g, s_state.astype(F32))
    o = o * lax.rsqrt(jnp.mean(o * o, axis=-1, keepdims=True) + RMS_EPS) * norm_w.astype(F32)
    o = o * jax.nn.silu(hs(z))
    return o.reshape(Bn, T, B_WIDTH), S_fin, new_conv


def _layer(x, s_rwkv, s_shift, s_delta, s_conv,
           ffn1_norm, ffn1_w_gate, ffn1_w_up, ffn1_w_down, mix_norm, w_in,
           rwkv_mu, rwkv_w0, rwkv_w2, rwkv_a0, rwkv_a2, rwkv_g2, rwkv_k_k, rwkv_k_a, rwkv_r_k,
           rwkv_lnx_w, rwkv_lnx_b, gdn_conv_w, gdn_A_log, gdn_dt_bias, gdn_norm_w,
           proj_a, proj_b, w_out, ffn2_norm, ffn2_w_gate, ffn2_w_up, ffn2_w_down):
    dt = x.dtype
    h = x + 0.5 * _swiglu(_rms(x, ffn1_norm), ffn1_w_gate, ffn1_w_up, ffn1_w_down)
    u = _rms(h, mix_norm)
    P = u @ w_in
    pa = P[..., :A_PROJ]
    pb = P[..., A_PROJ:A_PROJ + B_PROJ]
    gate_a = P[..., A_PROJ + B_PROJ:A_PROJ + B_PROJ + D_MODEL]
    gate_b = P[..., A_PROJ + B_PROJ + D_MODEL:]
    oa, rwkv_new, shift_new = _rwkv7(pa, s_shift, s_rwkv, rwkv_mu, rwkv_w0, rwkv_w2, rwkv_a0, rwkv_a2,
                                     rwkv_g2, rwkv_k_k, rwkv_k_a, rwkv_r_k, rwkv_lnx_w, rwkv_lnx_b)
    ob, delta_new, conv_new = _gated_delta_branch(pb, s_conv, s_delta, gdn_conv_w, gdn_A_log,
                                                  gdn_dt_bias, gdn_norm_w)
    merged = (jax.nn.sigmoid(gate_a) * (oa.astype(dt) @ proj_a)
              + jax.nn.sigmoid(gate_b) * (ob.astype(dt) @ proj_b))
    h = h + merged @ w_out
    h = h + 0.5 * _swiglu(_rms(h, ffn2_norm), ffn2_w_gate, ffn2_w_up, ffn2_w_down)
    return h, rwkv_new, shift_new, delta_new, conv_new


def setup_inputs(seed: int = 0) -> dict:
    key = jax.random.key(seed)
    ks = iter(jax.random.split(key, 48))
    L, D = DEPTH, D_MODEL

    def nrm(shape, scale):
        return jax.random.normal(next(ks), shape, F32) * scale

    inp = {}
    inp['x_prompt'] = nrm((BATCH, SEQ, D), 1.0)
    inp['x_sample'] = nrm((DEC_BATCH, DEC_SEQ, D), 1.0)
    inp['state_rwkv'] = nrm((L, DEC_BATCH, A_HEADS, A_HEAD, A_HEAD), 0.5)
    inp['state_rwkv_shift'] = nrm((L, DEC_BATCH, A_PROJ), 1.0)
    inp['state_delta'] = nrm((L, DEC_BATCH, B_HEADS, B_HEAD, B_HEAD), 0.1)
    inp['state_conv'] = nrm((L, DEC_BATCH, CONV_K - 1, CONV_CH), 1.0)
    inp['ffn1_norm'] = 1.0 + nrm((L, D), 0.05)
    inp['ffn1_w_gate'] = nrm((L, D, D_FF), D ** -0.5)
    inp['ffn1_w_up'] = nrm((L, D, D_FF), D ** -0.5)
    inp['ffn1_w_down'] = nrm((L, D_FF, D), D_FF ** -0.5)
    inp['mix_norm'] = 1.0 + nrm((L, D), 0.05)
    inp['w_in'] = nrm((L, D, IN_COLS), D ** -0.5)
    inp['rwkv_mu'] = jax.random.uniform(next(ks), (L, A_PROJ), F32)
    inp['rwkv_w0'] = -0.5 + nrm((L, A_WIDTH), 0.5)
    inp['rwkv_w2'] = nrm((L, A_RANK_W, A_WIDTH), A_RANK_W ** -0.5)
    inp['rwkv_a0'] = nrm((L, A_WIDTH), 0.1)
    inp['rwkv_a2'] = nrm((L, A_RANK_A, A_WIDTH), A_RANK_A ** -0.5)
    inp['rwkv_g2'] = nrm((L, A_RANK_G, A_WIDTH), A_RANK_G ** -0.5)
    inp['rwkv_k_k'] = 0.85 + nrm((L, A_WIDTH), 0.05)
    inp['rwkv_k_a'] = 1.0 + nrm((L, A_WIDTH), 0.05)
    inp['rwkv_r_k'] = nrm((L, A_HEADS, A_HEAD), 0.1)
    inp['rwkv_lnx_w'] = 1.0 + nrm((L, A_WIDTH), 0.05)
    inp['rwkv_lnx_b'] = nrm((L, A_WIDTH), 0.01)
    inp['gdn_conv_w'] = nrm((L, CONV_K, CONV_CH), CONV_K ** -0.5)
    inp['gdn_A_log'] = jnp.log(jax.random.uniform(next(ks), (L, B_HEADS), F32, 1.0, 16.0))
    inp['gdn_dt_bias'] = nrm((L, B_HEADS), 0.1)
    inp['gdn_norm_w'] = 1.0 + nrm((L, B_HEAD), 0.05)
    inp['proj_a'] = nrm((L, A_WIDTH, D), A_WIDTH ** -0.5)
    inp['proj_b'] = nrm((L, B_WIDTH, D), B_WIDTH ** -0.5)
    inp['w_out'] = nrm((L, D, D), D ** -0.5)
    inp['ffn2_norm'] = 1.0 + nrm((L, D), 0.05)
    inp['ffn2_w_gate'] = nrm((L, D, D_FF), D ** -0.5)
    inp['ffn2_w_up'] = nrm((L, D, D_FF), D ** -0.5)
    inp['ffn2_w_down'] = nrm((L, D_FF, D), D_FF ** -0.5)
    inp['final_norm'] = 1.0 + nrm((D,), 0.05)
    return inp


def reference(x_prompt, x_sample, state_rwkv, state_rwkv_shift, state_delta, state_conv,
              ffn1_norm, ffn1_w_gate, ffn1_w_up, ffn1_w_down, mix_norm, w_in,
              rwkv_mu, rwkv_w0, rwkv_w2, rwkv_a0, rwkv_a2, rwkv_g2, rwkv_k_k, rwkv_k_a, rwkv_r_k,
              rwkv_lnx_w, rwkv_lnx_b, gdn_conv_w, gdn_A_log, gdn_dt_bias, gdn_norm_w,
              proj_a, proj_b, w_out, ffn2_norm, ffn2_w_gate, ffn2_w_up, ffn2_w_down, final_norm):
    Bp = x_prompt.shape[0]
    dt = x_prompt.dtype
    zr = jnp.zeros((Bp, A_HEADS, A_HEAD, A_HEAD), dt)
    zs = jnp.zeros((Bp, A_PROJ), dt)
    zd = jnp.zeros((Bp, B_HEADS, B_HEAD, B_HEAD), dt)
    zc = jnp.zeros((Bp, CONV_K - 1, CONV_CH), dt)
    hp, hs = x_prompt, x_sample
    rp, sp, dp, cp = [], [], [], []
    rs, ss, ds, cs = [], [], [], []
    for l in range(DEPTH):
        lp = (ffn1_norm[l], ffn1_w_gate[l], ffn1_w_up[l], ffn1_w_down[l], mix_norm[l], w_in[l],
              rwkv_mu[l], rwkv_w0[l], rwkv_w2[l], rwkv_a0[l], rwkv_a2[l], rwkv_g2[l], rwkv_k_k[l],
              rwkv_k_a[l], rwkv_r_k[l], rwkv_lnx_w[l], rwkv_lnx_b[l], gdn_conv_w[l], gdn_A_log[l],
              gdn_dt_bias[l], gdn_norm_w[l], proj_a[l], proj_b[l], w_out[l],
              ffn2_norm[l], ffn2_w_gate[l], ffn2_w_up[l], ffn2_w_down[l])
        hp, r1, s1, d1, c1 = _layer(hp, zr, zs, zd, zc, *lp)
        hs, r2, s2, d2, c2 = _layer(hs, state_rwkv[l], state_rwkv_shift[l], state_delta[l], state_conv[l], *lp)
        rp.append(r1); sp.append(s1); dp.append(d1); cp.append(c1)
        rs.append(r2); ss.append(s2); ds.append(d2); cs.append(c2)
    y_prompt = _rms(hp, final_norm)
    y_sample = _rms(hs, final_norm)
    return (y_prompt, y_sample,
            jnp.stack(rp), jnp.stack(sp), jnp.stack(dp), jnp.stack(cp),
            jnp.stack(rs), jnp.stack(ss), jnp.stack(ds), jnp.stack(cs))
```

```python
import functools

import jax
import jax.numpy as jnp
from jax import lax
from jax.experimental import pallas as pl
from jax.experimental.pallas import tpu as pltpu

F32 = jnp.float32
BF16 = jnp.bfloat16
HIGHEST = lax.Precision.HIGHEST

D_MODEL = 1024
D_FF = 2816
RMS_EPS = 1e-6
A_HEAD = 64
A_HEADS = 8
A_WIDTH = A_HEADS * A_HEAD
A_RANK_W = 64
A_RANK_A = 64
A_RANK_G = 128
A_PROJ = 3 * A_WIDTH + A_RANK_W + A_RANK_A + A_RANK_G
A_LNX_EPS = 64e-5
A_PAIRS = A_HEADS // 2
B_HEADS = 4
B_HEAD = 128
B_WIDTH = B_HEADS * B_HEAD
CONV_K = 4
CONV_CH = 3 * B_WIDTH
B_PROJ = CONV_CH + 2 * B_HEADS + B_WIDTH
GATE_COLS = 2 * D_MODEL
LANES = 128
SUBLANES = 8
VMEM_LIMIT_BYTES = 56 * 1024 * 1024
CHUNK = 64
PA_R, PA_K, PA_V, PA_WA, PA_G = 0, A_WIDTH, 2 * A_WIDTH, 3 * A_WIDTH, 3 * A_WIDTH + A_RANK_W + A_RANK_A
PROJ_SPLITS = (A_PROJ, CONV_CH, B_WIDTH, GATE_COLS, LANES)


def _cparams(n_grid_dims):
    return pltpu.CompilerParams(dimension_semantics=("arbitrary",) * n_grid_dims,
                                vmem_limit_bytes=VMEM_LIMIT_BYTES)


def _const_spec(shape):
    nd = len(shape)
    return pl.BlockSpec(shape, lambda *_: (0,) * nd, pipeline_mode=pl.Buffered(1))


def _dot(a, b, precision=None):
    return jnp.dot(a, b, preferred_element_type=F32, precision=precision)


def _dot_nt(a, b, precision=None):
    return lax.dot_general(a, b, (((1,), (1,)), ((), ())), preferred_element_type=F32, precision=precision)


def _hdot(a, b):
    return _dot(a, b, HIGHEST)


def _hdot_nt(a, b):
    return _dot_nt(a, b, HIGHEST)


def _rms(x, w):
    return x * lax.rsqrt(jnp.mean(x * x, axis=-1, keepdims=True) + RMS_EPS) * w


def _sigmoid(x):
    return 1.0 / (1.0 + jnp.exp(-x))


def _silu(x):
    return x * _sigmoid(x)


def _softplus(x):
    return jnp.maximum(x, 0.0) + jnp.log(1.0 + jnp.exp(-jnp.abs(x)))


def _iota(shape, dim):
    return lax.broadcasted_iota(jnp.int32, shape, dim)


def _group(idx, size):
    assert size & (size - 1) == 0
    return lax.shift_right_logical(idx, size.bit_length() - 1)


def _swiglu_half_step(x, nw, wg_ref, wu_ref, wd_ref):
    xn = _rms(x, nw).astype(BF16)
    g = _dot(xn, wg_ref[...])
    u = _dot(xn, wu_ref[...])
    act = (_silu(g) * u).astype(BF16)
    return x + 0.5 * _dot(act, wd_ref[...])


def _ffn_body(x_ref, nw_ref, wg_ref, wu_ref, wd_ref, o_ref):
    o_ref[...] = _swiglu_half_step(x_ref[...], nw_ref[...], wg_ref, wu_ref, wd_ref)


def _ffn(x, nw, wg, wu, wd, tm):
    n = x.shape[0]
    return pl.pallas_call(
        _ffn_body,
        grid=(n // tm,),
        in_specs=[pl.BlockSpec((tm, D_MODEL), lambda i: (i, 0)),
                  _const_spec((1, D_MODEL)),
                  _const_spec((D_MODEL, D_FF)), _const_spec((D_MODEL, D_FF)), _const_spec((D_FF, D_MODEL))],
        out_specs=pl.BlockSpec((tm, D_MODEL), lambda i: (i, 0)),
        out_shape=jax.ShapeDtypeStruct((n, D_MODEL), F32),
        compiler_params=_cparams(1),
        name="ffn1",
    )(x, nw, wg, wu, wd)


def _proj_body(h_ref, nw_ref, w_ref, *o_refs):
    u = _rms(h_ref[...], nw_ref[...]).astype(BF16)
    off = 0
    for o_ref, width in zip(o_refs, PROJ_SPLITS):
        o_ref[...] = _dot(u, w_ref[:, off:off + width])
        off += width


def _proj(h, nw, w_all, tm):
    n = h.shape[0]
    cols = sum(PROJ_SPLITS)
    return pl.pallas_call(
        _proj_body,
        grid=(n // tm,),
        in_specs=[pl.BlockSpec((tm, D_MODEL), lambda i: (i, 0)),
                  _const_spec((1, D_MODEL)), _const_spec((D_MODEL, cols))],
        out_specs=[pl.BlockSpec((tm, w), lambda i: (i, 0)) for w in PROJ_SPLITS],
        out_shape=[jax.ShapeDtypeStruct((n, w), F32) for w in PROJ_SPLITS],
        compiler_params=_cparams(1),
        name="proj",
    )(h, nw, w_all)


def _tail_body(h_ref, oa_ref, ob_ref, gates_ref, pa_ref, pb_ref, wo_ref, nw_ref, wg_ref, wu_ref, wd_ref,
               fn_ref, o_ref):
    ma = _dot(oa_ref[...].astype(BF16), pa_ref[...])
    mb = _dot(ob_ref[...].astype(BF16), pb_ref[...])
    merged = _sigmoid(gates_ref[:, :D_MODEL]) * ma + _sigmoid(gates_ref[:, D_MODEL:]) * mb
    h = h_ref[...] + _dot(merged.astype(BF16), wo_ref[...])
    h = _swiglu_half_step(h, nw_ref[...], wg_ref, wu_ref, wd_ref)
    o_ref[...] = _rms(h, fn_ref[...])


def _tail(h, oa, ob, gates, proj_a, proj_b, w_out, nw, wg, wu, wd, fn, tm):
    n = h.shape[0]
    row = lambda w: pl.BlockSpec((tm, w), lambda i: (i, 0))
    return pl.pallas_call(
        _tail_body,
        grid=(n // tm,),
        in_specs=[row(D_MODEL), row(A_WIDTH), row(B_WIDTH), row(GATE_COLS),
                  _const_spec((A_WIDTH, D_MODEL)), _const_spec((B_WIDTH, D_MODEL)),
                  _const_spec((D_MODEL, D_MODEL)), _const_spec((1, D_MODEL)),
                  _const_spec((D_MODEL, D_FF)), _const_spec((D_MODEL, D_FF)), _const_spec((D_FF, D_MODEL)),
                  _const_spec((1, D_MODEL))],
        out_specs=row(D_MODEL),
        out_shape=jax.ShapeDtypeStruct((n, D_MODEL), F32),
        compiler_params=_cparams(1),
        name="tail",
    )(h, oa, ob, gates, proj_a, proj_b, w_out, nw, wg, wu, wd, fn)


def _rwkv_token_math(x, prev, mu, w0, a0, k_k, k_a, w2a, g2):
    pm = x + (prev - x) * mu
    r = pm[:, PA_R:PA_R + A_WIDTH]
    k = pm[:, PA_K:PA_K + A_WIDTH]
    v = pm[:, PA_V:PA_V + A_WIDTH]
    wa = pm[:, PA_WA:PA_WA + LANES]
    gd = pm[:, PA_G:PA_G + A_RANK_G]
    lane = _iota((1, LANES), 1)
    lora_in = jnp.where(lane < A_RANK_W, jnp.tanh(wa), wa)
    lora = _hdot(lora_in, w2a)
    w_log = -_softplus(-(w0 + lora[:, :A_WIDTH])) - 0.5
    log_decay = -jnp.exp(w_log)
    a = _sigmoid(a0 + lora[:, A_WIDTH:])
    g = _hdot(_sigmoid(gd), g2)
    kk_raw = k * k_k
    k_mod = k * (1.0 + (a - 1.0) * k_a)
    return r, k_mod, v, kk_raw, a, log_decay, g


def _pair_mask(rows_per_head):
    shape = (2 * rows_per_head, LANES)
    return _group(_iota(shape, 0), rows_per_head) == _group(_iota(shape, 1), A_HEAD)


def _rwkv_prompt_body(pa_ref, mu_ref, w0_ref, a0_ref, kk_ref, ka_ref, rk_ref, lnw_ref, lnb_ref, w2a_ref, g2_ref,
                      o_ref, sfin_ref,
                      carry_ref, state_ref, r_s, k_s, v_s, kkraw_s, a_s, einc_s, eex_s, einv_s, etail_s, g_s):
    t = pl.program_id(1)
    tt = pa_ref.shape[0]
    C = CHUNK

    @pl.when(t == 0)
    def _():
        carry_ref[...] = jnp.zeros_like(carry_ref)
        state_ref[...] = jnp.zeros_like(state_ref)

    x = pa_ref[...]
    row = _iota((tt, 1), 0)
    prev = jnp.where(row == 0, carry_ref[SUBLANES - 1:SUBLANES, :], pltpu.roll(x, 1, axis=0))
    carry_ref[...] = x[tt - SUBLANES:tt, :]
    r, k_mod, v, kk_raw, a, log_decay, g = _rwkv_token_math(
        x, prev, mu_ref[...], w0_ref[...], a0_ref[...], kk_ref[...], ka_ref[...], w2a_ref[...], g2_ref[...])

    ri, ci = _iota((tt, tt), 0), _iota((tt, tt), 1)
    same_chunk = _group(ri, C) == _group(ci, C)
    cum = _hdot(jnp.where(same_chunk & (ci <= ri), 1.0, 0.0), log_decay)
    tot = _hdot(jnp.where(same_chunk, 1.0, 0.0), log_decay)
    r_s[...] = r
    k_s[...] = k_mod
    v_s[...] = v
    kkraw_s[...] = kk_raw
    a_s[...] = a
    g_s[...] = g
    einc_s[...] = jnp.exp(cum)
    eex_s[...] = jnp.exp(cum - log_decay)
    einv_s[...] = jnp.exp(-cum)
    etail_s[...] = jnp.exp(tot - cum)

    mask = _pair_mask(C)
    i2, j2 = _iota((2 * C, 2 * C), 0), _iota((2 * C, 2 * C), 1)
    strict = i2 > j2
    incl = i2 >= j2
    eye = jnp.where(i2 == j2, 1.0, 0.0)
    dup = lambda m: jnp.concatenate([m, m], axis=0)
    stack = lambda m: jnp.where(mask, dup(m), 0.0)

    def chunk(c, carry):
        r0 = pl.multiple_of(c * C, C)
        for p in range(A_PAIRS):
            sl = slice(p * LANES, (p + 1) * LANES)
            ld = lambda ref: ref[pl.ds(r0, C), sl]
            r_p, k_p, v_p, a_p = ld(r_s), ld(k_s), ld(v_s), ld(a_s)
            einc, eex, einv, etail = ld(einc_s), ld(eex_s), ld(einv_s), ld(etail_s)
            kks = stack(ld(kkraw_s))
            kks = kks / jnp.maximum(jnp.sqrt(jnp.sum(kks * kks, axis=-1, keepdims=True)), 1e-12)
            As = -kks * dup(eex)
            Bs = kks * dup(a_p * einv)
            Bh = kks * dup(a_p * etail)
            Ks = stack(k_p * einv)
            Kh = stack(k_p * etail)
            Rs = stack(r_p * einc)
            Vs = stack(v_p)
            G = _hdot_nt(jnp.concatenate([As, Rs], axis=0), jnp.concatenate([Bs, Ks], axis=0))
            Aab = jnp.where(strict, G[:2 * C, :2 * C], 0.0)
            Aak = jnp.where(strict, G[:2 * C, 2 * C:], 0.0)
            Arb = jnp.where(incl, G[2 * C:, :2 * C], 0.0)
            Ark = jnp.where(incl, G[2 * C:, 2 * C:], 0.0)
            T = eye + Aab
            N = Aab
            for _ in range(5):
                N = _hdot(N, N)
                T = T + _hdot(T, N)
            WU = _hdot(T, jnp.concatenate([As, _hdot(Aak, Vs)], axis=1))
            S = state_ref[p]
            W = _hdot_nt(WU[:, :LANES], S) + WU[:, LANES:]
            O = _hdot_nt(Rs, S) + _hdot(Arb, W) + _hdot(Ark, Vs)
            p_end = einc[C - 1:C, :]
            state_ref[p] = S * p_end + _hdot(jnp.concatenate([W, Vs], axis=0).T,
                                             jnp.concatenate([Bh, Kh], axis=0))
            mean = jnp.sum(O, axis=-1, keepdims=True) * (1.0 / A_HEAD)
            cen = jnp.where(mask, O - mean, 0.0)
            var = jnp.sum(cen * cen, axis=-1, keepdims=True) * (1.0 / A_HEAD)
            normed = jnp.where(mask, cen * lax.rsqrt(var + A_LNX_EPS) * lnw_ref[:, sl] + lnb_ref[:, sl], 0.0)
            bonus = jnp.sum(stack(r_p * k_p * rk_ref[:, sl]), axis=-1, keepdims=True) * Vs
            full = normed + bonus
            o_ref[pl.ds(r0, C), sl] = (full[:C] + full[C:]) * g_s[pl.ds(r0, C), sl]
        return carry

    lax.fori_loop(0, tt // C, chunk, 0)

    @pl.when(t == pl.num_programs(1) - 1)
    def _():
        sfin_ref[0] = state_ref[...]


def _rwkv_prompt(pa, B, T, params, tt):
    n = pa.shape[0]
    nt = T // tt
    big = lambda: pltpu.VMEM((tt, A_WIDTH), F32)
    return pl.pallas_call(
        _rwkv_prompt_body,
        grid=(B, nt),
        in_specs=[pl.BlockSpec((tt, A_PROJ), lambda b, t: (b * nt + t, 0))] + [_const_spec(p.shape) for p in params],
        out_specs=[pl.BlockSpec((tt, A_WIDTH), lambda b, t: (b * nt + t, 0)),
                   pl.BlockSpec((1, A_PAIRS, LANES, LANES), lambda b, t: (b, 0, 0, 0))],
        out_shape=[jax.ShapeDtypeStruct((n, A_WIDTH), F32),
                   jax.ShapeDtypeStruct((B, A_PAIRS, LANES, LANES), F32)],
        scratch_shapes=[pltpu.VMEM((SUBLANES, A_PROJ), F32), pltpu.VMEM((A_PAIRS, LANES, LANES), F32)]
                       + [big() for _ in range(10)],
        compiler_params=_cparams(2),
        name="rwkv_prompt",
    )(pa, *params)


def _rwkv_sample_body(pa_ref, prev_ref, s_ref, mu_ref, w0_ref, a0_ref, kk_ref, ka_ref, rk_ref, lnw_ref, lnb_ref,
                      w2a_ref, g2_ref, o_ref, sout_ref, o_s, *, seqs, steps):
    r, k_mod, v, kk_raw, a, log_decay, g = _rwkv_token_math(
        pa_ref[...], prev_ref[...], mu_ref[...], w0_ref[...], a0_ref[...], kk_ref[...], ka_ref[...],
        w2a_ref[...], g2_ref[...])
    hi, hj = _iota((A_WIDTH, A_WIDTH), 0), _iota((A_WIDTH, A_WIDTH), 1)
    head_ones = jnp.where(_group(hi, A_HEAD) == _group(hj, A_HEAD), 1.0, 0.0)
    head_sum = lambda m: _hdot(m, head_ones)
    kk = kk_raw / jnp.maximum(jnp.sqrt(head_sum(kk_raw * kk_raw)), 1e-12)
    decay = jnp.exp(log_decay)
    beta = kk * a
    pair_ones = head_ones[:LANES, :LANES]
    diag = _iota((A_HEAD, LANES), 0) == (_iota((A_HEAD, LANES), 1) & (A_HEAD - 1))
    for b in range(seqs):
        for p in range(A_PAIRS):
            sl = slice(p * LANES, (p + 1) * LANES)
            S = s_ref[b, p]
            for t in range(steps):
                rr = b * steps + t
                rowof = lambda m: m[rr:rr + 1, sl]
                sa = _hdot(S * (-rowof(kk)), pair_ones)
                v_col = _hdot(jnp.where(diag, rowof(v), 0.0), pair_ones)
                S = S * rowof(decay) + sa * rowof(beta) + v_col * rowof(k_mod)
                out = _hdot(S * rowof(r), pair_ones)
                o_s[rr:rr + 1, sl] = jnp.sum(jnp.where(diag, out, 0.0), axis=0, keepdims=True)
            sout_ref[b, p] = S
    o = o_s[...]
    mean = head_sum(o) * (1.0 / A_HEAD)
    cen = o - mean
    var = head_sum(cen * cen) * (1.0 / A_HEAD)
    o = cen * lax.rsqrt(var + A_LNX_EPS) * lnw_ref[...] + lnb_ref[...]
    o_ref[...] = (o + head_sum(r * k_mod * rk_ref[...]) * v) * g


def _rwkv_sample(pa, prev, state_pairs, steps, params, seqs):
    n = pa.shape[0]
    nb = state_pairs.shape[0]
    rows = seqs * steps
    sspec = pl.BlockSpec((seqs, A_PAIRS, A_HEAD, LANES), lambda i: (i, 0, 0, 0))
    return pl.pallas_call(
        functools.partial(_rwkv_sample_body, seqs=seqs, steps=steps),
        grid=(nb // seqs,),
        in_specs=[pl.BlockSpec((rows, A_PROJ), lambda i: (i, 0)), pl.BlockSpec((rows, A_PROJ), lambda i: (i, 0)),
                  sspec] + [_const_spec(p.shape) for p in params],
        out_specs=[pl.BlockSpec((rows, A_WIDTH), lambda i: (i, 0)), sspec],
        out_shape=[jax.ShapeDtypeStruct((n, A_WIDTH), F32),
                   jax.ShapeDtypeStruct(state_pairs.shape, F32)],
        scratch_shapes=[pltpu.VMEM((rows, A_WIDTH), F32)],
        compiler_params=_cparams(1),
        name="rwkv_sample",
    )(pa, prev, state_pairs, *params)


def _gdn_token_math(conv, ab, alog, dtb):
    c = _silu(conv)
    qs, ks = [], []
    for h in range(B_HEADS):
        q = c[:, h * B_HEAD:(h + 1) * B_HEAD]
        k = c[:, B_WIDTH + h * B_HEAD:B_WIDTH + (h + 1) * B_HEAD]
        qs.append(q * lax.rsqrt(jnp.sum(q * q, axis=-1, keepdims=True) + 1e-6) * (B_HEAD ** -0.5))
        ks.append(k * lax.rsqrt(jnp.sum(k * k, axis=-1, keepdims=True) + 1e-6))
    q = jnp.concatenate(qs, axis=1)
    k = jnp.concatenate(ks, axis=1)
    v = c[:, 2 * B_WIDTH:]
    lane = _iota((1, LANES), 1)
    g = -jnp.exp(alog) * _softplus(ab + dtb)
    beta = _sigmoid(ab)
    gb = jnp.where(lane < B_HEADS, g, beta)
    si, sj = _iota((LANES, 2 * B_WIDTH), 0), _iota((LANES, 2 * B_WIDTH), 1)
    spread = _hdot(gb, jnp.where(si == _group(sj, B_HEAD), 1.0, 0.0))
    return q, k, v, spread[:, :B_WIDTH], spread[:, B_WIDTH:]


def _gdn_out(o, norm_w, z):
    return o * lax.rsqrt(jnp.mean(o * o, axis=-1, keepdims=True) + RMS_EPS) * norm_w * _silu(z)


def _gdn_prompt_body(qkv_ref, ab_ref, z_ref, cw_ref, alog_ref, dtb_ref, nw_ref, o_ref, sfin_ref,
                     carry_ref, state_ref, q_s, k_s, v_s, gc_s, beta_s):
    t = pl.program_id(1)
    tt = qkv_ref.shape[0]
    C = CHUNK

    @pl.when(t == 0)
    def _():
        carry_ref[...] = jnp.zeros_like(carry_ref)
        state_ref[...] = jnp.zeros_like(state_ref)

    x = qkv_ref[...]
    row8 = _iota((SUBLANES, 1), 0)
    conv = x * cw_ref[CONV_K - 1:CONV_K, :]
    for i in range(1, CONV_K):
        xs = pltpu.roll(x, i, axis=0)
        top = jnp.where(row8 < i, pltpu.roll(carry_ref[...], i, axis=0), xs[:SUBLANES])
        xs = jnp.concatenate([top, xs[SUBLANES:]], axis=0)
        conv = conv + xs * cw_ref[CONV_K - 1 - i:CONV_K - i, :]
    carry_ref[...] = x[tt - SUBLANES:tt, :]
    q, k, v, g, beta = _gdn_token_math(conv, ab_ref[...], alog_ref[...], dtb_ref[...])
    ri, ci = _iota((tt, tt), 0), _iota((tt, tt), 1)
    gc = _hdot(jnp.where((_group(ri, C) == _group(ci, C)) & (ci <= ri), 1.0, 0.0), g)
    q_s[...] = q
    k_s[...] = k
    v_s[...] = v
    gc_s[...] = gc
    beta_s[...] = beta

    i2, j2 = _iota((C, C), 0), _iota((C, C), 1)
    strict = i2 > j2
    incl = i2 >= j2
    ones_cc = jnp.ones((C, C), F32)

    def chunk(c, carry):
        r0 = pl.multiple_of(c * C, C)
        for h in range(B_HEADS):
            sl = slice(h * B_HEAD, (h + 1) * B_HEAD)
            ld = lambda ref: ref[pl.ds(r0, C), sl]
            q_h, k_h, v_h, gc_h, beta_h = ld(q_s), ld(k_s), ld(v_s), ld(gc_s), ld(beta_s)
            gc_cols = gc_h[:, :C]
            gc_rows = _hdot(ones_cc, jnp.where(i2 == j2, gc_cols, 0.0))
            dm = jnp.where(incl, jnp.exp(jnp.where(incl, gc_cols - gc_rows, 0.0)), 0.0)
            kb = k_h * beta_h
            QK = _hdot_nt(jnp.concatenate([kb, q_h], axis=0), k_h)
            N = -jnp.where(strict, QK[:C] * dm, 0.0)
            qk = QK[C:] * dm
            egc = jnp.exp(gc_h)
            X = jnp.concatenate([v_h * beta_h, kb * egc], axis=1)
            for i in range(6):
                X = X + _hdot(N, X)
                if i < 5:
                    N = _hdot(N, N)
            S = state_ref[h]
            v_new = X[:, :B_HEAD] - _hdot(X[:, B_HEAD:], S)
            o = _hdot(q_h * egc, S) + _hdot(qk, v_new)
            g_last = gc_h[C - 1:C, :]
            k_dec = k_h * jnp.exp(g_last - gc_h)
            state_ref[h] = S * jnp.exp(g_last) + _hdot(k_dec.T, v_new)
            o_ref[pl.ds(r0, C), sl] = _gdn_out(o, nw_ref[...], z_ref[pl.ds(r0, C), sl])
        return carry

    lax.fori_loop(0, tt // C, chunk, 0)

    @pl.when(t == pl.num_programs(1) - 1)
    def _():
        sfin_ref[0] = state_ref[...]


def _gdn_prompt(qkv, ab, z, B, T, params, tt):
    n = qkv.shape[0]
    nt = T // tt
    rows = lambda w: pl.BlockSpec((tt, w), lambda b, t: (b * nt + t, 0))
    big = lambda: pltpu.VMEM((tt, B_WIDTH), F32)
    return pl.pallas_call(
        _gdn_prompt_body,
        grid=(B, nt),
        in_specs=[rows(CONV_CH), rows(LANES), rows(B_WIDTH)] + [_const_spec(p.shape) for p in params],
        out_specs=[rows(B_WIDTH), pl.BlockSpec((1, B_HEADS, B_HEAD, B_HEAD), lambda b, t: (b, 0, 0, 0))],
        out_shape=[jax.ShapeDtypeStruct((n, B_WIDTH), F32),
                   jax.ShapeDtypeStruct((B, B_HEADS, B_HEAD, B_HEAD), F32)],
        scratch_shapes=[pltpu.VMEM((SUBLANES, CONV_CH), F32), pltpu.VMEM((B_HEADS, B_HEAD, B_HEAD), F32)]
                       + [big() for _ in range(5)],
        compiler_params=_cparams(2),
        name="gdn_prompt",
    )(qkv, ab, z, *params)


def _gdn_sample_body(x0_ref, x1_ref, x2_ref, x3_ref, ab_ref, z_ref, s_ref, cw_ref, alog_ref, dtb_ref, nw_ref,
                     o_ref, sout_ref, o_s, *, seqs, steps):
    conv = (x0_ref[...] * cw_ref[0:1, :] + x1_ref[...] * cw_ref[1:2, :]
            + x2_ref[...] * cw_ref[2:3, :] + x3_ref[...] * cw_ref[3:4, :])
    q, k, v, g, beta = _gdn_token_math(conv, ab_ref[...], alog_ref[...], dtb_ref[...])
    eg = jnp.exp(g)
    diag = _iota((B_HEAD, B_HEAD), 0) == _iota((B_HEAD, B_HEAD), 1)
    ones = jnp.ones((B_HEAD, B_HEAD), F32)
    to_col = lambda row: _hdot(jnp.where(diag, row, 0.0), ones)
    for b in range(seqs):
        for h in range(B_HEADS):
            sl = slice(h * B_HEAD, (h + 1) * B_HEAD)
            S = s_ref[b, h]
            for t in range(steps):
                rr = b * steps + t
                rowof = lambda m: m[rr:rr + 1, sl]
                k_col = to_col(rowof(k))
                S = S * rowof(eg)
                u = jnp.sum(k_col * S, axis=0, keepdims=True)
                S = S + k_col * (rowof(beta) * (rowof(v) - u))
                o_s[rr:rr + 1, sl] = jnp.sum(to_col(rowof(q)) * S, axis=0, keepdims=True)
            sout_ref[b, h] = S
    for h in range(B_HEADS):
        sl = slice(h * B_HEAD, (h + 1) * B_HEAD)
        o_ref[:, sl] = _gdn_out(o_s[:, sl], nw_ref[...], z_ref[:, sl])


def _gdn_sample(xs, ab, z, state, steps, params, seqs):
    n = ab.shape[0]
    nb = state.shape[0]
    rows = seqs * steps
    row = lambda w: pl.BlockSpec((rows, w), lambda i: (i, 0))
    sspec = pl.BlockSpec((seqs, B_HEADS, B_HEAD, B_HEAD), lambda i: (i, 0, 0, 0))
    return pl.pallas_call(
        functools.partial(_gdn_sample_body, seqs=seqs, steps=steps),
        grid=(nb // seqs,),
        in_specs=[row(CONV_CH)] * CONV_K + [row(LANES), row(B_WIDTH), sspec] + [_const_spec(p.shape) for p in params],
        out_specs=[row(B_WIDTH), sspec],
        out_shape=[jax.ShapeDtypeStruct((n, B_WIDTH), F32), jax.ShapeDtypeStruct(state.shape, F32)],
        scratch_shapes=[pltpu.VMEM((rows, B_WIDTH), F32)],
        compiler_params=_cparams(1),
        name="gdn_sample",
    )(*xs, ab, z, state, *params)


def _pa_perm():
    import numpy as np
    r = np.arange(0, A_WIDTH)
    wd = np.arange(A_WIDTH, A_WIDTH + A_RANK_W)
    k = np.arange(A_WIDTH + A_RANK_W, 2 * A_WIDTH + A_RANK_W)
    v = np.arange(2 * A_WIDTH + A_RANK_W, 3 * A_WIDTH + A_RANK_W)
    ad = np.arange(3 * A_WIDTH + A_RANK_W, 3 * A_WIDTH + A_RANK_W + A_RANK_A)
    gd = np.arange(3 * A_WIDTH + A_RANK_W + A_RANK_A, A_PROJ)
    perm = np.concatenate([r, k, v, wd, ad, gd])
    inv = np.argsort(perm)
    return perm, inv


def _token_tile(n, want):
    tm = want
    while n % tm:
        tm //= 2
    return tm


def kernel(x_prompt, x_sample, state_rwkv, state_rwkv_shift, state_delta, state_conv, ffn1_norm, ffn1_w_gate, ffn1_w_up, ffn1_w_down, mix_norm, w_in, rwkv_mu, rwkv_w0, rwkv_w2, rwkv_a0, rwkv_a2, rwkv_g2, rwkv_k_k, rwkv_k_a, rwkv_r_k, rwkv_lnx_w, rwkv_lnx_b, gdn_conv_w, gdn_A_log, gdn_dt_bias, gdn_norm_w, proj_a, proj_b, w_out, ffn2_norm, ffn2_w_gate, ffn2_w_up, ffn2_w_down, final_norm):
    depth = ffn1_norm.shape[0]
    assert depth == 1, "single-layer trunk"
    Bp, Tp, _ = x_prompt.shape
    Bs, Ts, _ = x_sample.shape
    perm, inv = _pa_perm()
    l = 0
    row = lambda a: a.reshape(1, -1).astype(F32)

    wi = w_in[l]
    o_b = A_PROJ
    w_all = jnp.concatenate([
        wi[:, :A_PROJ][:, perm],
        wi[:, o_b:o_b + CONV_CH],
        wi[:, o_b + CONV_CH + 2 * B_HEADS:o_b + B_PROJ],
        wi[:, o_b + B_PROJ:],
        jnp.pad(wi[:, o_b + CONV_CH:o_b + CONV_CH + 2 * B_HEADS], ((0, 0), (0, LANES - 2 * B_HEADS))),
    ], axis=1).astype(BF16)
    ffn1 = (row(ffn1_norm[l]), ffn1_w_gate[l].astype(BF16), ffn1_w_up[l].astype(BF16), ffn1_w_down[l].astype(BF16))
    ffn2 = (row(ffn2_norm[l]), ffn2_w_gate[l].astype(BF16), ffn2_w_up[l].astype(BF16), ffn2_w_down[l].astype(BF16))
    merge_w = (proj_a[l].astype(BF16), proj_b[l].astype(BF16), w_out[l].astype(BF16))
    zw = jnp.zeros((A_RANK_W, A_WIDTH), F32)
    w2a = jnp.concatenate([jnp.concatenate([rwkv_w2[l], zw], axis=1),
                           jnp.concatenate([zw, rwkv_a2[l]], axis=1)], axis=0)
    rwkv_params = (row(rwkv_mu[l][perm]), row(rwkv_w0[l]), row(rwkv_a0[l]), row(rwkv_k_k[l]), row(rwkv_k_a[l]),
                   row(rwkv_r_k[l]), row(rwkv_lnx_w[l]), row(rwkv_lnx_b[l]), w2a, rwkv_g2[l].astype(F32))
    pad_lane = lambda a: jnp.pad(a.reshape(1, -1).astype(F32), ((0, 0), (0, LANES - a.size)))
    gdn_params = (gdn_conv_w[l].astype(F32), pad_lane(gdn_A_log[l]), pad_lane(gdn_dt_bias[l]), row(gdn_norm_w[l]))

    def trunk_front(x2):
        n = x2.shape[0]
        h = _ffn(x2, *ffn1, tm=_token_tile(n, 512))
        return (h,) + tuple(_proj(h, row(mix_norm[l]), w_all, tm=_token_tile(n, 256)))

    def trunk_back(h, oa, ob, gates):
        n = h.shape[0]
        return _tail(h, oa, ob, gates, *merge_w, *ffn2, row(final_norm), tm=_token_tile(n, 256))

    xp = x_prompt.reshape(Bp * Tp, D_MODEL)
    h, pa, qkv, z, gates, ab = trunk_front(xp)
    tt = _token_tile(Tp, 256)
    oa, s_pairs = _rwkv_prompt(pa, Bp, Tp, rwkv_params, tt)
    ob, delta_p = _gdn_prompt(qkv, ab, z, Bp, Tp, gdn_params, tt)
    y_prompt = trunk_back(h, oa, ob, gates).reshape(Bp, Tp, D_MODEL)
    sp = s_pairs.reshape(Bp, A_PAIRS, 2, A_HEAD, 2, A_HEAD)
    rwkv_p = jnp.stack([sp[:, :, 0, :, 0], sp[:, :, 1, :, 1]], axis=2).reshape(Bp, A_HEADS, A_HEAD, A_HEAD)
    shift_p = pa.reshape(Bp, Tp, A_PROJ)[:, -1][:, inv]
    conv_p = qkv.reshape(Bp, Tp, CONV_CH)[:, Tp - (CONV_K - 1):]

    xs = x_sample.reshape(Bs * Ts, D_MODEL)
    h, pa, qkv, z, gates, ab = trunk_front(xs)
    pa3 = pa.reshape(Bs, Ts, A_PROJ)
    prev = jnp.concatenate([state_rwkv_shift[l][:, None, perm].astype(F32), pa3[:, :-1]], axis=1)
    s_in = state_rwkv[l].astype(F32).reshape(Bs, A_PAIRS, 2, A_HEAD, A_HEAD)
    s_in = jnp.transpose(s_in, (0, 1, 3, 2, 4)).reshape(Bs, A_PAIRS, A_HEAD, LANES)
    seqs = _token_tile(Bs, 8)
    oa, s_out = _rwkv_sample(pa, prev.reshape(Bs * Ts, A_PROJ), s_in, Ts, rwkv_params, seqs)
    s_out = jnp.transpose(s_out.reshape(Bs, A_PAIRS, A_HEAD, 2, A_HEAD), (0, 1, 3, 2, 4))
    rwkv_s = s_out.reshape(Bs, A_HEADS, A_HEAD, A_HEAD)
    shift_s = pa3[:, -1][:, inv]
    xpad = jnp.concatenate([state_conv[l].astype(F32), qkv.reshape(Bs, Ts, CONV_CH)], axis=1)
    shifted = [xpad[:, i:i + Ts].reshape(Bs * Ts, CONV_CH) for i in range(CONV_K)]
    ob, delta_s = _gdn_sample(shifted, ab, z, state_delta[l].astype(F32), Ts, gdn_params, seqs)
    conv_s = xpad[:, Ts:]
    y_sample = trunk_back(h, oa, ob, gates).reshape(Bs, Ts, D_MODEL)

    add_depth = lambda a: a[None]
    return (y_prompt, y_sample,
            add_depth(rwkv_p), add_depth(shift_p), add_depth(delta_p), add_depth(conv_p),
            add_depth(rwkv_s), add_depth(shift_s), add_depth(delta_s), add_depth(conv_s))
```

```python
import functools

import jax
import jax.numpy as jnp
from jax import lax
from jax.experimental import pallas as pl
from jax.experimental.pallas import tpu as pltpu

F32 = jnp.float32
BF16 = jnp.bfloat16

D_MODEL = 1024
D_FF = 2816
RMS_EPS = 1e-6
A_HEAD = 64
A_HEADS = 8
A_WIDTH = A_HEADS * A_HEAD
A_RANK_W = 64
A_RANK_A = 64
A_RANK_G = 128
A_PROJ = 3 * A_WIDTH + A_RANK_W + A_RANK_A + A_RANK_G
A_LNX_EPS = 64e-5
A_PAIRS = A_HEADS // 2
B_HEADS = 4
B_HEAD = 128
B_WIDTH = B_HEADS * B_HEAD
CONV_K = 4
CONV_CH = 3 * B_WIDTH
B_PROJ = CONV_CH + 2 * B_HEADS + B_WIDTH
GATE_COLS = 2 * D_MODEL
LANES = 128
SUBLANES = 8
VMEM_LIMIT_BYTES = 56 * 1024 * 1024
CHUNK = 64
PA_R, PA_K, PA_V, PA_WA, PA_G = 0, A_WIDTH, 2 * A_WIDTH, 3 * A_WIDTH, 3 * A_WIDTH + A_RANK_W + A_RANK_A
PROJ_SPLITS = (A_PROJ, CONV_CH, B_WIDTH, GATE_COLS, LANES)


def _cparams(n_grid_dims):
    return pltpu.CompilerParams(dimension_semantics=("arbitrary",) * n_grid_dims,
                                vmem_limit_bytes=VMEM_LIMIT_BYTES)


def _const_spec(shape):
    nd = len(shape)
    return pl.BlockSpec(shape, lambda *_: (0,) * nd, pipeline_mode=pl.Buffered(1))


def _dot(a, b):
    return jnp.dot(a, b, preferred_element_type=F32)


def _dot_nt(a, b):
    return lax.dot_general(a, b, (((1,), (1,)), ((), ())), preferred_element_type=F32)


def _split(x):
    hi = x.astype(BF16)
    lo = (x - hi.astype(F32)).astype(BF16)
    return hi, lo


def _split3(x):
    hi = x.astype(BF16)
    rest = x - hi.astype(F32)
    mid = rest.astype(BF16)
    lo = (rest - mid.astype(F32)).astype(BF16)
    return hi, mid, lo


def _mm(a, b):
    (ah, al), (bh, bl) = a, b
    return _dot(jnp.concatenate([ah, al, ah, al], axis=1), jnp.concatenate([bh, bh, bl, bl], axis=0))


def _mm_nt(a, b):
    (ah, al), (bh, bl) = a, b
    return _dot_nt(jnp.concatenate([ah, al, ah, al], axis=1), jnp.concatenate([bh, bh, bl, bl], axis=1))


def _mm_f32(a, b):
    return _mm(_split(a), _split(b))


def _sel_mm(sel, x):
    return _dot(jnp.concatenate([sel, sel, sel], axis=1), jnp.concatenate(_split3(x), axis=0))


def _mm_sel(x, sel, pieces=3):
    parts = _split3(x) if pieces == 3 else _split(x)
    return _dot(jnp.concatenate(parts, axis=1), jnp.concatenate([sel] * pieces, axis=0))


def _nilpotent_inverse(n, eye):
    t = eye + n
    sn = _split(n)
    for _ in range(CHUNK.bit_length() - 2):
        n = _mm(sn, sn)
        yield
        sn = _split(n)
        t = t + _mm(_split(t), sn)
        yield
    return t


def _round_robin(chains):
    chains = list(chains)
    while chains:
        for chain in list(chains):
            try:
                next(chain)
            except StopIteration:
                chains.remove(chain)


def _rms(x, w):
    return x * lax.rsqrt(jnp.mean(x * x, axis=-1, keepdims=True) + RMS_EPS) * w


def _sigmoid(x):
    return 1.0 / (1.0 + jnp.exp(-x))


def _silu(x):
    return x * _sigmoid(x)


def _softplus(x):
    return jnp.maximum(x, 0.0) + jnp.log(1.0 + jnp.exp(-jnp.abs(x)))


def _iota(shape, dim):
    return lax.broadcasted_iota(jnp.int32, shape, dim)


def _group(idx, size):
    assert size & (size - 1) == 0
    return lax.shift_right_logical(idx, size.bit_length() - 1)


def _one_hot(cond):
    return jnp.where(cond, 1.0, 0.0).astype(BF16)


def _rows_to_tile(rows):
    rid = _iota((SUBLANES, 1), 0)
    tile = jnp.zeros((SUBLANES, rows[0].shape[1]), F32)
    for i, row in enumerate(rows):
        tile = jnp.where(rid == i, row, tile)
    return tile


def _swiglu_half_step(x, nw, wg_ref, wu_ref, wd_ref):
    xn = _rms(x, nw).astype(BF16)
    g = _dot(xn, wg_ref[...])
    u = _dot(xn, wu_ref[...])
    act = (_silu(g) * u).astype(BF16)
    return x + 0.5 * _dot(act, wd_ref[...])


def _ffn_body(x_ref, nw_ref, wg_ref, wu_ref, wd_ref, o_ref):
    o_ref[...] = _swiglu_half_step(x_ref[...], nw_ref[...], wg_ref, wu_ref, wd_ref)


def _ffn(x, nw, wg, wu, wd, tm):
    n = x.shape[0]
    return pl.pallas_call(
        _ffn_body,
        grid=(n // tm,),
        in_specs=[pl.BlockSpec((tm, D_MODEL), lambda i: (i, 0)),
                  _const_spec((1, D_MODEL)),
                  _const_spec((D_MODEL, D_FF)), _const_spec((D_MODEL, D_FF)), _const_spec((D_FF, D_MODEL))],
        out_specs=pl.BlockSpec((tm, D_MODEL), lambda i: (i, 0)),
        out_shape=jax.ShapeDtypeStruct((n, D_MODEL), F32),
        compiler_params=_cparams(1),
        name="ffn1",
    )(x, nw, wg, wu, wd)


def _proj_body(h_ref, nw_ref, w_ref, *o_refs):
    u = _rms(h_ref[...], nw_ref[...]).astype(BF16)
    off = 0
    for o_ref, width in zip(o_refs, PROJ_SPLITS):
        o_ref[...] = _dot(u, w_ref[:, off:off + width])
        off += width


def _proj(h, nw, w_all, tm):
    n = h.shape[0]
    cols = sum(PROJ_SPLITS)
    return pl.pallas_call(
        _proj_body,
        grid=(n // tm,),
        in_specs=[pl.BlockSpec((tm, D_MODEL), lambda i: (i, 0)),
                  _const_spec((1, D_MODEL)), _const_spec((D_MODEL, cols))],
        out_specs=[pl.BlockSpec((tm, w), lambda i: (i, 0)) for w in PROJ_SPLITS],
        out_shape=[jax.ShapeDtypeStruct((n, w), F32) for w in PROJ_SPLITS],
        compiler_params=_cparams(1),
        name="proj",
    )(h, nw, w_all)


def _tail_body(h_ref, oa_ref, ob_ref, gates_ref, pa_ref, pb_ref, wo_ref, nw_ref, wg_ref, wu_ref, wd_ref,
               fn_ref, o_ref):
    ma = _dot(oa_ref[...].astype(BF16), pa_ref[...])
    mb = _dot(ob_ref[...].astype(BF16), pb_ref[...])
    merged = _sigmoid(gates_ref[:, :D_MODEL]) * ma + _sigmoid(gates_ref[:, D_MODEL:]) * mb
    h = h_ref[...] + _dot(merged.astype(BF16), wo_ref[...])
    h = _swiglu_half_step(h, nw_ref[...], wg_ref, wu_ref, wd_ref)
    o_ref[...] = _rms(h, fn_ref[...])


def _tail(h, oa, ob, gates, proj_a, proj_b, w_out, nw, wg, wu, wd, fn, tm):
    n = h.shape[0]
    row = lambda w: pl.BlockSpec((tm, w), lambda i: (i, 0))
    return pl.pallas_call(
        _tail_body,
        grid=(n // tm,),
        in_specs=[row(D_MODEL), row(A_WIDTH), row(B_WIDTH), row(GATE_COLS),
                  _const_spec((A_WIDTH, D_MODEL)), _const_spec((B_WIDTH, D_MODEL)),
                  _const_spec((D_MODEL, D_MODEL)), _const_spec((1, D_MODEL)),
                  _const_spec((D_MODEL, D_FF)), _const_spec((D_MODEL, D_FF)), _const_spec((D_FF, D_MODEL)),
                  _const_spec((1, D_MODEL))],
        out_specs=row(D_MODEL),
        out_shape=jax.ShapeDtypeStruct((n, D_MODEL), F32),
        compiler_params=_cparams(1),
        name="tail",
    )(h, oa, ob, gates, proj_a, proj_b, w_out, nw, wg, wu, wd, fn)


def _rwkv_token_math(x, prev, mu, w0, a0, k_k, k_a, w2a, g2):
    pm = x + (prev - x) * mu
    r = pm[:, PA_R:PA_R + A_WIDTH]
    k = pm[:, PA_K:PA_K + A_WIDTH]
    v = pm[:, PA_V:PA_V + A_WIDTH]
    wa = pm[:, PA_WA:PA_WA + LANES]
    gd = pm[:, PA_G:PA_G + A_RANK_G]
    lane = _iota((1, LANES), 1)
    lora_in = jnp.where(lane < A_RANK_W, jnp.tanh(wa), wa)
    lora = _mm_f32(lora_in, w2a)
    w_log = -_softplus(-(w0 + lora[:, :A_WIDTH])) - 0.5
    log_decay = -jnp.exp(w_log)
    a = _sigmoid(a0 + lora[:, A_WIDTH:])
    g = _mm_f32(_sigmoid(gd), g2)
    kk_raw = k * k_k
    k_mod = k * (1.0 + (a - 1.0) * k_a)
    return r, k_mod, v, kk_raw, a, log_decay, g


def _pair_mask(rows_per_head):
    shape = (2 * rows_per_head, LANES)
    return _group(_iota(shape, 0), rows_per_head) == _group(_iota(shape, 1), A_HEAD)


def _rwkv_prompt_body(pa_ref, mu_ref, w0_ref, a0_ref, kk_ref, ka_ref, rk_ref, lnw_ref, lnb_ref, w2a_ref, g2_ref,
                      o_ref, sfin_ref,
                      carry_ref, state_ref, r_s, k_s, v_s, kkraw_s, a_s, cum_s, ld_s, g_s):
    t = pl.program_id(1)
    tt = pa_ref.shape[0]
    C = CHUNK

    @pl.when(t == 0)
    def _():
        carry_ref[...] = jnp.zeros_like(carry_ref)
        state_ref[...] = jnp.zeros_like(state_ref)

    x = pa_ref[...]
    row = _iota((tt, 1), 0)
    prev = jnp.where(row == 0, carry_ref[SUBLANES - 1:SUBLANES, :], pltpu.roll(x, 1, axis=0))
    carry_ref[...] = x[tt - SUBLANES:tt, :]
    r, k_mod, v, kk_raw, a, log_decay, g = _rwkv_token_math(
        x, prev, mu_ref[...], w0_ref[...], a0_ref[...], kk_ref[...], ka_ref[...], w2a_ref[...], g2_ref[...])

    ri, ci = _iota((tt, tt), 0), _iota((tt, tt), 1)
    r_s[...] = r
    k_s[...] = k_mod
    v_s[...] = v
    kkraw_s[...] = kk_raw
    a_s[...] = a
    g_s[...] = g
    ld_s[...] = log_decay
    cum_s[...] = _sel_mm(_one_hot((_group(ri, C) == _group(ci, C)) & (ci <= ri)), log_decay)

    mask = _pair_mask(C)
    i2, j2 = _iota((2 * C, 2 * C), 0), _iota((2 * C, 2 * C), 1)
    strict = i2 > j2
    incl = i2 >= j2
    eye = jnp.where(i2 == j2, 1.0, 0.0)
    dup = lambda m: jnp.concatenate([m, m], axis=0)
    stack = lambda m: jnp.where(mask, dup(m), 0.0)

    def pair_chain(p, r0):
        sl = slice(p * LANES, (p + 1) * LANES)
        ld = lambda ref: ref[pl.ds(r0, C), sl]
        r_p, k_p, v_p, a_p, cum, ldec = ld(r_s), ld(k_s), ld(v_s), ld(a_s), ld(cum_s), ld(ld_s)
        einc = jnp.exp(cum)
        eex = jnp.exp(cum - ldec)
        einv = jnp.exp(-cum)
        etail = jnp.exp(cum[C - 1:C, :] - cum)
        kks = stack(ld(kkraw_s))
        kks = kks / jnp.maximum(jnp.sqrt(jnp.sum(kks * kks, axis=-1, keepdims=True)), 1e-12)
        As = -kks * dup(eex)
        Bs = kks * dup(a_p * einv)
        Bh = kks * dup(a_p * etail)
        Ks = stack(k_p * einv)
        Kh = stack(k_p * etail)
        Rs = stack(r_p * einc)
        Vs = stack(v_p)
        sVs = _split(Vs)
        sAR = _split(jnp.concatenate([As, Rs], axis=0))
        G = _mm_nt(sAR, _split(jnp.concatenate([Bs, Ks], axis=0)))
        yield
        Aab = jnp.where(strict, G[:2 * C, :2 * C], 0.0)
        Aak = jnp.where(strict, G[:2 * C, 2 * C:], 0.0)
        Arb = jnp.where(incl, G[2 * C:, :2 * C], 0.0)
        Ark = jnp.where(incl, G[2 * C:, 2 * C:], 0.0)
        Y = _mm(_split(Aak), sVs)
        yield
        T = yield from _nilpotent_inverse(Aab, eye)
        WU = _mm_f32(T, jnp.concatenate([As, Y], axis=1))
        yield
        S = state_ref[p]
        sS = _split(S)
        W = _mm_nt(_split(WU[:, :LANES]), sS) + WU[:, LANES:]
        yield
        WV = jnp.concatenate([W, Vs], axis=0)
        O = (_dot_nt(sAR[0][2 * C:], sS[0])
             + _dot(jnp.concatenate([Arb, Ark], axis=1).astype(BF16), WV.astype(BF16)))
        state_ref[p] = S * einc[C - 1:C, :] + _mm_f32(WV.T, jnp.concatenate([Bh, Kh], axis=0))
        yield
        mean = jnp.sum(O, axis=-1, keepdims=True) * (1.0 / A_HEAD)
        cen = jnp.where(mask, O - mean, 0.0)
        var = jnp.sum(cen * cen, axis=-1, keepdims=True) * (1.0 / A_HEAD)
        normed = jnp.where(mask, cen * lax.rsqrt(var + A_LNX_EPS) * lnw_ref[:, sl] + lnb_ref[:, sl], 0.0)
        bonus = jnp.sum(stack(r_p * k_p * rk_ref[:, sl]), axis=-1, keepdims=True) * Vs
        full = normed + bonus
        o_ref[pl.ds(r0, C), sl] = (full[:C] + full[C:]) * g_s[pl.ds(r0, C), sl]

    def chunk(c, carry):
        r0 = pl.multiple_of(c * C, C)
        _round_robin(pair_chain(p, r0) for p in range(A_PAIRS))
        return carry

    lax.fori_loop(0, tt // C, chunk, 0)

    @pl.when(t == pl.num_programs(1) - 1)
    def _():
        sfin_ref[0] = state_ref[...]


def _rwkv_prompt(pa, B, T, params, tt):
    n = pa.shape[0]
    nt = T // tt
    big = lambda: pltpu.VMEM((tt, A_WIDTH), F32)
    return pl.pallas_call(
        _rwkv_prompt_body,
        grid=(B, nt),
        in_specs=[pl.BlockSpec((tt, A_PROJ), lambda b, t: (b * nt + t, 0))] + [_const_spec(p.shape) for p in params],
        out_specs=[pl.BlockSpec((tt, A_WIDTH), lambda b, t: (b * nt + t, 0)),
                   pl.BlockSpec((1, A_PAIRS, LANES, LANES), lambda b, t: (b, 0, 0, 0))],
        out_shape=[jax.ShapeDtypeStruct((n, A_WIDTH), F32),
                   jax.ShapeDtypeStruct((B, A_PAIRS, LANES, LANES), F32)],
        scratch_shapes=[pltpu.VMEM((SUBLANES, A_PROJ), F32), pltpu.VMEM((A_PAIRS, LANES, LANES), F32)]
                       + [big() for _ in range(8)],
        compiler_params=_cparams(2),
        name="rwkv_prompt",
    )(pa, *params)


def _rwkv_sample_body(pa_ref, prev_ref, s_ref, mu_ref, w0_ref, a0_ref, kk_ref, ka_ref, rk_ref, lnw_ref, lnb_ref,
                      w2a_ref, g2_ref, o_ref, sout_ref, r_s, nkk_s, beta_s, dec_s, k_s, v_s, o_s, *, steps):
    rows = pa_ref.shape[0]
    per_tile = SUBLANES // steps
    r, k_mod, v, kk_raw, a, log_decay, g = _rwkv_token_math(
        pa_ref[...], prev_ref[...], mu_ref[...], w0_ref[...], a0_ref[...], kk_ref[...], ka_ref[...],
        w2a_ref[...], g2_ref[...])
    hi, hj = _iota((A_WIDTH, A_WIDTH), 0), _iota((A_WIDTH, A_WIDTH), 1)
    head_ones = _one_hot(_group(hi, A_HEAD) == _group(hj, A_HEAD))
    head_sum = lambda m: _mm_sel(m, head_ones)
    kk = kk_raw / jnp.maximum(jnp.sqrt(head_sum(kk_raw * kk_raw)), 1e-12)
    r_s[...] = r
    nkk_s[...] = -kk
    beta_s[...] = kk * a
    dec_s[...] = jnp.exp(log_decay)
    k_s[...] = k_mod
    v_s[...] = v
    pair_ones = head_ones[:LANES, :LANES]
    diag = _iota((A_HEAD, LANES), 0) == (_iota((A_HEAD, LANES), 1) & (A_HEAD - 1))

    def tile(j, carry):
        r0 = pl.multiple_of(j * SUBLANES, SUBLANES)
        ld = lambda ref: ref[pl.ds(r0, SUBLANES), :]
        r_t, nkk_t, beta_t, dec_t, k_t, v_t = ld(r_s), ld(nkk_s), ld(beta_s), ld(dec_s), ld(k_s), ld(v_s)
        out_rows = [[None] * A_PAIRS for _ in range(SUBLANES)]

        def seq_pair_chain(i, p):
            sl = slice(p * LANES, (p + 1) * LANES)
            rowof = lambda m, rr: m[rr:rr + 1, sl]
            v_diag = jnp.concatenate([jnp.where(diag, rowof(v_t, i * steps + s), 0.0) for s in range(steps)],
                                     axis=0)
            v_cols = _mm_sel(v_diag, pair_ones, pieces=2)
            S = s_ref[j * per_tile + i, p]
            for s in range(steps):
                rr = i * steps + s
                sa = _mm_sel(S * rowof(nkk_t, rr), pair_ones, pieces=2)
                yield
                S = (S * rowof(dec_t, rr) + sa * rowof(beta_t, rr)
                     + v_cols[s * A_HEAD:(s + 1) * A_HEAD] * rowof(k_t, rr))
                out = _mm_sel(S * rowof(r_t, rr), pair_ones, pieces=2)
                out_rows[rr][p] = jnp.sum(jnp.where(diag, out, 0.0), axis=0, keepdims=True)
            sout_ref[j * per_tile + i, p] = S

        _round_robin(seq_pair_chain(i, p) for i in range(per_tile) for p in range(A_PAIRS))
        o_s[pl.ds(r0, SUBLANES), :] = _rows_to_tile([jnp.concatenate(parts, axis=1) for parts in out_rows])
        return carry

    lax.fori_loop(0, rows // SUBLANES, tile, 0)
    o = o_s[...]
    mean = head_sum(o) * (1.0 / A_HEAD)
    cen = o - mean
    var = head_sum(cen * cen) * (1.0 / A_HEAD)
    o = cen * lax.rsqrt(var + A_LNX_EPS) * lnw_ref[...] + lnb_ref[...]
    o_ref[...] = (o + head_sum(r * k_mod * rk_ref[...]) * v) * g


def _rwkv_sample(pa, prev, state_pairs, steps, params, seqs):
    n = pa.shape[0]
    nb = state_pairs.shape[0]
    rows = seqs * steps
    assert SUBLANES % steps == 0 and rows % SUBLANES == 0
    sspec = pl.BlockSpec((seqs, A_PAIRS, A_HEAD, LANES), lambda i: (i, 0, 0, 0))
    return pl.pallas_call(
        functools.partial(_rwkv_sample_body, steps=steps),
        grid=(nb // seqs,),
        in_specs=[pl.BlockSpec((rows, A_PROJ), lambda i: (i, 0)), pl.BlockSpec((rows, A_PROJ), lambda i: (i, 0)),
                  sspec] + [_const_spec(p.shape) for p in params],
        out_specs=[pl.BlockSpec((rows, A_WIDTH), lambda i: (i, 0)), sspec],
        out_shape=[jax.ShapeDtypeStruct((n, A_WIDTH), F32),
                   jax.ShapeDtypeStruct(state_pairs.shape, F32)],
        scratch_shapes=[pltpu.VMEM((rows, A_WIDTH), F32) for _ in range(7)],
        compiler_params=_cparams(1),
        name="rwkv_sample",
    )(pa, prev, state_pairs, *params)


def _gdn_token_math(conv, ab, alog, dtb):
    c = _silu(conv)
    qs, ks = [], []
    for h in range(B_HEADS):
        q = c[:, h * B_HEAD:(h + 1) * B_HEAD]
        k = c[:, B_WIDTH + h * B_HEAD:B_WIDTH + (h + 1) * B_HEAD]
        qs.append(q * lax.rsqrt(jnp.sum(q * q, axis=-1, keepdims=True) + 1e-6) * (B_HEAD ** -0.5))
        ks.append(k * lax.rsqrt(jnp.sum(k * k, axis=-1, keepdims=True) + 1e-6))
    q = jnp.concatenate(qs, axis=1)
    k = jnp.concatenate(ks, axis=1)
    v = c[:, 2 * B_WIDTH:]
    lane = _iota((1, LANES), 1)
    g = -jnp.exp(alog) * _softplus(ab + dtb)
    beta = _sigmoid(ab)
    gb = jnp.where(lane < B_HEADS, g, beta)
    si, sj = _iota((LANES, 2 * B_WIDTH), 0), _iota((LANES, 2 * B_WIDTH), 1)
    spread = _mm_sel(gb, _one_hot(si == _group(sj, B_HEAD)))
    return q, k, v, spread[:, :B_WIDTH], spread[:, B_WIDTH:]


def _gdn_out(o, norm_w, z):
    return o * lax.rsqrt(jnp.mean(o * o, axis=-1, keepdims=True) + RMS_EPS) * norm_w * _silu(z)


def _gdn_prompt_body(qkv_ref, ab_ref, z_ref, cw_ref, alog_ref, dtb_ref, nw_ref, o_ref, sfin_ref,
                     carry_ref, state_ref, q_s, k_s, v_s, gc_s, beta_s):
    t = pl.program_id(1)
    tt = qkv_ref.shape[0]
    C = CHUNK

    @pl.when(t == 0)
    def _():
        carry_ref[...] = jnp.zeros_like(carry_ref)
        state_ref[...] = jnp.zeros_like(state_ref)

    x = qkv_ref[...]
    row8 = _iota((SUBLANES, 1), 0)
    conv = x * cw_ref[CONV_K - 1:CONV_K, :]
    for i in range(1, CONV_K):
        xs = pltpu.roll(x, i, axis=0)
        top = jnp.where(row8 < i, pltpu.roll(carry_ref[...], i, axis=0), xs[:SUBLANES])
        xs = jnp.concatenate([top, xs[SUBLANES:]], axis=0)
        conv = conv + xs * cw_ref[CONV_K - 1 - i:CONV_K - i, :]
    carry_ref[...] = x[tt - SUBLANES:tt, :]
    q, k, v, g, beta = _gdn_token_math(conv, ab_ref[...], alog_ref[...], dtb_ref[...])
    ri, ci = _iota((tt, tt), 0), _iota((tt, tt), 1)
    q_s[...] = q
    k_s[...] = k
    v_s[...] = v
    beta_s[...] = beta
    gc_s[...] = _sel_mm(_one_hot((_group(ri, C) == _group(ci, C)) & (ci <= ri)), g)

    i2, j2 = _iota((2 * C, 2 * C), 0), _iota((2 * C, 2 * C), 1)
    same_head = _group(i2, C) == _group(j2, C)
    strict = same_head & (i2 > j2)
    incl = same_head & (i2 >= j2)
    eye = jnp.where(i2 == j2, 1.0, 0.0)
    first = _iota((2 * C, 1), 0) < C

    def pair_chain(pr, r0):
        heads = (2 * pr, 2 * pr + 1)
        sls = [slice(h * B_HEAD, (h + 1) * B_HEAD) for h in heads]
        ld = lambda ref: jnp.concatenate([ref[pl.ds(r0, C), sl] for sl in sls], axis=0)
        q_h, k_h, v_h, gc_h, beta_h = ld(q_s), ld(k_s), ld(v_s), ld(gc_s), ld(beta_s)
        diff = gc_h - gc_h.T
        dm = jnp.where(incl, jnp.exp(jnp.where(incl, diff, 0.0)), 0.0)
        kb = k_h * beta_h
        QK = _mm_nt(_split(jnp.concatenate([kb, q_h], axis=0)), _split(k_h))
        yield
        N = -jnp.where(strict, QK[:2 * C] * dm, 0.0)
        qk = QK[2 * C:] * dm
        egc = jnp.exp(gc_h)
        X = jnp.concatenate([v_h * beta_h, kb * egc], axis=1)
        T = yield from _nilpotent_inverse(N, eye)
        UW = _mm_f32(T, X)
        yield
        qd = (q_h * egc).astype(BF16)
        g_last = jnp.where(first, gc_h[C - 1:C, :], gc_h[2 * C - 1:2 * C, :])
        k_dec_t = (k_h * jnp.exp(g_last - gc_h)).T
        sw = _split(UW[:, B_HEAD:])
        wS, qS, S_old = [], [], []
        for j, h in enumerate(heads):
            S = state_ref[h]
            sS = _split(S)
            rows = slice(j * C, (j + 1) * C)
            wS.append(_mm((sw[0][rows], sw[1][rows]), sS))
            qS.append(_dot(qd[rows], sS[0]))
            S_old.append(S)
        yield
        v_new = UW[:, :B_HEAD] - jnp.concatenate(wS, axis=0)
        o = jnp.concatenate(qS, axis=0) + _dot(qk.astype(BF16), v_new.astype(BF16))
        sk = _split(k_dec_t)
        for j, h in enumerate(heads):
            mine = first if j == 0 else jnp.logical_not(first)
            rows = slice(j * C, (j + 1) * C)
            decay = jnp.exp(gc_h[(j + 1) * C - 1:(j + 1) * C, :])
            state_ref[h] = S_old[j] * decay + _mm(sk, _split(jnp.where(mine, v_new, 0.0)))
            o_ref[pl.ds(r0, C), sls[j]] = _gdn_out(o[rows], nw_ref[...], z_ref[pl.ds(r0, C), sls[j]])

    def chunk(c, carry):
        r0 = pl.multiple_of(c * C, C)
        _round_robin(pair_chain(pr, r0) for pr in range(B_HEADS // 2))
        return carry

    lax.fori_loop(0, tt // C, chunk, 0)

    @pl.when(t == pl.num_programs(1) - 1)
    def _():
        sfin_ref[0] = state_ref[...]


def _gdn_prompt(qkv, ab, z, B, T, params, tt):
    n = qkv.shape[0]
    nt = T // tt
    rows = lambda w: pl.BlockSpec((tt, w), lambda b, t: (b * nt + t, 0))
    big = lambda: pltpu.VMEM((tt, B_WIDTH), F32)
    return pl.pallas_call(
        _gdn_prompt_body,
        grid=(B, nt),
        in_specs=[rows(CONV_CH), rows(LANES), rows(B_WIDTH)] + [_const_spec(p.shape) for p in params],
        out_specs=[rows(B_WIDTH), pl.BlockSpec((1, B_HEADS, B_HEAD, B_HEAD), lambda b, t: (b, 0, 0, 0))],
        out_shape=[jax.ShapeDtypeStruct((n, B_WIDTH), F32),
                   jax.ShapeDtypeStruct((B, B_HEADS, B_HEAD, B_HEAD), F32)],
        scratch_shapes=[pltpu.VMEM((SUBLANES, CONV_CH), F32), pltpu.VMEM((B_HEADS, B_HEAD, B_HEAD), F32)]
                       + [big() for _ in range(5)],
        compiler_params=_cparams(2),
        name="gdn_prompt",
    )(qkv, ab, z, *params)


def _gdn_sample_body(x0_ref, x1_ref, x2_ref, x3_ref, ab_ref, z_ref, s_ref, cw_ref, alog_ref, dtb_ref, nw_ref,
                     o_ref, sout_ref, q_s, k_s, v_s, eg_s, beta_s, o_s, *, steps):
    rows = ab_ref.shape[0]
    per_tile = SUBLANES // steps
    conv = (x0_ref[...] * cw_ref[0:1, :] + x1_ref[...] * cw_ref[1:2, :]
            + x2_ref[...] * cw_ref[2:3, :] + x3_ref[...] * cw_ref[3:4, :])
    q, k, v, g, beta = _gdn_token_math(conv, ab_ref[...], alog_ref[...], dtb_ref[...])
    q_s[...] = q
    k_s[...] = k
    v_s[...] = v
    eg_s[...] = jnp.exp(g)
    beta_s[...] = beta
    diag = _iota((B_HEAD, B_HEAD), 0) == _iota((B_HEAD, B_HEAD), 1)
    ones = jnp.ones((B_HEAD, B_HEAD), BF16)

    def to_cols(tile_rows):
        diags = jnp.concatenate([jnp.where(diag, rv, 0.0) for rv in tile_rows], axis=0)
        return _mm_sel(diags, ones, pieces=2)

    def tile(j, carry):
        r0 = pl.multiple_of(j * SUBLANES, SUBLANES)
        ld = lambda ref: ref[pl.ds(r0, SUBLANES), :]
        q_t, k_t, v_t, eg_t, beta_t = ld(q_s), ld(k_s), ld(v_s), ld(eg_s), ld(beta_s)
        out_rows = [[] for _ in range(SUBLANES)]
        for i in range(per_tile):
            for h in range(B_HEADS):
                sl = slice(h * B_HEAD, (h + 1) * B_HEAD)
                rowof = lambda m, rr: m[rr:rr + 1, sl]
                rrs = [i * steps + s for s in range(steps)]
                kq_cols = to_cols([rowof(k_t, rr) for rr in rrs] + [rowof(q_t, rr) for rr in rrs])
                S = s_ref[j * per_tile + i, h]
                for s, rr in enumerate(rrs):
                    k_col = kq_cols[s * B_HEAD:(s + 1) * B_HEAD]
                    q_col = kq_cols[(steps + s) * B_HEAD:(steps + s + 1) * B_HEAD]
                    S = S * rowof(eg_t, rr)
                    u = jnp.sum(k_col * S, axis=0, keepdims=True)
                    S = S + k_col * (rowof(beta_t, rr) * (rowof(v_t, rr) - u))
                    out_rows[rr].append(jnp.sum(q_col * S, axis=0, keepdims=True))
                sout_ref[j * per_tile + i, h] = S
        o_s[pl.ds(r0, SUBLANES), :] = _rows_to_tile([jnp.concatenate(parts, axis=1) for parts in out_rows])
        return carry

    lax.fori_loop(0, rows // SUBLANES, tile, 0)
    for h in range(B_HEADS):
        sl = slice(h * B_HEAD, (h + 1) * B_HEAD)
        o_ref[:, sl] = _gdn_out(o_s[:, sl], nw_ref[...], z_ref[:, sl])


def _gdn_sample(xs, ab, z, state, steps, params, seqs):
    n = ab.shape[0]
    nb = state.shape[0]
    rows = seqs * steps
    assert SUBLANES % steps == 0 and rows % SUBLANES == 0
    row = lambda w: pl.BlockSpec((rows, w), lambda i: (i, 0))
    sspec = pl.BlockSpec((seqs, B_HEADS, B_HEAD, B_HEAD), lambda i: (i, 0, 0, 0))
    return pl.pallas_call(
        functools.partial(_gdn_sample_body, steps=steps),
        grid=(nb // seqs,),
        in_specs=[row(CONV_CH)] * CONV_K + [row(LANES), row(B_WIDTH), sspec] + [_const_spec(p.shape) for p in params],
        out_specs=[row(B_WIDTH), sspec],
        out_shape=[jax.ShapeDtypeStruct((n, B_WIDTH), F32), jax.ShapeDtypeStruct(state.shape, F32)],
        scratch_shapes=[pltpu.VMEM((rows, B_WIDTH), F32) for _ in range(6)],
        compiler_params=_cparams(1),
        name="gdn_sample",
    )(*xs, ab, z, state, *params)


def _pa_perm():
    import numpy as np
    r = np.arange(0, A_WIDTH)
    wd = np.arange(A_WIDTH, A_WIDTH + A_RANK_W)
    k = np.arange(A_WIDTH + A_RANK_W, 2 * A_WIDTH + A_RANK_W)
    v = np.arange(2 * A_WIDTH + A_RANK_W, 3 * A_WIDTH + A_RANK_W)
    ad = np.arange(3 * A_WIDTH + A_RANK_W, 3 * A_WIDTH + A_RANK_W + A_RANK_A)
    gd = np.arange(3 * A_WIDTH + A_RANK_W + A_RANK_A, A_PROJ)
    perm = np.concatenate([r, k, v, wd, ad, gd])
    inv = np.argsort(perm)
    return perm, inv


def _token_tile(n, want):
    tm = want
    while n % tm:
        tm //= 2
    return tm


def kernel(x_prompt, x_sample, state_rwkv, state_rwkv_shift, state_delta, state_conv, ffn1_norm, ffn1_w_gate, ffn1_w_up, ffn1_w_down, mix_norm, w_in, rwkv_mu, rwkv_w0, rwkv_w2, rwkv_a0, rwkv_a2, rwkv_g2, rwkv_k_k, rwkv_k_a, rwkv_r_k, rwkv_lnx_w, rwkv_lnx_b, gdn_conv_w, gdn_A_log, gdn_dt_bias, gdn_norm_w, proj_a, proj_b, w_out, ffn2_norm, ffn2_w_gate, ffn2_w_up, ffn2_w_down, final_norm):
    depth = ffn1_norm.shape[0]
    assert depth == 1, "single-layer trunk"
    Bp, Tp, _ = x_prompt.shape
    Bs, Ts, _ = x_sample.shape
    perm, inv = _pa_perm()
    l = 0
    row = lambda a: a.reshape(1, -1).astype(F32)

    wi = w_in[l]
    o_b = A_PROJ
    w_all = jnp.concatenate([
        wi[:, :A_PROJ][:, perm],
        wi[:, o_b:o_b + CONV_CH],
        wi[:, o_b + CONV_CH + 2 * B_HEADS:o_b + B_PROJ],
        wi[:, o_b + B_PROJ:],
        jnp.pad(wi[:, o_b + CONV_CH:o_b + CONV_CH + 2 * B_HEADS], ((0, 0), (0, LANES - 2 * B_HEADS))),
    ], axis=1).astype(BF16)
    ffn1 = (row(ffn1_norm[l]), ffn1_w_gate[l].astype(BF16), ffn1_w_up[l].astype(BF16), ffn1_w_down[l].astype(BF16))
    ffn2 = (row(ffn2_norm[l]), ffn2_w_gate[l].astype(BF16), ffn2_w_up[l].astype(BF16), ffn2_w_down[l].astype(BF16))
    merge_w = (proj_a[l].astype(BF16), proj_b[l].astype(BF16), w_out[l].astype(BF16))
    zw = jnp.zeros((A_RANK_W, A_WIDTH), F32)
    w2a = jnp.concatenate([jnp.concatenate([rwkv_w2[l], zw], axis=1),
                           jnp.concatenate([zw, rwkv_a2[l]], axis=1)], axis=0)
    rwkv_params = (row(rwkv_mu[l][perm]), row(rwkv_w0[l]), row(rwkv_a0[l]), row(rwkv_k_k[l]), row(rwkv_k_a[l]),
                   row(rwkv_r_k[l]), row(rwkv_lnx_w[l]), row(rwkv_lnx_b[l]), w2a, rwkv_g2[l].astype(F32))
    pad_lane = lambda a: jnp.pad(a.reshape(1, -1).astype(F32), ((0, 0), (0, LANES - a.size)))
    gdn_params = (gdn_conv_w[l].astype(F32), pad_lane(gdn_A_log[l]), pad_lane(gdn_dt_bias[l]), row(gdn_norm_w[l]))

    def trunk_front(x2):
        n = x2.shape[0]
        h = _ffn(x2, *ffn1, tm=_token_tile(n, 512))
        return (h,) + tuple(_proj(h, row(mix_norm[l]), w_all, tm=_token_tile(n, 256)))

    def trunk_back(h, oa, ob, gates):
        n = h.shape[0]
        return _tail(h, oa, ob, gates, *merge_w, *ffn2, row(final_norm), tm=_token_tile(n, 256))

    xp = x_prompt.reshape(Bp * Tp, D_MODEL)
    h, pa, qkv, z, gates, ab = trunk_front(xp)
    tt = _token_tile(Tp, 256)
    oa, s_pairs = _rwkv_prompt(pa, Bp, Tp, rwkv_params, tt)
    ob, delta_p = _gdn_prompt(qkv, ab, z, Bp, Tp, gdn_params, tt)
    y_prompt = trunk_back(h, oa, ob, gates).reshape(Bp, Tp, D_MODEL)
    sp = s_pairs.reshape(Bp, A_PAIRS, 2, A_HEAD, 2, A_HEAD)
    rwkv_p = jnp.stack([sp[:, :, 0, :, 0], sp[:, :, 1, :, 1]], axis=2).reshape(Bp, A_HEADS, A_HEAD, A_HEAD)
    shift_p = pa.reshape(Bp, Tp, A_PROJ)[:, -1][:, inv]
    conv_p = qkv.reshape(Bp, Tp, CONV_CH)[:, Tp - (CONV_K - 1):]

    xs = x_sample.reshape(Bs * Ts, D_MODEL)
    h, pa, qkv, z, gates, ab = trunk_front(xs)
    pa3 = pa.reshape(Bs, Ts, A_PROJ)
    prev = jnp.concatenate([state_rwkv_shift[l][:, None, perm].astype(F32), pa3[:, :-1]], axis=1)
    s_in = state_rwkv[l].astype(F32).reshape(Bs, A_PAIRS, 2, A_HEAD, A_HEAD)
    s_in = jnp.transpose(s_in, (0, 1, 3, 2, 4)).reshape(Bs, A_PAIRS, A_HEAD, LANES)
    oa, s_out = _rwkv_sample(pa, prev.reshape(Bs * Ts, A_PROJ), s_in, Ts, rwkv_params, seqs=_token_tile(Bs, 32))
    s_out = jnp.transpose(s_out.reshape(Bs, A_PAIRS, A_HEAD, 2, A_HEAD), (0, 1, 3, 2, 4))
    rwkv_s = s_out.reshape(Bs, A_HEADS, A_HEAD, A_HEAD)
    shift_s = pa3[:, -1][:, inv]
    xpad = jnp.concatenate([state_conv[l].astype(F32), qkv.reshape(Bs, Ts, CONV_CH)], axis=1)
    shifted = [xpad[:, i:i + Ts].reshape(Bs * Ts, CONV_CH) for i in range(CONV_K)]
    ob, delta_s = _gdn_sample(shifted, ab, z, state_delta[l].astype(F32), Ts, gdn_params, seqs=_token_tile(Bs, 16))
    conv_s = xpad[:, Ts:]
    y_sample = trunk_back(h, oa, ob, gates).reshape(Bs, Ts, D_MODEL)

    add_depth = lambda a: a[None]
    return (y_prompt, y_sample,
            add_depth(rwkv_p), add_depth(shift_p), add_depth(delta_p), add_depth(conv_p),
            add_depth(rwkv_s), add_depth(shift_s), add_depth(delta_s), add_depth(conv_s))
```

```python
import functools

import jax
import jax.numpy as jnp
from jax import lax
from jax.experimental import pallas as pl
from jax.experimental.pallas import tpu as pltpu

F32 = jnp.float32
BF16 = jnp.bfloat16

D_MODEL = 1024
D_FF = 2816
RMS_EPS = 1e-6
A_HEAD = 64
A_HEADS = 8
A_WIDTH = A_HEADS * A_HEAD
A_RANK_W = 64
A_RANK_A = 64
A_RANK_G = 128
A_PROJ = 3 * A_WIDTH + A_RANK_W + A_RANK_A + A_RANK_G
A_LNX_EPS = 64e-5
A_PAIRS = A_HEADS // 2
B_HEADS = 4
B_HEAD = 128
B_WIDTH = B_HEADS * B_HEAD
CONV_K = 4
CONV_CH = 3 * B_WIDTH
B_PROJ = CONV_CH + 2 * B_HEADS + B_WIDTH
GATE_COLS = 2 * D_MODEL
LANES = 128
SUBLANES = 8
VMEM_LIMIT_BYTES = 56 * 1024 * 1024
CHUNK = 64
PA_R, PA_K, PA_V, PA_WA, PA_G = 0, A_WIDTH, 2 * A_WIDTH, 3 * A_WIDTH, 3 * A_WIDTH + A_RANK_W + A_RANK_A
PROJ_SPLITS = (A_PROJ, CONV_CH, B_WIDTH, GATE_COLS, LANES)


def _cparams(n_grid_dims):
    return pltpu.CompilerParams(dimension_semantics=("arbitrary",) * n_grid_dims,
                                vmem_limit_bytes=VMEM_LIMIT_BYTES)


def _const_spec(shape):
    nd = len(shape)
    return pl.BlockSpec(shape, lambda *_: (0,) * nd, pipeline_mode=pl.Buffered(1))


def _dot(a, b):
    return jnp.dot(a, b, preferred_element_type=F32)


def _dot_nt(a, b):
    return lax.dot_general(a, b, (((1,), (1,)), ((), ())), preferred_element_type=F32)


def _split(x):
    hi = x.astype(BF16)
    lo = (x - hi.astype(F32)).astype(BF16)
    return hi, lo


def _split3(x):
    hi = x.astype(BF16)
    rest = x - hi.astype(F32)
    mid = rest.astype(BF16)
    lo = (rest - mid.astype(F32)).astype(BF16)
    return hi, mid, lo


def _mm(a, b):
    (ah, al), (bh, bl) = a, b
    return _dot(jnp.concatenate([ah, al, ah, al], axis=1), jnp.concatenate([bh, bh, bl, bl], axis=0))


def _mm_nt(a, b):
    (ah, al), (bh, bl) = a, b
    return _dot_nt(jnp.concatenate([ah, al, ah, al], axis=1), jnp.concatenate([bh, bh, bl, bl], axis=1))


def _mm_f32(a, b):
    return _mm(_split(a), _split(b))


def _sel_mm(sel, x):
    return _dot(jnp.concatenate([sel, sel, sel], axis=1), jnp.concatenate(_split3(x), axis=0))


def _mm_sel(x, sel, pieces=3):
    parts = _split3(x) if pieces == 3 else _split(x)
    return _dot(jnp.concatenate(parts, axis=1), jnp.concatenate([sel] * pieces, axis=0))


def _nilpotent_inverse(n, eye):
    width = n.shape[1]
    t = eye + n
    sn = _split(n)
    n = _mm(sn, sn)
    yield
    for _ in range(CHUNK.bit_length() - 3):
        both = _mm(_split(n), _split(jnp.concatenate([t, n], axis=1)))
        yield
        t = t + both[:, :width]
        n = both[:, width:]
    t = t + _mm(_split(n), _split(t))
    yield
    return t


def _round_robin(chains):
    chains = list(chains)
    while chains:
        for chain in list(chains):
            try:
                next(chain)
            except StopIteration:
                chains.remove(chain)


def _rms(x, w):
    return x * lax.rsqrt(jnp.mean(x * x, axis=-1, keepdims=True) + RMS_EPS) * w


def _sigmoid(x):
    return 1.0 / (1.0 + jnp.exp(-x))


def _silu(x):
    return x * _sigmoid(x)


def _softplus(x):
    return jnp.maximum(x, 0.0) + jnp.log(1.0 + jnp.exp(-jnp.abs(x)))


def _iota(shape, dim):
    return lax.broadcasted_iota(jnp.int32, shape, dim)


def _group(idx, size):
    assert size & (size - 1) == 0
    return lax.shift_right_logical(idx, size.bit_length() - 1)


def _one_hot(cond):
    return jnp.where(cond, 1.0, 0.0).astype(BF16)


def _rows_to_tile(rows):
    rid = _iota((SUBLANES, 1), 0)
    tile = jnp.zeros((SUBLANES, rows[0].shape[1]), F32)
    for i, row in enumerate(rows):
        tile = jnp.where(rid == i, row, tile)
    return tile


def _swiglu_half_step(x, nw, wg_ref, wu_ref, wd_ref):
    xn = _rms(x, nw).astype(BF16)
    g = _dot(xn, wg_ref[...])
    u = _dot(xn, wu_ref[...])
    act = (_silu(g) * u).astype(BF16)
    return x + 0.5 * _dot(act, wd_ref[...])


def _ffn_body(x_ref, nw_ref, wg_ref, wu_ref, wd_ref, o_ref):
    o_ref[...] = _swiglu_half_step(x_ref[...], nw_ref[...], wg_ref, wu_ref, wd_ref)


def _ffn(x, nw, wg, wu, wd, tm):
    n = x.shape[0]
    return pl.pallas_call(
        _ffn_body,
        grid=(n // tm,),
        in_specs=[pl.BlockSpec((tm, D_MODEL), lambda i: (i, 0)),
                  _const_spec((1, D_MODEL)),
                  _const_spec((D_MODEL, D_FF)), _const_spec((D_MODEL, D_FF)), _const_spec((D_FF, D_MODEL))],
        out_specs=pl.BlockSpec((tm, D_MODEL), lambda i: (i, 0)),
        out_shape=jax.ShapeDtypeStruct((n, D_MODEL), F32),
        compiler_params=_cparams(1),
        name="ffn1",
    )(x, nw, wg, wu, wd)


def _proj_body(h_ref, nw_ref, w_ref, *o_refs):
    u = _rms(h_ref[...], nw_ref[...]).astype(BF16)
    off = 0
    for o_ref, width in zip(o_refs, PROJ_SPLITS):
        o_ref[...] = _dot(u, w_ref[:, off:off + width])
        off += width


def _proj(h, nw, w_all, tm):
    n = h.shape[0]
    cols = sum(PROJ_SPLITS)
    return pl.pallas_call(
        _proj_body,
        grid=(n // tm,),
        in_specs=[pl.BlockSpec((tm, D_MODEL), lambda i: (i, 0)),
                  _const_spec((1, D_MODEL)), _const_spec((D_MODEL, cols))],
        out_specs=[pl.BlockSpec((tm, w), lambda i: (i, 0)) for w in PROJ_SPLITS],
        out_shape=[jax.ShapeDtypeStruct((n, w), F32) for w in PROJ_SPLITS],
        compiler_params=_cparams(1),
        name="proj",
    )(h, nw, w_all)


def _tail_body(h_ref, oa_ref, ob_ref, gates_ref, pa_ref, pb_ref, wo_ref, nw_ref, wg_ref, wu_ref, wd_ref,
               fn_ref, o_ref):
    ma = _dot(oa_ref[...].astype(BF16), pa_ref[...])
    mb = _dot(ob_ref[...].astype(BF16), pb_ref[...])
    merged = _sigmoid(gates_ref[:, :D_MODEL]) * ma + _sigmoid(gates_ref[:, D_MODEL:]) * mb
    h = h_ref[...] + _dot(merged.astype(BF16), wo_ref[...])
    h = _swiglu_half_step(h, nw_ref[...], wg_ref, wu_ref, wd_ref)
    o_ref[...] = _rms(h, fn_ref[...])


def _tail(h, oa, ob, gates, proj_a, proj_b, w_out, nw, wg, wu, wd, fn, tm):
    n = h.shape[0]
    row = lambda w: pl.BlockSpec((tm, w), lambda i: (i, 0))
    return pl.pallas_call(
        _tail_body,
        grid=(n // tm,),
        in_specs=[row(D_MODEL), row(A_WIDTH), row(B_WIDTH), row(GATE_COLS),
                  _const_spec((A_WIDTH, D_MODEL)), _const_spec((B_WIDTH, D_MODEL)),
                  _const_spec((D_MODEL, D_MODEL)), _const_spec((1, D_MODEL)),
                  _const_spec((D_MODEL, D_FF)), _const_spec((D_MODEL, D_FF)), _const_spec((D_FF, D_MODEL)),
                  _const_spec((1, D_MODEL))],
        out_specs=row(D_MODEL),
        out_shape=jax.ShapeDtypeStruct((n, D_MODEL), F32),
        compiler_params=_cparams(1),
        name="tail",
    )(h, oa, ob, gates, proj_a, proj_b, w_out, nw, wg, wu, wd, fn)


def _rwkv_token_math(x, prev, mu, w0, a0, k_k, k_a, w2a, g2):
    pm = x + (prev - x) * mu
    r = pm[:, PA_R:PA_R + A_WIDTH]
    k = pm[:, PA_K:PA_K + A_WIDTH]
    v = pm[:, PA_V:PA_V + A_WIDTH]
    wa = pm[:, PA_WA:PA_WA + LANES]
    gd = pm[:, PA_G:PA_G + A_RANK_G]
    lane = _iota((1, LANES), 1)
    lora_in = jnp.where(lane < A_RANK_W, jnp.tanh(wa), wa)
    lora = _mm_f32(lora_in, w2a)
    w_log = -_softplus(-(w0 + lora[:, :A_WIDTH])) - 0.5
    log_decay = -jnp.exp(w_log)
    a = _sigmoid(a0 + lora[:, A_WIDTH:])
    g = _mm_f32(_sigmoid(gd), g2)
    kk_raw = k * k_k
    k_mod = k * (1.0 + (a - 1.0) * k_a)
    return r, k_mod, v, kk_raw, a, log_decay, g


def _pair_mask(rows_per_head):
    shape = (2 * rows_per_head, LANES)
    return _group(_iota(shape, 0), rows_per_head) == _group(_iota(shape, 1), A_HEAD)


def _rwkv_prompt_part(pa_ref, mu_ref, w0_ref, a0_ref, kk_ref, ka_ref, rk_ref, lnw_ref, lnb_ref, w2a_ref, g2_ref,
                      o_ref, carry_ref, state_ref, r_s, k_s, v_s, kkraw_s, a_s, cum_s, ld_s, g_s):
    tt = pa_ref.shape[0]
    C = CHUNK
    x = pa_ref[...]
    row = _iota((tt, 1), 0)
    prev = jnp.where(row == 0, carry_ref[SUBLANES - 1:SUBLANES, :], pltpu.roll(x, 1, axis=0))
    carry_ref[...] = x[tt - SUBLANES:tt, :]
    r, k_mod, v, kk_raw, a, log_decay, g = _rwkv_token_math(
        x, prev, mu_ref[...], w0_ref[...], a0_ref[...], kk_ref[...], ka_ref[...], w2a_ref[...], g2_ref[...])

    ri, ci = _iota((tt, tt), 0), _iota((tt, tt), 1)
    r_s[...] = r
    k_s[...] = k_mod
    v_s[...] = v
    kkraw_s[...] = kk_raw
    a_s[...] = a
    g_s[...] = g
    ld_s[...] = log_decay
    cum_s[...] = _sel_mm(_one_hot((_group(ri, C) == _group(ci, C)) & (ci <= ri)), log_decay)

    mask = _pair_mask(C)
    i2, j2 = _iota((2 * C, 2 * C), 0), _iota((2 * C, 2 * C), 1)
    strict = i2 > j2
    incl = i2 >= j2
    eye = jnp.where(i2 == j2, 1.0, 0.0)
    dup = lambda m: jnp.concatenate([m, m], axis=0)
    stack = lambda m: jnp.where(mask, dup(m), 0.0)

    def pair_chain(p, r0):
        sl = slice(p * LANES, (p + 1) * LANES)
        ld = lambda ref: ref[pl.ds(r0, C), sl]
        r_p, k_p, v_p, a_p, cum, ldec = ld(r_s), ld(k_s), ld(v_s), ld(a_s), ld(cum_s), ld(ld_s)
        einc = jnp.exp(cum)
        eex = jnp.exp(cum - ldec)
        einv = jnp.exp(-cum)
        etail = jnp.exp(cum[C - 1:C, :] - cum)
        kks = stack(ld(kkraw_s))
        kks = kks / jnp.maximum(jnp.sqrt(jnp.sum(kks * kks, axis=-1, keepdims=True)), 1e-12)
        As = -kks * dup(eex)
        Bs = kks * dup(a_p * einv)
        Bh = kks * dup(a_p * etail)
        Ks = stack(k_p * einv)
        Kh = stack(k_p * etail)
        Rs = stack(r_p * einc)
        Vs = stack(v_p)
        sVs = _split(Vs)
        sAR = _split(jnp.concatenate([As, Rs], axis=0))
        G = _mm_nt(sAR, _split(jnp.concatenate([Bs, Ks], axis=0)))
        yield
        Aab = jnp.where(strict, G[:2 * C, :2 * C], 0.0)
        Aak = jnp.where(strict, G[:2 * C, 2 * C:], 0.0)
        Arb = jnp.where(incl, G[2 * C:, :2 * C], 0.0)
        Ark = jnp.where(incl, G[2 * C:, 2 * C:], 0.0)
        Y = _mm(_split(Aak), sVs)
        yield
        T = yield from _nilpotent_inverse(Aab, eye)
        WU = _mm_f32(T, jnp.concatenate([As, Y], axis=1))
        yield
        S = state_ref[p]
        sS = _split(S)
        W = _mm_nt(_split(WU[:, :LANES]), sS) + WU[:, LANES:]
        yield
        WV = jnp.concatenate([W, Vs], axis=0)
        O = (_dot_nt(sAR[0][2 * C:], sS[0])
             + _dot(jnp.concatenate([Arb, Ark], axis=1).astype(BF16), WV.astype(BF16)))
        state_ref[p] = S * einc[C - 1:C, :] + _mm_f32(WV.T, jnp.concatenate([Bh, Kh], axis=0))
        yield
        mean = jnp.sum(O, axis=-1, keepdims=True) * (1.0 / A_HEAD)
        cen = jnp.where(mask, O - mean, 0.0)
        var = jnp.sum(cen * cen, axis=-1, keepdims=True) * (1.0 / A_HEAD)
        normed = jnp.where(mask, cen * lax.rsqrt(var + A_LNX_EPS) * lnw_ref[:, sl] + lnb_ref[:, sl], 0.0)
        bonus = jnp.sum(stack(r_p * k_p * rk_ref[:, sl]), axis=-1, keepdims=True) * Vs
        full = normed + bonus
        o_ref[pl.ds(r0, C), sl] = (full[:C] + full[C:]) * g_s[pl.ds(r0, C), sl]

    return lambda r0: [pair_chain(p, r0) for p in range(A_PAIRS)]


N_RWKV_PARAMS = 10
N_GDN_PARAMS = 4
N_RWKV_SCRATCH = 8
N_GDN_SCRATCH = 5


def _mix_prompt_body(pa_ref, qkv_ref, ab_ref, z_ref, *refs):
    refs = list(refs)
    take = lambda n: [refs.pop(0) for _ in range(n)]
    rwkv_prm, gdn_prm = take(N_RWKV_PARAMS), take(N_GDN_PARAMS)
    oa_ref, ob_ref, sfa_ref, sfb_ref = take(4)
    carry_a, state_a, carry_b, state_b = take(4)
    rwkv_scr, gdn_scr = take(N_RWKV_SCRATCH), take(N_GDN_SCRATCH)
    t = pl.program_id(1)
    tt = pa_ref.shape[0]

    @pl.when(t == 0)
    def _():
        for ref in (carry_a, state_a, carry_b, state_b):
            ref[...] = jnp.zeros_like(ref)

    rwkv_chains = _rwkv_prompt_part(pa_ref, *rwkv_prm, oa_ref, carry_a, state_a, *rwkv_scr)
    gdn_chains = _gdn_prompt_part(qkv_ref, ab_ref, z_ref, *gdn_prm, ob_ref, carry_b, state_b, *gdn_scr)

    def chunk(c, carry):
        r0 = pl.multiple_of(c * CHUNK, CHUNK)
        _round_robin(rwkv_chains(r0) + gdn_chains(r0))
        return carry

    lax.fori_loop(0, tt // CHUNK, chunk, 0)

    @pl.when(t == pl.num_programs(1) - 1)
    def _():
        sfa_ref[0] = state_a[...]
        sfb_ref[0] = state_b[...]


def _mix_prompt(pa, qkv, ab, z, B, T, rwkv_params, gdn_params, tt):
    n = pa.shape[0]
    nt = T // tt
    assert len(rwkv_params) == N_RWKV_PARAMS and len(gdn_params) == N_GDN_PARAMS
    rows = lambda w: pl.BlockSpec((tt, w), lambda b, t: (b * nt + t, 0))
    state = lambda: pl.BlockSpec((1, 4, LANES, LANES), lambda b, t: (b, 0, 0, 0))
    big = lambda: pltpu.VMEM((tt, A_WIDTH), F32)
    return pl.pallas_call(
        _mix_prompt_body,
        grid=(B, nt),
        in_specs=[rows(A_PROJ), rows(CONV_CH), rows(LANES), rows(B_WIDTH)]
                 + [_const_spec(p.shape) for p in rwkv_params + gdn_params],
        out_specs=[rows(A_WIDTH), rows(B_WIDTH), state(), state()],
        out_shape=[jax.ShapeDtypeStruct((n, A_WIDTH), F32), jax.ShapeDtypeStruct((n, B_WIDTH), F32),
                   jax.ShapeDtypeStruct((B, A_PAIRS, LANES, LANES), F32),
                   jax.ShapeDtypeStruct((B, B_HEADS, B_HEAD, B_HEAD), F32)],
        scratch_shapes=[pltpu.VMEM((SUBLANES, A_PROJ), F32), pltpu.VMEM((A_PAIRS, LANES, LANES), F32),
                        pltpu.VMEM((SUBLANES, CONV_CH), F32), pltpu.VMEM((B_HEADS, B_HEAD, B_HEAD), F32)]
                       + [big() for _ in range(N_RWKV_SCRATCH + N_GDN_SCRATCH)],
        compiler_params=_cparams(2),
        name="mix_prompt",
    )(pa, qkv, ab, z, *rwkv_params, *gdn_params)


def _rwkv_sample_body(pa_ref, prev_ref, s_ref, mu_ref, w0_ref, a0_ref, kk_ref, ka_ref, rk_ref, lnw_ref, lnb_ref,
                      w2a_ref, g2_ref, o_ref, sout_ref, r_s, nkk_s, beta_s, dec_s, k_s, v_s, o_s, *, steps):
    rows = pa_ref.shape[0]
    per_tile = SUBLANES // steps
    r, k_mod, v, kk_raw, a, log_decay, g = _rwkv_token_math(
        pa_ref[...], prev_ref[...], mu_ref[...], w0_ref[...], a0_ref[...], kk_ref[...], ka_ref[...],
        w2a_ref[...], g2_ref[...])
    hi, hj = _iota((A_WIDTH, A_WIDTH), 0), _iota((A_WIDTH, A_WIDTH), 1)
    head_ones = _one_hot(_group(hi, A_HEAD) == _group(hj, A_HEAD))
    head_sum = lambda m: _mm_sel(m, head_ones)
    kk = kk_raw / jnp.maximum(jnp.sqrt(head_sum(kk_raw * kk_raw)), 1e-12)
    r_s[...] = r
    nkk_s[...] = -kk
    beta_s[...] = kk * a
    dec_s[...] = jnp.exp(log_decay)
    k_s[...] = k_mod
    v_s[...] = v
    pair_ones = head_ones[:LANES, :LANES]
    diag = _iota((A_HEAD, LANES), 0) == (_iota((A_HEAD, LANES), 1) & (A_HEAD - 1))

    def tile(j, carry):
        r0 = pl.multiple_of(j * SUBLANES, SUBLANES)
        ld = lambda ref: ref[pl.ds(r0, SUBLANES), :]
        r_t, nkk_t, beta_t, dec_t, k_t, v_t = ld(r_s), ld(nkk_s), ld(beta_s), ld(dec_s), ld(k_s), ld(v_s)
        out_rows = [[None] * A_PAIRS for _ in range(SUBLANES)]

        def seq_pair_chain(i, p):
            sl = slice(p * LANES, (p + 1) * LANES)
            rowof = lambda m, rr: m[rr:rr + 1, sl]
            v_diag = jnp.concatenate([jnp.where(diag, rowof(v_t, i * steps + s), 0.0) for s in range(steps)],
                                     axis=0)
            v_cols = _mm_sel(v_diag, pair_ones, pieces=2)
            S = s_ref[j * per_tile + i, p]
            for s in range(steps):
                rr = i * steps + s
                sa = _mm_sel(S * rowof(nkk_t, rr), pair_ones, pieces=2)
                yield
                S = (S * rowof(dec_t, rr) + sa * rowof(beta_t, rr)
                     + v_cols[s * A_HEAD:(s + 1) * A_HEAD] * rowof(k_t, rr))
                out = _mm_sel(S * rowof(r_t, rr), pair_ones, pieces=2)
                out_rows[rr][p] = jnp.sum(jnp.where(diag, out, 0.0), axis=0, keepdims=True)
            sout_ref[j * per_tile + i, p] = S

        _round_robin(seq_pair_chain(i, p) for i in range(per_tile) for p in range(A_PAIRS))
        o_s[pl.ds(r0, SUBLANES), :] = _rows_to_tile([jnp.concatenate(parts, axis=1) for parts in out_rows])
        return carry

    lax.fori_loop(0, rows // SUBLANES, tile, 0)
    o = o_s[...]
    mean = head_sum(o) * (1.0 / A_HEAD)
    cen = o - mean
    var = head_sum(cen * cen) * (1.0 / A_HEAD)
    o = cen * lax.rsqrt(var + A_LNX_EPS) * lnw_ref[...] + lnb_ref[...]
    o_ref[...] = (o + head_sum(r * k_mod * rk_ref[...]) * v) * g


def _rwkv_sample(pa, prev, state_pairs, steps, params, seqs):
    n = pa.shape[0]
    nb = state_pairs.shape[0]
    rows = seqs * steps
    assert SUBLANES % steps == 0 and rows % SUBLANES == 0
    sspec = pl.BlockSpec((seqs, A_PAIRS, A_HEAD, LANES), lambda i: (i, 0, 0, 0))
    return pl.pallas_call(
        functools.partial(_rwkv_sample_body, steps=steps),
        grid=(nb // seqs,),
        in_specs=[pl.BlockSpec((rows, A_PROJ), lambda i: (i, 0)), pl.BlockSpec((rows, A_PROJ), lambda i: (i, 0)),
                  sspec] + [_const_spec(p.shape) for p in params],
        out_specs=[pl.BlockSpec((rows, A_WIDTH), lambda i: (i, 0)), sspec],
        out_shape=[jax.ShapeDtypeStruct((n, A_WIDTH), F32),
                   jax.ShapeDtypeStruct(state_pairs.shape, F32)],
        scratch_shapes=[pltpu.VMEM((rows, A_WIDTH), F32) for _ in range(7)],
        compiler_params=_cparams(1),
        name="rwkv_sample",
    )(pa, prev, state_pairs, *params)


def _gdn_token_math(conv, ab, alog, dtb):
    c = _silu(conv)
    qs, ks = [], []
    for h in range(B_HEADS):
        q = c[:, h * B_HEAD:(h + 1) * B_HEAD]
        k = c[:, B_WIDTH + h * B_HEAD:B_WIDTH + (h + 1) * B_HEAD]
        qs.append(q * lax.rsqrt(jnp.sum(q * q, axis=-1, keepdims=True) + 1e-6) * (B_HEAD ** -0.5))
        ks.append(k * lax.rsqrt(jnp.sum(k * k, axis=-1, keepdims=True) + 1e-6))
    q = jnp.concatenate(qs, axis=1)
    k = jnp.concatenate(ks, axis=1)
    v = c[:, 2 * B_WIDTH:]
    lane = _iota((1, LANES), 1)
    g = -jnp.exp(alog) * _softplus(ab + dtb)
    beta = _sigmoid(ab)
    gb = jnp.where(lane < B_HEADS, g, beta)
    si, sj = _iota((LANES, 2 * B_WIDTH), 0), _iota((LANES, 2 * B_WIDTH), 1)
    spread = _mm_sel(gb, _one_hot(si == _group(sj, B_HEAD)))
    return q, k, v, spread[:, :B_WIDTH], spread[:, B_WIDTH:]


def _gdn_out(o, norm_w, z):
    return o * lax.rsqrt(jnp.mean(o * o, axis=-1, keepdims=True) + RMS_EPS) * norm_w * _silu(z)


def _gdn_prompt_part(qkv_ref, ab_ref, z_ref, cw_ref, alog_ref, dtb_ref, nw_ref, o_ref,
                     carry_ref, state_ref, q_s, k_s, v_s, gc_s, beta_s):
    tt = qkv_ref.shape[0]
    C = CHUNK
    x = qkv_ref[...]
    row8 = _iota((SUBLANES, 1), 0)
    conv = x * cw_ref[CONV_K - 1:CONV_K, :]
    for i in range(1, CONV_K):
        xs = pltpu.roll(x, i, axis=0)
        top = jnp.where(row8 < i, pltpu.roll(carry_ref[...], i, axis=0), xs[:SUBLANES])
        xs = jnp.concatenate([top, xs[SUBLANES:]], axis=0)
        conv = conv + xs * cw_ref[CONV_K - 1 - i:CONV_K - i, :]
    carry_ref[...] = x[tt - SUBLANES:tt, :]
    q, k, v, g, beta = _gdn_token_math(conv, ab_ref[...], alog_ref[...], dtb_ref[...])
    ri, ci = _iota((tt, tt), 0), _iota((tt, tt), 1)
    q_s[...] = q
    k_s[...] = k
    v_s[...] = v
    beta_s[...] = beta
    gc_s[...] = _sel_mm(_one_hot((_group(ri, C) == _group(ci, C)) & (ci <= ri)), g)

    i2, j2 = _iota((2 * C, 2 * C), 0), _iota((2 * C, 2 * C), 1)
    same_head = _group(i2, C) == _group(j2, C)
    strict = same_head & (i2 > j2)
    incl = same_head & (i2 >= j2)
    eye = jnp.where(i2 == j2, 1.0, 0.0)
    first = _iota((2 * C, 1), 0) < C

    def pair_chain(pr, r0):
        heads = (2 * pr, 2 * pr + 1)
        sls = [slice(h * B_HEAD, (h + 1) * B_HEAD) for h in heads]
        ld = lambda ref: jnp.concatenate([ref[pl.ds(r0, C), sl] for sl in sls], axis=0)
        q_h, k_h, v_h, gc_h, beta_h = ld(q_s), ld(k_s), ld(v_s), ld(gc_s), ld(beta_s)
        diff = gc_h - gc_h.T
        dm = jnp.where(incl, jnp.exp(jnp.where(incl, diff, 0.0)), 0.0)
        kb = k_h * beta_h
        QK = _mm_nt(_split(jnp.concatenate([kb, q_h], axis=0)), _split(k_h))
        yield
        N = -jnp.where(strict, QK[:2 * C] * dm, 0.0)
        qk = QK[2 * C:] * dm
        egc = jnp.exp(gc_h)
        X = jnp.concatenate([v_h * beta_h, kb * egc], axis=1)
        T = yield from _nilpotent_inverse(N, eye)
        UW = _mm_f32(T, X)
        yield
        qd = (q_h * egc).astype(BF16)
        g_last = jnp.where(first, gc_h[C - 1:C, :], gc_h[2 * C - 1:2 * C, :])
        k_dec_t = (k_h * jnp.exp(g_last - gc_h)).T
        sw = _split(UW[:, B_HEAD:])
        wS, qS, S_old = [], [], []
        for j, h in enumerate(heads):
            S = state_ref[h]
            sS = _split(S)
            rows = slice(j * C, (j + 1) * C)
            wS.append(_mm((sw[0][rows], sw[1][rows]), sS))
            qS.append(_dot(qd[rows], sS[0]))
            S_old.append(S)
        yield
        v_new = UW[:, :B_HEAD] - jnp.concatenate(wS, axis=0)
        o = jnp.concatenate(qS, axis=0) + _dot(qk.astype(BF16), v_new.astype(BF16))
        sk = _split(k_dec_t)
        for j, h in enumerate(heads):
            mine = first if j == 0 else jnp.logical_not(first)
            rows = slice(j * C, (j + 1) * C)
            decay = jnp.exp(gc_h[(j + 1) * C - 1:(j + 1) * C, :])
            state_ref[h] = S_old[j] * decay + _mm(sk, _split(jnp.where(mine, v_new, 0.0)))
            o_ref[pl.ds(r0, C), sls[j]] = _gdn_out(o[rows], nw_ref[...], z_ref[pl.ds(r0, C), sls[j]])

    return lambda r0: [pair_chain(pr, r0) for pr in range(B_HEADS // 2)]


def _gdn_sample_body(x0_ref, x1_ref, x2_ref, x3_ref, ab_ref, z_ref, s_ref, cw_ref, alog_ref, dtb_ref, nw_ref,
                     o_ref, sout_ref, q_s, k_s, v_s, eg_s, beta_s, o_s, *, steps):
    rows = ab_ref.shape[0]
    per_tile = SUBLANES // steps
    conv = (x0_ref[...] * cw_ref[0:1, :] + x1_ref[...] * cw_ref[1:2, :]
            + x2_ref[...] * cw_ref[2:3, :] + x3_ref[...] * cw_ref[3:4, :])
    q, k, v, g, beta = _gdn_token_math(conv, ab_ref[...], alog_ref[...], dtb_ref[...])
    q_s[...] = q
    k_s[...] = k
    v_s[...] = v
    eg_s[...] = jnp.exp(g)
    beta_s[...] = beta
    diag = _iota((B_HEAD, B_HEAD), 0) == _iota((B_HEAD, B_HEAD), 1)
    ones = jnp.ones((B_HEAD, B_HEAD), BF16)

    def to_cols(tile_rows):
        diags = jnp.concatenate([jnp.where(diag, rv, 0.0) for rv in tile_rows], axis=0)
        return _mm_sel(diags, ones, pieces=2)

    def tile(j, carry):
        r0 = pl.multiple_of(j * SUBLANES, SUBLANES)
        ld = lambda ref: ref[pl.ds(r0, SUBLANES), :]
        q_t, k_t, v_t, eg_t, beta_t = ld(q_s), ld(k_s), ld(v_s), ld(eg_s), ld(beta_s)
        out_rows = [[] for _ in range(SUBLANES)]
        for i in range(per_tile):
            for h in range(B_HEADS):
                sl = slice(h * B_HEAD, (h + 1) * B_HEAD)
                rowof = lambda m, rr: m[rr:rr + 1, sl]
                rrs = [i * steps + s for s in range(steps)]
                kq_cols = to_cols([rowof(k_t, rr) for rr in rrs] + [rowof(q_t, rr) for rr in rrs])
                S = s_ref[j * per_tile + i, h]
                for s, rr in enumerate(rrs):
                    k_col = kq_cols[s * B_HEAD:(s + 1) * B_HEAD]
                    q_col = kq_cols[(steps + s) * B_HEAD:(steps + s + 1) * B_HEAD]
                    S = S * rowof(eg_t, rr)
                    u = jnp.sum(k_col * S, axis=0, keepdims=True)
                    S = S + k_col * (rowof(beta_t, rr) * (rowof(v_t, rr) - u))
                    out_rows[rr].append(jnp.sum(q_col * S, axis=0, keepdims=True))
                sout_ref[j * per_tile + i, h] = S
        o_s[pl.ds(r0, SUBLANES), :] = _rows_to_tile([jnp.concatenate(parts, axis=1) for parts in out_rows])
        return carry

    lax.fori_loop(0, rows // SUBLANES, tile, 0)
    for h in range(B_HEADS):
        sl = slice(h * B_HEAD, (h + 1) * B_HEAD)
        o_ref[:, sl] = _gdn_out(o_s[:, sl], nw_ref[...], z_ref[:, sl])


def _gdn_sample(xs, ab, z, state, steps, params, seqs):
    n = ab.shape[0]
    nb = state.shape[0]
    rows = seqs * steps
    assert SUBLANES % steps == 0 and rows % SUBLANES == 0
    row = lambda w: pl.BlockSpec((rows, w), lambda i: (i, 0))
    sspec = pl.BlockSpec((seqs, B_HEADS, B_HEAD, B_HEAD), lambda i: (i, 0, 0, 0))
    return pl.pallas_call(
        functools.partial(_gdn_sample_body, steps=steps),
        grid=(nb // seqs,),
        in_specs=[row(CONV_CH)] * CONV_K + [row(LANES), row(B_WIDTH), sspec] + [_const_spec(p.shape) for p in params],
        out_specs=[row(B_WIDTH), sspec],
        out_shape=[jax.ShapeDtypeStruct((n, B_WIDTH), F32), jax.ShapeDtypeStruct(state.shape, F32)],
        scratch_shapes=[pltpu.VMEM((rows, B_WIDTH), F32) for _ in range(6)],
        compiler_params=_cparams(1),
        name="gdn_sample",
    )(*xs, ab, z, state, *params)


def _pa_perm():
    import numpy as np
    r = np.arange(0, A_WIDTH)
    wd = np.arange(A_WIDTH, A_WIDTH + A_RANK_W)
    k = np.arange(A_WIDTH + A_RANK_W, 2 * A_WIDTH + A_RANK_W)
    v = np.arange(2 * A_WIDTH + A_RANK_W, 3 * A_WIDTH + A_RANK_W)
    ad = np.arange(3 * A_WIDTH + A_RANK_W, 3 * A_WIDTH + A_RANK_W + A_RANK_A)
    gd = np.arange(3 * A_WIDTH + A_RANK_W + A_RANK_A, A_PROJ)
    perm = np.concatenate([r, k, v, wd, ad, gd])
    inv = np.argsort(perm)
    return perm, inv


def _token_tile(n, want):
    tm = want
    while n % tm:
        tm //= 2
    return tm


def kernel(x_prompt, x_sample, state_rwkv, state_rwkv_shift, state_delta, state_conv, ffn1_norm, ffn1_w_gate, ffn1_w_up, ffn1_w_down, mix_norm, w_in, rwkv_mu, rwkv_w0, rwkv_w2, rwkv_a0, rwkv_a2, rwkv_g2, rwkv_k_k, rwkv_k_a, rwkv_r_k, rwkv_lnx_w, rwkv_lnx_b, gdn_conv_w, gdn_A_log, gdn_dt_bias, gdn_norm_w, proj_a, proj_b, w_out, ffn2_norm, ffn2_w_gate, ffn2_w_up, ffn2_w_down, final_norm):
    depth = ffn1_norm.shape[0]
    assert depth == 1, "single-layer trunk"
    Bp, Tp, _ = x_prompt.shape
    Bs, Ts, _ = x_sample.shape
    perm, inv = _pa_perm()
    l = 0
    row = lambda a: a.reshape(1, -1).astype(F32)

    wi = w_in[l]
    o_b = A_PROJ
    w_all = jnp.concatenate([
        wi[:, :A_PROJ][:, perm],
        wi[:, o_b:o_b + CONV_CH],
        wi[:, o_b + CONV_CH + 2 * B_HEADS:o_b + B_PROJ],
        wi[:, o_b + B_PROJ:],
        jnp.pad(wi[:, o_b + CONV_CH:o_b + CONV_CH + 2 * B_HEADS], ((0, 0), (0, LANES - 2 * B_HEADS))),
    ], axis=1).astype(BF16)
    ffn1 = (row(ffn1_norm[l]), ffn1_w_gate[l].astype(BF16), ffn1_w_up[l].astype(BF16), ffn1_w_down[l].astype(BF16))
    ffn2 = (row(ffn2_norm[l]), ffn2_w_gate[l].astype(BF16), ffn2_w_up[l].astype(BF16), ffn2_w_down[l].astype(BF16))
    merge_w = (proj_a[l].astype(BF16), proj_b[l].astype(BF16), w_out[l].astype(BF16))
    zw = jnp.zeros((A_RANK_W, A_WIDTH), F32)
    w2a = jnp.concatenate([jnp.concatenate([rwkv_w2[l], zw], axis=1),
                           jnp.concatenate([zw, rwkv_a2[l]], axis=1)], axis=0)
    rwkv_params = (row(rwkv_mu[l][perm]), row(rwkv_w0[l]), row(rwkv_a0[l]), row(rwkv_k_k[l]), row(rwkv_k_a[l]),
                   row(rwkv_r_k[l]), row(rwkv_lnx_w[l]), row(rwkv_lnx_b[l]), w2a, rwkv_g2[l].astype(F32))
    pad_lane = lambda a: jnp.pad(a.reshape(1, -1).astype(F32), ((0, 0), (0, LANES - a.size)))
    gdn_params = (gdn_conv_w[l].astype(F32), pad_lane(gdn_A_log[l]), pad_lane(gdn_dt_bias[l]), row(gdn_norm_w[l]))

    def trunk_front(x2):
        n = x2.shape[0]
        h = _ffn(x2, *ffn1, tm=_token_tile(n, 512))
        return (h,) + tuple(_proj(h, row(mix_norm[l]), w_all, tm=_token_tile(n, 256)))

    def trunk_back(h, oa, ob, gates):
        n = h.shape[0]
        return _tail(h, oa, ob, gates, *merge_w, *ffn2, row(final_norm), tm=_token_tile(n, 256))

    xp = x_prompt.reshape(Bp * Tp, D_MODEL)
    h, pa, qkv, z, gates, ab = trunk_front(xp)
    tt = _token_tile(Tp, 256)
    oa, ob, s_pairs, delta_p = _mix_prompt(pa, qkv, ab, z, Bp, Tp, rwkv_params, gdn_params, tt)
    y_prompt = trunk_back(h, oa, ob, gates).reshape(Bp, Tp, D_MODEL)
    sp = s_pairs.reshape(Bp, A_PAIRS, 2, A_HEAD, 2, A_HEAD)
    rwkv_p = jnp.stack([sp[:, :, 0, :, 0], sp[:, :, 1, :, 1]], axis=2).reshape(Bp, A_HEADS, A_HEAD, A_HEAD)
    shift_p = pa.reshape(Bp, Tp, A_PROJ)[:, -1][:, inv]
    conv_p = qkv.reshape(Bp, Tp, CONV_CH)[:, Tp - (CONV_K - 1):]

    xs = x_sample.reshape(Bs * Ts, D_MODEL)
    h, pa, qkv, z, gates, ab = trunk_front(xs)
    pa3 = pa.reshape(Bs, Ts, A_PROJ)
    prev = jnp.concatenate([state_rwkv_shift[l][:, None, perm].astype(F32), pa3[:, :-1]], axis=1)
    s_in = state_rwkv[l].astype(F32).reshape(Bs, A_PAIRS, 2, A_HEAD, A_HEAD)
    s_in = jnp.transpose(s_in, (0, 1, 3, 2, 4)).reshape(Bs, A_PAIRS, A_HEAD, LANES)
    oa, s_out = _rwkv_sample(pa, prev.reshape(Bs * Ts, A_PROJ), s_in, Ts, rwkv_params, seqs=_token_tile(Bs, 32))
    s_out = jnp.transpose(s_out.reshape(Bs, A_PAIRS, A_HEAD, 2, A_HEAD), (0, 1, 3, 2, 4))
    rwkv_s = s_out.reshape(Bs, A_HEADS, A_HEAD, A_HEAD)
    shift_s = pa3[:, -1][:, inv]
    xpad = jnp.concatenate([state_conv[l].astype(F32), qkv.reshape(Bs, Ts, CONV_CH)], axis=1)
    shifted = [xpad[:, i:i + Ts].reshape(Bs * Ts, CONV_CH) for i in range(CONV_K)]
    ob, delta_s = _gdn_sample(shifted, ab, z, state_delta[l].astype(F32), Ts, gdn_params, seqs=_token_tile(Bs, 16))
    conv_s = xpad[:, Ts:]
    y_sample = trunk_back(h, oa, ob, gates).reshape(Bs, Ts, D_MODEL)

    add_depth = lambda a: a[None]
    return (y_prompt, y_sample,
            add_depth(rwkv_p), add_depth(shift_p), add_depth(delta_p), add_depth(conv_p),
            add_depth(rwkv_s), add_depth(shift_s), add_depth(delta_s), add_depth(conv_s))
```

```python
import functools

import jax
import jax.numpy as jnp
from jax import lax
from jax.experimental import pallas as pl
from jax.experimental.pallas import tpu as pltpu

F32 = jnp.float32
BF16 = jnp.bfloat16

D_MODEL = 1024
D_FF = 2816
RMS_EPS = 1e-6
A_HEAD = 64
A_HEADS = 8
A_WIDTH = A_HEADS * A_HEAD
A_RANK_W = 64
A_RANK_A = 64
A_RANK_G = 128
A_PROJ = 3 * A_WIDTH + A_RANK_W + A_RANK_A + A_RANK_G
A_LNX_EPS = 64e-5
A_PAIRS = A_HEADS // 2
B_HEADS = 4
B_HEAD = 128
B_WIDTH = B_HEADS * B_HEAD
CONV_K = 4
CONV_CH = 3 * B_WIDTH
B_PROJ = CONV_CH + 2 * B_HEADS + B_WIDTH
GATE_COLS = 2 * D_MODEL
LANES = 128
SUBLANES = 8
VMEM_LIMIT_BYTES = 56 * 1024 * 1024
CHUNK = 64
PA_R, PA_K, PA_V, PA_WA, PA_G = 0, A_WIDTH, 2 * A_WIDTH, 3 * A_WIDTH, 3 * A_WIDTH + A_RANK_W + A_RANK_A
PROJ_SPLITS = (A_PROJ, CONV_CH, B_WIDTH, GATE_COLS, LANES)


def _cparams(n_grid_dims):
    return pltpu.CompilerParams(dimension_semantics=("arbitrary",) * n_grid_dims,
                                vmem_limit_bytes=VMEM_LIMIT_BYTES)


def _const_spec(shape):
    nd = len(shape)
    return pl.BlockSpec(shape, lambda *_: (0,) * nd, pipeline_mode=pl.Buffered(1))


def _dot(a, b):
    return jnp.dot(a, b, preferred_element_type=F32)


def _dot_nt(a, b):
    return lax.dot_general(a, b, (((1,), (1,)), ((), ())), preferred_element_type=F32)


def _split(x):
    hi = x.astype(BF16)
    lo = (x - hi.astype(F32)).astype(BF16)
    return hi, lo


def _split3(x):
    hi = x.astype(BF16)
    rest = x - hi.astype(F32)
    mid = rest.astype(BF16)
    lo = (rest - mid.astype(F32)).astype(BF16)
    return hi, mid, lo


def _mm(a, b):
    return _dot(a.astype(BF16), b.astype(BF16))


def _mm_nt(a, b):
    return _dot_nt(a.astype(BF16), b.astype(BF16))


def _sel_mm(sel, x):
    return _dot(jnp.concatenate([sel, sel, sel], axis=1), jnp.concatenate(_split3(x), axis=0))


def _mm_sel(x, sel, pieces=3):
    parts = _split3(x) if pieces == 3 else _split(x)
    return _dot(jnp.concatenate(parts, axis=1), jnp.concatenate([sel] * pieces, axis=0))


INV_BASE = 8


def _nilpotent_inverse(n, eye):
    width = n.shape[1]
    bi, bj = _iota(n.shape, 0), _iota(n.shape, 1)
    same = lambda size: _group(bi, size) == _group(bj, size)
    d = jnp.where(same(INV_BASE), n, 0.0)
    t = eye + d
    d = _mm(d, d)
    yield
    for _ in range(INV_BASE.bit_length() - 3):
        both = _mm(d, jnp.concatenate([t, d], axis=1))
        yield
        t = t + both[:, :width]
        d = both[:, width:]
    t = t + _mm(d, t)
    yield
    size = INV_BASE
    while size < CHUNK:
        coupling = jnp.where(same(2 * size) & jnp.logical_not(same(size)), n, 0.0)
        tb = t.astype(BF16)
        lt = _mm(coupling, tb)
        yield
        t = t + _mm(tb, lt)
        yield
        size *= 2
    return t


def _round_robin(chains):
    chains = list(chains)
    while chains:
        for chain in list(chains):
            try:
                next(chain)
            except StopIteration:
                chains.remove(chain)


def _rms(x, w):
    return x * lax.rsqrt(jnp.mean(x * x, axis=-1, keepdims=True) + RMS_EPS) * w


def _sigmoid(x):
    return 1.0 / (1.0 + jnp.exp(-x))


def _silu(x):
    return x * _sigmoid(x)


def _softplus(x):
    return jnp.maximum(x, 0.0) + jnp.log(1.0 + jnp.exp(-jnp.abs(x)))


def _iota(shape, dim):
    return lax.broadcasted_iota(jnp.int32, shape, dim)


def _group(idx, size):
    assert size & (size - 1) == 0
    return lax.shift_right_logical(idx, size.bit_length() - 1)


def _one_hot(cond):
    return jnp.where(cond, 1.0, 0.0).astype(BF16)


def _rows_to_tile(rows):
    rid = _iota((SUBLANES, 1), 0)
    tile = jnp.zeros((SUBLANES, rows[0].shape[1]), F32)
    for i, row in enumerate(rows):
        tile = jnp.where(rid == i, row, tile)
    return tile


def _swiglu_half_step(x, nw, wg_ref, wu_ref, wd_ref):
    xn = _rms(x, nw).astype(BF16)
    g = _dot(xn, wg_ref[...])
    u = _dot(xn, wu_ref[...])
    act = (_silu(g) * u).astype(BF16)
    return x + 0.5 * _dot(act, wd_ref[...])


def _ffn_body(x_ref, nw_ref, wg_ref, wu_ref, wd_ref, o_ref):
    o_ref[...] = _swiglu_half_step(x_ref[...], nw_ref[...], wg_ref, wu_ref, wd_ref)


def _ffn(x, nw, wg, wu, wd, tm):
    n = x.shape[0]
    return pl.pallas_call(
        _ffn_body,
        grid=(n // tm,),
        in_specs=[pl.BlockSpec((tm, D_MODEL), lambda i: (i, 0)),
                  _const_spec((1, D_MODEL)),
                  _const_spec((D_MODEL, D_FF)), _const_spec((D_MODEL, D_FF)), _const_spec((D_FF, D_MODEL))],
        out_specs=pl.BlockSpec((tm, D_MODEL), lambda i: (i, 0)),
        out_shape=jax.ShapeDtypeStruct((n, D_MODEL), F32),
        compiler_params=_cparams(1),
        name="ffn1",
    )(x, nw, wg, wu, wd)


def _proj_body(h_ref, nw_ref, w_ref, *o_refs):
    u = _rms(h_ref[...], nw_ref[...]).astype(BF16)
    off = 0
    for o_ref, width in zip(o_refs, PROJ_SPLITS):
        o_ref[...] = _dot(u, w_ref[:, off:off + width])
        off += width


def _proj(h, nw, w_all, tm):
    n = h.shape[0]
    cols = sum(PROJ_SPLITS)
    return pl.pallas_call(
        _proj_body,
        grid=(n // tm,),
        in_specs=[pl.BlockSpec((tm, D_MODEL), lambda i: (i, 0)),
                  _const_spec((1, D_MODEL)), _const_spec((D_MODEL, cols))],
        out_specs=[pl.BlockSpec((tm, w), lambda i: (i, 0)) for w in PROJ_SPLITS],
        out_shape=[jax.ShapeDtypeStruct((n, w), F32) for w in PROJ_SPLITS],
        compiler_params=_cparams(1),
        name="proj",
    )(h, nw, w_all)


def _tail_body(h_ref, oa_ref, ob_ref, gates_ref, pa_ref, pb_ref, wo_ref, nw_ref, wg_ref, wu_ref, wd_ref,
               fn_ref, o_ref):
    ma = _dot(oa_ref[...].astype(BF16), pa_ref[...])
    mb = _dot(ob_ref[...].astype(BF16), pb_ref[...])
    merged = _sigmoid(gates_ref[:, :D_MODEL]) * ma + _sigmoid(gates_ref[:, D_MODEL:]) * mb
    h = h_ref[...] + _dot(merged.astype(BF16), wo_ref[...])
    h = _swiglu_half_step(h, nw_ref[...], wg_ref, wu_ref, wd_ref)
    o_ref[...] = _rms(h, fn_ref[...])


def _tail(h, oa, ob, gates, proj_a, proj_b, w_out, nw, wg, wu, wd, fn, tm):
    n = h.shape[0]
    row = lambda w: pl.BlockSpec((tm, w), lambda i: (i, 0))
    return pl.pallas_call(
        _tail_body,
        grid=(n // tm,),
        in_specs=[row(D_MODEL), row(A_WIDTH), row(B_WIDTH), row(GATE_COLS),
                  _const_spec((A_WIDTH, D_MODEL)), _const_spec((B_WIDTH, D_MODEL)),
                  _const_spec((D_MODEL, D_MODEL)), _const_spec((1, D_MODEL)),
                  _const_spec((D_MODEL, D_FF)), _const_spec((D_MODEL, D_FF)), _const_spec((D_FF, D_MODEL)),
                  _const_spec((1, D_MODEL))],
        out_specs=row(D_MODEL),
        out_shape=jax.ShapeDtypeStruct((n, D_MODEL), F32),
        compiler_params=_cparams(1),
        name="tail",
    )(h, oa, ob, gates, proj_a, proj_b, w_out, nw, wg, wu, wd, fn)


def _rwkv_token_math(x, prev, mu, w0, a0, k_k, k_a, w2a, g2):
    pm = x + (prev - x) * mu
    r = pm[:, PA_R:PA_R + A_WIDTH]
    k = pm[:, PA_K:PA_K + A_WIDTH]
    v = pm[:, PA_V:PA_V + A_WIDTH]
    wa = pm[:, PA_WA:PA_WA + LANES]
    gd = pm[:, PA_G:PA_G + A_RANK_G]
    lane = _iota((1, LANES), 1)
    lora_in = jnp.where(lane < A_RANK_W, jnp.tanh(wa), wa)
    lora = _mm(lora_in, w2a)
    w_log = -_softplus(-(w0 + lora[:, :A_WIDTH])) - 0.5
    log_decay = -jnp.exp(w_log)
    a = _sigmoid(a0 + lora[:, A_WIDTH:])
    g = _mm(_sigmoid(gd), g2)
    kk_raw = k * k_k
    k_mod = k * (1.0 + (a - 1.0) * k_a)
    return r, k_mod, v, kk_raw, a, log_decay, g


def _pair_mask(rows_per_head):
    shape = (2 * rows_per_head, LANES)
    return _group(_iota(shape, 0), rows_per_head) == _group(_iota(shape, 1), A_HEAD)


def _rwkv_prompt_part(pa_ref, mu_ref, w0_ref, a0_ref, kk_ref, ka_ref, rk_ref, lnw_ref, lnb_ref, w2a_ref, g2_ref,
                      o_ref, carry_ref, state_ref, r_s, k_s, v_s, kkraw_s, a_s, cum_s, ld_s, g_s):
    tt = pa_ref.shape[0]
    C = CHUNK
    x = pa_ref[...]
    row = _iota((tt, 1), 0)
    prev = jnp.where(row == 0, carry_ref[SUBLANES - 1:SUBLANES, :], pltpu.roll(x, 1, axis=0))
    carry_ref[...] = x[tt - SUBLANES:tt, :]
    r, k_mod, v, kk_raw, a, log_decay, g = _rwkv_token_math(
        x, prev, mu_ref[...], w0_ref[...], a0_ref[...], kk_ref[...], ka_ref[...], w2a_ref[...], g2_ref[...])

    ri, ci = _iota((tt, tt), 0), _iota((tt, tt), 1)
    r_s[...] = r
    k_s[...] = k_mod
    v_s[...] = v
    kkraw_s[...] = kk_raw
    a_s[...] = a
    g_s[...] = g
    ld_s[...] = log_decay
    cum_s[...] = _sel_mm(_one_hot((_group(ri, C) == _group(ci, C)) & (ci <= ri)), log_decay)

    mask = _pair_mask(C)
    i2, j2 = _iota((2 * C, 2 * C), 0), _iota((2 * C, 2 * C), 1)
    strict = i2 > j2
    incl = i2 >= j2
    eye = jnp.where(i2 == j2, 1.0, 0.0)
    dup = lambda m: jnp.concatenate([m, m], axis=0)
    stack = lambda m: jnp.where(mask, dup(m), 0.0)

    def solve_chain(p, r0, stash):
        sl = slice(p * LANES, (p + 1) * LANES)
        ld = lambda ref: ref[pl.ds(r0, C), sl]
        r_p, k_p, v_p, a_p, cum, ldec = ld(r_s), ld(k_s), ld(v_s), ld(a_s), ld(cum_s), ld(ld_s)
        einc = jnp.exp(cum)
        eex = jnp.exp(cum - ldec)
        einv = jnp.exp(-cum)
        etail = jnp.exp(cum[C - 1:C, :] - cum)
        kks = stack(ld(kkraw_s))
        kks = kks * (1.0 / jnp.maximum(jnp.sqrt(jnp.sum(kks * kks, axis=-1, keepdims=True)), 1e-12))
        As = -kks * dup(eex)
        Bs = kks * dup(a_p * einv)
        Bh = kks * dup(a_p * etail)
        Ks = stack(k_p * einv)
        Kh = stack(k_p * etail)
        Rs = stack(r_p * einc)
        Vs = stack(v_p)
        AR = jnp.concatenate([As, Rs], axis=0).astype(BF16)
        Vb = Vs.astype(BF16)
        G = _mm_nt(AR, jnp.concatenate([Bs, Ks], axis=0))
        yield
        Aab = jnp.where(strict, G[:2 * C, :2 * C], 0.0)
        Aak = jnp.where(strict, G[:2 * C, 2 * C:], 0.0)
        Arb = jnp.where(incl, G[2 * C:, :2 * C], 0.0)
        Ark = jnp.where(incl, G[2 * C:, 2 * C:], 0.0)
        Y = _mm(Aak, Vb)
        yield
        T = yield from _nilpotent_inverse(Aab, eye)
        WU = _mm(T, jnp.concatenate([AR[:2 * C], Y.astype(BF16)], axis=1))
        yield
        bonus = jnp.sum(stack(r_p * k_p * rk_ref[:, sl]), axis=-1, keepdims=True) * Vs
        stash[p] = dict(WU=WU, R=AR[2 * C:], Vs=Vs, bonus=bonus,
                        Aro=jnp.concatenate([Arb, Ark], axis=1).astype(BF16),
                        BKh=jnp.concatenate([Bh, Kh], axis=0).astype(BF16), decay=einc[C - 1:C, :])

    def state_chain(p, r0, stash):
        sl = slice(p * LANES, (p + 1) * LANES)
        s = stash[p]
        S = state_ref[p]
        Sb = S.astype(BF16)
        W = _mm_nt(s["WU"][:, :LANES], Sb) + s["WU"][:, LANES:]
        yield
        WV = jnp.concatenate([W, s["Vs"]], axis=0)
        O = _dot_nt(s["R"], Sb) + _mm(s["Aro"], WV)
        state_ref[p] = S * s["decay"] + _mm(WV.T, s["BKh"])
        yield
        mean = jnp.sum(O, axis=-1, keepdims=True) * (1.0 / A_HEAD)
        cen = jnp.where(mask, O - mean, 0.0)
        var = jnp.sum(cen * cen, axis=-1, keepdims=True) * (1.0 / A_HEAD)
        normed = jnp.where(mask, cen * lax.rsqrt(var + A_LNX_EPS) * lnw_ref[:, sl] + lnb_ref[:, sl], 0.0)
        full = normed + s["bonus"]
        o_ref[pl.ds(r0, C), sl] = (full[:C] + full[C:]) * g_s[pl.ds(r0, C), sl]

    return (lambda r0, stash: [solve_chain(p, r0, stash) for p in range(A_PAIRS)],
            lambda r0, stash: [state_chain(p, r0, stash) for p in range(A_PAIRS)])


N_RWKV_PARAMS = 10
N_GDN_PARAMS = 4
N_RWKV_SCRATCH = 8
N_GDN_SCRATCH = 5


def _mix_prompt_body(pa_ref, qkv_ref, ab_ref, z_ref, *refs):
    refs = list(refs)
    take = lambda n: [refs.pop(0) for _ in range(n)]
    rwkv_prm, gdn_prm = take(N_RWKV_PARAMS), take(N_GDN_PARAMS)
    oa_ref, ob_ref, sfa_ref, sfb_ref = take(4)
    carry_a, state_a, carry_b, state_b = take(4)
    rwkv_scr, gdn_scr = take(N_RWKV_SCRATCH), take(N_GDN_SCRATCH)
    t = pl.program_id(1)
    tt = pa_ref.shape[0]

    @pl.when(t == 0)
    def _():
        for ref in (carry_a, state_a, carry_b, state_b):
            ref[...] = jnp.zeros_like(ref)

    rwkv_solve, rwkv_state = _rwkv_prompt_part(pa_ref, *rwkv_prm, oa_ref, carry_a, state_a, *rwkv_scr)
    gdn_solve, gdn_state = _gdn_prompt_part(qkv_ref, ab_ref, z_ref, *gdn_prm, ob_ref, carry_b, state_b, *gdn_scr)

    n_chunks = tt // CHUNK
    stashes = [({}, {}) for _ in range(n_chunks)]
    for c in range(n_chunks + 1):
        chains = []
        if c < n_chunks:
            chains += rwkv_solve(c * CHUNK, stashes[c][0]) + gdn_solve(c * CHUNK, stashes[c][1])
        if c > 0:
            chains += rwkv_state((c - 1) * CHUNK, stashes[c - 1][0]) + gdn_state((c - 1) * CHUNK, stashes[c - 1][1])
        _round_robin(chains)

    @pl.when(t == pl.num_programs(1) - 1)
    def _():
        sfa_ref[0] = state_a[...]
        sfb_ref[0] = state_b[...]


def _mix_prompt(pa, qkv, ab, z, B, T, rwkv_params, gdn_params, tt):
    n = pa.shape[0]
    nt = T // tt
    assert len(rwkv_params) == N_RWKV_PARAMS and len(gdn_params) == N_GDN_PARAMS
    rows = lambda w: pl.BlockSpec((tt, w), lambda b, t: (b * nt + t, 0))
    state = lambda: pl.BlockSpec((1, 4, LANES, LANES), lambda b, t: (b, 0, 0, 0))
    big = lambda: pltpu.VMEM((tt, A_WIDTH), F32)
    return pl.pallas_call(
        _mix_prompt_body,
        grid=(B, nt),
        in_specs=[rows(A_PROJ), rows(CONV_CH), rows(LANES), rows(B_WIDTH)]
                 + [_const_spec(p.shape) for p in rwkv_params + gdn_params],
        out_specs=[rows(A_WIDTH), rows(B_WIDTH), state(), state()],
        out_shape=[jax.ShapeDtypeStruct((n, A_WIDTH), F32), jax.ShapeDtypeStruct((n, B_WIDTH), F32),
                   jax.ShapeDtypeStruct((B, A_PAIRS, LANES, LANES), F32),
                   jax.ShapeDtypeStruct((B, B_HEADS, B_HEAD, B_HEAD), F32)],
        scratch_shapes=[pltpu.VMEM((SUBLANES, A_PROJ), F32), pltpu.VMEM((A_PAIRS, LANES, LANES), F32),
                        pltpu.VMEM((SUBLANES, CONV_CH), F32), pltpu.VMEM((B_HEADS, B_HEAD, B_HEAD), F32)]
                       + [big() for _ in range(N_RWKV_SCRATCH + N_GDN_SCRATCH)],
        compiler_params=_cparams(2),
        name="mix_prompt",
    )(pa, qkv, ab, z, *rwkv_params, *gdn_params)


def _rwkv_sample_body(pa_ref, prev_ref, s_ref, mu_ref, w0_ref, a0_ref, kk_ref, ka_ref, rk_ref, lnw_ref, lnb_ref,
                      w2a_ref, g2_ref, o_ref, sout_ref, r_s, nkk_s, beta_s, dec_s, k_s, v_s, o_s, *, steps):
    rows = pa_ref.shape[0]
    per_tile = SUBLANES // steps
    r, k_mod, v, kk_raw, a, log_decay, g = _rwkv_token_math(
        pa_ref[...], prev_ref[...], mu_ref[...], w0_ref[...], a0_ref[...], kk_ref[...], ka_ref[...],
        w2a_ref[...], g2_ref[...])
    hi, hj = _iota((A_WIDTH, A_WIDTH), 0), _iota((A_WIDTH, A_WIDTH), 1)
    head_ones = _one_hot(_group(hi, A_HEAD) == _group(hj, A_HEAD))
    head_sum = lambda m: _mm_sel(m, head_ones)
    kk = kk_raw / jnp.maximum(jnp.sqrt(head_sum(kk_raw * kk_raw)), 1e-12)
    r_s[...] = r
    nkk_s[...] = -kk
    beta_s[...] = kk * a
    dec_s[...] = jnp.exp(log_decay)
    k_s[...] = k_mod
    v_s[...] = v
    pair_ones = head_ones[:LANES, :LANES]
    diag = _iota((A_HEAD, LANES), 0) == (_iota((A_HEAD, LANES), 1) & (A_HEAD - 1))

    def tile(j, carry):
        r0 = pl.multiple_of(j * SUBLANES, SUBLANES)
        ld = lambda ref: ref[pl.ds(r0, SUBLANES), :]
        r_t, nkk_t, beta_t, dec_t, k_t, v_t = ld(r_s), ld(nkk_s), ld(beta_s), ld(dec_s), ld(k_s), ld(v_s)
        out_rows = [[None] * A_PAIRS for _ in range(SUBLANES)]

        def seq_pair_chain(i, p):
            sl = slice(p * LANES, (p + 1) * LANES)
            rowof = lambda m, rr: m[rr:rr + 1, sl]
            v_diag = jnp.concatenate([jnp.where(diag, rowof(v_t, i * steps + s), 0.0) for s in range(steps)],
                                     axis=0)
            v_cols = _mm_sel(v_diag, pair_ones, pieces=2)
            S = s_ref[j * per_tile + i, p]
            for s in range(steps):
                rr = i * steps + s
                sa = _mm_sel(S * rowof(nkk_t, rr), pair_ones, pieces=2)
                yield
                S = (S * rowof(dec_t, rr) + sa * rowof(beta_t, rr)
                     + v_cols[s * A_HEAD:(s + 1) * A_HEAD] * rowof(k_t, rr))
                out = _mm_sel(S * rowof(r_t, rr), pair_ones, pieces=2)
                out_rows[rr][p] = jnp.sum(jnp.where(diag, out, 0.0), axis=0, keepdims=True)
            sout_ref[j * per_tile + i, p] = S

        _round_robin(seq_pair_chain(i, p) for i in range(per_tile) for p in range(A_PAIRS))
        o_s[pl.ds(r0, SUBLANES), :] = _rows_to_tile([jnp.concatenate(parts, axis=1) for parts in out_rows])
        return carry

    lax.fori_loop(0, rows // SUBLANES, tile, 0)
    o = o_s[...]
    mean = head_sum(o) * (1.0 / A_HEAD)
    cen = o - mean
    var = head_sum(cen * cen) * (1.0 / A_HEAD)
    o = cen * lax.rsqrt(var + A_LNX_EPS) * lnw_ref[...] + lnb_ref[...]
    o_ref[...] = (o + head_sum(r * k_mod * rk_ref[...]) * v) * g


def _rwkv_sample(pa, prev, state_pairs, steps, params, seqs):
    n = pa.shape[0]
    nb = state_pairs.shape[0]
    rows = seqs * steps
    assert SUBLANES % steps == 0 and rows % SUBLANES == 0
    sspec = pl.BlockSpec((seqs, A_PAIRS, A_HEAD, LANES), lambda i: (i, 0, 0, 0))
    return pl.pallas_call(
        functools.partial(_rwkv_sample_body, steps=steps),
        grid=(nb // seqs,),
        in_specs=[pl.BlockSpec((rows, A_PROJ), lambda i: (i, 0)), pl.BlockSpec((rows, A_PROJ), lambda i: (i, 0)),
                  sspec] + [_const_spec(p.shape) for p in params],
        out_specs=[pl.BlockSpec((rows, A_WIDTH), lambda i: (i, 0)), sspec],
        out_shape=[jax.ShapeDtypeStruct((n, A_WIDTH), F32),
                   jax.ShapeDtypeStruct(state_pairs.shape, F32)],
        scratch_shapes=[pltpu.VMEM((rows, A_WIDTH), F32) for _ in range(7)],
        compiler_params=_cparams(1),
        name="rwkv_sample",
    )(pa, prev, state_pairs, *params)


def _gdn_token_math(conv, ab, alog, dtb):
    c = _silu(conv)
    qs, ks = [], []
    for h in range(B_HEADS):
        q = c[:, h * B_HEAD:(h + 1) * B_HEAD]
        k = c[:, B_WIDTH + h * B_HEAD:B_WIDTH + (h + 1) * B_HEAD]
        qs.append(q * (lax.rsqrt(jnp.sum(q * q, axis=-1, keepdims=True) + 1e-6) * (B_HEAD ** -0.5)))
        ks.append(k * lax.rsqrt(jnp.sum(k * k, axis=-1, keepdims=True) + 1e-6))
    q = jnp.concatenate(qs, axis=1)
    k = jnp.concatenate(ks, axis=1)
    v = c[:, 2 * B_WIDTH:]
    lane = _iota((1, LANES), 1)
    g = -jnp.exp(alog) * _softplus(ab + dtb)
    beta = _sigmoid(ab)
    gb = jnp.where(lane < B_HEADS, g, beta)
    si, sj = _iota((LANES, 2 * B_WIDTH), 0), _iota((LANES, 2 * B_WIDTH), 1)
    spread = _mm_sel(gb, _one_hot(si == _group(sj, B_HEAD)))
    return q, k, v, spread[:, :B_WIDTH], spread[:, B_WIDTH:]


def _gdn_out(o, norm_w, z):
    return o * lax.rsqrt(jnp.mean(o * o, axis=-1, keepdims=True) + RMS_EPS) * norm_w * _silu(z)


def _gdn_prompt_part(qkv_ref, ab_ref, z_ref, cw_ref, alog_ref, dtb_ref, nw_ref, o_ref,
                     carry_ref, state_ref, q_s, k_s, v_s, gc_s, beta_s):
    tt = qkv_ref.shape[0]
    C = CHUNK
    x = qkv_ref[...]
    row8 = _iota((SUBLANES, 1), 0)
    conv = x * cw_ref[CONV_K - 1:CONV_K, :]
    for i in range(1, CONV_K):
        xs = pltpu.roll(x, i, axis=0)
        top = jnp.where(row8 < i, pltpu.roll(carry_ref[...], i, axis=0), xs[:SUBLANES])
        xs = jnp.concatenate([top, xs[SUBLANES:]], axis=0)
        conv = conv + xs * cw_ref[CONV_K - 1 - i:CONV_K - i, :]
    carry_ref[...] = x[tt - SUBLANES:tt, :]
    q, k, v, g, beta = _gdn_token_math(conv, ab_ref[...], alog_ref[...], dtb_ref[...])
    ri, ci = _iota((tt, tt), 0), _iota((tt, tt), 1)
    q_s[...] = q
    k_s[...] = k
    v_s[...] = v
    beta_s[...] = beta
    gc_s[...] = _sel_mm(_one_hot((_group(ri, C) == _group(ci, C)) & (ci <= ri)), g)

    i2, j2 = _iota((2 * C, 2 * C), 0), _iota((2 * C, 2 * C), 1)
    same_head = _group(i2, C) == _group(j2, C)
    strict = same_head & (i2 > j2)
    incl = same_head & (i2 >= j2)
    eye = jnp.where(i2 == j2, 1.0, 0.0)
    first = _iota((2 * C, 1), 0) < C

    def solve_chain(pr, r0, stash):
        sls = [slice(h * B_HEAD, (h + 1) * B_HEAD) for h in (2 * pr, 2 * pr + 1)]
        ld = lambda ref: jnp.concatenate([ref[pl.ds(r0, C), sl] for sl in sls], axis=0)
        q_h, k_h, v_h, gc_h, beta_h = ld(q_s), ld(k_s), ld(v_s), ld(gc_s), ld(beta_s)
        diff = gc_h - gc_h.T
        dm = jnp.where(incl, jnp.exp(jnp.where(incl, diff, 0.0)), 0.0)
        kb = k_h * beta_h
        QK = _mm_nt(jnp.concatenate([kb, q_h], axis=0), k_h)
        yield
        N = -jnp.where(strict, QK[:2 * C] * dm, 0.0)
        qk = QK[2 * C:] * dm
        egc = jnp.exp(gc_h)
        X = jnp.concatenate([v_h * beta_h, kb * egc], axis=1)
        T = yield from _nilpotent_inverse(N, eye)
        UW = _mm(T, X)
        yield
        g_last = jnp.where(first, gc_h[C - 1:C, :], gc_h[2 * C - 1:2 * C, :])
        stash[pr] = dict(u=UW[:, :B_HEAD], w=UW[:, B_HEAD:].astype(BF16), qd=(q_h * egc).astype(BF16),
                         qk=qk.astype(BF16), k_dec_t=(k_h * jnp.exp(g_last - gc_h)).T.astype(BF16),
                         decay=[jnp.exp(gc_h[(j + 1) * C - 1:(j + 1) * C, :]) for j in range(2)])

    def state_chain(pr, r0, stash):
        heads = (2 * pr, 2 * pr + 1)
        sls = [slice(h * B_HEAD, (h + 1) * B_HEAD) for h in heads]
        s = stash[pr]
        wS, qS, S_old = [], [], []
        for j, h in enumerate(heads):
            S = state_ref[h]
            rows = slice(j * C, (j + 1) * C)
            wq = _dot(jnp.concatenate([s["w"][rows], s["qd"][rows]], axis=0), S.astype(BF16))
            wS.append(wq[:C])
            qS.append(wq[C:])
            S_old.append(S)
        yield
        v_new = s["u"] - jnp.concatenate(wS, axis=0)
        o = jnp.concatenate(qS, axis=0) + _dot(s["qk"], v_new.astype(BF16))
        for j, h in enumerate(heads):
            mine = first if j == 0 else jnp.logical_not(first)
            rows = slice(j * C, (j + 1) * C)
            state_ref[h] = S_old[j] * s["decay"][j] + _mm(s["k_dec_t"], jnp.where(mine, v_new, 0.0))
            o_ref[pl.ds(r0, C), sls[j]] = _gdn_out(o[rows], nw_ref[...], z_ref[pl.ds(r0, C), sls[j]])
        yield

    return (lambda r0, stash: [solve_chain(pr, r0, stash) for pr in range(B_HEADS // 2)],
            lambda r0, stash: [state_chain(pr, r0, stash) for pr in range(B_HEADS // 2)])


def _gdn_sample_body(x0_ref, x1_ref, x2_ref, x3_ref, ab_ref, z_ref, s_ref, cw_ref, alog_ref, dtb_ref, nw_ref,
                     o_ref, sout_ref, q_s, k_s, v_s, eg_s, beta_s, o_s, *, steps):
    rows = ab_ref.shape[0]
    per_tile = SUBLANES // steps
    conv = (x0_ref[...] * cw_ref[0:1, :] + x1_ref[...] * cw_ref[1:2, :]
            + x2_ref[...] * cw_ref[2:3, :] + x3_ref[...] * cw_ref[3:4, :])
    q, k, v, g, beta = _gdn_token_math(conv, ab_ref[...], alog_ref[...], dtb_ref[...])
    q_s[...] = q
    k_s[...] = k
    v_s[...] = v
    eg_s[...] = jnp.exp(g)
    beta_s[...] = beta
    diag = _iota((B_HEAD, B_HEAD), 0) == _iota((B_HEAD, B_HEAD), 1)
    ones = jnp.ones((B_HEAD, B_HEAD), BF16)

    def to_cols(tile_rows):
        diags = jnp.concatenate([jnp.where(diag, rv, 0.0) for rv in tile_rows], axis=0)
        return _mm_sel(diags, ones, pieces=2)

    def tile(j, carry):
        r0 = pl.multiple_of(j * SUBLANES, SUBLANES)
        ld = lambda ref: ref[pl.ds(r0, SUBLANES), :]
        q_t, k_t, v_t, eg_t, beta_t = ld(q_s), ld(k_s), ld(v_s), ld(eg_s), ld(beta_s)
        out_rows = [[] for _ in range(SUBLANES)]
        for i in range(per_tile):
            for h in range(B_HEADS):
                sl = slice(h * B_HEAD, (h + 1) * B_HEAD)
                rowof = lambda m, rr: m[rr:rr + 1, sl]
                rrs = [i * steps + s for s in range(steps)]
                kq_cols = to_cols([rowof(k_t, rr) for rr in rrs] + [rowof(q_t, rr) for rr in rrs])
                S = s_ref[j * per_tile + i, h]
                for s, rr in enumerate(rrs):
                    k_col = kq_cols[s * B_HEAD:(s + 1) * B_HEAD]
                    q_col = kq_cols[(steps + s) * B_HEAD:(steps + s + 1) * B_HEAD]
                    S = S * rowof(eg_t, rr)
                    u = jnp.sum(k_col * S, axis=0, keepdims=True)
                    S = S + k_col * (rowof(beta_t, rr) * (rowof(v_t, rr) - u))
                    out_rows[rr].append(jnp.sum(q_col * S, axis=0, keepdims=True))
                sout_ref[j * per_tile + i, h] = S
        o_s[pl.ds(r0, SUBLANES), :] = _rows_to_tile([jnp.concatenate(parts, axis=1) for parts in out_rows])
        return carry

    lax.fori_loop(0, rows // SUBLANES, tile, 0)
    for h in range(B_HEADS):
        sl = slice(h * B_HEAD, (h + 1) * B_HEAD)
        o_ref[:, sl] = _gdn_out(o_s[:, sl], nw_ref[...], z_ref[:, sl])


def _gdn_sample(xs, ab, z, state, steps, params, seqs):
    n = ab.shape[0]
    nb = state.shape[0]
    rows = seqs * steps
    assert SUBLANES % steps == 0 and rows % SUBLANES == 0
    row = lambda w: pl.BlockSpec((rows, w), lambda i: (i, 0))
    sspec = pl.BlockSpec((seqs, B_HEADS, B_HEAD, B_HEAD), lambda i: (i, 0, 0, 0))
    return pl.pallas_call(
        functools.partial(_gdn_sample_body, steps=steps),
        grid=(nb // seqs,),
        in_specs=[row(CONV_CH)] * CONV_K + [row(LANES), row(B_WIDTH), sspec] + [_const_spec(p.shape) for p in params],
        out_specs=[row(B_WIDTH), sspec],
        out_shape=[jax.ShapeDtypeStruct((n, B_WIDTH), F32), jax.ShapeDtypeStruct(state.shape, F32)],
        scratch_shapes=[pltpu.VMEM((rows, B_WIDTH), F32) for _ in range(6)],
        compiler_params=_cparams(1),
        name="gdn_sample",
    )(*xs, ab, z, state, *params)


def _pa_perm():
    import numpy as np
    r = np.arange(0, A_WIDTH)
    wd = np.arange(A_WIDTH, A_WIDTH + A_RANK_W)
    k = np.arange(A_WIDTH + A_RANK_W, 2 * A_WIDTH + A_RANK_W)
    v = np.arange(2 * A_WIDTH + A_RANK_W, 3 * A_WIDTH + A_RANK_W)
    ad = np.arange(3 * A_WIDTH + A_RANK_W, 3 * A_WIDTH + A_RANK_W + A_RANK_A)
    gd = np.arange(3 * A_WIDTH + A_RANK_W + A_RANK_A, A_PROJ)
    perm = np.concatenate([r, k, v, wd, ad, gd])
    inv = np.argsort(perm)
    return perm, inv


def _token_tile(n, want):
    tm = want
    while n % tm:
        tm //= 2
    return tm


def kernel(x_prompt, x_sample, state_rwkv, state_rwkv_shift, state_delta, state_conv, ffn1_norm, ffn1_w_gate, ffn1_w_up, ffn1_w_down, mix_norm, w_in, rwkv_mu, rwkv_w0, rwkv_w2, rwkv_a0, rwkv_a2, rwkv_g2, rwkv_k_k, rwkv_k_a, rwkv_r_k, rwkv_lnx_w, rwkv_lnx_b, gdn_conv_w, gdn_A_log, gdn_dt_bias, gdn_norm_w, proj_a, proj_b, w_out, ffn2_norm, ffn2_w_gate, ffn2_w_up, ffn2_w_down, final_norm):
    depth = ffn1_norm.shape[0]
    assert depth == 1, "single-layer trunk"
    Bp, Tp, _ = x_prompt.shape
    Bs, Ts, _ = x_sample.shape
    perm, inv = _pa_perm()
    l = 0
    row = lambda a: a.reshape(1, -1).astype(F32)

    wi = w_in[l]
    o_b = A_PROJ
    w_all = jnp.concatenate([
        wi[:, :A_PROJ][:, perm],
        wi[:, o_b:o_b + CONV_CH],
        wi[:, o_b + CONV_CH + 2 * B_HEADS:o_b + B_PROJ],
        wi[:, o_b + B_PROJ:],
        jnp.pad(wi[:, o_b + CONV_CH:o_b + CONV_CH + 2 * B_HEADS], ((0, 0), (0, LANES - 2 * B_HEADS))),
    ], axis=1).astype(BF16)
    ffn1 = (row(ffn1_norm[l]), ffn1_w_gate[l].astype(BF16), ffn1_w_up[l].astype(BF16), ffn1_w_down[l].astype(BF16))
    ffn2 = (row(ffn2_norm[l]), ffn2_w_gate[l].astype(BF16), ffn2_w_up[l].astype(BF16), ffn2_w_down[l].astype(BF16))
    merge_w = (proj_a[l].astype(BF16), proj_b[l].astype(BF16), w_out[l].astype(BF16))
    zw = jnp.zeros((A_RANK_W, A_WIDTH), F32)
    w2a = jnp.concatenate([jnp.concatenate([rwkv_w2[l], zw], axis=1),
                           jnp.concatenate([zw, rwkv_a2[l]], axis=1)], axis=0)
    rwkv_params = (row(rwkv_mu[l][perm]), row(rwkv_w0[l]), row(rwkv_a0[l]), row(rwkv_k_k[l]), row(rwkv_k_a[l]),
                   row(rwkv_r_k[l]), row(rwkv_lnx_w[l]), row(rwkv_lnx_b[l]), w2a, rwkv_g2[l].astype(F32))
    pad_lane = lambda a: jnp.pad(a.reshape(1, -1).astype(F32), ((0, 0), (0, LANES - a.size)))
    gdn_params = (gdn_conv_w[l].astype(F32), pad_lane(gdn_A_log[l]), pad_lane(gdn_dt_bias[l]), row(gdn_norm_w[l]))

    def trunk_front(x2):
        n = x2.shape[0]
        h = _ffn(x2, *ffn1, tm=_token_tile(n, 512))
        return (h,) + tuple(_proj(h, row(mix_norm[l]), w_all, tm=_token_tile(n, 256)))

    def trunk_back(h, oa, ob, gates):
        n = h.shape[0]
        return _tail(h, oa, ob, gates, *merge_w, *ffn2, row(final_norm), tm=_token_tile(n, 256))

    xp = x_prompt.reshape(Bp * Tp, D_MODEL)
    h, pa, qkv, z, gates, ab = trunk_front(xp)
    tt = _token_tile(Tp, 256)
    oa, ob, s_pairs, delta_p = _mix_prompt(pa, qkv, ab, z, Bp, Tp, rwkv_params, gdn_params, tt)
    y_prompt = trunk_back(h, oa, ob, gates).reshape(Bp, Tp, D_MODEL)
    sp = s_pairs.reshape(Bp, A_PAIRS, 2, A_HEAD, 2, A_HEAD)
    rwkv_p = jnp.stack([sp[:, :, 0, :, 0], sp[:, :, 1, :, 1]], axis=2).reshape(Bp, A_HEADS, A_HEAD, A_HEAD)
    shift_p = pa.reshape(Bp, Tp, A_PROJ)[:, -1][:, inv]
    conv_p = qkv.reshape(Bp, Tp, CONV_CH)[:, Tp - (CONV_K - 1):]

    xs = x_sample.reshape(Bs * Ts, D_MODEL)
    h, pa, qkv, z, gates, ab = trunk_front(xs)
    pa3 = pa.reshape(Bs, Ts, A_PROJ)
    prev = jnp.concatenate([state_rwkv_shift[l][:, None, perm].astype(F32), pa3[:, :-1]], axis=1)
    s_in = state_rwkv[l].astype(F32).reshape(Bs, A_PAIRS, 2, A_HEAD, A_HEAD)
    s_in = jnp.transpose(s_in, (0, 1, 3, 2, 4)).reshape(Bs, A_PAIRS, A_HEAD, LANES)
    oa, s_out = _rwkv_sample(pa, prev.reshape(Bs * Ts, A_PROJ), s_in, Ts, rwkv_params, seqs=_token_tile(Bs, 32))
    s_out = jnp.transpose(s_out.reshape(Bs, A_PAIRS, A_HEAD, 2, A_HEAD), (0, 1, 3, 2, 4))
    rwkv_s = s_out.reshape(Bs, A_HEADS, A_HEAD, A_HEAD)
    shift_s = pa3[:, -1][:, inv]
    xpad = jnp.concatenate([state_conv[l].astype(F32), qkv.reshape(Bs, Ts, CONV_CH)], axis=1)
    shifted = [xpad[:, i:i + Ts].reshape(Bs * Ts, CONV_CH) for i in range(CONV_K)]
    ob, delta_s = _gdn_sample(shifted, ab, z, state_delta[l].astype(F32), Ts, gdn_params, seqs=_token_tile(Bs, 16))
    conv_s = xpad[:, Ts:]
    y_sample = trunk_back(h, oa, ob, gates).reshape(Bs, Ts, D_MODEL)

    add_depth = lambda a: a[None]
    return (y_prompt, y_sample,
            add_depth(rwkv_p), add_depth(shift_p), add_depth(delta_p), add_depth(conv_p),
            add_depth(rwkv_s), add_depth(shift_s), add_depth(delta_s), add_depth(conv_s))
```

```python
import functools

import jax
import jax.numpy as jnp
from jax import lax
from jax.experimental import pallas as pl
from jax.experimental.pallas import tpu as pltpu

F32 = jnp.float32
BF16 = jnp.bfloat16

D_MODEL = 1024
D_FF = 2816
RMS_EPS = 1e-6
A_HEAD = 64
A_HEADS = 8
A_WIDTH = A_HEADS * A_HEAD
A_RANK_W = 64
A_RANK_A = 64
A_RANK_G = 128
A_PROJ = 3 * A_WIDTH + A_RANK_W + A_RANK_A + A_RANK_G
A_LNX_EPS = 64e-5
A_PAIRS = A_HEADS // 2
B_HEADS = 4
B_HEAD = 128
B_WIDTH = B_HEADS * B_HEAD
CONV_K = 4
CONV_CH = 3 * B_WIDTH
B_PROJ = CONV_CH + 2 * B_HEADS + B_WIDTH
GATE_COLS = 2 * D_MODEL
LANES = 128
SUBLANES = 8
VMEM_LIMIT_BYTES = 56 * 1024 * 1024
CHUNK = 64
PA_R, PA_K, PA_V, PA_WA, PA_G = 0, A_WIDTH, 2 * A_WIDTH, 3 * A_WIDTH, 3 * A_WIDTH + A_RANK_W + A_RANK_A
PROJ_SPLITS = (A_PROJ, CONV_CH, B_WIDTH, GATE_COLS, LANES)


def _cparams(n_grid_dims):
    return pltpu.CompilerParams(dimension_semantics=("arbitrary",) * n_grid_dims,
                                vmem_limit_bytes=VMEM_LIMIT_BYTES)


def _const_spec(shape):
    nd = len(shape)
    return pl.BlockSpec(shape, lambda *_: (0,) * nd, pipeline_mode=pl.Buffered(1))


def _dot(a, b):
    return jnp.dot(a, b, preferred_element_type=F32)


def _dot_nt(a, b):
    return lax.dot_general(a, b, (((1,), (1,)), ((), ())), preferred_element_type=F32)


def _split(x):
    hi = x.astype(BF16)
    lo = (x - hi.astype(F32)).astype(BF16)
    return hi, lo


def _split3(x):
    hi = x.astype(BF16)
    rest = x - hi.astype(F32)
    mid = rest.astype(BF16)
    lo = (rest - mid.astype(F32)).astype(BF16)
    return hi, mid, lo


def _mm(a, b):
    return _dot(a.astype(BF16), b.astype(BF16))


def _mm_nt(a, b):
    return _dot_nt(a.astype(BF16), b.astype(BF16))


def _sel_mm(sel, x):
    return _dot(jnp.concatenate([sel, sel, sel], axis=1), jnp.concatenate(_split3(x), axis=0))


def _mm_sel(x, sel, pieces=3):
    parts = _split3(x) if pieces == 3 else _split(x)
    return _dot(jnp.concatenate(parts, axis=1), jnp.concatenate([sel] * pieces, axis=0))


INV_BASE = 8


def _nilpotent_inverse(n, eye):
    width = n.shape[1]
    bi, bj = _iota(n.shape, 0), _iota(n.shape, 1)
    same = lambda size: _group(bi, size) == _group(bj, size)
    d = jnp.where(same(INV_BASE), n, 0.0)
    t = eye + d
    d = _mm(d, d)
    yield
    for _ in range(INV_BASE.bit_length() - 3):
        both = _mm(d, jnp.concatenate([t, d], axis=1))
        yield
        t = t + both[:, :width]
        d = both[:, width:]
    t = t + _mm(d, t)
    yield
    size = INV_BASE
    while size < CHUNK:
        coupling = jnp.where(same(2 * size) & jnp.logical_not(same(size)), n, 0.0)
        tb = t.astype(BF16)
        lt = _mm(coupling, tb)
        yield
        t = t + _mm(tb, lt)
        yield
        size *= 2
    return t


def _round_robin(chains):
    chains = list(chains)
    while chains:
        for chain in list(chains):
            try:
                next(chain)
            except StopIteration:
                chains.remove(chain)


def _rms(x, w):
    return x * lax.rsqrt(jnp.mean(x * x, axis=-1, keepdims=True) + RMS_EPS) * w


def _sigmoid(x):
    return 1.0 / (1.0 + jnp.exp(-x))


def _silu(x):
    return x * _sigmoid(x)


def _softplus(x):
    return jnp.maximum(x, 0.0) + jnp.log(1.0 + jnp.exp(-jnp.abs(x)))


def _iota(shape, dim):
    return lax.broadcasted_iota(jnp.int32, shape, dim)


def _group(idx, size):
    assert size & (size - 1) == 0
    return lax.shift_right_logical(idx, size.bit_length() - 1)


def _one_hot(cond):
    return jnp.where(cond, 1.0, 0.0).astype(BF16)


def _rows_to_tile(rows):
    rid = _iota((SUBLANES, 1), 0)
    tile = jnp.zeros((SUBLANES, rows[0].shape[1]), F32)
    for i, row in enumerate(rows):
        tile = jnp.where(rid == i, row, tile)
    return tile


def _swiglu_half_step(x, nw, wg_ref, wu_ref, wd_ref):
    xn = _rms(x, nw).astype(BF16)
    g = _dot(xn, wg_ref[...])
    u = _dot(xn, wu_ref[...])
    act = (_silu(g) * u).astype(BF16)
    return x + 0.5 * _dot(act, wd_ref[...])


def _ffn_body(x_ref, nw_ref, wg_ref, wu_ref, wd_ref, o_ref):
    o_ref[...] = _swiglu_half_step(x_ref[...], nw_ref[...], wg_ref, wu_ref, wd_ref)


def _ffn(x, nw, wg, wu, wd, tm):
    n = x.shape[0]
    return pl.pallas_call(
        _ffn_body,
        grid=(n // tm,),
        in_specs=[pl.BlockSpec((tm, D_MODEL), lambda i: (i, 0)),
                  _const_spec((1, D_MODEL)),
                  _const_spec((D_MODEL, D_FF)), _const_spec((D_MODEL, D_FF)), _const_spec((D_FF, D_MODEL))],
        out_specs=pl.BlockSpec((tm, D_MODEL), lambda i: (i, 0)),
        out_shape=jax.ShapeDtypeStruct((n, D_MODEL), F32),
        compiler_params=_cparams(1),
        name="ffn1",
    )(x, nw, wg, wu, wd)


def _proj_body(h_ref, nw_ref, w_ref, *o_refs):
    u = _rms(h_ref[...], nw_ref[...]).astype(BF16)
    off = 0
    for o_ref, width in zip(o_refs, PROJ_SPLITS):
        o_ref[...] = _dot(u, w_ref[:, off:off + width])
        off += width


def _proj(h, nw, w_all, tm):
    n = h.shape[0]
    cols = sum(PROJ_SPLITS)
    return pl.pallas_call(
        _proj_body,
        grid=(n // tm,),
        in_specs=[pl.BlockSpec((tm, D_MODEL), lambda i: (i, 0)),
                  _const_spec((1, D_MODEL)), _const_spec((D_MODEL, cols))],
        out_specs=[pl.BlockSpec((tm, w), lambda i: (i, 0)) for w in PROJ_SPLITS],
        out_shape=[jax.ShapeDtypeStruct((n, w), F32) for w in PROJ_SPLITS],
        compiler_params=_cparams(1),
        name="proj",
    )(h, nw, w_all)


def _tail_body(h_ref, oa_ref, ob_ref, gates_ref, pa_ref, pb_ref, wo_ref, nw_ref, wg_ref, wu_ref, wd_ref,
               fn_ref, o_ref):
    ma = _dot(oa_ref[...].astype(BF16), pa_ref[...])
    mb = _dot(ob_ref[...].astype(BF16), pb_ref[...])
    merged = _sigmoid(gates_ref[:, :D_MODEL]) * ma + _sigmoid(gates_ref[:, D_MODEL:]) * mb
    h = h_ref[...] + _dot(merged.astype(BF16), wo_ref[...])
    h = _swiglu_half_step(h, nw_ref[...], wg_ref, wu_ref, wd_ref)
    o_ref[...] = _rms(h, fn_ref[...])


def _tail(h, oa, ob, gates, proj_a, proj_b, w_out, nw, wg, wu, wd, fn, tm):
    n = h.shape[0]
    row = lambda w: pl.BlockSpec((tm, w), lambda i: (i, 0))
    return pl.pallas_call(
        _tail_body,
        grid=(n // tm,),
        in_specs=[row(D_MODEL), row(A_WIDTH), row(B_WIDTH), row(GATE_COLS),
                  _const_spec((A_WIDTH, D_MODEL)), _const_spec((B_WIDTH, D_MODEL)),
                  _const_spec((D_MODEL, D_MODEL)), _const_spec((1, D_MODEL)),
                  _const_spec((D_MODEL, D_FF)), _const_spec((D_MODEL, D_FF)), _const_spec((D_FF, D_MODEL)),
                  _const_spec((1, D_MODEL))],
        out_specs=row(D_MODEL),
        out_shape=jax.ShapeDtypeStruct((n, D_MODEL), F32),
        compiler_params=_cparams(1),
        name="tail",
    )(h, oa, ob, gates, proj_a, proj_b, w_out, nw, wg, wu, wd, fn)


def _drain(chain):
    try:
        while True:
            next(chain)
    except StopIteration as stop:
        return stop.value


def _rwkv_token_math(x, prev, mu, w0, a0, k_k, k_a, w2a, g2):
    pm = x + (prev - x) * mu
    r = pm[:, PA_R:PA_R + A_WIDTH]
    k = pm[:, PA_K:PA_K + A_WIDTH]
    v = pm[:, PA_V:PA_V + A_WIDTH]
    wa = pm[:, PA_WA:PA_WA + LANES]
    gd = pm[:, PA_G:PA_G + A_RANK_G]
    lane = _iota((1, LANES), 1)
    lora_in = jnp.where(lane < A_RANK_W, jnp.tanh(wa), wa)
    lora = _mm(lora_in, w2a)
    g = _mm(_sigmoid(gd), g2)
    yield
    w_log = -_softplus(-(w0 + lora[:, :A_WIDTH])) - 0.5
    log_decay = -jnp.exp(w_log)
    yield
    a = _sigmoid(a0 + lora[:, A_WIDTH:])
    kk_raw = k * k_k
    k_mod = k * (1.0 + (a - 1.0) * k_a)
    return r, k_mod, v, kk_raw, a, log_decay, g


def _pair_mask(rows_per_head):
    shape = (2 * rows_per_head, LANES)
    return _group(_iota(shape, 0), rows_per_head) == _group(_iota(shape, 1), A_HEAD)


def _rwkv_prompt_part(pa_ref, mu_ref, w0_ref, a0_ref, kk_ref, ka_ref, rk_ref, lnw_ref, lnb_ref, w2a_ref, g2_ref,
                      o_ref, carry_ref, state_ref, r_s, k_s, v_s, kkraw_s, a_s, cum_s, ld_s, g_s):
    tt = pa_ref.shape[0]
    C = CHUNK
    lower = _one_hot(_iota((C, C), 1) <= _iota((C, C), 0))

    def token_chain(r0):
        rows = slice(r0, r0 + C)
        x = pa_ref[rows, :]
        before = carry_ref[SUBLANES - 1:SUBLANES, :] if r0 == 0 else pa_ref[r0 - 1:r0, :]
        prev = jnp.where(_iota((C, 1), 0) == 0, before, pltpu.roll(x, 1, axis=0))
        if r0 + C == tt:
            carry_ref[...] = x[C - SUBLANES:, :]
        r, k_mod, v, kk_raw, a, log_decay, g = yield from _rwkv_token_math(
            x, prev, mu_ref[...], w0_ref[...], a0_ref[...], kk_ref[...], ka_ref[...], w2a_ref[...], g2_ref[...])
        r_s[rows, :] = r
        k_s[rows, :] = k_mod
        v_s[rows, :] = v
        kkraw_s[rows, :] = kk_raw
        a_s[rows, :] = a
        g_s[rows, :] = g
        ld_s[rows, :] = log_decay
        yield
        cum_s[rows, :] = sum(_dot(lower, piece) for piece in _split3(log_decay))
        yield

    mask = _pair_mask(C)
    i2, j2 = _iota((2 * C, 2 * C), 0), _iota((2 * C, 2 * C), 1)
    strict = i2 > j2
    incl = i2 >= j2
    eye = jnp.where(i2 == j2, 1.0, 0.0)
    dup = lambda m: jnp.concatenate([m, m], axis=0)
    stack = lambda m: jnp.where(mask, dup(m), 0.0)

    def solve_chain(p, r0, stash):
        sl = slice(p * LANES, (p + 1) * LANES)
        ld = lambda ref: ref[pl.ds(r0, C), sl]
        r_p, k_p, v_p, a_p, cum, ldec = ld(r_s), ld(k_s), ld(v_s), ld(a_s), ld(cum_s), ld(ld_s)
        einc = jnp.exp(cum)
        eex = jnp.exp(cum - ldec)
        einv = jnp.exp(-cum)
        etail = jnp.exp(cum[C - 1:C, :] - cum)
        kks = stack(ld(kkraw_s))
        kks = kks * jnp.minimum(lax.rsqrt(jnp.sum(kks * kks, axis=-1, keepdims=True)), 1e12)
        As = -kks * dup(eex)
        Bs = kks * dup(a_p * einv)
        Bh = kks * dup(a_p * etail)
        Ks = stack(k_p * einv)
        Kh = stack(k_p * etail)
        Rs = stack(r_p * einc)
        Vs = stack(v_p)
        AR = jnp.concatenate([As, Rs], axis=0).astype(BF16)
        Vb = Vs.astype(BF16)
        G = _mm_nt(AR, jnp.concatenate([Bs, Ks], axis=0))
        yield
        Aab = jnp.where(strict, G[:2 * C, :2 * C], 0.0)
        Aak = jnp.where(strict, G[:2 * C, 2 * C:], 0.0)
        Arb = jnp.where(incl, G[2 * C:, :2 * C], 0.0)
        Ark = jnp.where(incl, G[2 * C:, 2 * C:], 0.0)
        Y = _mm(Aak, Vb)
        yield
        T = yield from _nilpotent_inverse(Aab, eye)
        WU = _mm(T, jnp.concatenate([AR[:2 * C], Y.astype(BF16)], axis=1))
        yield
        bonus = jnp.sum(stack(r_p * k_p * rk_ref[:, sl]), axis=-1, keepdims=True) * Vs
        stash[p] = dict(WU=WU, R=AR[2 * C:], Vs=Vs, bonus=bonus,
                        Aro=jnp.concatenate([Arb, Ark], axis=1).astype(BF16),
                        BKh=jnp.concatenate([Bh, Kh], axis=0).astype(BF16), decay=einc[C - 1:C, :])

    def state_chain(p, r0, stash):
        sl = slice(p * LANES, (p + 1) * LANES)
        s = stash[p]
        S = state_ref[p]
        Sb = S.astype(BF16)
        W = _mm_nt(s["WU"][:, :LANES], Sb) + s["WU"][:, LANES:]
        yield
        WV = jnp.concatenate([W, s["Vs"]], axis=0)
        O = _dot_nt(s["R"], Sb) + _mm(s["Aro"], WV)
        state_ref[p] = S * s["decay"] + _mm(WV.T, s["BKh"])
        yield
        mean = jnp.sum(O, axis=-1, keepdims=True) * (1.0 / A_HEAD)
        cen = jnp.where(mask, O - mean, 0.0)
        var = jnp.sum(cen * cen, axis=-1, keepdims=True) * (1.0 / A_HEAD)
        normed = jnp.where(mask, cen * lax.rsqrt(var + A_LNX_EPS) * lnw_ref[:, sl] + lnb_ref[:, sl], 0.0)
        full = normed + s["bonus"]
        o_ref[pl.ds(r0, C), sl] = (full[:C] + full[C:]) * g_s[pl.ds(r0, C), sl]

    return (lambda r0: [token_chain(r0)],
            lambda r0, stash: [solve_chain(p, r0, stash) for p in range(A_PAIRS)],
            lambda r0, stash: [state_chain(p, r0, stash) for p in range(A_PAIRS)])


CHUNKS_IN_FLIGHT = 2
N_RWKV_PARAMS = 10
N_GDN_PARAMS = 4
N_RWKV_SCRATCH = 8
N_GDN_SCRATCH = 5


def _mix_prompt_body(pa_ref, qkv_ref, ab_ref, z_ref, *refs):
    refs = list(refs)
    take = lambda n: [refs.pop(0) for _ in range(n)]
    rwkv_prm, gdn_prm = take(N_RWKV_PARAMS), take(N_GDN_PARAMS)
    oa_ref, ob_ref, sfa_ref, sfb_ref = take(4)
    carry_a, state_a, carry_b, state_b = take(4)
    rwkv_scr, gdn_scr = take(N_RWKV_SCRATCH), take(N_GDN_SCRATCH)
    t = pl.program_id(1)
    tt = pa_ref.shape[0]

    @pl.when(t == 0)
    def _():
        for ref in (carry_a, state_a, carry_b, state_b):
            ref[...] = jnp.zeros_like(ref)

    rwkv_token, rwkv_solve, rwkv_state = _rwkv_prompt_part(pa_ref, *rwkv_prm, oa_ref, carry_a, state_a, *rwkv_scr)
    gdn_token, gdn_solve, gdn_state = _gdn_prompt_part(qkv_ref, ab_ref, z_ref, *gdn_prm, ob_ref, carry_b, state_b,
                                                       *gdn_scr)

    n_chunks = tt // CHUNK
    group = min(CHUNKS_IN_FLIGHT, n_chunks)
    n_groups = n_chunks // group
    stashes = [({}, {}) for _ in range(n_chunks)]
    chunks_of = lambda gi: range(gi * group, (gi + 1) * group) if 0 <= gi < n_groups else ()

    def in_sequence(per_chunk_chains):
        for chains in zip(*per_chunk_chains):
            for chain in chains:
                yield from chain

    for gi in range(n_groups + 2):
        chains = []
        for c in chunks_of(gi):
            chains += rwkv_token(c * CHUNK) + gdn_token(c * CHUNK)
        for c in chunks_of(gi - 1):
            chains += rwkv_solve(c * CHUNK, stashes[c][0]) + gdn_solve(c * CHUNK, stashes[c][1])
        state_chains = [rwkv_state(c * CHUNK, stashes[c][0]) + gdn_state(c * CHUNK, stashes[c][1])
                        for c in chunks_of(gi - 2)]
        if state_chains:
            chains += [in_sequence([per_chunk[i:i + 1] for per_chunk in state_chains])
                       for i in range(len(state_chains[0]))]
        _round_robin(chains)

    @pl.when(t == pl.num_programs(1) - 1)
    def _():
        sfa_ref[0] = state_a[...]
        sfb_ref[0] = state_b[...]


def _mix_prompt(pa, qkv, ab, z, B, T, rwkv_params, gdn_params, tt):
    n = pa.shape[0]
    nt = T // tt
    assert len(rwkv_params) == N_RWKV_PARAMS and len(gdn_params) == N_GDN_PARAMS
    rows = lambda w: pl.BlockSpec((tt, w), lambda b, t: (b * nt + t, 0))
    state = lambda: pl.BlockSpec((1, 4, LANES, LANES), lambda b, t: (b, 0, 0, 0))
    big = lambda: pltpu.VMEM((tt, A_WIDTH), F32)
    return pl.pallas_call(
        _mix_prompt_body,
        grid=(B, nt),
        in_specs=[rows(A_PROJ), rows(CONV_CH), rows(LANES), rows(B_WIDTH)]
                 + [_const_spec(p.shape) for p in rwkv_params + gdn_params],
        out_specs=[rows(A_WIDTH), rows(B_WIDTH), state(), state()],
        out_shape=[jax.ShapeDtypeStruct((n, A_WIDTH), F32), jax.ShapeDtypeStruct((n, B_WIDTH), F32),
                   jax.ShapeDtypeStruct((B, A_PAIRS, LANES, LANES), F32),
                   jax.ShapeDtypeStruct((B, B_HEADS, B_HEAD, B_HEAD), F32)],
        scratch_shapes=[pltpu.VMEM((SUBLANES, A_PROJ), F32), pltpu.VMEM((A_PAIRS, LANES, LANES), F32),
                        pltpu.VMEM((SUBLANES, CONV_CH), F32), pltpu.VMEM((B_HEADS, B_HEAD, B_HEAD), F32)]
                       + [big() for _ in range(N_RWKV_SCRATCH + N_GDN_SCRATCH)],
        compiler_params=_cparams(2),
        name="mix_prompt",
    )(pa, qkv, ab, z, *rwkv_params, *gdn_params)


def _rwkv_sample_body(pa_ref, prev_ref, s_ref, mu_ref, w0_ref, a0_ref, kk_ref, ka_ref, rk_ref, lnw_ref, lnb_ref,
                      w2a_ref, g2_ref, o_ref, sout_ref, r_s, nkk_s, beta_s, dec_s, k_s, v_s, o_s, *, steps):
    rows = pa_ref.shape[0]
    per_tile = SUBLANES // steps
    r, k_mod, v, kk_raw, a, log_decay, g = _drain(_rwkv_token_math(
        pa_ref[...], prev_ref[...], mu_ref[...], w0_ref[...], a0_ref[...], kk_ref[...], ka_ref[...],
        w2a_ref[...], g2_ref[...]))
    hi, hj = _iota((A_WIDTH, A_WIDTH), 0), _iota((A_WIDTH, A_WIDTH), 1)
    head_ones = _one_hot(_group(hi, A_HEAD) == _group(hj, A_HEAD))
    head_sum = lambda m: _mm_sel(m, head_ones)
    kk = kk_raw / jnp.maximum(jnp.sqrt(head_sum(kk_raw * kk_raw)), 1e-12)
    r_s[...] = r
    nkk_s[...] = -kk
    beta_s[...] = kk * a
    dec_s[...] = jnp.exp(log_decay)
    k_s[...] = k_mod
    v_s[...] = v
    pair_ones = head_ones[:LANES, :LANES]
    diag = _iota((A_HEAD, LANES), 0) == (_iota((A_HEAD, LANES), 1) & (A_HEAD - 1))

    def tile(j, carry):
        r0 = pl.multiple_of(j * SUBLANES, SUBLANES)
        ld = lambda ref: ref[pl.ds(r0, SUBLANES), :]
        r_t, nkk_t, beta_t, dec_t, k_t, v_t = ld(r_s), ld(nkk_s), ld(beta_s), ld(dec_s), ld(k_s), ld(v_s)
        out_rows = [[None] * A_PAIRS for _ in range(SUBLANES)]

        def seq_pair_chain(i, p):
            sl = slice(p * LANES, (p + 1) * LANES)
            rowof = lambda m, rr: m[rr:rr + 1, sl]
            v_diag = jnp.concatenate([jnp.where(diag, rowof(v_t, i * steps + s), 0.0) for s in range(steps)],
                                     axis=0)
            v_cols = _mm_sel(v_diag, pair_ones, pieces=2)
            S = s_ref[j * per_tile + i, p]
            for s in range(steps):
                rr = i * steps + s
                sa = _mm(S * rowof(nkk_t, rr), pair_ones)
                yield
                S = (S * rowof(dec_t, rr) + sa * rowof(beta_t, rr)
                     + v_cols[s * A_HEAD:(s + 1) * A_HEAD] * rowof(k_t, rr))
                out = _mm(S * rowof(r_t, rr), pair_ones)
                out_rows[rr][p] = jnp.sum(jnp.where(diag, out, 0.0), axis=0, keepdims=True)
            sout_ref[j * per_tile + i, p] = S

        _round_robin(seq_pair_chain(i, p) for i in range(per_tile) for p in range(A_PAIRS))
        o_s[pl.ds(r0, SUBLANES), :] = _rows_to_tile([jnp.concatenate(parts, axis=1) for parts in out_rows])
        return carry

    lax.fori_loop(0, rows // SUBLANES, tile, 0)
    o = o_s[...]
    mean = head_sum(o) * (1.0 / A_HEAD)
    cen = o - mean
    var = head_sum(cen * cen) * (1.0 / A_HEAD)
    o = cen * lax.rsqrt(var + A_LNX_EPS) * lnw_ref[...] + lnb_ref[...]
    o_ref[...] = (o + head_sum(r * k_mod * rk_ref[...]) * v) * g


def _rwkv_sample(pa, prev, state_pairs, steps, params, seqs):
    n = pa.shape[0]
    nb = state_pairs.shape[0]
    rows = seqs * steps
    assert SUBLANES % steps == 0 and rows % SUBLANES == 0
    sspec = pl.BlockSpec((seqs, A_PAIRS, A_HEAD, LANES), lambda i: (i, 0, 0, 0))
    return pl.pallas_call(
        functools.partial(_rwkv_sample_body, steps=steps),
        grid=(nb // seqs,),
        in_specs=[pl.BlockSpec((rows, A_PROJ), lambda i: (i, 0)), pl.BlockSpec((rows, A_PROJ), lambda i: (i, 0)),
                  sspec] + [_const_spec(p.shape) for p in params],
        out_specs=[pl.BlockSpec((rows, A_WIDTH), lambda i: (i, 0)), sspec],
        out_shape=[jax.ShapeDtypeStruct((n, A_WIDTH), F32),
                   jax.ShapeDtypeStruct(state_pairs.shape, F32)],
        scratch_shapes=[pltpu.VMEM((rows, A_WIDTH), F32) for _ in range(7)],
        compiler_params=_cparams(1),
        name="rwkv_sample",
    )(pa, prev, state_pairs, *params)


def _gdn_qkv(conv):
    c = _silu(conv)
    qs, ks = [], []
    for h in range(B_HEADS):
        q = c[:, h * B_HEAD:(h + 1) * B_HEAD]
        k = c[:, B_WIDTH + h * B_HEAD:B_WIDTH + (h + 1) * B_HEAD]
        qs.append(q * (lax.rsqrt(jnp.sum(q * q, axis=-1, keepdims=True) + 1e-6) * (B_HEAD ** -0.5)))
        ks.append(k * lax.rsqrt(jnp.sum(k * k, axis=-1, keepdims=True) + 1e-6))
    q = jnp.concatenate(qs, axis=1)
    k = jnp.concatenate(ks, axis=1)
    v = c[:, 2 * B_WIDTH:]
    return q, k, v


def _gdn_gates(ab, alog, dtb):
    lane = _iota((1, LANES), 1)
    g = -jnp.exp(alog) * _softplus(ab + dtb)
    beta = _sigmoid(ab)
    gb = jnp.where(lane < B_HEADS, g, beta)
    si, sj = _iota((LANES, 2 * B_WIDTH), 0), _iota((LANES, 2 * B_WIDTH), 1)
    spread = _mm_sel(gb, _one_hot(si == _group(sj, B_HEAD)))
    return spread[:, :B_WIDTH], spread[:, B_WIDTH:]


def _gdn_out(o, norm_w, z):
    return o * lax.rsqrt(jnp.mean(o * o, axis=-1, keepdims=True) + RMS_EPS) * norm_w * _silu(z)


def _gdn_prompt_part(qkv_ref, ab_ref, z_ref, cw_ref, alog_ref, dtb_ref, nw_ref, o_ref,
                     carry_ref, state_ref, q_s, k_s, v_s, gc_s, beta_s):
    tt = qkv_ref.shape[0]
    C = CHUNK
    g, beta = _gdn_gates(ab_ref[...], alog_ref[...], dtb_ref[...])
    ri, ci = _iota((tt, tt), 0), _iota((tt, tt), 1)
    beta_s[...] = beta
    gc_s[...] = _sel_mm(_one_hot((_group(ri, C) == _group(ci, C)) & (ci <= ri)), g)

    def token_chain(r0):
        rows = slice(r0, r0 + C)
        x = qkv_ref[rows, :]
        before = carry_ref[...] if r0 == 0 else qkv_ref[r0 - SUBLANES:r0, :]
        if r0 + C == tt:
            carry_ref[...] = x[C - SUBLANES:, :]
        row8 = _iota((SUBLANES, 1), 0)
        conv = x * cw_ref[CONV_K - 1:CONV_K, :]
        for i in range(1, CONV_K):
            xs = pltpu.roll(x, i, axis=0)
            top = jnp.where(row8 < i, pltpu.roll(before, i, axis=0), xs[:SUBLANES])
            xs = jnp.concatenate([top, xs[SUBLANES:]], axis=0)
            conv = conv + xs * cw_ref[CONV_K - 1 - i:CONV_K - i, :]
            yield
        q, k, v = _gdn_qkv(conv)
        q_s[rows, :] = q
        k_s[rows, :] = k
        v_s[rows, :] = v
        yield

    i2, j2 = _iota((2 * C, 2 * C), 0), _iota((2 * C, 2 * C), 1)
    same_head = _group(i2, C) == _group(j2, C)
    strict = same_head & (i2 > j2)
    incl = same_head & (i2 >= j2)
    eye = jnp.where(i2 == j2, 1.0, 0.0)
    first = _iota((2 * C, 1), 0) < C

    def solve_chain(pr, r0, stash):
        sls = [slice(h * B_HEAD, (h + 1) * B_HEAD) for h in (2 * pr, 2 * pr + 1)]
        ld = lambda ref: jnp.concatenate([ref[pl.ds(r0, C), sl] for sl in sls], axis=0)
        q_h, k_h, v_h, gc_h, beta_h = ld(q_s), ld(k_s), ld(v_s), ld(gc_s), ld(beta_s)
        diff = gc_h - gc_h.T
        dm = jnp.where(incl, jnp.exp(jnp.where(incl, diff, 0.0)), 0.0)
        kb = k_h * beta_h
        QK = _mm_nt(jnp.concatenate([kb, q_h], axis=0), k_h)
        yield
        N = -jnp.where(strict, QK[:2 * C] * dm, 0.0)
        qk = QK[2 * C:] * dm
        egc = jnp.exp(gc_h)
        X = jnp.concatenate([v_h * beta_h, kb * egc], axis=1)
        T = yield from _nilpotent_inverse(N, eye)
        UW = _mm(T, X)
        yield
        g_last = jnp.where(first, gc_h[C - 1:C, :], gc_h[2 * C - 1:2 * C, :])
        stash[pr] = dict(u=UW[:, :B_HEAD], w=UW[:, B_HEAD:].astype(BF16), qd=(q_h * egc).astype(BF16),
                         qk=qk.astype(BF16), k_dec_t=(k_h * jnp.exp(g_last - gc_h)).T.astype(BF16),
                         decay=[jnp.exp(gc_h[(j + 1) * C - 1:(j + 1) * C, :]) for j in range(2)])

    def state_chain(pr, r0, stash):
        heads = (2 * pr, 2 * pr + 1)
        sls = [slice(h * B_HEAD, (h + 1) * B_HEAD) for h in heads]
        s = stash[pr]
        wS, qS, S_old = [], [], []
        for j, h in enumerate(heads):
            S = state_ref[h]
            rows = slice(j * C, (j + 1) * C)
            wq = _dot(jnp.concatenate([s["w"][rows], s["qd"][rows]], axis=0), S.astype(BF16))
            wS.append(wq[:C])
            qS.append(wq[C:])
            S_old.append(S)
        yield
        v_new = s["u"] - jnp.concatenate(wS, axis=0)
        o = jnp.concatenate(qS, axis=0) + _dot(s["qk"], v_new.astype(BF16))
        for j, h in enumerate(heads):
            mine = first if j == 0 else jnp.logical_not(first)
            rows = slice(j * C, (j + 1) * C)
            state_ref[h] = S_old[j] * s["decay"][j] + _mm(s["k_dec_t"], jnp.where(mine, v_new, 0.0))
            o_ref[pl.ds(r0, C), sls[j]] = _gdn_out(o[rows], nw_ref[...], z_ref[pl.ds(r0, C), sls[j]])
        yield

    return (lambda r0: [token_chain(r0)],
            lambda r0, stash: [solve_chain(pr, r0, stash) for pr in range(B_HEADS // 2)],
            lambda r0, stash: [state_chain(pr, r0, stash) for pr in range(B_HEADS // 2)])


def _gdn_sample_body(x0_ref, x1_ref, x2_ref, x3_ref, ab_ref, z_ref, s_ref, cw_ref, alog_ref, dtb_ref, nw_ref,
                     o_ref, sout_ref, q_s, k_s, v_s, eg_s, beta_s, o_s, *, steps):
    rows = ab_ref.shape[0]
    per_tile = SUBLANES // steps
    conv = (x0_ref[...] * cw_ref[0:1, :] + x1_ref[...] * cw_ref[1:2, :]
            + x2_ref[...] * cw_ref[2:3, :] + x3_ref[...] * cw_ref[3:4, :])
    q, k, v = _gdn_qkv(conv)
    g, beta = _gdn_gates(ab_ref[...], alog_ref[...], dtb_ref[...])
    q_s[...] = q
    k_s[...] = k
    v_s[...] = v
    eg_s[...] = jnp.exp(g)
    beta_s[...] = beta
    diag = _iota((B_HEAD, B_HEAD), 0) == _iota((B_HEAD, B_HEAD), 1)
    ones = jnp.ones((B_HEAD, B_HEAD), BF16)

    def to_cols(tile_rows):
        diags = jnp.concatenate([jnp.where(diag, rv, 0.0) for rv in tile_rows], axis=0)
        return _mm(diags, ones)

    def tile(j, carry):
        r0 = pl.multiple_of(j * SUBLANES, SUBLANES)
        ld = lambda ref: ref[pl.ds(r0, SUBLANES), :]
        q_t, k_t, v_t, eg_t, beta_t = ld(q_s), ld(k_s), ld(v_s), ld(eg_s), ld(beta_s)
        out_rows = [[] for _ in range(SUBLANES)]
        for i in range(per_tile):
            for h in range(B_HEADS):
                sl = slice(h * B_HEAD, (h + 1) * B_HEAD)
                rowof = lambda m, rr: m[rr:rr + 1, sl]
                S = s_ref[j * per_tile + i, h]
                for s in range(steps):
                    rr = i * steps + s
                    kq_cols = to_cols([rowof(k_t, rr), rowof(q_t, rr)])
                    k_col, q_col = kq_cols[:B_HEAD], kq_cols[B_HEAD:]
                    S = S * rowof(eg_t, rr)
                    u = jnp.sum(k_col * S, axis=0, keepdims=True)
                    S = S + k_col * (rowof(beta_t, rr) * (rowof(v_t, rr) - u))
                    out_rows[rr].append(jnp.sum(q_col * S, axis=0, keepdims=True))
                sout_ref[j * per_tile + i, h] = S
        o_s[pl.ds(r0, SUBLANES), :] = _rows_to_tile([jnp.concatenate(parts, axis=1) for parts in out_rows])
        return carry

    lax.fori_loop(0, rows // SUBLANES, tile, 0)
    for h in range(B_HEADS):
        sl = slice(h * B_HEAD, (h + 1) * B_HEAD)
        o_ref[:, sl] = _gdn_out(o_s[:, sl], nw_ref[...], z_ref[:, sl])


def _gdn_sample(xs, ab, z, state, steps, params, seqs):
    n = ab.shape[0]
    nb = state.shape[0]
    rows = seqs * steps
    assert SUBLANES % steps == 0 and rows % SUBLANES == 0
    row = lambda w: pl.BlockSpec((rows, w), lambda i: (i, 0))
    sspec = pl.BlockSpec((seqs, B_HEADS, B_HEAD, B_HEAD), lambda i: (i, 0, 0, 0))
    return pl.pallas_call(
        functools.partial(_gdn_sample_body, steps=steps),
        grid=(nb // seqs,),
        in_specs=[row(CONV_CH)] * CONV_K + [row(LANES), row(B_WIDTH), sspec] + [_const_spec(p.shape) for p in params],
        out_specs=[row(B_WIDTH), sspec],
        out_shape=[jax.ShapeDtypeStruct((n, B_WIDTH), F32), jax.ShapeDtypeStruct(state.shape, F32)],
        scratch_shapes=[pltpu.VMEM((rows, B_WIDTH), F32) for _ in range(6)],
        compiler_params=_cparams(1),
        name="gdn_sample",
    )(*xs, ab, z, state, *params)


def _cuts(widths):
    edges, total = [], 0
    for w in widths[:-1]:
        total += w
        edges.append(total)
    return edges


def _regroup_pa(a):
    r, wd, k, v, ad, gd = jnp.split(a, _cuts((A_WIDTH, A_RANK_W, A_WIDTH, A_WIDTH, A_RANK_A, A_RANK_G)), axis=-1)
    return jnp.concatenate([r, k, v, wd, ad, gd], axis=-1)


def _ungroup_pa(a):
    r, k, v, wd, ad, gd = jnp.split(a, _cuts((A_WIDTH, A_WIDTH, A_WIDTH, A_RANK_W, A_RANK_A, A_RANK_G)), axis=-1)
    return jnp.concatenate([r, wd, k, v, ad, gd], axis=-1)


def _token_tile(n, want):
    tm = want
    while n % tm:
        tm //= 2
    return tm


def kernel(x_prompt, x_sample, state_rwkv, state_rwkv_shift, state_delta, state_conv, ffn1_norm, ffn1_w_gate, ffn1_w_up, ffn1_w_down, mix_norm, w_in, rwkv_mu, rwkv_w0, rwkv_w2, rwkv_a0, rwkv_a2, rwkv_g2, rwkv_k_k, rwkv_k_a, rwkv_r_k, rwkv_lnx_w, rwkv_lnx_b, gdn_conv_w, gdn_A_log, gdn_dt_bias, gdn_norm_w, proj_a, proj_b, w_out, ffn2_norm, ffn2_w_gate, ffn2_w_up, ffn2_w_down, final_norm):
    depth = ffn1_norm.shape[0]
    assert depth == 1, "single-layer trunk"
    Bp, Tp, _ = x_prompt.shape
    Bs, Ts, _ = x_sample.shape
    l = 0
    row = lambda a: a.reshape(1, -1).astype(F32)

    wi = w_in[l]
    o_b = A_PROJ
    w_all = jnp.concatenate([
        _regroup_pa(wi[:, :A_PROJ]),
        wi[:, o_b:o_b + CONV_CH],
        wi[:, o_b + CONV_CH + 2 * B_HEADS:o_b + B_PROJ],
        wi[:, o_b + B_PROJ:],
        jnp.pad(wi[:, o_b + CONV_CH:o_b + CONV_CH + 2 * B_HEADS], ((0, 0), (0, LANES - 2 * B_HEADS))),
    ], axis=1).astype(BF16)
    ffn1 = (row(ffn1_norm[l]), ffn1_w_gate[l].astype(BF16), ffn1_w_up[l].astype(BF16), ffn1_w_down[l].astype(BF16))
    ffn2 = (row(ffn2_norm[l]), ffn2_w_gate[l].astype(BF16), ffn2_w_up[l].astype(BF16), ffn2_w_down[l].astype(BF16))
    merge_w = (proj_a[l].astype(BF16), proj_b[l].astype(BF16), w_out[l].astype(BF16))
    zw = jnp.zeros((A_RANK_W, A_WIDTH), F32)
    w2a = jnp.concatenate([jnp.concatenate([rwkv_w2[l], zw], axis=1),
                           jnp.concatenate([zw, rwkv_a2[l]], axis=1)], axis=0)
    rwkv_params = (row(_regroup_pa(rwkv_mu[l])), row(rwkv_w0[l]), row(rwkv_a0[l]), row(rwkv_k_k[l]), row(rwkv_k_a[l]),
                   row(rwkv_r_k[l]), row(rwkv_lnx_w[l]), row(rwkv_lnx_b[l]), w2a, rwkv_g2[l].astype(F32))
    pad_lane = lambda a: jnp.pad(a.reshape(1, -1).astype(F32), ((0, 0), (0, LANES - a.size)))
    gdn_params = (gdn_conv_w[l].astype(F32), pad_lane(gdn_A_log[l]), pad_lane(gdn_dt_bias[l]), row(gdn_norm_w[l]))

    def trunk_front(x2):
        n = x2.shape[0]
        h = _ffn(x2, *ffn1, tm=_token_tile(n, 512))
        return (h,) + tuple(_proj(h, row(mix_norm[l]), w_all, tm=_token_tile(n, 256)))

    def trunk_back(h, oa, ob, gates):
        n = h.shape[0]
        return _tail(h, oa, ob, gates, *merge_w, *ffn2, row(final_norm), tm=_token_tile(n, 256))

    xp = x_prompt.reshape(Bp * Tp, D_MODEL)
    h, pa, qkv, z, gates, ab = trunk_front(xp)
    tt = _token_tile(Tp, 256)
    oa, ob, s_pairs, delta_p = _mix_prompt(pa, qkv, ab, z, Bp, Tp, rwkv_params, gdn_params, tt)
    y_prompt = trunk_back(h, oa, ob, gates).reshape(Bp, Tp, D_MODEL)
    sp = s_pairs.reshape(Bp, A_PAIRS, 2, A_HEAD, 2, A_HEAD)
    rwkv_p = jnp.stack([sp[:, :, 0, :, 0], sp[:, :, 1, :, 1]], axis=2).reshape(Bp, A_HEADS, A_HEAD, A_HEAD)
    shift_p = _ungroup_pa(pa.reshape(Bp, Tp, A_PROJ)[:, -1])
    conv_p = qkv.reshape(Bp, Tp, CONV_CH)[:, Tp - (CONV_K - 1):]

    xs = x_sample.reshape(Bs * Ts, D_MODEL)
    h, pa, qkv, z, gates, ab = trunk_front(xs)
    pa3 = pa.reshape(Bs, Ts, A_PROJ)
    prev = jnp.concatenate([_regroup_pa(state_rwkv_shift[l].astype(F32))[:, None], pa3[:, :-1]], axis=1)
    s_in = state_rwkv[l].astype(F32).reshape(Bs, A_PAIRS, 2, A_HEAD, A_HEAD)
    s_in = jnp.transpose(s_in, (0, 1, 3, 2, 4)).reshape(Bs, A_PAIRS, A_HEAD, LANES)
    oa, s_out = _rwkv_sample(pa, prev.reshape(Bs * Ts, A_PROJ), s_in, Ts, rwkv_params, seqs=_token_tile(Bs, 32))
    s_out = jnp.transpose(s_out.reshape(Bs, A_PAIRS, A_HEAD, 2, A_HEAD), (0, 1, 3, 2, 4))
    rwkv_s = s_out.reshape(Bs, A_HEADS, A_HEAD, A_HEAD)
    shift_s = _ungroup_pa(pa3[:, -1])
    xpad = jnp.concatenate([state_conv[l].astype(F32), qkv.reshape(Bs, Ts, CONV_CH)], axis=1)
    shifted = [xpad[:, i:i + Ts].reshape(Bs * Ts, CONV_CH) for i in range(CONV_K)]
    ob, delta_s = _gdn_sample(shifted, ab, z, state_delta[l].astype(F32), Ts, gdn_params, seqs=_token_tile(Bs, 16))
    conv_s = xpad[:, Ts:]
    y_sample = trunk_back(h, oa, ob, gates).reshape(Bs, Ts, D_MODEL)

    add_depth = lambda a: a[None]
    return (y_prompt, y_sample,
            add_depth(rwkv_p), add_depth(shift_p), add_depth(delta_p), add_depth(conv_p),
            add_depth(rwkv_s), add_depth(shift_s), add_depth(delta_s), add_depth(conv_s))
```

```python
import functools

import jax
import jax.numpy as jnp
from jax import lax
from jax.experimental import pallas as pl
from jax.experimental.pallas import tpu as pltpu

F32 = jnp.float32
BF16 = jnp.bfloat16

D_MODEL = 1024
D_FF = 2816
RMS_EPS = 1e-6
A_HEAD = 64
A_HEADS = 8
A_WIDTH = A_HEADS * A_HEAD
A_RANK_W = 64
A_RANK_A = 64
A_RANK_G = 128
A_PROJ = 3 * A_WIDTH + A_RANK_W + A_RANK_A + A_RANK_G
A_LNX_EPS = 64e-5
A_PAIRS = A_HEADS // 2
B_HEADS = 4
B_HEAD = 128
B_WIDTH = B_HEADS * B_HEAD
CONV_K = 4
CONV_CH = 3 * B_WIDTH
B_PROJ = CONV_CH + 2 * B_HEADS + B_WIDTH
GATE_COLS = 2 * D_MODEL
LANES = 128
SUBLANES = 8
VMEM_LIMIT_BYTES = 56 * 1024 * 1024
CHUNK = 64
PA_R, PA_K, PA_V, PA_WA, PA_G = 0, A_WIDTH, 2 * A_WIDTH, 3 * A_WIDTH, 3 * A_WIDTH + A_RANK_W + A_RANK_A
PROJ_SPLITS = (A_PROJ, CONV_CH, B_WIDTH, GATE_COLS, LANES)


def _cparams(n_grid_dims):
    return pltpu.CompilerParams(dimension_semantics=("arbitrary",) * n_grid_dims,
                                vmem_limit_bytes=VMEM_LIMIT_BYTES)


def _const_spec(shape):
    nd = len(shape)
    return pl.BlockSpec(shape, lambda *_: (0,) * nd, pipeline_mode=pl.Buffered(1))


def _dot(a, b):
    return jnp.dot(a, b, preferred_element_type=F32)


def _dot_nt(a, b):
    return lax.dot_general(a, b, (((1,), (1,)), ((), ())), preferred_element_type=F32)


def _split(x):
    hi = x.astype(BF16)
    lo = (x - hi.astype(F32)).astype(BF16)
    return hi, lo


def _split3(x):
    hi = x.astype(BF16)
    rest = x - hi.astype(F32)
    mid = rest.astype(BF16)
    lo = (rest - mid.astype(F32)).astype(BF16)
    return hi, mid, lo


def _mm(a, b):
    return _dot(a.astype(BF16), b.astype(BF16))


def _mm_nt(a, b):
    return _dot_nt(a.astype(BF16), b.astype(BF16))


def _sel_mm(sel, x):
    return _dot(jnp.concatenate([sel, sel, sel], axis=1), jnp.concatenate(_split3(x), axis=0))


def _mm_sel(x, sel, pieces=3):
    parts = _split3(x) if pieces == 3 else _split(x)
    return _dot(jnp.concatenate(parts, axis=1), jnp.concatenate([sel] * pieces, axis=0))


INV_BASE = 8


def _nilpotent_inverse(n, eye):
    width = n.shape[1]
    bi, bj = _iota(n.shape, 0), _iota(n.shape, 1)
    same = lambda size: _group(bi, size) == _group(bj, size)
    d = jnp.where(same(INV_BASE), n, 0.0)
    t = eye + d
    d = _mm(d, d)
    yield
    for _ in range(INV_BASE.bit_length() - 3):
        both = _mm(d, jnp.concatenate([t, d], axis=1))
        yield
        t = t + both[:, :width]
        d = both[:, width:]
    t = t + _mm(d, t)
    yield
    size = INV_BASE
    while size < CHUNK:
        coupling = jnp.where(same(2 * size) & jnp.logical_not(same(size)), n, 0.0)
        tb = t.astype(BF16)
        lt = _mm(coupling, tb)
        yield
        t = t + _mm(tb, lt)
        yield
        size *= 2
    return t


def _round_robin(chains):
    chains = list(chains)
    while chains:
        for chain in list(chains):
            try:
                next(chain)
            except StopIteration:
                chains.remove(chain)


def _rms(x, w):
    return x * lax.rsqrt(jnp.mean(x * x, axis=-1, keepdims=True) + RMS_EPS) * w


def _sigmoid(x):
    return 1.0 / (1.0 + jnp.exp(-x))


def _silu(x):
    return x * _sigmoid(x)


def _softplus(x):
    return jnp.maximum(x, 0.0) + jnp.log(1.0 + jnp.exp(-jnp.abs(x)))


def _iota(shape, dim):
    return lax.broadcasted_iota(jnp.int32, shape, dim)


def _group(idx, size):
    assert size & (size - 1) == 0
    return lax.shift_right_logical(idx, size.bit_length() - 1)


def _one_hot(cond):
    return jnp.where(cond, 1.0, 0.0).astype(BF16)


def _rows_to_tile(rows):
    rid = _iota((SUBLANES, 1), 0)
    tile = jnp.zeros((SUBLANES, rows[0].shape[1]), F32)
    for i, row in enumerate(rows):
        tile = jnp.where(rid == i, row, tile)
    return tile


def _swiglu_half_step(x, nw, wg_ref, wu_ref, wd_ref):
    xn = _rms(x, nw).astype(BF16)
    g = _dot(xn, wg_ref[...])
    u = _dot(xn, wu_ref[...])
    act = (_silu(g) * u).astype(BF16)
    return x + 0.5 * _dot(act, wd_ref[...])


def _ffn_body(x_ref, nw_ref, wg_ref, wu_ref, wd_ref, o_ref):
    o_ref[...] = _swiglu_half_step(x_ref[...], nw_ref[...], wg_ref, wu_ref, wd_ref)


def _ffn(x, nw, wg, wu, wd, tm):
    n = x.shape[0]
    return pl.pallas_call(
        _ffn_body,
        grid=(n // tm,),
        in_specs=[pl.BlockSpec((tm, D_MODEL), lambda i: (i, 0)),
                  _const_spec((1, D_MODEL)),
                  _const_spec((D_MODEL, D_FF)), _const_spec((D_MODEL, D_FF)), _const_spec((D_FF, D_MODEL))],
        out_specs=pl.BlockSpec((tm, D_MODEL), lambda i: (i, 0)),
        out_shape=jax.ShapeDtypeStruct((n, D_MODEL), F32),
        compiler_params=_cparams(1),
        name="ffn1",
    )(x, nw, wg, wu, wd)


def _proj_body(h_ref, nw_ref, w_ref, *o_refs):
    u = _rms(h_ref[...], nw_ref[...]).astype(BF16)
    off = 0
    for o_ref, width in zip(o_refs, PROJ_SPLITS):
        o_ref[...] = _dot(u, w_ref[:, off:off + width])
        off += width


def _proj(h, nw, w_all, tm):
    n = h.shape[0]
    cols = sum(PROJ_SPLITS)
    return pl.pallas_call(
        _proj_body,
        grid=(n // tm,),
        in_specs=[pl.BlockSpec((tm, D_MODEL), lambda i: (i, 0)),
                  _const_spec((1, D_MODEL)), _const_spec((D_MODEL, cols))],
        out_specs=[pl.BlockSpec((tm, w), lambda i: (i, 0)) for w in PROJ_SPLITS],
        out_shape=[jax.ShapeDtypeStruct((n, w), F32) for w in PROJ_SPLITS],
        compiler_params=_cparams(1),
        name="proj",
    )(h, nw, w_all)


def _tail_body(h_ref, oa_ref, ob_ref, gates_ref, pa_ref, pb_ref, wo_ref, nw_ref, wg_ref, wu_ref, wd_ref,
               fn_ref, o_ref):
    ma = _dot(oa_ref[...].astype(BF16), pa_ref[...])
    mb = _dot(ob_ref[...].astype(BF16), pb_ref[...])
    merged = _sigmoid(gates_ref[:, :D_MODEL]) * ma + _sigmoid(gates_ref[:, D_MODEL:]) * mb
    h = h_ref[...] + _dot(merged.astype(BF16), wo_ref[...])
    h = _swiglu_half_step(h, nw_ref[...], wg_ref, wu_ref, wd_ref)
    o_ref[...] = _rms(h, fn_ref[...])


def _tail(h, oa, ob, gates, proj_a, proj_b, w_out, nw, wg, wu, wd, fn, tm):
    n = h.shape[0]
    row = lambda w: pl.BlockSpec((tm, w), lambda i: (i, 0))
    return pl.pallas_call(
        _tail_body,
        grid=(n // tm,),
        in_specs=[row(D_MODEL), row(A_WIDTH), row(B_WIDTH), row(GATE_COLS),
                  _const_spec((A_WIDTH, D_MODEL)), _const_spec((B_WIDTH, D_MODEL)),
                  _const_spec((D_MODEL, D_MODEL)), _const_spec((1, D_MODEL)),
                  _const_spec((D_MODEL, D_FF)), _const_spec((D_MODEL, D_FF)), _const_spec((D_FF, D_MODEL)),
                  _const_spec((1, D_MODEL))],
        out_specs=row(D_MODEL),
        out_shape=jax.ShapeDtypeStruct((n, D_MODEL), F32),
        compiler_params=_cparams(1),
        name="tail",
    )(h, oa, ob, gates, proj_a, proj_b, w_out, nw, wg, wu, wd, fn)


def _drain(chain):
    try:
        while True:
            next(chain)
    except StopIteration as stop:
        return stop.value


def _rwkv_token_math(x, prev, mu, w0, a0, k_k, k_a, w2a, g2):
    pm = x + (prev - x) * mu
    r = pm[:, PA_R:PA_R + A_WIDTH]
    k = pm[:, PA_K:PA_K + A_WIDTH]
    v = pm[:, PA_V:PA_V + A_WIDTH]
    wa = pm[:, PA_WA:PA_WA + LANES]
    gd = pm[:, PA_G:PA_G + A_RANK_G]
    lane = _iota((1, LANES), 1)
    lora_in = jnp.where(lane < A_RANK_W, jnp.tanh(wa), wa)
    lora = _mm(lora_in, w2a)
    g = _mm(_sigmoid(gd), g2)
    yield
    w_log = -_softplus(-(w0 + lora[:, :A_WIDTH])) - 0.5
    log_decay = -jnp.exp(w_log)
    yield
    a = _sigmoid(a0 + lora[:, A_WIDTH:])
    kk_raw = k * k_k
    k_mod = k * (1.0 + (a - 1.0) * k_a)
    return r, k_mod, v, kk_raw, a, log_decay, g


def _pair_mask(rows_per_head):
    shape = (2 * rows_per_head, LANES)
    return _group(_iota(shape, 0), rows_per_head) == _group(_iota(shape, 1), A_HEAD)


def _rwkv_prompt_part(pa_ref, mu_ref, w0_ref, a0_ref, kk_ref, ka_ref, rk_ref, lnw_ref, lnb_ref, w2a_ref, g2_ref,
                      o_ref, carry_ref, state_ref, r_s, k_s, v_s, kkraw_s, a_s, cum_s, ld_s, g_s):
    tt = pa_ref.shape[0]
    C = CHUNK
    lower = _one_hot(_iota((C, C), 1) <= _iota((C, C), 0))

    def token_chain(r0):
        rows = slice(r0, r0 + C)
        x = pa_ref[rows, :]
        before = carry_ref[SUBLANES - 1:SUBLANES, :] if r0 == 0 else pa_ref[r0 - 1:r0, :]
        prev = jnp.where(_iota((C, 1), 0) == 0, before, pltpu.roll(x, 1, axis=0))
        if r0 + C == tt:
            carry_ref[...] = x[C - SUBLANES:, :]
        r, k_mod, v, kk_raw, a, log_decay, g = yield from _rwkv_token_math(
            x, prev, mu_ref[...], w0_ref[...], a0_ref[...], kk_ref[...], ka_ref[...], w2a_ref[...], g2_ref[...])
        r_s[rows, :] = r
        k_s[rows, :] = k_mod
        v_s[rows, :] = v
        kkraw_s[rows, :] = kk_raw
        a_s[rows, :] = a
        g_s[rows, :] = g
        ld_s[rows, :] = log_decay
        yield
        cum_s[rows, :] = sum(_dot(lower, piece) for piece in _split3(log_decay))
        yield

    mask = _pair_mask(C)
    i2, j2 = _iota((2 * C, 2 * C), 0), _iota((2 * C, 2 * C), 1)
    strict = i2 > j2
    incl = i2 >= j2
    eye = jnp.where(i2 == j2, 1.0, 0.0)
    dup = lambda m: jnp.concatenate([m, m], axis=0)
    stack = lambda m: jnp.where(mask, dup(m), 0.0)

    def solve_chain(p, r0, stash):
        sl = slice(p * LANES, (p + 1) * LANES)
        ld = lambda ref: ref[pl.ds(r0, C), sl]
        r_p, k_p, v_p, a_p, cum, ldec = ld(r_s), ld(k_s), ld(v_s), ld(a_s), ld(cum_s), ld(ld_s)
        einc = jnp.exp(cum)
        eex = jnp.exp(cum - ldec)
        einv = jnp.exp(-cum)
        etail = jnp.exp(cum[C - 1:C, :] - cum)
        kks = stack(ld(kkraw_s))
        kks = kks * jnp.minimum(lax.rsqrt(jnp.sum(kks * kks, axis=-1, keepdims=True)), 1e12)
        As = -kks * dup(eex)
        Bs = kks * dup(a_p * einv)
        Bh = kks * dup(a_p * etail)
        Ks = stack(k_p * einv)
        Kh = stack(k_p * etail)
        Rs = stack(r_p * einc)
        Vs = stack(v_p)
        AR = jnp.concatenate([As, Rs], axis=0).astype(BF16)
        Vb = Vs.astype(BF16)
        G = _mm_nt(AR, jnp.concatenate([Bs, Ks], axis=0))
        yield
        Aab = jnp.where(strict, G[:2 * C, :2 * C], 0.0)
        Aak = jnp.where(strict, G[:2 * C, 2 * C:], 0.0)
        Arb = jnp.where(incl, G[2 * C:, :2 * C], 0.0)
        Ark = jnp.where(incl, G[2 * C:, 2 * C:], 0.0)
        Y = _mm(Aak, Vb)
        yield
        T = yield from _nilpotent_inverse(Aab, eye)
        WU = _mm(T, jnp.concatenate([AR[:2 * C], Y.astype(BF16)], axis=1))
        yield
        bonus = jnp.sum(stack(r_p * k_p * rk_ref[:, sl]), axis=-1, keepdims=True) * Vs
        stash[p] = dict(WU=WU, R=AR[2 * C:], Vs=Vs, bonus=bonus,
                        Aro=jnp.concatenate([Arb, Ark], axis=1).astype(BF16),
                        BKh=jnp.concatenate([Bh, Kh], axis=0).astype(BF16), decay=einc[C - 1:C, :])

    def state_chain(p, r0, stash):
        sl = slice(p * LANES, (p + 1) * LANES)
        s = stash[p]
        S = state_ref[p]
        Sb = S.astype(BF16)
        W = _mm_nt(s["WU"][:, :LANES], Sb) + s["WU"][:, LANES:]
        yield
        WV = jnp.concatenate([W, s["Vs"]], axis=0)
        O = _dot_nt(s["R"], Sb) + _mm(s["Aro"], WV)
        state_ref[p] = S * s["decay"] + _mm(WV.T, s["BKh"])
        yield
        mean = jnp.sum(O, axis=-1, keepdims=True) * (1.0 / A_HEAD)
        cen = jnp.where(mask, O - mean, 0.0)
        var = jnp.sum(cen * cen, axis=-1, keepdims=True) * (1.0 / A_HEAD)
        normed = jnp.where(mask, cen * lax.rsqrt(var + A_LNX_EPS) * lnw_ref[:, sl] + lnb_ref[:, sl], 0.0)
        full = normed + s["bonus"]
        o_ref[pl.ds(r0, C), sl] = (full[:C] + full[C:]) * g_s[pl.ds(r0, C), sl]

    return (lambda r0: [token_chain(r0)],
            lambda r0, stash: [solve_chain(p, r0, stash) for p in range(A_PAIRS)],
            lambda r0, stash: [state_chain(p, r0, stash) for p in range(A_PAIRS)])


CHUNKS_IN_FLIGHT = 2
N_RWKV_PARAMS = 10
N_GDN_PARAMS = 4
N_RWKV_SCRATCH = 8
N_GDN_SCRATCH = 5


def _mix_prompt_body(pa_ref, qkv_ref, ab_ref, z_ref, *refs):
    refs = list(refs)
    take = lambda n: [refs.pop(0) for _ in range(n)]
    rwkv_prm, gdn_prm = take(N_RWKV_PARAMS), take(N_GDN_PARAMS)
    oa_ref, ob_ref, sfa_ref, sfb_ref = take(4)
    carry_a, state_a, carry_b, state_b = take(4)
    rwkv_scr, gdn_scr = take(N_RWKV_SCRATCH), take(N_GDN_SCRATCH)
    t = pl.program_id(1)
    tt = pa_ref.shape[0]

    @pl.when(t == 0)
    def _():
        for ref in (carry_a, state_a, carry_b, state_b):
            ref[...] = jnp.zeros_like(ref)

    rwkv_token, rwkv_solve, rwkv_state = _rwkv_prompt_part(pa_ref, *rwkv_prm, oa_ref, carry_a, state_a, *rwkv_scr)
    gdn_token, gdn_solve, gdn_state = _gdn_prompt_part(qkv_ref, ab_ref, z_ref, *gdn_prm, ob_ref, carry_b, state_b,
                                                       *gdn_scr)

    n_chunks = tt // CHUNK
    group = min(CHUNKS_IN_FLIGHT, n_chunks)
    n_groups = n_chunks // group
    stashes = [({}, {}) for _ in range(n_chunks)]
    chunks_of = lambda gi: range(gi * group, (gi + 1) * group) if 0 <= gi < n_groups else ()

    def in_sequence(per_chunk_chains):
        for chains in zip(*per_chunk_chains):
            for chain in chains:
                yield from chain

    for gi in range(n_groups + 2):
        chains = []
        for c in chunks_of(gi):
            chains += rwkv_token(c * CHUNK) + gdn_token(c * CHUNK)
        for c in chunks_of(gi - 1):
            chains += rwkv_solve(c * CHUNK, stashes[c][0]) + gdn_solve(c * CHUNK, stashes[c][1])
        state_chains = [rwkv_state(c * CHUNK, stashes[c][0]) + gdn_state(c * CHUNK, stashes[c][1])
                        for c in chunks_of(gi - 2)]
        if state_chains:
            chains += [in_sequence([per_chunk[i:i + 1] for per_chunk in state_chains])
                       for i in range(len(state_chains[0]))]
        _round_robin(chains)

    @pl.when(t == pl.num_programs(1) - 1)
    def _():
        sfa_ref[0] = state_a[...]
        sfb_ref[0] = state_b[...]


def _mix_prompt(pa, qkv, ab, z, B, T, rwkv_params, gdn_params, tt):
    n = pa.shape[0]
    nt = T // tt
    assert len(rwkv_params) == N_RWKV_PARAMS and len(gdn_params) == N_GDN_PARAMS
    rows = lambda w: pl.BlockSpec((tt, w), lambda b, t: (b * nt + t, 0))
    state = lambda: pl.BlockSpec((1, 4, LANES, LANES), lambda b, t: (b, 0, 0, 0))
    big = lambda: pltpu.VMEM((tt, A_WIDTH), F32)
    return pl.pallas_call(
        _mix_prompt_body,
        grid=(B, nt),
        in_specs=[rows(A_PROJ), rows(CONV_CH), rows(LANES), rows(B_WIDTH)]
                 + [_const_spec(p.shape) for p in rwkv_params + gdn_params],
        out_specs=[rows(A_WIDTH), rows(B_WIDTH), state(), state()],
        out_shape=[jax.ShapeDtypeStruct((n, A_WIDTH), F32), jax.ShapeDtypeStruct((n, B_WIDTH), F32),
                   jax.ShapeDtypeStruct((B, A_PAIRS, LANES, LANES), F32),
                   jax.ShapeDtypeStruct((B, B_HEADS, B_HEAD, B_HEAD), F32)],
        scratch_shapes=[pltpu.VMEM((SUBLANES, A_PROJ), F32), pltpu.VMEM((A_PAIRS, LANES, LANES), F32),
                        pltpu.VMEM((SUBLANES, CONV_CH), F32), pltpu.VMEM((B_HEADS, B_HEAD, B_HEAD), F32)]
                       + [big() for _ in range(N_RWKV_SCRATCH + N_GDN_SCRATCH)],
        compiler_params=_cparams(2),
        name="mix_prompt",
    )(pa, qkv, ab, z, *rwkv_params, *gdn_params)


def _first_step_rows(rows, seqs, steps, state_rows, offset=0):
    hist = state_rows.shape[0] // seqs
    r, c = _iota((rows, seqs * hist), 0), _iota((rows, seqs * hist), 1)
    t = r & (steps - 1)
    sel = _one_hot((c == _group(r, steps) * hist + offset + t) & (t < hist - offset))
    return sum(_dot(sel, piece) for piece in _split3(state_rows))


def _rwkv_sample_body(pa_ref, shift_ref, s_ref, mu_ref, w0_ref, a0_ref, kk_ref, ka_ref, rk_ref, lnw_ref, lnb_ref,
                      w2a_ref, g2_ref, o_ref, sout_ref, r_s, nkk_s, beta_s, dec_s, k_s, v_s, o_s, *, steps):
    rows = pa_ref.shape[0]
    seqs = rows // steps
    per_tile = SUBLANES // steps
    x = pa_ref[...]
    t_idx = _iota((rows, 1), 0) & (steps - 1)
    prev = jnp.where(t_idx == 0, _first_step_rows(rows, seqs, steps, shift_ref[...]), pltpu.roll(x, 1, axis=0))
    r, k_mod, v, kk_raw, a, log_decay, g = _drain(_rwkv_token_math(
        x, prev, mu_ref[...], w0_ref[...], a0_ref[...], kk_ref[...], ka_ref[...],
        w2a_ref[...], g2_ref[...]))
    hi, hj = _iota((A_WIDTH, A_WIDTH), 0), _iota((A_WIDTH, A_WIDTH), 1)
    head_ones = _one_hot(_group(hi, A_HEAD) == _group(hj, A_HEAD))
    head_sum = lambda m: _mm_sel(m, head_ones)
    kk = kk_raw / jnp.maximum(jnp.sqrt(head_sum(kk_raw * kk_raw)), 1e-12)
    r_s[...] = r
    nkk_s[...] = -kk
    beta_s[...] = kk * a
    dec_s[...] = jnp.exp(log_decay)
    k_s[...] = k_mod
    v_s[...] = v
    pair_ones = head_ones[:LANES, :LANES]
    diag = _iota((A_HEAD, LANES), 0) == (_iota((A_HEAD, LANES), 1) & (A_HEAD - 1))

    def tile(j, carry):
        r0 = pl.multiple_of(j * SUBLANES, SUBLANES)
        ld = lambda ref: ref[pl.ds(r0, SUBLANES), :]
        r_t, nkk_t, beta_t, dec_t, k_t, v_t = ld(r_s), ld(nkk_s), ld(beta_s), ld(dec_s), ld(k_s), ld(v_s)
        out_rows = [[None] * A_PAIRS for _ in range(SUBLANES)]

        def seq_pair_chain(i, p):
            sl = slice(p * LANES, (p + 1) * LANES)
            rowof = lambda m, rr: m[rr:rr + 1, sl]
            v_diag = jnp.concatenate([jnp.where(diag, rowof(v_t, i * steps + s), 0.0) for s in range(steps)],
                                     axis=0)
            v_cols = _mm_sel(v_diag, pair_ones, pieces=2)
            b = j * per_tile + i
            S = jnp.concatenate([s_ref[b, 2 * p], s_ref[b, 2 * p + 1]], axis=1)
            for s in range(steps):
                rr = i * steps + s
                sa = _mm(S * rowof(nkk_t, rr), pair_ones)
                yield
                S = (S * rowof(dec_t, rr) + sa * rowof(beta_t, rr)
                     + v_cols[s * A_HEAD:(s + 1) * A_HEAD] * rowof(k_t, rr))
                out = _mm(S * rowof(r_t, rr), pair_ones)
                out_rows[rr][p] = jnp.sum(jnp.where(diag, out, 0.0), axis=0, keepdims=True)
            sout_ref[b, 2 * p] = S[:, :A_HEAD]
            sout_ref[b, 2 * p + 1] = S[:, A_HEAD:]

        _round_robin(seq_pair_chain(i, p) for i in range(per_tile) for p in range(A_PAIRS))
        o_s[pl.ds(r0, SUBLANES), :] = _rows_to_tile([jnp.concatenate(parts, axis=1) for parts in out_rows])
        return carry

    lax.fori_loop(0, rows // SUBLANES, tile, 0)
    o = o_s[...]
    mean = head_sum(o) * (1.0 / A_HEAD)
    cen = o - mean
    var = head_sum(cen * cen) * (1.0 / A_HEAD)
    o = cen * lax.rsqrt(var + A_LNX_EPS) * lnw_ref[...] + lnb_ref[...]
    o_ref[...] = (o + head_sum(r * k_mod * rk_ref[...]) * v) * g


def _rwkv_sample(pa, shift, state, steps, params, seqs):
    n = pa.shape[0]
    nb = state.shape[0]
    rows = seqs * steps
    assert SUBLANES % steps == 0 and rows % SUBLANES == 0 and seqs % SUBLANES == 0
    sspec = pl.BlockSpec((seqs, A_HEADS, A_HEAD, A_HEAD), lambda i: (i, 0, 0, 0))
    return pl.pallas_call(
        functools.partial(_rwkv_sample_body, steps=steps),
        grid=(nb // seqs,),
        in_specs=[pl.BlockSpec((rows, A_PROJ), lambda i: (i, 0)), pl.BlockSpec((seqs, A_PROJ), lambda i: (i, 0)),
                  sspec] + [_const_spec(p.shape) for p in params],
        out_specs=[pl.BlockSpec((rows, A_WIDTH), lambda i: (i, 0)), sspec],
        out_shape=[jax.ShapeDtypeStruct((n, A_WIDTH), F32), jax.ShapeDtypeStruct(state.shape, F32)],
        scratch_shapes=[pltpu.VMEM((rows, A_WIDTH), F32) for _ in range(7)],
        compiler_params=_cparams(1),
        name="rwkv_sample",
    )(pa, shift, state, *params)


def _gdn_qkv(conv):
    c = _silu(conv)
    qs, ks = [], []
    for h in range(B_HEADS):
        q = c[:, h * B_HEAD:(h + 1) * B_HEAD]
        k = c[:, B_WIDTH + h * B_HEAD:B_WIDTH + (h + 1) * B_HEAD]
        qs.append(q * (lax.rsqrt(jnp.sum(q * q, axis=-1, keepdims=True) + 1e-6) * (B_HEAD ** -0.5)))
        ks.append(k * lax.rsqrt(jnp.sum(k * k, axis=-1, keepdims=True) + 1e-6))
    q = jnp.concatenate(qs, axis=1)
    k = jnp.concatenate(ks, axis=1)
    v = c[:, 2 * B_WIDTH:]
    return q, k, v


def _gdn_gates(ab, alog, dtb):
    lane = _iota((1, LANES), 1)
    g = -jnp.exp(alog) * _softplus(ab + dtb)
    beta = _sigmoid(ab)
    gb = jnp.where(lane < B_HEADS, g, beta)
    si, sj = _iota((LANES, 2 * B_WIDTH), 0), _iota((LANES, 2 * B_WIDTH), 1)
    spread = _mm_sel(gb, _one_hot(si == _group(sj, B_HEAD)))
    return spread[:, :B_WIDTH], spread[:, B_WIDTH:]


def _gdn_out(o, norm_w, z):
    return o * lax.rsqrt(jnp.mean(o * o, axis=-1, keepdims=True) + RMS_EPS) * norm_w * _silu(z)


def _gdn_prompt_part(qkv_ref, ab_ref, z_ref, cw_ref, alog_ref, dtb_ref, nw_ref, o_ref,
                     carry_ref, state_ref, q_s, k_s, v_s, gc_s, beta_s):
    tt = qkv_ref.shape[0]
    C = CHUNK
    g, beta = _gdn_gates(ab_ref[...], alog_ref[...], dtb_ref[...])
    ri, ci = _iota((tt, tt), 0), _iota((tt, tt), 1)
    beta_s[...] = beta
    gc_s[...] = _sel_mm(_one_hot((_group(ri, C) == _group(ci, C)) & (ci <= ri)), g)

    def token_chain(r0):
        rows = slice(r0, r0 + C)
        x = qkv_ref[rows, :]
        before = carry_ref[...] if r0 == 0 else qkv_ref[r0 - SUBLANES:r0, :]
        if r0 + C == tt:
            carry_ref[...] = x[C - SUBLANES:, :]
        row8 = _iota((SUBLANES, 1), 0)
        conv = x * cw_ref[CONV_K - 1:CONV_K, :]
        for i in range(1, CONV_K):
            xs = pltpu.roll(x, i, axis=0)
            top = jnp.where(row8 < i, pltpu.roll(before, i, axis=0), xs[:SUBLANES])
            xs = jnp.concatenate([top, xs[SUBLANES:]], axis=0)
            conv = conv + xs * cw_ref[CONV_K - 1 - i:CONV_K - i, :]
            yield
        q, k, v = _gdn_qkv(conv)
        q_s[rows, :] = q
        k_s[rows, :] = k
        v_s[rows, :] = v
        yield

    i2, j2 = _iota((2 * C, 2 * C), 0), _iota((2 * C, 2 * C), 1)
    same_head = _group(i2, C) == _group(j2, C)
    strict = same_head & (i2 > j2)
    incl = same_head & (i2 >= j2)
    eye = jnp.where(i2 == j2, 1.0, 0.0)
    first = _iota((2 * C, 1), 0) < C

    def solve_chain(pr, r0, stash):
        sls = [slice(h * B_HEAD, (h + 1) * B_HEAD) for h in (2 * pr, 2 * pr + 1)]
        ld = lambda ref: jnp.concatenate([ref[pl.ds(r0, C), sl] for sl in sls], axis=0)
        q_h, k_h, v_h, gc_h, beta_h = ld(q_s), ld(k_s), ld(v_s), ld(gc_s), ld(beta_s)
        diff = gc_h - gc_h.T
        dm = jnp.where(incl, jnp.exp(jnp.where(incl, diff, 0.0)), 0.0)
        kb = k_h * beta_h
        QK = _mm_nt(jnp.concatenate([kb, q_h], axis=0), k_h)
        yield
        N = -jnp.where(strict, QK[:2 * C] * dm, 0.0)
        qk = QK[2 * C:] * dm
        egc = jnp.exp(gc_h)
        X = jnp.concatenate([v_h * beta_h, kb * egc], axis=1)
        T = yield from _nilpotent_inverse(N, eye)
        UW = _mm(T, X)
        yield
        g_last = jnp.where(first, gc_h[C - 1:C, :], gc_h[2 * C - 1:2 * C, :])
        stash[pr] = dict(u=UW[:, :B_HEAD], w=UW[:, B_HEAD:].astype(BF16), qd=(q_h * egc).astype(BF16),
                         qk=qk.astype(BF16), k_dec_t=(k_h * jnp.exp(g_last - gc_h)).T.astype(BF16),
                         decay=[jnp.exp(gc_h[(j + 1) * C - 1:(j + 1) * C, :]) for j in range(2)])

    def state_chain(pr, r0, stash):
        heads = (2 * pr, 2 * pr + 1)
        sls = [slice(h * B_HEAD, (h + 1) * B_HEAD) for h in heads]
        s = stash[pr]
        wS, qS, S_old = [], [], []
        for j, h in enumerate(heads):
            S = state_ref[h]
            rows = slice(j * C, (j + 1) * C)
            wq = _dot(jnp.concatenate([s["w"][rows], s["qd"][rows]], axis=0), S.astype(BF16))
            wS.append(wq[:C])
            qS.append(wq[C:])
            S_old.append(S)
        yield
        v_new = s["u"] - jnp.concatenate(wS, axis=0)
        o = jnp.concatenate(qS, axis=0) + _dot(s["qk"], v_new.astype(BF16))
        for j, h in enumerate(heads):
            mine = first if j == 0 else jnp.logical_not(first)
            rows = slice(j * C, (j + 1) * C)
            state_ref[h] = S_old[j] * s["decay"][j] + _mm(s["k_dec_t"], jnp.where(mine, v_new, 0.0))
            o_ref[pl.ds(r0, C), sls[j]] = _gdn_out(o[rows], nw_ref[...], z_ref[pl.ds(r0, C), sls[j]])
        yield

    return (lambda r0: [token_chain(r0)],
            lambda r0, stash: [solve_chain(pr, r0, stash) for pr in range(B_HEADS // 2)],
            lambda r0, stash: [state_chain(pr, r0, stash) for pr in range(B_HEADS // 2)])


def _gdn_sample_body(qkv_ref, hist_ref, ab_ref, z_ref, s_ref, cw_ref, alog_ref, dtb_ref, nw_ref,
                     o_ref, sout_ref, q_s, k_s, v_s, eg_s, beta_s, o_s, *, steps):
    rows = ab_ref.shape[0]
    seqs = rows // steps
    per_tile = SUBLANES // steps
    x = qkv_ref[...]
    hist = hist_ref[...]
    t_idx = _iota((rows, 1), 0) & (steps - 1)
    conv = x * cw_ref[CONV_K - 1:CONV_K, :]
    for i in range(1, CONV_K):
        tap = jnp.where(t_idx >= i, pltpu.roll(x, i, axis=0),
                        _first_step_rows(rows, seqs, steps, hist, offset=CONV_K - 1 - i))
        conv = conv + tap * cw_ref[CONV_K - 1 - i:CONV_K - i, :]
    q, k, v = _gdn_qkv(conv)
    g, beta = _gdn_gates(ab_ref[...], alog_ref[...], dtb_ref[...])
    q_s[...] = q
    k_s[...] = k
    v_s[...] = v
    eg_s[...] = jnp.exp(g)
    beta_s[...] = beta
    diag = _iota((B_HEAD, B_HEAD), 0) == _iota((B_HEAD, B_HEAD), 1)
    ones = jnp.ones((B_HEAD, B_HEAD), BF16)

    def to_cols(tile_rows):
        diags = jnp.concatenate([jnp.where(diag, rv, 0.0) for rv in tile_rows], axis=0)
        return _mm(diags, ones)

    def tile(j, carry):
        r0 = pl.multiple_of(j * SUBLANES, SUBLANES)
        ld = lambda ref: ref[pl.ds(r0, SUBLANES), :]
        q_t, k_t, v_t, eg_t, beta_t = ld(q_s), ld(k_s), ld(v_s), ld(eg_s), ld(beta_s)
        out_rows = [[] for _ in range(SUBLANES)]
        for i in range(per_tile):
            for h in range(B_HEADS):
                sl = slice(h * B_HEAD, (h + 1) * B_HEAD)
                rowof = lambda m, rr: m[rr:rr + 1, sl]
                S = s_ref[j * per_tile + i, h]
                for s in range(steps):
                    rr = i * steps + s
                    kq_cols = to_cols([rowof(k_t, rr), rowof(q_t, rr)])
                    k_col, q_col = kq_cols[:B_HEAD], kq_cols[B_HEAD:]
                    S = S * rowof(eg_t, rr)
                    u = jnp.sum(k_col * S, axis=0, keepdims=True)
                    S = S + k_col * (rowof(beta_t, rr) * (rowof(v_t, rr) - u))
                    out_rows[rr].append(jnp.sum(q_col * S, axis=0, keepdims=True))
                sout_ref[j * per_tile + i, h] = S
        o_s[pl.ds(r0, SUBLANES), :] = _rows_to_tile([jnp.concatenate(parts, axis=1) for parts in out_rows])
        return carry

    lax.fori_loop(0, rows // SUBLANES, tile, 0)
    for h in range(B_HEADS):
        sl = slice(h * B_HEAD, (h + 1) * B_HEAD)
        o_ref[:, sl] = _gdn_out(o_s[:, sl], nw_ref[...], z_ref[:, sl])


def _gdn_sample(qkv, hist, ab, z, state, steps, params, seqs):
    n = ab.shape[0]
    nb = state.shape[0]
    rows = seqs * steps
    hist_rows = seqs * (CONV_K - 1)
    assert SUBLANES % steps == 0 and rows % SUBLANES == 0 and hist_rows % SUBLANES == 0
    row = lambda w: pl.BlockSpec((rows, w), lambda i: (i, 0))
    sspec = pl.BlockSpec((seqs, B_HEADS, B_HEAD, B_HEAD), lambda i: (i, 0, 0, 0))
    return pl.pallas_call(
        functools.partial(_gdn_sample_body, steps=steps),
        grid=(nb // seqs,),
        in_specs=[row(CONV_CH), pl.BlockSpec((hist_rows, CONV_CH), lambda i: (i, 0)), row(LANES), row(B_WIDTH), sspec]
                 + [_const_spec(p.shape) for p in params],
        out_specs=[row(B_WIDTH), sspec],
        out_shape=[jax.ShapeDtypeStruct((n, B_WIDTH), F32), jax.ShapeDtypeStruct(state.shape, F32)],
        scratch_shapes=[pltpu.VMEM((rows, B_WIDTH), F32) for _ in range(6)],
        compiler_params=_cparams(1),
        name="gdn_sample",
    )(qkv, hist, ab, z, state, *params)


def _cuts(widths):
    edges, total = [], 0
    for w in widths[:-1]:
        total += w
        edges.append(total)
    return edges


def _regroup_pa(a):
    r, wd, k, v, ad, gd = jnp.split(a, _cuts((A_WIDTH, A_RANK_W, A_WIDTH, A_WIDTH, A_RANK_A, A_RANK_G)), axis=-1)
    return jnp.concatenate([r, k, v, wd, ad, gd], axis=-1)


def _ungroup_pa(a):
    r, k, v, wd, ad, gd = jnp.split(a, _cuts((A_WIDTH, A_WIDTH, A_WIDTH, A_RANK_W, A_RANK_A, A_RANK_G)), axis=-1)
    return jnp.concatenate([r, wd, k, v, ad, gd], axis=-1)


def _token_tile(n, want):
    tm = want
    while n % tm:
        tm //= 2
    return tm


def kernel(x_prompt, x_sample, state_rwkv, state_rwkv_shift, state_delta, state_conv, ffn1_norm, ffn1_w_gate, ffn1_w_up, ffn1_w_down, mix_norm, w_in, rwkv_mu, rwkv_w0, rwkv_w2, rwkv_a0, rwkv_a2, rwkv_g2, rwkv_k_k, rwkv_k_a, rwkv_r_k, rwkv_lnx_w, rwkv_lnx_b, gdn_conv_w, gdn_A_log, gdn_dt_bias, gdn_norm_w, proj_a, proj_b, w_out, ffn2_norm, ffn2_w_gate, ffn2_w_up, ffn2_w_down, final_norm):
    depth = ffn1_norm.shape[0]
    assert depth == 1, "single-layer trunk"
    Bp, Tp, _ = x_prompt.shape
    Bs, Ts, _ = x_sample.shape
    l = 0
    row = lambda a: a.reshape(1, -1).astype(F32)

    wi = w_in[l].astype(BF16)
    o_b = A_PROJ
    w_all = jnp.concatenate([
        _regroup_pa(wi[:, :A_PROJ]),
        wi[:, o_b:o_b + CONV_CH],
        wi[:, o_b + CONV_CH + 2 * B_HEADS:o_b + B_PROJ],
        wi[:, o_b + B_PROJ:],
        jnp.pad(wi[:, o_b + CONV_CH:o_b + CONV_CH + 2 * B_HEADS], ((0, 0), (0, LANES - 2 * B_HEADS))),
    ], axis=1).astype(BF16)
    ffn1 = (row(ffn1_norm[l]), ffn1_w_gate[l].astype(BF16), ffn1_w_up[l].astype(BF16), ffn1_w_down[l].astype(BF16))
    ffn2 = (row(ffn2_norm[l]), ffn2_w_gate[l].astype(BF16), ffn2_w_up[l].astype(BF16), ffn2_w_down[l].astype(BF16))
    merge_w = (proj_a[l].astype(BF16), proj_b[l].astype(BF16), w_out[l].astype(BF16))
    zw = jnp.zeros((A_RANK_W, A_WIDTH), F32)
    w2a = jnp.concatenate([jnp.concatenate([rwkv_w2[l], zw], axis=1),
                           jnp.concatenate([zw, rwkv_a2[l]], axis=1)], axis=0)
    rwkv_params = (row(_regroup_pa(rwkv_mu[l])), row(rwkv_w0[l]), row(rwkv_a0[l]), row(rwkv_k_k[l]), row(rwkv_k_a[l]),
                   row(rwkv_r_k[l]), row(rwkv_lnx_w[l]), row(rwkv_lnx_b[l]), w2a, rwkv_g2[l].astype(F32))
    pad_lane = lambda a: jnp.pad(a.reshape(1, -1).astype(F32), ((0, 0), (0, LANES - a.size)))
    gdn_params = (gdn_conv_w[l].astype(F32), pad_lane(gdn_A_log[l]), pad_lane(gdn_dt_bias[l]), row(gdn_norm_w[l]))

    def trunk_front(x2):
        n = x2.shape[0]
        h = _ffn(x2, *ffn1, tm=_token_tile(n, 512))
        return (h,) + tuple(_proj(h, row(mix_norm[l]), w_all, tm=_token_tile(n, 512)))

    def trunk_back(h, oa, ob, gates):
        n = h.shape[0]
        return _tail(h, oa, ob, gates, *merge_w, *ffn2, row(final_norm), tm=_token_tile(n, 256))

    xp = x_prompt.reshape(Bp * Tp, D_MODEL)
    h, pa, qkv, z, gates, ab = trunk_front(xp)
    tt = _token_tile(Tp, 256)
    oa, ob, s_pairs, delta_p = _mix_prompt(pa, qkv, ab, z, Bp, Tp, rwkv_params, gdn_params, tt)
    y_prompt = trunk_back(h, oa, ob, gates).reshape(Bp, Tp, D_MODEL)
    sp = s_pairs.reshape(Bp, A_PAIRS, 2, A_HEAD, 2, A_HEAD)
    rwkv_p = jnp.stack([sp[:, :, 0, :, 0], sp[:, :, 1, :, 1]], axis=2).reshape(Bp, A_HEADS, A_HEAD, A_HEAD)
    shift_p = _ungroup_pa(pa.reshape(Bp, Tp, A_PROJ)[:, -1])
    conv_p = qkv.reshape(Bp, Tp, CONV_CH)[:, Tp - (CONV_K - 1):]

    xs = x_sample.reshape(Bs * Ts, D_MODEL)
    h, pa, qkv, z, gates, ab = trunk_front(xs)
    assert Ts & (Ts - 1) == 0 and Ts >= CONV_K - 1, "sample steps: power of two covering the conv history"
    oa, rwkv_s = _rwkv_sample(pa, _regroup_pa(state_rwkv_shift[l].astype(F32)), state_rwkv[l].astype(F32), Ts,
                              rwkv_params, seqs=_token_tile(Bs, 32))
    shift_s = _ungroup_pa(pa[Ts - 1::Ts])
    hist = state_conv[l].astype(F32).reshape(Bs * (CONV_K - 1), CONV_CH)
    ob, delta_s = _gdn_sample(qkv, hist, ab, z, state_delta[l].astype(F32), Ts, gdn_params, seqs=_token_tile(Bs, 16))
    conv_s = jnp.stack([qkv[Ts - (CONV_K - 1) + i::Ts] for i in range(CONV_K - 1)], axis=1)
    y_sample = trunk_back(h, oa, ob, gates).reshape(Bs, Ts, D_MODEL)

    add_depth = lambda a: a[None]
    return (y_prompt, y_sample,
            add_depth(rwkv_p), add_depth(shift_p), add_depth(delta_p), add_depth(conv_p),
            add_depth(rwkv_s), add_depth(shift_s), add_depth(delta_s), add_depth(conv_s))
```

```python
import functools

import jax
import jax.numpy as jnp
from jax import lax
from jax.experimental import pallas as pl
from jax.experimental.pallas import tpu as pltpu

F32 = jnp.float32
BF16 = jnp.bfloat16

D_MODEL = 1024
D_FF = 2816
RMS_EPS = 1e-6
A_HEAD = 64
A_HEADS = 8
A_WIDTH = A_HEADS * A_HEAD
A_RANK_W = 64
A_RANK_A = 64
A_RANK_G = 128
A_PROJ = 3 * A_WIDTH + A_RANK_W + A_RANK_A + A_RANK_G
A_LNX_EPS = 64e-5
A_PAIRS = A_HEADS // 2
B_HEADS = 4
B_HEAD = 128
B_WIDTH = B_HEADS * B_HEAD
CONV_K = 4
CONV_CH = 3 * B_WIDTH
B_PROJ = CONV_CH + 2 * B_HEADS + B_WIDTH
GATE_COLS = 2 * D_MODEL
LANES = 128
SUBLANES = 8
VMEM_LIMIT_BYTES = 56 * 1024 * 1024
CHUNK = 64
PA_R, PA_K, PA_V, PA_WA, PA_G = 0, A_WIDTH, 2 * A_WIDTH, 3 * A_WIDTH, 3 * A_WIDTH + A_RANK_W + A_RANK_A
PROJ_SPLITS = (A_PROJ, CONV_CH, B_WIDTH, GATE_COLS, LANES)


def _cparams(n_grid_dims):
    return pltpu.CompilerParams(dimension_semantics=("arbitrary",) * n_grid_dims,
                                vmem_limit_bytes=VMEM_LIMIT_BYTES)


def _const_spec(shape):
    nd = len(shape)
    return pl.BlockSpec(shape, lambda *_: (0,) * nd, pipeline_mode=pl.Buffered(1))


def _dot(a, b):
    return jnp.dot(a, b, preferred_element_type=F32)


def _dot_nt(a, b):
    return lax.dot_general(a, b, (((1,), (1,)), ((), ())), preferred_element_type=F32)


def _split(x):
    hi = x.astype(BF16)
    lo = (x - hi.astype(F32)).astype(BF16)
    return hi, lo


def _split3(x):
    hi = x.astype(BF16)
    rest = x - hi.astype(F32)
    mid = rest.astype(BF16)
    lo = (rest - mid.astype(F32)).astype(BF16)
    return hi, mid, lo


def _mm(a, b):
    return _dot(a.astype(BF16), b.astype(BF16))


def _mm_nt(a, b):
    return _dot_nt(a.astype(BF16), b.astype(BF16))


def _sel_mm(sel, x):
    return _dot(jnp.concatenate([sel, sel, sel], axis=1), jnp.concatenate(_split3(x), axis=0))


def _mm_sel(x, sel, pieces=3):
    parts = _split3(x) if pieces == 3 else _split(x)
    return _dot(jnp.concatenate(parts, axis=1), jnp.concatenate([sel] * pieces, axis=0))


INV_BASE = 8


def _nilpotent_inverse(n, eye):
    width = n.shape[1]
    bi, bj = _iota(n.shape, 0), _iota(n.shape, 1)
    same = lambda size: _group(bi, size) == _group(bj, size)
    d = jnp.where(same(INV_BASE), n, 0.0)
    t = eye + d
    d = _mm(d, d)
    yield
    for _ in range(INV_BASE.bit_length() - 3):
        both = _mm(d, jnp.concatenate([t, d], axis=1))
        yield
        t = t + both[:, :width]
        d = both[:, width:]
    t = t + _mm(d, t)
    yield
    size = INV_BASE
    while size < CHUNK:
        coupling = jnp.where(same(2 * size) & jnp.logical_not(same(size)), n, 0.0)
        tb = t.astype(BF16)
        lt = _mm(coupling, tb)
        yield
        t = t + _mm(tb, lt)
        yield
        size *= 2
    return t


def _round_robin(chains):
    chains = list(chains)
    while chains:
        for chain in list(chains):
            try:
                next(chain)
            except StopIteration:
                chains.remove(chain)


def _rms(x, w):
    return x * lax.rsqrt(jnp.mean(x * x, axis=-1, keepdims=True) + RMS_EPS) * w


def _sigmoid(x):
    return 1.0 / (1.0 + jnp.exp(-x))


def _silu(x):
    return x * _sigmoid(x)


def _softplus(x):
    return jnp.maximum(x, 0.0) + jnp.log(1.0 + jnp.exp(-jnp.abs(x)))


def _iota(shape, dim):
    return lax.broadcasted_iota(jnp.int32, shape, dim)


def _group(idx, size):
    assert size & (size - 1) == 0
    return lax.shift_right_logical(idx, size.bit_length() - 1)


def _one_hot(cond):
    return jnp.where(cond, 1.0, 0.0).astype(BF16)


def _rows_to_tile(rows):
    rid = _iota((SUBLANES, 1), 0)
    tile = jnp.zeros((SUBLANES, rows[0].shape[1]), F32)
    for i, row in enumerate(rows):
        tile = jnp.where(rid == i, row, tile)
    return tile


def _swiglu_half_step(x, nw, wg_ref, wu_ref, wd_ref):
    xn = _rms(x, nw).astype(BF16)
    g = _dot(xn, wg_ref[...])
    u = _dot(xn, wu_ref[...])
    act = (_silu(g) * u).astype(BF16)
    return x + 0.5 * _dot(act, wd_ref[...])


def _ffn_body(x_ref, nw_ref, wg_ref, wu_ref, wd_ref, o_ref):
    o_ref[...] = _swiglu_half_step(x_ref[...], nw_ref[...], wg_ref, wu_ref, wd_ref)


def _ffn(x, nw, wg, wu, wd, tm):
    n = x.shape[0]
    return pl.pallas_call(
        _ffn_body,
        grid=(n // tm,),
        in_specs=[pl.BlockSpec((tm, D_MODEL), lambda i: (i, 0)),
                  _const_spec((1, D_MODEL)),
                  _const_spec((D_MODEL, D_FF)), _const_spec((D_MODEL, D_FF)), _const_spec((D_FF, D_MODEL))],
        out_specs=pl.BlockSpec((tm, D_MODEL), lambda i: (i, 0)),
        out_shape=jax.ShapeDtypeStruct((n, D_MODEL), F32),
        compiler_params=_cparams(1),
        name="ffn1",
    )(x, nw, wg, wu, wd)


def _proj_body(h_ref, nw_ref, w_ref, *o_refs):
    u = _rms(h_ref[...], nw_ref[...]).astype(BF16)
    off = 0
    for o_ref, width in zip(o_refs, PROJ_SPLITS):
        o_ref[...] = _dot(u, w_ref[:, off:off + width])
        off += width


def _proj(h, nw, w_all, tm):
    n = h.shape[0]
    cols = sum(PROJ_SPLITS)
    return pl.pallas_call(
        _proj_body,
        grid=(n // tm,),
        in_specs=[pl.BlockSpec((tm, D_MODEL), lambda i: (i, 0)),
                  _const_spec((1, D_MODEL)), _const_spec((D_MODEL, cols))],
        out_specs=[pl.BlockSpec((tm, w), lambda i: (i, 0)) for w in PROJ_SPLITS],
        out_shape=[jax.ShapeDtypeStruct((n, w), F32) for w in PROJ_SPLITS],
        compiler_params=_cparams(1),
        name="proj",
    )(h, nw, w_all)


def _tail_body(h_ref, oa_ref, ob_ref, gates_ref, pa_ref, pb_ref, wo_ref, nw_ref, wg_ref, wu_ref, wd_ref,
               fn_ref, o_ref):
    ma = _dot(oa_ref[...].astype(BF16), pa_ref[...])
    mb = _dot(ob_ref[...].astype(BF16), pb_ref[...])
    merged = _sigmoid(gates_ref[:, :D_MODEL]) * ma + _sigmoid(gates_ref[:, D_MODEL:]) * mb
    h = h_ref[...] + _dot(merged.astype(BF16), wo_ref[...])
    h = _swiglu_half_step(h, nw_ref[...], wg_ref, wu_ref, wd_ref)
    o_ref[...] = _rms(h, fn_ref[...])


def _tail(h, oa, ob, gates, proj_a, proj_b, w_out, nw, wg, wu, wd, fn, tm):
    n = h.shape[0]
    row = lambda w: pl.BlockSpec((tm, w), lambda i: (i, 0))
    return pl.pallas_call(
        _tail_body,
        grid=(n // tm,),
        in_specs=[row(D_MODEL), row(A_WIDTH), row(B_WIDTH), row(GATE_COLS),
                  _const_spec((A_WIDTH, D_MODEL)), _const_spec((B_WIDTH, D_MODEL)),
                  _const_spec((D_MODEL, D_MODEL)), _const_spec((1, D_MODEL)),
                  _const_spec((D_MODEL, D_FF)), _const_spec((D_MODEL, D_FF)), _const_spec((D_FF, D_MODEL)),
                  _const_spec((1, D_MODEL))],
        out_specs=row(D_MODEL),
        out_shape=jax.ShapeDtypeStruct((n, D_MODEL), F32),
        compiler_params=_cparams(1),
        name="tail",
    )(h, oa, ob, gates, proj_a, proj_b, w_out, nw, wg, wu, wd, fn)


def _drain(chain):
    try:
        while True:
            next(chain)
    except StopIteration as stop:
        return stop.value


def _rwkv_token_math(x, prev, mu, w0, a0, k_k, k_a, w2a, g2):
    pm = x + (prev - x) * mu
    r = pm[:, PA_R:PA_R + A_WIDTH]
    k = pm[:, PA_K:PA_K + A_WIDTH]
    v = pm[:, PA_V:PA_V + A_WIDTH]
    wa = pm[:, PA_WA:PA_WA + LANES]
    gd = pm[:, PA_G:PA_G + A_RANK_G]
    lane = _iota((1, LANES), 1)
    lora_in = jnp.where(lane < A_RANK_W, jnp.tanh(wa), wa)
    lora = _mm(lora_in, w2a)
    g = _mm(_sigmoid(gd), g2)
    yield
    w_log = -_softplus(-(w0 + lora[:, :A_WIDTH])) - 0.5
    log_decay = -jnp.exp(w_log)
    yield
    a = _sigmoid(a0 + lora[:, A_WIDTH:])
    kk_raw = k * k_k
    k_mod = k * (1.0 + (a - 1.0) * k_a)
    return r, k_mod, v, kk_raw, a, log_decay, g


def _pair_mask(rows_per_head):
    shape = (2 * rows_per_head, LANES)
    return _group(_iota(shape, 0), rows_per_head) == _group(_iota(shape, 1), A_HEAD)


def _rwkv_prompt_part(pa_ref, mu_ref, w0_ref, a0_ref, kk_ref, ka_ref, rk_ref, lnw_ref, lnb_ref, w2a_ref, g2_ref,
                      o_ref, carry_ref, state_ref, r_s, k_s, v_s, kkraw_s, a_s, cum_s, ld_s, g_s):
    tt = pa_ref.shape[0]
    C = CHUNK
    lower = _one_hot(_iota((C, C), 1) <= _iota((C, C), 0))

    def token_chain(r0):
        rows = slice(r0, r0 + C)
        x = pa_ref[rows, :]
        before = carry_ref[SUBLANES - 1:SUBLANES, :] if r0 == 0 else pa_ref[r0 - 1:r0, :]
        prev = jnp.where(_iota((C, 1), 0) == 0, before, pltpu.roll(x, 1, axis=0))
        if r0 + C == tt:
            carry_ref[...] = x[C - SUBLANES:, :]
        r, k_mod, v, kk_raw, a, log_decay, g = yield from _rwkv_token_math(
            x, prev, mu_ref[...], w0_ref[...], a0_ref[...], kk_ref[...], ka_ref[...], w2a_ref[...], g2_ref[...])
        r_s[rows, :] = r
        k_s[rows, :] = k_mod
        v_s[rows, :] = v
        kkraw_s[rows, :] = kk_raw
        a_s[rows, :] = a
        g_s[rows, :] = g
        ld_s[rows, :] = log_decay
        yield
        cum_s[rows, :] = sum(_dot(lower, piece) for piece in _split3(log_decay))
        yield

    mask = _pair_mask(C)
    i2, j2 = _iota((2 * C, 2 * C), 0), _iota((2 * C, 2 * C), 1)
    strict = i2 > j2
    incl = i2 >= j2
    eye = jnp.where(i2 == j2, 1.0, 0.0)
    dup = lambda m: jnp.concatenate([m, m], axis=0)
    stack = lambda m: jnp.where(mask, dup(m), 0.0)

    def solve_chain(p, r0, stash):
        sl = slice(p * LANES, (p + 1) * LANES)
        ld = lambda ref: ref[pl.ds(r0, C), sl]
        r_p, k_p, v_p, a_p, cum, ldec = ld(r_s), ld(k_s), ld(v_s), ld(a_s), ld(cum_s), ld(ld_s)
        einc = jnp.exp(cum)
        eex = jnp.exp(cum - ldec)
        einv = jnp.exp(-cum)
        etail = jnp.exp(cum[C - 1:C, :] - cum)
        kks = stack(ld(kkraw_s))
        kks = kks * jnp.minimum(lax.rsqrt(jnp.sum(kks * kks, axis=-1, keepdims=True)), 1e12)
        As = -kks * dup(eex)
        Bs = kks * dup(a_p * einv)
        Bh = kks * dup(a_p * etail)
        Ks = stack(k_p * einv)
        Kh = stack(k_p * etail)
        Rs = stack(r_p * einc)
        Vs = stack(v_p)
        AR = jnp.concatenate([As, Rs], axis=0).astype(BF16)
        Vb = Vs.astype(BF16)
        G = _mm_nt(AR, jnp.concatenate([Bs, Ks], axis=0))
        yield
        Aab = jnp.where(strict, G[:2 * C, :2 * C], 0.0)
        Aak = jnp.where(strict, G[:2 * C, 2 * C:], 0.0)
        Arb = jnp.where(incl, G[2 * C:, :2 * C], 0.0)
        Ark = jnp.where(incl, G[2 * C:, 2 * C:], 0.0)
        Y = _mm(Aak, Vb)
        yield
        T = yield from _nilpotent_inverse(Aab, eye)
        WU = _mm(T, jnp.concatenate([AR[:2 * C], Y.astype(BF16)], axis=1))
        yield
        bonus = jnp.sum(stack(r_p * k_p * rk_ref[:, sl]), axis=-1, keepdims=True) * Vs
        stash[p] = dict(WU=WU, R=AR[2 * C:], Vs=Vs, bonus=bonus,
                        Aro=jnp.concatenate([Arb, Ark], axis=1).astype(BF16),
                        BKh=jnp.concatenate([Bh, Kh], axis=0).astype(BF16), decay=einc[C - 1:C, :])

    def state_chain(p, r0, stash):
        sl = slice(p * LANES, (p + 1) * LANES)
        s = stash[p]
        S = state_ref[p]
        Sb = S.astype(BF16)
        W = _mm_nt(s["WU"][:, :LANES], Sb) + s["WU"][:, LANES:]
        yield
        WV = jnp.concatenate([W, s["Vs"]], axis=0)
        O = _dot_nt(s["R"], Sb) + _mm(s["Aro"], WV)
        state_ref[p] = S * s["decay"] + _mm(WV.T, s["BKh"])
        yield
        mean = jnp.sum(O, axis=-1, keepdims=True) * (1.0 / A_HEAD)
        cen = jnp.where(mask, O - mean, 0.0)
        var = jnp.sum(cen * cen, axis=-1, keepdims=True) * (1.0 / A_HEAD)
        normed = jnp.where(mask, cen * lax.rsqrt(var + A_LNX_EPS) * lnw_ref[:, sl] + lnb_ref[:, sl], 0.0)
        full = normed + s["bonus"]
        o_ref[pl.ds(r0, C), sl] = (full[:C] + full[C:]) * g_s[pl.ds(r0, C), sl]

    return (lambda r0: [token_chain(r0)],
            lambda r0, stash: [solve_chain(p, r0, stash) for p in range(A_PAIRS)],
            lambda r0, stash: [state_chain(p, r0, stash) for p in range(A_PAIRS)])


CHUNKS_IN_FLIGHT = 2
SAMPLE_TILES = 2
N_RWKV_PARAMS = 10
N_GDN_PARAMS = 4
N_RWKV_SCRATCH = 8
N_GDN_SCRATCH = 5


def _mix_prompt_body(pa_ref, qkv_ref, ab_ref, z_ref, *refs):
    refs = list(refs)
    take = lambda n: [refs.pop(0) for _ in range(n)]
    rwkv_prm, gdn_prm = take(N_RWKV_PARAMS), take(N_GDN_PARAMS)
    oa_ref, ob_ref, sfa_ref, sfb_ref = take(4)
    carry_a, state_a, carry_b, state_b = take(4)
    rwkv_scr, gdn_scr = take(N_RWKV_SCRATCH), take(N_GDN_SCRATCH)
    t = pl.program_id(1)
    tt = pa_ref.shape[0]

    @pl.when(t == 0)
    def _():
        for ref in (carry_a, state_a, carry_b, state_b):
            ref[...] = jnp.zeros_like(ref)

    rwkv_token, rwkv_solve, rwkv_state = _rwkv_prompt_part(pa_ref, *rwkv_prm, oa_ref, carry_a, state_a, *rwkv_scr)
    gdn_token, gdn_solve, gdn_state = _gdn_prompt_part(qkv_ref, ab_ref, z_ref, *gdn_prm, ob_ref, carry_b, state_b,
                                                       *gdn_scr)

    n_chunks = tt // CHUNK
    group = min(CHUNKS_IN_FLIGHT, n_chunks)
    n_groups = n_chunks // group
    stashes = [({}, {}) for _ in range(n_chunks)]
    chunks_of = lambda gi: range(gi * group, (gi + 1) * group) if 0 <= gi < n_groups else ()

    def in_sequence(per_chunk_chains):
        for chains in zip(*per_chunk_chains):
            for chain in chains:
                yield from chain

    for gi in range(n_groups + 2):
        chains = []
        for c in chunks_of(gi):
            chains += rwkv_token(c * CHUNK) + gdn_token(c * CHUNK)
        for c in chunks_of(gi - 1):
            chains += rwkv_solve(c * CHUNK, stashes[c][0]) + gdn_solve(c * CHUNK, stashes[c][1])
        state_chains = [rwkv_state(c * CHUNK, stashes[c][0]) + gdn_state(c * CHUNK, stashes[c][1])
                        for c in chunks_of(gi - 2)]
        if state_chains:
            chains += [in_sequence([per_chunk[i:i + 1] for per_chunk in state_chains])
                       for i in range(len(state_chains[0]))]
        _round_robin(chains)

    @pl.when(t == pl.num_programs(1) - 1)
    def _():
        sfa_ref[0] = state_a[...]
        sfb_ref[0] = state_b[...]


def _mix_prompt(pa, qkv, ab, z, B, T, rwkv_params, gdn_params, tt):
    n = pa.shape[0]
    nt = T // tt
    assert len(rwkv_params) == N_RWKV_PARAMS and len(gdn_params) == N_GDN_PARAMS
    rows = lambda w: pl.BlockSpec((tt, w), lambda b, t: (b * nt + t, 0))
    state = lambda: pl.BlockSpec((1, 4, LANES, LANES), lambda b, t: (b, 0, 0, 0))
    big = lambda: pltpu.VMEM((tt, A_WIDTH), F32)
    return pl.pallas_call(
        _mix_prompt_body,
        grid=(B, nt),
        in_specs=[rows(A_PROJ), rows(CONV_CH), rows(LANES), rows(B_WIDTH)]
                 + [_const_spec(p.shape) for p in rwkv_params + gdn_params],
        out_specs=[rows(A_WIDTH), rows(B_WIDTH), state(), state()],
        out_shape=[jax.ShapeDtypeStruct((n, A_WIDTH), F32), jax.ShapeDtypeStruct((n, B_WIDTH), F32),
                   jax.ShapeDtypeStruct((B, A_PAIRS, LANES, LANES), F32),
                   jax.ShapeDtypeStruct((B, B_HEADS, B_HEAD, B_HEAD), F32)],
        scratch_shapes=[pltpu.VMEM((SUBLANES, A_PROJ), F32), pltpu.VMEM((A_PAIRS, LANES, LANES), F32),
                        pltpu.VMEM((SUBLANES, CONV_CH), F32), pltpu.VMEM((B_HEADS, B_HEAD, B_HEAD), F32)]
                       + [big() for _ in range(N_RWKV_SCRATCH + N_GDN_SCRATCH)],
        compiler_params=_cparams(2),
        name="mix_prompt",
    )(pa, qkv, ab, z, *rwkv_params, *gdn_params)


def _first_step_rows(rows, seqs, steps, state_rows, offset=0):
    hist = state_rows.shape[0] // seqs
    r, c = _iota((rows, seqs * hist), 0), _iota((rows, seqs * hist), 1)
    t = r & (steps - 1)
    sel = _one_hot((c == _group(r, steps) * hist + offset + t) & (t < hist - offset))
    return sum(_dot(sel, piece) for piece in _split3(state_rows))


def _last_step_rows(x, seqs, steps, keep):
    r, c = _iota((seqs * keep, seqs * steps), 0), _iota((seqs * keep, seqs * steps), 1)
    i = (c & (steps - 1)) - (steps - keep)
    sel = _one_hot((i >= 0) & (r == _group(c, steps) * keep + i))
    return sum(_dot(sel, piece) for piece in _split3(x))


def _rwkv_sample_body(pa_ref, shift_ref, s_ref, mu_ref, w0_ref, a0_ref, kk_ref, ka_ref, rk_ref, lnw_ref, lnb_ref,
                      w2a_ref, g2_ref, o_ref, sout_ref, shift_out_ref, r_s, nkk_s, beta_s, dec_s, k_s, v_s, o_s, *,
                      steps):
    rows = pa_ref.shape[0]
    seqs = rows // steps
    per_tile = SUBLANES // steps
    x = pa_ref[...]
    shift_out_ref[...] = _last_step_rows(x, seqs, steps, 1)
    t_idx = _iota((rows, 1), 0) & (steps - 1)
    prev = jnp.where(t_idx == 0, _first_step_rows(rows, seqs, steps, shift_ref[...]), pltpu.roll(x, 1, axis=0))
    r, k_mod, v, kk_raw, a, log_decay, g = _drain(_rwkv_token_math(
        x, prev, mu_ref[...], w0_ref[...], a0_ref[...], kk_ref[...], ka_ref[...],
        w2a_ref[...], g2_ref[...]))
    hi, hj = _iota((A_WIDTH, A_WIDTH), 0), _iota((A_WIDTH, A_WIDTH), 1)
    head_ones = _one_hot(_group(hi, A_HEAD) == _group(hj, A_HEAD))
    head_sum = lambda m: _mm_sel(m, head_ones)
    kk = kk_raw / jnp.maximum(jnp.sqrt(head_sum(kk_raw * kk_raw)), 1e-12)
    r_s[...] = r
    nkk_s[...] = -kk
    beta_s[...] = kk * a
    dec_s[...] = jnp.exp(log_decay)
    k_s[...] = k_mod
    v_s[...] = v
    pair_ones = head_ones[:LANES, :LANES]
    diag = _iota((A_HEAD, LANES), 0) == (_iota((A_HEAD, LANES), 1) & (A_HEAD - 1))

    tile_rows = SAMPLE_TILES * SUBLANES

    def tile(j, carry):
        r0 = pl.multiple_of(j * tile_rows, tile_rows)
        ld = lambda ref: ref[pl.ds(r0, tile_rows), :]
        r_t, nkk_t, beta_t, dec_t, k_t, v_t = ld(r_s), ld(nkk_s), ld(beta_s), ld(dec_s), ld(k_s), ld(v_s)
        out_rows = [[None] * A_PAIRS for _ in range(tile_rows)]

        def seq_pair_chain(i, p):
            sl = slice(p * LANES, (p + 1) * LANES)
            rowof = lambda m, rr: m[rr:rr + 1, sl]
            v_diag = jnp.concatenate([jnp.where(diag, rowof(v_t, i * steps + s), 0.0) for s in range(steps)],
                                     axis=0)
            v_cols = _mm_sel(v_diag, pair_ones, pieces=2)
            b = j * (SAMPLE_TILES * per_tile) + i
            S = jnp.concatenate([s_ref[b, 2 * p], s_ref[b, 2 * p + 1]], axis=1)
            for s in range(steps):
                rr = i * steps + s
                sa = _mm(S * rowof(nkk_t, rr), pair_ones)
                yield
                S = (S * rowof(dec_t, rr) + sa * rowof(beta_t, rr)
                     + v_cols[s * A_HEAD:(s + 1) * A_HEAD] * rowof(k_t, rr))
                out = _mm(S * rowof(r_t, rr), pair_ones)
                out_rows[rr][p] = jnp.sum(jnp.where(diag, out, 0.0), axis=0, keepdims=True)
            sout_ref[b, 2 * p] = S[:, :A_HEAD]
            sout_ref[b, 2 * p + 1] = S[:, A_HEAD:]

        _round_robin(seq_pair_chain(i, p) for i in range(SAMPLE_TILES * per_tile) for p in range(A_PAIRS))
        for k8 in range(SAMPLE_TILES):
            eight = out_rows[k8 * SUBLANES:(k8 + 1) * SUBLANES]
            o_s[pl.ds(r0 + k8 * SUBLANES, SUBLANES), :] = _rows_to_tile(
                [jnp.concatenate(parts, axis=1) for parts in eight])
        return carry

    lax.fori_loop(0, rows // tile_rows, tile, 0)
    o = o_s[...]
    mean = head_sum(o) * (1.0 / A_HEAD)
    cen = o - mean
    var = head_sum(cen * cen) * (1.0 / A_HEAD)
    o = cen * lax.rsqrt(var + A_LNX_EPS) * lnw_ref[...] + lnb_ref[...]
    o_ref[...] = (o + head_sum(r * k_mod * rk_ref[...]) * v) * g


def _rwkv_sample(pa, shift, state, layer, steps, params, seqs):
    n = pa.shape[0]
    nb = state.shape[1]
    rows = seqs * steps
    assert SUBLANES % steps == 0 and rows % SUBLANES == 0 and seqs % SUBLANES == 0
    block = (None, seqs, A_HEADS, A_HEAD, A_HEAD)
    sspec = pl.BlockSpec(block, lambda i: (layer, i, 0, 0, 0))
    ospec = pl.BlockSpec(block, lambda i: (0, i, 0, 0, 0))
    return pl.pallas_call(
        functools.partial(_rwkv_sample_body, steps=steps),
        grid=(nb // seqs,),
        in_specs=[pl.BlockSpec((rows, A_PROJ), lambda i: (i, 0)), pl.BlockSpec((seqs, A_PROJ), lambda i: (i, 0)),
                  sspec] + [_const_spec(p.shape) for p in params],
        out_specs=[pl.BlockSpec((rows, A_WIDTH), lambda i: (i, 0)), ospec,
                   pl.BlockSpec((seqs, A_PROJ), lambda i: (i, 0))],
        out_shape=[jax.ShapeDtypeStruct((n, A_WIDTH), F32), jax.ShapeDtypeStruct((1,) + state.shape[1:], F32),
                   jax.ShapeDtypeStruct((nb, A_PROJ), F32)],
        scratch_shapes=[pltpu.VMEM((rows, A_WIDTH), F32) for _ in range(7)],
        compiler_params=_cparams(1),
        name="rwkv_sample",
    )(pa, shift, state, *params)


def _gdn_qkv(conv):
    c = _silu(conv)
    qs, ks = [], []
    for h in range(B_HEADS):
        q = c[:, h * B_HEAD:(h + 1) * B_HEAD]
        k = c[:, B_WIDTH + h * B_HEAD:B_WIDTH + (h + 1) * B_HEAD]
        qs.append(q * (lax.rsqrt(jnp.sum(q * q, axis=-1, keepdims=True) + 1e-6) * (B_HEAD ** -0.5)))
        ks.append(k * lax.rsqrt(jnp.sum(k * k, axis=-1, keepdims=True) + 1e-6))
    q = jnp.concatenate(qs, axis=1)
    k = jnp.concatenate(ks, axis=1)
    v = c[:, 2 * B_WIDTH:]
    return q, k, v


def _gdn_gates(ab, alog, dtb):
    lane = _iota((1, LANES), 1)
    g = -jnp.exp(alog) * _softplus(ab + dtb)
    beta = _sigmoid(ab)
    gb = jnp.where(lane < B_HEADS, g, beta)
    si, sj = _iota((LANES, 2 * B_WIDTH), 0), _iota((LANES, 2 * B_WIDTH), 1)
    spread = _mm_sel(gb, _one_hot(si == _group(sj, B_HEAD)))
    return spread[:, :B_WIDTH], spread[:, B_WIDTH:]


def _gdn_out(o, norm_w, z):
    return o * lax.rsqrt(jnp.mean(o * o, axis=-1, keepdims=True) + RMS_EPS) * norm_w * _silu(z)


def _gdn_prompt_part(qkv_ref, ab_ref, z_ref, cw_ref, alog_ref, dtb_ref, nw_ref, o_ref,
                     carry_ref, state_ref, q_s, k_s, v_s, gc_s, beta_s):
    tt = qkv_ref.shape[0]
    C = CHUNK
    g, beta = _gdn_gates(ab_ref[...], alog_ref[...], dtb_ref[...])
    ri, ci = _iota((tt, tt), 0), _iota((tt, tt), 1)
    beta_s[...] = beta
    gc_s[...] = _sel_mm(_one_hot((_group(ri, C) == _group(ci, C)) & (ci <= ri)), g)

    def token_chain(r0):
        rows = slice(r0, r0 + C)
        x = qkv_ref[rows, :]
        before = carry_ref[...] if r0 == 0 else qkv_ref[r0 - SUBLANES:r0, :]
        if r0 + C == tt:
            carry_ref[...] = x[C - SUBLANES:, :]
        row8 = _iota((SUBLANES, 1), 0)
        conv = x * cw_ref[CONV_K - 1:CONV_K, :]
        for i in range(1, CONV_K):
            xs = pltpu.roll(x, i, axis=0)
            top = jnp.where(row8 < i, pltpu.roll(before, i, axis=0), xs[:SUBLANES])
            xs = jnp.concatenate([top, xs[SUBLANES:]], axis=0)
            conv = conv + xs * cw_ref[CONV_K - 1 - i:CONV_K - i, :]
            yield
        q, k, v = _gdn_qkv(conv)
        q_s[rows, :] = q
        k_s[rows, :] = k
        v_s[rows, :] = v
        yield

    i2, j2 = _iota((2 * C, 2 * C), 0), _iota((2 * C, 2 * C), 1)
    same_head = _group(i2, C) == _group(j2, C)
    strict = same_head & (i2 > j2)
    incl = same_head & (i2 >= j2)
    eye = jnp.where(i2 == j2, 1.0, 0.0)
    first = _iota((2 * C, 1), 0) < C

    def solve_chain(pr, r0, stash):
        sls = [slice(h * B_HEAD, (h + 1) * B_HEAD) for h in (2 * pr, 2 * pr + 1)]
        ld = lambda ref: jnp.concatenate([ref[pl.ds(r0, C), sl] for sl in sls], axis=0)
        q_h, k_h, v_h, gc_h, beta_h = ld(q_s), ld(k_s), ld(v_s), ld(gc_s), ld(beta_s)
        diff = gc_h - gc_h.T
        dm = jnp.where(incl, jnp.exp(jnp.where(incl, diff, 0.0)), 0.0)
        kb = k_h * beta_h
        QK = _mm_nt(jnp.concatenate([kb, q_h], axis=0), k_h)
        yield
        N = -jnp.where(strict, QK[:2 * C] * dm, 0.0)
        qk = QK[2 * C:] * dm
        egc = jnp.exp(gc_h)
        X = jnp.concatenate([v_h * beta_h, kb * egc], axis=1)
        T = yield from _nilpotent_inverse(N, eye)
        UW = _mm(T, X)
        yield
        g_last = jnp.where(first, gc_h[C - 1:C, :], gc_h[2 * C - 1:2 * C, :])
        stash[pr] = dict(u=UW[:, :B_HEAD], w=UW[:, B_HEAD:].astype(BF16), qd=(q_h * egc).astype(BF16),
                         qk=qk.astype(BF16), k_dec_t=(k_h * jnp.exp(g_last - gc_h)).T.astype(BF16),
                         decay=[jnp.exp(gc_h[(j + 1) * C - 1:(j + 1) * C, :]) for j in range(2)])

    def state_chain(pr, r0, stash):
        heads = (2 * pr, 2 * pr + 1)
        sls = [slice(h * B_HEAD, (h + 1) * B_HEAD) for h in heads]
        s = stash[pr]
        wS, qS, S_old = [], [], []
        for j, h in enumerate(heads):
            S = state_ref[h]
            rows = slice(j * C, (j + 1) * C)
            wq = _dot(jnp.concatenate([s["w"][rows], s["qd"][rows]], axis=0), S.astype(BF16))
            wS.append(wq[:C])
            qS.append(wq[C:])
            S_old.append(S)
        yield
        v_new = s["u"] - jnp.concatenate(wS, axis=0)
        o = jnp.concatenate(qS, axis=0) + _dot(s["qk"], v_new.astype(BF16))
        for j, h in enumerate(heads):
            mine = first if j == 0 else jnp.logical_not(first)
            rows = slice(j * C, (j + 1) * C)
            state_ref[h] = S_old[j] * s["decay"][j] + _mm(s["k_dec_t"], jnp.where(mine, v_new, 0.0))
            o_ref[pl.ds(r0, C), sls[j]] = _gdn_out(o[rows], nw_ref[...], z_ref[pl.ds(r0, C), sls[j]])
        yield

    return (lambda r0: [token_chain(r0)],
            lambda r0, stash: [solve_chain(pr, r0, stash) for pr in range(B_HEADS // 2)],
            lambda r0, stash: [state_chain(pr, r0, stash) for pr in range(B_HEADS // 2)])


def _gdn_sample_body(qkv_ref, hist_ref, ab_ref, z_ref, s_ref, cw_ref, alog_ref, dtb_ref, nw_ref,
                     o_ref, sout_ref, hist_out_ref, q_s, k_s, v_s, eg_s, beta_s, o_s, *, steps):
    rows = ab_ref.shape[0]
    seqs = rows // steps
    per_tile = SUBLANES // steps
    x = qkv_ref[...]
    hist_out_ref[...] = _last_step_rows(x, seqs, steps, CONV_K - 1)
    hist = hist_ref[...]
    t_idx = _iota((rows, 1), 0) & (steps - 1)
    conv = x * cw_ref[CONV_K - 1:CONV_K, :]
    for i in range(1, CONV_K):
        tap = jnp.where(t_idx >= i, pltpu.roll(x, i, axis=0),
                        _first_step_rows(rows, seqs, steps, hist, offset=CONV_K - 1 - i))
        conv = conv + tap * cw_ref[CONV_K - 1 - i:CONV_K - i, :]
    q, k, v = _gdn_qkv(conv)
    g, beta = _gdn_gates(ab_ref[...], alog_ref[...], dtb_ref[...])
    q_s[...] = q
    k_s[...] = k
    v_s[...] = v
    eg_s[...] = jnp.exp(g)
    beta_s[...] = beta
    diag = _iota((B_HEAD, B_HEAD), 0) == _iota((B_HEAD, B_HEAD), 1)
    ones = jnp.ones((B_HEAD, B_HEAD), BF16)

    def to_cols(tile_rows):
        diags = jnp.concatenate([jnp.where(diag, rv, 0.0) for rv in tile_rows], axis=0)
        return _mm(diags, ones)

    def tile(j, carry):
        r0 = pl.multiple_of(j * SUBLANES, SUBLANES)
        ld = lambda ref: ref[pl.ds(r0, SUBLANES), :]
        q_t, k_t, v_t, eg_t, beta_t = ld(q_s), ld(k_s), ld(v_s), ld(eg_s), ld(beta_s)
        out_rows = [[] for _ in range(SUBLANES)]
        for i in range(per_tile):
            for h in range(B_HEADS):
                sl = slice(h * B_HEAD, (h + 1) * B_HEAD)
                rowof = lambda m, rr: m[rr:rr + 1, sl]
                S = s_ref[j * per_tile + i, h]
                for s in range(steps):
                    rr = i * steps + s
                    kq_cols = to_cols([rowof(k_t, rr), rowof(q_t, rr)])
                    k_col, q_col = kq_cols[:B_HEAD], kq_cols[B_HEAD:]
                    S = S * rowof(eg_t, rr)
                    u = jnp.sum(k_col * S, axis=0, keepdims=True)
                    S = S + k_col * (rowof(beta_t, rr) * (rowof(v_t, rr) - u))
                    out_rows[rr].append(jnp.sum(q_col * S, axis=0, keepdims=True))
                sout_ref[j * per_tile + i, h] = S
        o_s[pl.ds(r0, SUBLANES), :] = _rows_to_tile([jnp.concatenate(parts, axis=1) for parts in out_rows])
        return carry

    lax.fori_loop(0, rows // SUBLANES, tile, 0)
    for h in range(B_HEADS):
        sl = slice(h * B_HEAD, (h + 1) * B_HEAD)
        o_ref[:, sl] = _gdn_out(o_s[:, sl], nw_ref[...], z_ref[:, sl])


def _gdn_sample(qkv, hist, ab, z, state, layer, steps, params, seqs):
    n = ab.shape[0]
    nb = state.shape[1]
    rows = seqs * steps
    hist_rows = seqs * (CONV_K - 1)
    assert SUBLANES % steps == 0 and rows % SUBLANES == 0 and hist_rows % SUBLANES == 0
    row = lambda w: pl.BlockSpec((rows, w), lambda i: (i, 0))
    block = (None, seqs, B_HEADS, B_HEAD, B_HEAD)
    sspec = pl.BlockSpec(block, lambda i: (layer, i, 0, 0, 0))
    ospec = pl.BlockSpec(block, lambda i: (0, i, 0, 0, 0))
    return pl.pallas_call(
        functools.partial(_gdn_sample_body, steps=steps),
        grid=(nb // seqs,),
        in_specs=[row(CONV_CH), pl.BlockSpec((hist_rows, CONV_CH), lambda i: (i, 0)), row(LANES), row(B_WIDTH), sspec]
                 + [_const_spec(p.shape) for p in params],
        out_specs=[row(B_WIDTH), ospec, pl.BlockSpec((hist_rows, CONV_CH), lambda i: (i, 0))],
        out_shape=[jax.ShapeDtypeStruct((n, B_WIDTH), F32), jax.ShapeDtypeStruct((1,) + state.shape[1:], F32),
                   jax.ShapeDtypeStruct(hist.shape, F32)],
        scratch_shapes=[pltpu.VMEM((rows, B_WIDTH), F32) for _ in range(6)],
        compiler_params=_cparams(1),
        name="gdn_sample",
    )(qkv, hist, ab, z, state, *params)


def _cuts(widths):
    edges, total = [], 0
    for w in widths[:-1]:
        total += w
        edges.append(total)
    return edges


def _regroup_pa(a):
    r, wd, k, v, ad, gd = jnp.split(a, _cuts((A_WIDTH, A_RANK_W, A_WIDTH, A_WIDTH, A_RANK_A, A_RANK_G)), axis=-1)
    return jnp.concatenate([r, k, v, wd, ad, gd], axis=-1)


def _ungroup_pa(a):
    r, k, v, wd, ad, gd = jnp.split(a, _cuts((A_WIDTH, A_WIDTH, A_WIDTH, A_RANK_W, A_RANK_A, A_RANK_G)), axis=-1)
    return jnp.concatenate([r, wd, k, v, ad, gd], axis=-1)


def _token_tile(n, want):
    tm = want
    while n % tm:
        tm //= 2
    return tm


def kernel(x_prompt, x_sample, state_rwkv, state_rwkv_shift, state_delta, state_conv, ffn1_norm, ffn1_w_gate, ffn1_w_up, ffn1_w_down, mix_norm, w_in, rwkv_mu, rwkv_w0, rwkv_w2, rwkv_a0, rwkv_a2, rwkv_g2, rwkv_k_k, rwkv_k_a, rwkv_r_k, rwkv_lnx_w, rwkv_lnx_b, gdn_conv_w, gdn_A_log, gdn_dt_bias, gdn_norm_w, proj_a, proj_b, w_out, ffn2_norm, ffn2_w_gate, ffn2_w_up, ffn2_w_down, final_norm):
    depth = ffn1_norm.shape[0]
    assert depth == 1, "single-layer trunk"
    Bp, Tp, _ = x_prompt.shape
    Bs, Ts, _ = x_sample.shape
    l = 0
    row = lambda a: a.reshape(1, -1).astype(F32)

    wi = w_in[l].astype(BF16)
    o_b = A_PROJ
    w_all = jnp.concatenate([
        _regroup_pa(wi[:, :A_PROJ]),
        wi[:, o_b:o_b + CONV_CH],
        wi[:, o_b + CONV_CH + 2 * B_HEADS:o_b + B_PROJ],
        wi[:, o_b + B_PROJ:],
        jnp.pad(wi[:, o_b + CONV_CH:o_b + CONV_CH + 2 * B_HEADS], ((0, 0), (0, LANES - 2 * B_HEADS))),
    ], axis=1).astype(BF16)
    ffn1 = (row(ffn1_norm[l]), ffn1_w_gate[l].astype(BF16), ffn1_w_up[l].astype(BF16), ffn1_w_down[l].astype(BF16))
    ffn2 = (row(ffn2_norm[l]), ffn2_w_gate[l].astype(BF16), ffn2_w_up[l].astype(BF16), ffn2_w_down[l].astype(BF16))
    merge_w = (proj_a[l].astype(BF16), proj_b[l].astype(BF16), w_out[l].astype(BF16))
    zw = jnp.zeros((A_RANK_W, A_WIDTH), F32)
    w2a = jnp.concatenate([jnp.concatenate([rwkv_w2[l], zw], axis=1),
                           jnp.concatenate([zw, rwkv_a2[l]], axis=1)], axis=0)
    rwkv_params = (row(_regroup_pa(rwkv_mu[l])), row(rwkv_w0[l]), row(rwkv_a0[l]), row(rwkv_k_k[l]), row(rwkv_k_a[l]),
                   row(rwkv_r_k[l]), row(rwkv_lnx_w[l]), row(rwkv_lnx_b[l]), w2a, rwkv_g2[l].astype(F32))
    pad_lane = lambda a: jnp.pad(a.reshape(1, -1).astype(F32), ((0, 0), (0, LANES - a.size)))
    gdn_params = (gdn_conv_w[l].astype(F32), pad_lane(gdn_A_log[l]), pad_lane(gdn_dt_bias[l]), row(gdn_norm_w[l]))

    def trunk_front(x2):
        n = x2.shape[0]
        h = _ffn(x2, *ffn1, tm=_token_tile(n, 512))
        return (h,) + tuple(_proj(h, row(mix_norm[l]), w_all, tm=_token_tile(n, 512)))

    def trunk_back(h, oa, ob, gates):
        n = h.shape[0]
        return _tail(h, oa, ob, gates, *merge_w, *ffn2, row(final_norm), tm=_token_tile(n, 256))

    xp = x_prompt.reshape(Bp * Tp, D_MODEL)
    h, pa, qkv, z, gates, ab = trunk_front(xp)
    tt = _token_tile(Tp, 256)
    oa, ob, s_pairs, delta_p = _mix_prompt(pa, qkv, ab, z, Bp, Tp, rwkv_params, gdn_params, tt)
    y_prompt = trunk_back(h, oa, ob, gates).reshape(Bp, Tp, D_MODEL)
    sp = s_pairs.reshape(Bp, A_PAIRS, 2, A_HEAD, 2, A_HEAD)
    rwkv_p = jnp.stack([sp[:, :, 0, :, 0], sp[:, :, 1, :, 1]], axis=2).reshape(Bp, A_HEADS, A_HEAD, A_HEAD)
    shift_p = _ungroup_pa(pa.reshape(Bp, Tp, A_PROJ)[:, -1])
    conv_p = qkv.reshape(Bp, Tp, CONV_CH)[:, Tp - (CONV_K - 1):]

    xs = x_sample.reshape(Bs * Ts, D_MODEL)
    h, pa, qkv, z, gates, ab = trunk_front(xs)
    assert Ts & (Ts - 1) == 0 and Ts >= CONV_K - 1, "sample steps: power of two covering the conv history"
    oa, rwkv_s, last_pa = _rwkv_sample(pa, _regroup_pa(state_rwkv_shift[l].astype(F32)), state_rwkv.astype(F32), l,
                                       Ts, rwkv_params, seqs=_token_tile(Bs, 32))
    shift_s = _ungroup_pa(last_pa)
    hist = state_conv[l].astype(F32).reshape(Bs * (CONV_K - 1), CONV_CH)
    ob, delta_s, new_hist = _gdn_sample(qkv, hist, ab, z, state_delta.astype(F32), l, Ts, gdn_params,
                                        seqs=_token_tile(Bs, 16))
    conv_s = new_hist.reshape(Bs, CONV_K - 1, CONV_CH)
    y_sample = trunk_back(h, oa, ob, gates).reshape(Bs, Ts, D_MODEL)

    add_depth = lambda a: a[None]
    return (y_prompt, y_sample,
            add_depth(rwkv_p), add_depth(shift_p), add_depth(delta_p), add_depth(conv_p),
            rwkv_s, add_depth(shift_s), delta_s, add_depth(conv_s))
```

```python
import functools

import jax
import jax.numpy as jnp
from jax import lax
from jax.experimental import pallas as pl
from jax.experimental.pallas import tpu as pltpu

F32 = jnp.float32
BF16 = jnp.bfloat16

D_MODEL = 1024
D_FF = 2816
RMS_EPS = 1e-6
A_HEAD = 64
A_HEADS = 8
A_WIDTH = A_HEADS * A_HEAD
A_RANK_W = 64
A_RANK_A = 64
A_RANK_G = 128
A_PROJ = 3 * A_WIDTH + A_RANK_W + A_RANK_A + A_RANK_G
A_LNX_EPS = 64e-5
A_PAIRS = A_HEADS // 2
B_HEADS = 4
B_HEAD = 128
B_WIDTH = B_HEADS * B_HEAD
CONV_K = 4
CONV_CH = 3 * B_WIDTH
B_PROJ = CONV_CH + 2 * B_HEADS + B_WIDTH
GATE_COLS = 2 * D_MODEL
LANES = 128
SUBLANES = 8
VMEM_LIMIT_BYTES = 56 * 1024 * 1024
CHUNK = 64
PA_R, PA_K, PA_V, PA_WA, PA_G = 0, A_WIDTH, 2 * A_WIDTH, 3 * A_WIDTH, 3 * A_WIDTH + A_RANK_W + A_RANK_A
PROJ_SPLITS = (A_PROJ, CONV_CH, B_WIDTH, GATE_COLS, LANES)


def _cparams(n_grid_dims):
    return pltpu.CompilerParams(dimension_semantics=("arbitrary",) * n_grid_dims,
                                vmem_limit_bytes=VMEM_LIMIT_BYTES)


def _const_spec(shape):
    nd = len(shape)
    return pl.BlockSpec(shape, lambda *_: (0,) * nd, pipeline_mode=pl.Buffered(1))


def _dot(a, b):
    return jnp.dot(a, b, preferred_element_type=F32)


def _dot_nt(a, b):
    return lax.dot_general(a, b, (((1,), (1,)), ((), ())), preferred_element_type=F32)


def _split(x):
    hi = x.astype(BF16)
    lo = (x - hi.astype(F32)).astype(BF16)
    return hi, lo


def _split3(x):
    hi = x.astype(BF16)
    rest = x - hi.astype(F32)
    mid = rest.astype(BF16)
    lo = (rest - mid.astype(F32)).astype(BF16)
    return hi, mid, lo


def _mm(a, b):
    return _dot(a.astype(BF16), b.astype(BF16))


def _mm_nt(a, b):
    return _dot_nt(a.astype(BF16), b.astype(BF16))


def _sel_mm(sel, x):
    return _dot(jnp.concatenate([sel, sel, sel], axis=1), jnp.concatenate(_split3(x), axis=0))


def _mm_sel(x, sel, pieces=3):
    parts = _split3(x) if pieces == 3 else _split(x)
    return _dot(jnp.concatenate(parts, axis=1), jnp.concatenate([sel] * pieces, axis=0))


INV_BASE = 8


def _nilpotent_inverse(n, eye):
    width = n.shape[1]
    bi, bj = _iota(n.shape, 0), _iota(n.shape, 1)
    same = lambda size: _group(bi, size) == _group(bj, size)
    d = jnp.where(same(INV_BASE), n, 0.0)
    t = eye + d
    d = _mm(d, d)
    yield
    for _ in range(INV_BASE.bit_length() - 3):
        both = _mm(d, jnp.concatenate([t, d], axis=1))
        yield
        t = t + both[:, :width]
        d = both[:, width:]
    t = t + _mm(d, t)
    yield
    size = INV_BASE
    while size < CHUNK:
        coupling = jnp.where(same(2 * size) & jnp.logical_not(same(size)), n, 0.0)
        tb = t.astype(BF16)
        lt = _mm(coupling, tb)
        yield
        t = t + _mm(tb, lt)
        yield
        size *= 2
    return t


def _round_robin(chains):
    chains = list(chains)
    while chains:
        for chain in list(chains):
            try:
                next(chain)
            except StopIteration:
                chains.remove(chain)


def _rms(x, w):
    return x * lax.rsqrt(jnp.mean(x * x, axis=-1, keepdims=True) + RMS_EPS) * w


def _sigmoid(x):
    return 1.0 / (1.0 + jnp.exp(-x))


def _silu(x):
    return x * _sigmoid(x)


def _softplus(x):
    return jnp.maximum(x, 0.0) + jnp.log(1.0 + jnp.exp(-jnp.abs(x)))


def _iota(shape, dim):
    return lax.broadcasted_iota(jnp.int32, shape, dim)


def _group(idx, size):
    assert size & (size - 1) == 0
    return lax.shift_right_logical(idx, size.bit_length() - 1)


def _one_hot(cond):
    return jnp.where(cond, 1.0, 0.0).astype(BF16)


def _rows_to_tile(rows):
    rid = _iota((SUBLANES, 1), 0)
    tile = jnp.zeros((SUBLANES, rows[0].shape[1]), F32)
    for i, row in enumerate(rows):
        tile = jnp.where(rid == i, row, tile)
    return tile


def _swiglu_half_step(x, nw, wg_ref, wu_ref, wd_ref):
    xn = _rms(x, nw).astype(BF16)
    g = _dot(xn, wg_ref[...])
    u = _dot(xn, wu_ref[...])
    act = (_silu(g) * u).astype(BF16)
    return x + 0.5 * _dot(act, wd_ref[...])


def _ffn_body(x_ref, nw_ref, wg_ref, wu_ref, wd_ref, o_ref):
    o_ref[...] = _swiglu_half_step(x_ref[...], nw_ref[...], wg_ref, wu_ref, wd_ref)


def _ffn(x, nw, wg, wu, wd, tm):
    n = x.shape[0]
    return pl.pallas_call(
        _ffn_body,
        grid=(n // tm,),
        in_specs=[pl.BlockSpec((tm, D_MODEL), lambda i: (i, 0)),
                  _const_spec((1, D_MODEL)),
                  _const_spec((D_MODEL, D_FF)), _const_spec((D_MODEL, D_FF)), _const_spec((D_FF, D_MODEL))],
        out_specs=pl.BlockSpec((tm, D_MODEL), lambda i: (i, 0)),
        out_shape=jax.ShapeDtypeStruct((n, D_MODEL), F32),
        compiler_params=_cparams(1),
        name="ffn1",
    )(x, nw, wg, wu, wd)


def _proj_body(h_ref, nw_ref, w_ref, *o_refs):
    u = _rms(h_ref[...], nw_ref[...]).astype(BF16)
    off = 0
    for o_ref, width in zip(o_refs, PROJ_SPLITS):
        o_ref[...] = _dot(u, w_ref[:, off:off + width])
        off += width


def _proj(h, nw, w_all, tm):
    n = h.shape[0]
    cols = sum(PROJ_SPLITS)
    return pl.pallas_call(
        _proj_body,
        grid=(n // tm,),
        in_specs=[pl.BlockSpec((tm, D_MODEL), lambda i: (i, 0)),
                  _const_spec((1, D_MODEL)), _const_spec((D_MODEL, cols))],
        out_specs=[pl.BlockSpec((tm, w), lambda i: (i, 0)) for w in PROJ_SPLITS],
        out_shape=[jax.ShapeDtypeStruct((n, w), F32) for w in PROJ_SPLITS],
        compiler_params=_cparams(1),
        name="proj",
    )(h, nw, w_all)


def _tail_body(h_ref, oa_ref, ob_ref, gates_ref, pa_ref, pb_ref, wo_ref, nw_ref, wg_ref, wu_ref, wd_ref,
               fn_ref, o_ref):
    ma = _dot(oa_ref[...].astype(BF16), pa_ref[...])
    mb = _dot(ob_ref[...].astype(BF16), pb_ref[...])
    merged = _sigmoid(gates_ref[:, :D_MODEL]) * ma + _sigmoid(gates_ref[:, D_MODEL:]) * mb
    h = h_ref[...] + _dot(merged.astype(BF16), wo_ref[...])
    h = _swiglu_half_step(h, nw_ref[...], wg_ref, wu_ref, wd_ref)
    o_ref[...] = _rms(h, fn_ref[...])


def _tail(h, oa, ob, gates, proj_a, proj_b, w_out, nw, wg, wu, wd, fn, tm):
    n = h.shape[0]
    row = lambda w: pl.BlockSpec((tm, w), lambda i: (i, 0))
    return pl.pallas_call(
        _tail_body,
        grid=(n // tm,),
        in_specs=[row(D_MODEL), row(A_WIDTH), row(B_WIDTH), row(GATE_COLS),
                  _const_spec((A_WIDTH, D_MODEL)), _const_spec((B_WIDTH, D_MODEL)),
                  _const_spec((D_MODEL, D_MODEL)), _const_spec((1, D_MODEL)),
                  _const_spec((D_MODEL, D_FF)), _const_spec((D_MODEL, D_FF)), _const_spec((D_FF, D_MODEL)),
                  _const_spec((1, D_MODEL))],
        out_specs=row(D_MODEL),
        out_shape=jax.ShapeDtypeStruct((n, D_MODEL), F32),
        compiler_params=_cparams(1),
        name="tail",
    )(h, oa, ob, gates, proj_a, proj_b, w_out, nw, wg, wu, wd, fn)


def _drain(chain):
    try:
        while True:
            next(chain)
    except StopIteration as stop:
        return stop.value


def _rwkv_token_math(x, prev, mu, w0, a0, k_k, k_a, w2a, g2):
    pm = x + (prev - x) * mu
    r = pm[:, PA_R:PA_R + A_WIDTH]
    k = pm[:, PA_K:PA_K + A_WIDTH]
    v = pm[:, PA_V:PA_V + A_WIDTH]
    wa = pm[:, PA_WA:PA_WA + LANES]
    gd = pm[:, PA_G:PA_G + A_RANK_G]
    lane = _iota((1, LANES), 1)
    lora_in = jnp.where(lane < A_RANK_W, jnp.tanh(wa), wa)
    lora = _mm(lora_in, w2a)
    g = _mm(_sigmoid(gd), g2)
    yield
    w_log = -_softplus(-(w0 + lora[:, :A_WIDTH])) - 0.5
    log_decay = -jnp.exp(w_log)
    yield
    a = _sigmoid(a0 + lora[:, A_WIDTH:])
    kk_raw = k * k_k
    k_mod = k * (1.0 + (a - 1.0) * k_a)
    return r, k_mod, v, kk_raw, a, log_decay, g


def _pair_mask(rows_per_head):
    shape = (2 * rows_per_head, LANES)
    return _group(_iota(shape, 0), rows_per_head) == _group(_iota(shape, 1), A_HEAD)


def _rwkv_prompt_part(pa_ref, mu_ref, w0_ref, a0_ref, kk_ref, ka_ref, rk_ref, lnw_ref, lnb_ref, w2a_ref, g2_ref,
                      o_ref, carry_ref, state_ref, r_s, k_s, v_s, kkraw_s, a_s, cum_s, ld_s, g_s):
    tt = pa_ref.shape[0]
    C = CHUNK
    lower = _one_hot(_iota((C, C), 1) <= _iota((C, C), 0))

    def token_chain(r0):
        rows = slice(r0, r0 + C)
        x = pa_ref[rows, :]
        before = carry_ref[SUBLANES - 1:SUBLANES, :] if r0 == 0 else pa_ref[r0 - 1:r0, :]
        prev = jnp.where(_iota((C, 1), 0) == 0, before, pltpu.roll(x, 1, axis=0))
        if r0 + C == tt:
            carry_ref[...] = x[C - SUBLANES:, :]
        r, k_mod, v, kk_raw, a, log_decay, g = yield from _rwkv_token_math(
            x, prev, mu_ref[...], w0_ref[...], a0_ref[...], kk_ref[...], ka_ref[...], w2a_ref[...], g2_ref[...])
        r_s[rows, :] = r
        k_s[rows, :] = k_mod
        v_s[rows, :] = v
        kkraw_s[rows, :] = kk_raw
        a_s[rows, :] = a
        g_s[rows, :] = g
        ld_s[rows, :] = log_decay
        yield
        cum_s[rows, :] = sum(_dot(lower, piece) for piece in _split3(log_decay))
        yield

    mask = _pair_mask(C)
    i2, j2 = _iota((2 * C, 2 * C), 0), _iota((2 * C, 2 * C), 1)
    strict = i2 > j2
    incl = i2 >= j2
    eye = jnp.where(i2 == j2, 1.0, 0.0)
    dup = lambda m: jnp.concatenate([m, m], axis=0)
    stack = lambda m: jnp.where(mask, dup(m), 0.0)

    def solve_chain(p, r0, stash):
        sl = slice(p * LANES, (p + 1) * LANES)
        ld = lambda ref: ref[pl.ds(r0, C), sl]
        r_p, k_p, v_p, a_p, cum, ldec = ld(r_s), ld(k_s), ld(v_s), ld(a_s), ld(cum_s), ld(ld_s)
        einc = jnp.exp(cum)
        eex = jnp.exp(cum - ldec)
        einv = jnp.exp(-cum)
        etail = jnp.exp(cum[C - 1:C, :] - cum)
        kks = stack(ld(kkraw_s))
        kks = kks * jnp.minimum(lax.rsqrt(jnp.sum(kks * kks, axis=-1, keepdims=True)), 1e12)
        As = -kks * dup(eex)
        Bs = kks * dup(a_p * einv)
        Bh = kks * dup(a_p * etail)
        Ks = stack(k_p * einv)
        Kh = stack(k_p * etail)
        Rs = stack(r_p * einc)
        Vs = stack(v_p)
        AR = jnp.concatenate([As, Rs], axis=0).astype(BF16)
        Vb = Vs.astype(BF16)
        G = _mm_nt(AR, jnp.concatenate([Bs, Ks], axis=0))
        yield
        Aab = jnp.where(strict, G[:2 * C, :2 * C], 0.0)
        Aak = jnp.where(strict, G[:2 * C, 2 * C:], 0.0)
        Arb = jnp.where(incl, G[2 * C:, :2 * C], 0.0)
        Ark = jnp.where(incl, G[2 * C:, 2 * C:], 0.0)
        Y = _mm(Aak, Vb)
        yield
        T = yield from _nilpotent_inverse(Aab, eye)
        WU = _mm(T, jnp.concatenate([AR[:2 * C], Y.astype(BF16)], axis=1))
        yield
        bonus = jnp.sum(stack(r_p * k_p * rk_ref[:, sl]), axis=-1, keepdims=True) * Vs
        stash[p] = dict(WU=WU, R=AR[2 * C:], Vs=Vs, bonus=bonus,
                        Aro=jnp.concatenate([Arb, Ark], axis=1).astype(BF16),
                        BKh=jnp.concatenate([Bh, Kh], axis=0).astype(BF16), decay=einc[C - 1:C, :])

    def state_chain(p, r0, stash):
        sl = slice(p * LANES, (p + 1) * LANES)
        s = stash[p]
        S = state_ref[p]
        Sb = S.astype(BF16)
        W = _mm_nt(s["WU"][:, :LANES], Sb) + s["WU"][:, LANES:]
        yield
        WV = jnp.concatenate([W, s["Vs"]], axis=0)
        O = _dot_nt(s["R"], Sb) + _mm(s["Aro"], WV)
        state_ref[p] = S * s["decay"] + _mm(WV.T, s["BKh"])
        yield
        mean = jnp.sum(O, axis=-1, keepdims=True) * (1.0 / A_HEAD)
        cen = jnp.where(mask, O - mean, 0.0)
        var = jnp.sum(cen * cen, axis=-1, keepdims=True) * (1.0 / A_HEAD)
        normed = jnp.where(mask, cen * lax.rsqrt(var + A_LNX_EPS) * lnw_ref[:, sl] + lnb_ref[:, sl], 0.0)
        full = normed + s["bonus"]
        o_ref[pl.ds(r0, C), sl] = (full[:C] + full[C:]) * g_s[pl.ds(r0, C), sl]

    return (lambda r0: [token_chain(r0)],
            lambda r0, stash: [solve_chain(p, r0, stash) for p in range(A_PAIRS)],
            lambda r0, stash: [state_chain(p, r0, stash) for p in range(A_PAIRS)])


CHUNKS_IN_FLIGHT = 2
SAMPLE_TILES = 2
N_RWKV_PARAMS = 10
N_GDN_PARAMS = 4
N_RWKV_SCRATCH = 8
N_GDN_SCRATCH = 5


def _mix_prompt_body(pa_ref, qkv_ref, ab_ref, z_ref, *refs):
    refs = list(refs)
    take = lambda n: [refs.pop(0) for _ in range(n)]
    rwkv_prm, gdn_prm = take(N_RWKV_PARAMS), take(N_GDN_PARAMS)
    oa_ref, ob_ref, sfa_ref, sfb_ref = take(4)
    carry_a, state_a, carry_b, state_b = take(4)
    rwkv_scr, gdn_scr = take(N_RWKV_SCRATCH), take(N_GDN_SCRATCH)
    t = pl.program_id(1)
    tt = pa_ref.shape[0]

    @pl.when(t == 0)
    def _():
        for ref in (carry_a, state_a, carry_b, state_b):
            ref[...] = jnp.zeros_like(ref)

    rwkv_token, rwkv_solve, rwkv_state = _rwkv_prompt_part(pa_ref, *rwkv_prm, oa_ref, carry_a, state_a, *rwkv_scr)
    gdn_token, gdn_solve, gdn_state = _gdn_prompt_part(qkv_ref, ab_ref, z_ref, *gdn_prm, ob_ref, carry_b, state_b,
                                                       *gdn_scr)

    n_chunks = tt // CHUNK
    group = min(CHUNKS_IN_FLIGHT, n_chunks)
    n_groups = n_chunks // group
    stashes = [({}, {}) for _ in range(n_chunks)]
    chunks_of = lambda gi: range(gi * group, (gi + 1) * group) if 0 <= gi < n_groups else ()

    def in_sequence(per_chunk_chains):
        for chains in zip(*per_chunk_chains):
            for chain in chains:
                yield from chain

    for gi in range(n_groups + 2):
        chains = []
        for c in chunks_of(gi):
            chains += rwkv_token(c * CHUNK) + gdn_token(c * CHUNK)
        for c in chunks_of(gi - 1):
            chains += rwkv_solve(c * CHUNK, stashes[c][0]) + gdn_solve(c * CHUNK, stashes[c][1])
        state_chains = [rwkv_state(c * CHUNK, stashes[c][0]) + gdn_state(c * CHUNK, stashes[c][1])
                        for c in chunks_of(gi - 2)]
        if state_chains:
            chains += [in_sequence([per_chunk[i:i + 1] for per_chunk in state_chains])
                       for i in range(len(state_chains[0]))]
        _round_robin(chains)

    @pl.when(t == pl.num_programs(1) - 1)
    def _():
        sfa_ref[0] = state_a[...]
        sfb_ref[0] = state_b[...]


def _mix_prompt(pa, qkv, ab, z, B, T, rwkv_params, gdn_params, tt):
    n = pa.shape[0]
    nt = T // tt
    assert len(rwkv_params) == N_RWKV_PARAMS and len(gdn_params) == N_GDN_PARAMS
    rows = lambda w: pl.BlockSpec((tt, w), lambda b, t: (b * nt + t, 0))
    state = lambda: pl.BlockSpec((1, 4, LANES, LANES), lambda b, t: (b, 0, 0, 0))
    big = lambda: pltpu.VMEM((tt, A_WIDTH), F32)
    return pl.pallas_call(
        _mix_prompt_body,
        grid=(B, nt),
        in_specs=[rows(A_PROJ), rows(CONV_CH), rows(LANES), rows(B_WIDTH)]
                 + [_const_spec(p.shape) for p in rwkv_params + gdn_params],
        out_specs=[rows(A_WIDTH), rows(B_WIDTH), state(), state()],
        out_shape=[jax.ShapeDtypeStruct((n, A_WIDTH), F32), jax.ShapeDtypeStruct((n, B_WIDTH), F32),
                   jax.ShapeDtypeStruct((B, A_PAIRS, LANES, LANES), F32),
                   jax.ShapeDtypeStruct((B, B_HEADS, B_HEAD, B_HEAD), F32)],
        scratch_shapes=[pltpu.VMEM((SUBLANES, A_PROJ), F32), pltpu.VMEM((A_PAIRS, LANES, LANES), F32),
                        pltpu.VMEM((SUBLANES, CONV_CH), F32), pltpu.VMEM((B_HEADS, B_HEAD, B_HEAD), F32)]
                       + [big() for _ in range(N_RWKV_SCRATCH + N_GDN_SCRATCH)],
        compiler_params=_cparams(2),
        name="mix_prompt",
    )(pa, qkv, ab, z, *rwkv_params, *gdn_params)


def _first_step_rows(rows, seqs, steps, state_rows, offset=0):
    hist = state_rows.shape[0] // seqs
    r, c = _iota((rows, seqs * hist), 0), _iota((rows, seqs * hist), 1)
    t = r & (steps - 1)
    sel = _one_hot((c == _group(r, steps) * hist + offset + t) & (t < hist - offset))
    return sum(_dot(sel, piece) for piece in _split3(state_rows))


def _last_step_rows(x, seqs, steps, keep):
    r, c = _iota((seqs * keep, seqs * steps), 0), _iota((seqs * keep, seqs * steps), 1)
    i = (c & (steps - 1)) - (steps - keep)
    sel = _one_hot((i >= 0) & (r == _group(c, steps) * keep + i))
    return sum(_dot(sel, piece) for piece in _split3(x))


def _rwkv_sample_body(pa_ref, shift_ref, s_ref, mu_ref, w0_ref, a0_ref, kk_ref, ka_ref, rk_ref, lnw_ref, lnb_ref,
                      w2a_ref, g2_ref, o_ref, sout_ref, shift_out_ref, r_s, nkk_s, beta_s, dec_s, k_s, v_s, o_s, *,
                      steps):
    rows = pa_ref.shape[0]
    seqs = rows // steps
    per_tile = SUBLANES // steps
    x = pa_ref[...]
    shift_out_ref[...] = _last_step_rows(x, seqs, steps, 1)
    t_idx = _iota((rows, 1), 0) & (steps - 1)
    prev = jnp.where(t_idx == 0, _first_step_rows(rows, seqs, steps, shift_ref[...]), pltpu.roll(x, 1, axis=0))
    r, k_mod, v, kk_raw, a, log_decay, g = _drain(_rwkv_token_math(
        x, prev, mu_ref[...], w0_ref[...], a0_ref[...], kk_ref[...], ka_ref[...],
        w2a_ref[...], g2_ref[...]))
    hi, hj = _iota((A_WIDTH, A_WIDTH), 0), _iota((A_WIDTH, A_WIDTH), 1)
    head_ones = _one_hot(_group(hi, A_HEAD) == _group(hj, A_HEAD))
    head_sum = lambda m: _mm_sel(m, head_ones)
    kk = kk_raw / jnp.maximum(jnp.sqrt(head_sum(kk_raw * kk_raw)), 1e-12)
    r_s[...] = r
    nkk_s[...] = -kk
    beta_s[...] = kk * a
    dec_s[...] = jnp.exp(log_decay)
    k_s[...] = k_mod
    v_s[...] = v
    pair_ones = head_ones[:LANES, :LANES]
    diag = _iota((A_HEAD, LANES), 0) == (_iota((A_HEAD, LANES), 1) & (A_HEAD - 1))

    tile_rows = SAMPLE_TILES * SUBLANES

    def tile(j, carry):
        r0 = pl.multiple_of(j * tile_rows, tile_rows)
        ld = lambda ref: ref[pl.ds(r0, tile_rows), :]
        r_t, nkk_t, beta_t, dec_t, k_t, v_t = ld(r_s), ld(nkk_s), ld(beta_s), ld(dec_s), ld(k_s), ld(v_s)
        out_rows = [[None] * A_PAIRS for _ in range(tile_rows)]

        def seq_pair_chain(i, p):
            sl = slice(p * LANES, (p + 1) * LANES)
            rowof = lambda m, rr: m[rr:rr + 1, sl]
            v_diag = jnp.concatenate([jnp.where(diag, rowof(v_t, i * steps + s), 0.0) for s in range(steps)],
                                     axis=0)
            v_cols = _mm_sel(v_diag, pair_ones, pieces=2)
            b = j * (SAMPLE_TILES * per_tile) + i
            S = jnp.concatenate([s_ref[b, 2 * p], s_ref[b, 2 * p + 1]], axis=1)
            for s in range(steps):
                rr = i * steps + s
                sa = _mm(S * rowof(nkk_t, rr), pair_ones)
                yield
                S = (S * rowof(dec_t, rr) + sa * rowof(beta_t, rr)
                     + v_cols[s * A_HEAD:(s + 1) * A_HEAD] * rowof(k_t, rr))
                out = _mm(S * rowof(r_t, rr), pair_ones)
                out_rows[rr][p] = jnp.sum(jnp.where(diag, out, 0.0), axis=0, keepdims=True)
            sout_ref[b, 2 * p] = S[:, :A_HEAD]
            sout_ref[b, 2 * p + 1] = S[:, A_HEAD:]

        _round_robin(seq_pair_chain(i, p) for i in range(SAMPLE_TILES * per_tile) for p in range(A_PAIRS))
        for k8 in range(SAMPLE_TILES):
            eight = out_rows[k8 * SUBLANES:(k8 + 1) * SUBLANES]
            o_s[pl.ds(r0 + k8 * SUBLANES, SUBLANES), :] = _rows_to_tile(
                [jnp.concatenate(parts, axis=1) for parts in eight])
        return carry

    lax.fori_loop(0, rows // tile_rows, tile, 0)
    o = o_s[...]
    mean = head_sum(o) * (1.0 / A_HEAD)
    cen = o - mean
    var = head_sum(cen * cen) * (1.0 / A_HEAD)
    o = cen * lax.rsqrt(var + A_LNX_EPS) * lnw_ref[...] + lnb_ref[...]
    o_ref[...] = (o + head_sum(r * k_mod * rk_ref[...]) * v) * g


def _rwkv_sample(pa, shift, state, layer, steps, params, seqs):
    n = pa.shape[0]
    nb = state.shape[1]
    rows = seqs * steps
    assert SUBLANES % steps == 0 and rows % SUBLANES == 0 and seqs % SUBLANES == 0
    block = (None, seqs, A_HEADS, A_HEAD, A_HEAD)
    sspec = pl.BlockSpec(block, lambda i: (layer, i, 0, 0, 0))
    ospec = pl.BlockSpec(block, lambda i: (0, i, 0, 0, 0))
    return pl.pallas_call(
        functools.partial(_rwkv_sample_body, steps=steps),
        grid=(nb // seqs,),
        in_specs=[pl.BlockSpec((rows, A_PROJ), lambda i: (i, 0)), pl.BlockSpec((seqs, A_PROJ), lambda i: (i, 0)),
                  sspec] + [_const_spec(p.shape) for p in params],
        out_specs=[pl.BlockSpec((rows, A_WIDTH), lambda i: (i, 0)), ospec,
                   pl.BlockSpec((seqs, A_PROJ), lambda i: (i, 0))],
        out_shape=[jax.ShapeDtypeStruct((n, A_WIDTH), F32), jax.ShapeDtypeStruct((1,) + state.shape[1:], F32),
                   jax.ShapeDtypeStruct((nb, A_PROJ), F32)],
        scratch_shapes=[pltpu.VMEM((rows, A_WIDTH), F32) for _ in range(7)],
        compiler_params=_cparams(1),
        name="rwkv_sample",
    )(pa, shift, state, *params)


ROWP = dict(mu_r=0, mu_k=1, mu_v=2, w0=3, a0=4, k_k=5, k_a=6, r_k=7, lnw=8, lnb=9)
ROWP_ROWS = 16
VALUE_GROUP = SUBLANES


def _rwkv_lanes_body(par_ref, pak_ref, pav_ref, paw_ref, pag_ref, shr_ref, shk_ref, shv_ref, shw_ref, shg_ref,
                     pa_ref, s_ref, rowp_ref, shared_ref, w2a_ref, g2_ref,
                     o_ref, sout_ref, shift_out_ref,
                     tok_s, tr_s, ot_s, obm_s, *, steps):
    rows = par_ref.shape[0]
    B = rows // steps
    p = pl.program_id(0)
    rp = lambda name: rowp_ref[ROWP[name]:ROWP[name] + 1, :]
    t_idx = _iota((rows, 1), 0) & (steps - 1)

    @pl.when(p == 0)
    def _():
        shift_out_ref[...] = _last_step_rows(pa_ref[...], B, steps, 1)

    def lerp(x_ref, first_ref, mu):
        x = x_ref[...]
        prev = jnp.where(t_idx == 0, _first_step_rows(rows, B, steps, first_ref[...]), pltpu.roll(x, 1, axis=0))
        return x + (prev - x) * mu

    r = lerp(par_ref, shr_ref, rp("mu_r"))
    k = lerp(pak_ref, shk_ref, rp("mu_k"))
    v = lerp(pav_ref, shv_ref, rp("mu_v"))
    wa = lerp(paw_ref, shw_ref, shared_ref[0:1, :])
    gd = lerp(pag_ref, shg_ref, shared_ref[1:2, :])
    lane = _iota((1, LANES), 1)
    lora = _mm(jnp.where(lane < A_RANK_W, jnp.tanh(wa), wa), w2a_ref[...])
    g = _mm(_sigmoid(gd), g2_ref[...])
    w_log = -_softplus(-(rp("w0") + lora[:, :LANES])) - 0.5
    decay = jnp.exp(-jnp.exp(w_log))
    a = _sigmoid(rp("a0") + lora[:, LANES:])
    k_mod = k * (1.0 + (a - 1.0) * rp("k_a"))
    hi, hj = _iota((LANES, LANES), 0), _iota((LANES, LANES), 1)
    pair_ones = _one_hot(_group(hi, A_HEAD) == _group(hj, A_HEAD))
    head_sum = lambda m: _mm_sel(m, pair_ones)
    kk_raw = k * rp("k_k")
    kk = kk_raw * jnp.minimum(lax.rsqrt(head_sum(kk_raw * kk_raw)), 1e12)
    names = ("nkk", "beta", "decay", "k", "r", "v")
    for idx, m in enumerate((-kk, kk * a, decay, k_mod, r, v)):
        tok_s[idx] = m
        for t in range(steps):
            tr_s[idx, t] = tok_s[idx, pl.ds(t, B, stride=steps), :].T
    at = lambda name, t: tr_s.at[names.index(name), t]

    def group(gi, carry):
        j = gi // (A_HEAD // VALUE_GROUP)
        v0 = (gi % (A_HEAD // VALUE_GROUP)) * VALUE_GROUP
        keys = lambda name, t: at(name, t)[pl.ds(pl.multiple_of(j * A_HEAD, A_HEAD), A_HEAD), :]
        v_rows = [at("v", t)[pl.ds(pl.multiple_of(gi * VALUE_GROUP, VALUE_GROUP), VALUE_GROUP), :]
                  for t in range(steps)]
        outs = [[] for _ in range(steps)]
        for i in range(VALUE_GROUP):
            S = s_ref[j, v0 + i]
            for t in range(steps):
                sa = jnp.sum(S * keys("nkk", t), axis=0, keepdims=True)
                S = S * keys("decay", t) + sa * keys("beta", t) + v_rows[t][i:i + 1, :] * keys("k", t)
                outs[t].append(jnp.sum(S * keys("r", t), axis=0, keepdims=True))
            sout_ref[j, v0 + i] = S
        for t in range(steps):
            ot_s[t, pl.ds(pl.multiple_of(gi * VALUE_GROUP, VALUE_GROUP), VALUE_GROUP), :] = _rows_to_tile(outs[t])
        return carry

    lax.fori_loop(0, 2 * A_HEAD // VALUE_GROUP, group, 0)
    for t in range(steps):
        obm_s[pl.ds(t, B, stride=steps), :] = ot_s[t].T
    o = obm_s[...]
    mean = head_sum(o) * (1.0 / A_HEAD)
    cen = o - mean
    var = head_sum(cen * cen) * (1.0 / A_HEAD)
    o = cen * lax.rsqrt(var + A_LNX_EPS) * rp("lnw") + rp("lnb")
    o_ref[...] = (o + head_sum(r * k_mod * rp("r_k")) * v) * g


def _rwkv_lanes(pa, shift, state_t, layer, steps, tables):
    rowp, shared, w2a_p, g2_p = tables
    n = pa.shape[0]
    B = state_t.shape[-1]
    assert n == B * steps and B == LANES and steps & (steps - 1) == 0
    col = lambda rows_, j: pl.BlockSpec((rows_, LANES), lambda p: (0, j(p)))
    groups = [lambda p: p, lambda p: A_PAIRS + p, lambda p: 2 * A_PAIRS + p,
              lambda p: 3 * A_PAIRS, lambda p: 3 * A_PAIRS + 1]
    block = (None, 2, A_HEAD, A_HEAD, B)
    return pl.pallas_call(
        functools.partial(_rwkv_lanes_body, steps=steps),
        grid=(A_PAIRS,),
        in_specs=[col(n, j) for j in groups] + [col(B, j) for j in groups]
                 + [_const_spec((n, A_PROJ)), pl.BlockSpec(block, lambda p: (layer, p, 0, 0, 0)),
                    pl.BlockSpec((None, ROWP_ROWS, LANES), lambda p: (p, 0, 0)), _const_spec(shared.shape),
                    pl.BlockSpec((None, LANES, 2 * LANES), lambda p: (p, 0, 0)),
                    pl.BlockSpec((None, LANES, LANES), lambda p: (p, 0, 0))],
        out_specs=[pl.BlockSpec((n, LANES), lambda p: (0, p)), pl.BlockSpec(block, lambda p: (0, p, 0, 0, 0)),
                   pl.BlockSpec((B, A_PROJ), lambda p: (0, 0))],
        out_shape=[jax.ShapeDtypeStruct((n, A_WIDTH), F32), jax.ShapeDtypeStruct((1,) + state_t.shape[1:], F32),
                   jax.ShapeDtypeStruct((B, A_PROJ), F32)],
        scratch_shapes=[pltpu.VMEM((6, n, LANES), F32), pltpu.VMEM((6, steps, LANES, B), F32),
                        pltpu.VMEM((steps, LANES, B), F32), pltpu.VMEM((n, LANES), F32)],
        compiler_params=_cparams(1),
        name="rwkv_sample",
    )(*([pa] * 5), *([shift] * 5), pa, state_t, rowp, shared, w2a_p, g2_p)


def _rwkv_pair_tables(mu, w0, a0, k_k, k_a, r_k, lnw, lnb, w2, a2, g2):
    per_pair = lambda a: a.reshape(A_PAIRS, 1, LANES)
    rows = {"mu_r": mu[:A_WIDTH], "mu_k": mu[A_WIDTH:2 * A_WIDTH], "mu_v": mu[2 * A_WIDTH:3 * A_WIDTH],
            "w0": w0, "a0": a0, "k_k": k_k, "k_a": k_a, "r_k": r_k.reshape(-1), "lnw": lnw, "lnb": lnb}
    table = jnp.concatenate([per_pair(rows[name].astype(F32)) for name in sorted(ROWP, key=ROWP.get)]
                            + [jnp.zeros((A_PAIRS, ROWP_ROWS - len(ROWP), LANES), F32)], axis=1)
    shared = jnp.concatenate([mu[3 * A_WIDTH:3 * A_WIDTH + LANES].reshape(1, LANES),
                              mu[3 * A_WIDTH + LANES:].reshape(1, LANES),
                              jnp.zeros((SUBLANES - 2, LANES), F32)], axis=0).astype(F32)
    by_pair = lambda w: jnp.transpose(w.astype(F32).reshape(w.shape[0], A_PAIRS, LANES), (1, 0, 2))
    zeros = jnp.zeros((A_PAIRS, A_RANK_W, LANES), F32)
    w2a_p = jnp.concatenate([jnp.concatenate([by_pair(w2), zeros], axis=2),
                             jnp.concatenate([zeros, by_pair(a2)], axis=2)], axis=1)
    return table, shared, w2a_p, by_pair(g2)


def _gdn_qkv(conv):
    c = _silu(conv)
    qs, ks = [], []
    for h in range(B_HEADS):
        q = c[:, h * B_HEAD:(h + 1) * B_HEAD]
        k = c[:, B_WIDTH + h * B_HEAD:B_WIDTH + (h + 1) * B_HEAD]
        qs.append(q * (lax.rsqrt(jnp.sum(q * q, axis=-1, keepdims=True) + 1e-6) * (B_HEAD ** -0.5)))
        ks.append(k * lax.rsqrt(jnp.sum(k * k, axis=-1, keepdims=True) + 1e-6))
    q = jnp.concatenate(qs, axis=1)
    k = jnp.concatenate(ks, axis=1)
    v = c[:, 2 * B_WIDTH:]
    return q, k, v


def _gdn_gates(ab, alog, dtb):
    lane = _iota((1, LANES), 1)
    g = -jnp.exp(alog) * _softplus(ab + dtb)
    beta = _sigmoid(ab)
    gb = jnp.where(lane < B_HEADS, g, beta)
    si, sj = _iota((LANES, 2 * B_WIDTH), 0), _iota((LANES, 2 * B_WIDTH), 1)
    spread = _mm_sel(gb, _one_hot(si == _group(sj, B_HEAD)))
    return spread[:, :B_WIDTH], spread[:, B_WIDTH:]


def _gdn_out(o, norm_w, z):
    return o * lax.rsqrt(jnp.mean(o * o, axis=-1, keepdims=True) + RMS_EPS) * norm_w * _silu(z)


def _gdn_prompt_part(qkv_ref, ab_ref, z_ref, cw_ref, alog_ref, dtb_ref, nw_ref, o_ref,
                     carry_ref, state_ref, q_s, k_s, v_s, gc_s, beta_s):
    tt = qkv_ref.shape[0]
    C = CHUNK
    g, beta = _gdn_gates(ab_ref[...], alog_ref[...], dtb_ref[...])
    ri, ci = _iota((tt, tt), 0), _iota((tt, tt), 1)
    beta_s[...] = beta
    gc_s[...] = _sel_mm(_one_hot((_group(ri, C) == _group(ci, C)) & (ci <= ri)), g)

    def token_chain(r0):
        rows = slice(r0, r0 + C)
        x = qkv_ref[rows, :]
        before = carry_ref[...] if r0 == 0 else qkv_ref[r0 - SUBLANES:r0, :]
        if r0 + C == tt:
            carry_ref[...] = x[C - SUBLANES:, :]
        row8 = _iota((SUBLANES, 1), 0)
        conv = x * cw_ref[CONV_K - 1:CONV_K, :]
        for i in range(1, CONV_K):
            xs = pltpu.roll(x, i, axis=0)
            top = jnp.where(row8 < i, pltpu.roll(before, i, axis=0), xs[:SUBLANES])
            xs = jnp.concatenate([top, xs[SUBLANES:]], axis=0)
            conv = conv + xs * cw_ref[CONV_K - 1 - i:CONV_K - i, :]
            yield
        q, k, v = _gdn_qkv(conv)
        q_s[rows, :] = q
        k_s[rows, :] = k
        v_s[rows, :] = v
        yield

    i2, j2 = _iota((2 * C, 2 * C), 0), _iota((2 * C, 2 * C), 1)
    same_head = _group(i2, C) == _group(j2, C)
    strict = same_head & (i2 > j2)
    incl = same_head & (i2 >= j2)
    eye = jnp.where(i2 == j2, 1.0, 0.0)
    first = _iota((2 * C, 1), 0) < C

    def solve_chain(pr, r0, stash):
        sls = [slice(h * B_HEAD, (h + 1) * B_HEAD) for h in (2 * pr, 2 * pr + 1)]
        ld = lambda ref: jnp.concatenate([ref[pl.ds(r0, C), sl] for sl in sls], axis=0)
        q_h, k_h, v_h, gc_h, beta_h = ld(q_s), ld(k_s), ld(v_s), ld(gc_s), ld(beta_s)
        diff = gc_h - gc_h.T
        dm = jnp.where(incl, jnp.exp(jnp.where(incl, diff, 0.0)), 0.0)
        kb = k_h * beta_h
        QK = _mm_nt(jnp.concatenate([kb, q_h], axis=0), k_h)
        yield
        N = -jnp.where(strict, QK[:2 * C] * dm, 0.0)
        qk = QK[2 * C:] * dm
        egc = jnp.exp(gc_h)
        X = jnp.concatenate([v_h * beta_h, kb * egc], axis=1)
        T = yield from _nilpotent_inverse(N, eye)
        UW = _mm(T, X)
        yield
        g_last = jnp.where(first, gc_h[C - 1:C, :], gc_h[2 * C - 1:2 * C, :])
        stash[pr] = dict(u=UW[:, :B_HEAD], w=UW[:, B_HEAD:].astype(BF16), qd=(q_h * egc).astype(BF16),
                         qk=qk.astype(BF16), k_dec_t=(k_h * jnp.exp(g_last - gc_h)).T.astype(BF16),
                         decay=[jnp.exp(gc_h[(j + 1) * C - 1:(j + 1) * C, :]) for j in range(2)])

    def state_chain(pr, r0, stash):
        heads = (2 * pr, 2 * pr + 1)
        sls = [slice(h * B_HEAD, (h + 1) * B_HEAD) for h in heads]
        s = stash[pr]
        wS, qS, S_old = [], [], []
        for j, h in enumerate(heads):
            S = state_ref[h]
            rows = slice(j * C, (j + 1) * C)
            wq = _dot(jnp.concatenate([s["w"][rows], s["qd"][rows]], axis=0), S.astype(BF16))
            wS.append(wq[:C])
            qS.append(wq[C:])
            S_old.append(S)
        yield
        v_new = s["u"] - jnp.concatenate(wS, axis=0)
        o = jnp.concatenate(qS, axis=0) + _dot(s["qk"], v_new.astype(BF16))
        for j, h in enumerate(heads):
            mine = first if j == 0 else jnp.logical_not(first)
            rows = slice(j * C, (j + 1) * C)
            state_ref[h] = S_old[j] * s["decay"][j] + _mm(s["k_dec_t"], jnp.where(mine, v_new, 0.0))
            o_ref[pl.ds(r0, C), sls[j]] = _gdn_out(o[rows], nw_ref[...], z_ref[pl.ds(r0, C), sls[j]])
        yield

    return (lambda r0: [token_chain(r0)],
            lambda r0, stash: [solve_chain(pr, r0, stash) for pr in range(B_HEADS // 2)],
            lambda r0, stash: [state_chain(pr, r0, stash) for pr in range(B_HEADS // 2)])


def _gdn_sample_body(qkv_ref, hist_ref, ab_ref, z_ref, s_ref, cw_ref, alog_ref, dtb_ref, nw_ref,
                     o_ref, sout_ref, hist_out_ref, q_s, k_s, v_s, eg_s, beta_s, o_s, *, steps):
    rows = ab_ref.shape[0]
    seqs = rows // steps
    per_tile = SUBLANES // steps
    x = qkv_ref[...]
    hist_out_ref[...] = _last_step_rows(x, seqs, steps, CONV_K - 1)
    hist = hist_ref[...]
    t_idx = _iota((rows, 1), 0) & (steps - 1)
    conv = x * cw_ref[CONV_K - 1:CONV_K, :]
    for i in range(1, CONV_K):
        tap = jnp.where(t_idx >= i, pltpu.roll(x, i, axis=0),
                        _first_step_rows(rows, seqs, steps, hist, offset=CONV_K - 1 - i))
        conv = conv + tap * cw_ref[CONV_K - 1 - i:CONV_K - i, :]
    q, k, v = _gdn_qkv(conv)
    g, beta = _gdn_gates(ab_ref[...], alog_ref[...], dtb_ref[...])
    q_s[...] = q
    k_s[...] = k
    v_s[...] = v
    eg_s[...] = jnp.exp(g)
    beta_s[...] = beta
    diag = _iota((B_HEAD, B_HEAD), 0) == _iota((B_HEAD, B_HEAD), 1)
    ones = jnp.ones((B_HEAD, B_HEAD), BF16)

    def to_cols(tile_rows):
        diags = jnp.concatenate([jnp.where(diag, rv, 0.0) for rv in tile_rows], axis=0)
        return _mm(diags, ones)

    def tile(j, carry):
        r0 = pl.multiple_of(j * SUBLANES, SUBLANES)
        ld = lambda ref: ref[pl.ds(r0, SUBLANES), :]
        q_t, k_t, v_t, eg_t, beta_t = ld(q_s), ld(k_s), ld(v_s), ld(eg_s), ld(beta_s)
        out_rows = [[] for _ in range(SUBLANES)]
        for i in range(per_tile):
            for h in range(B_HEADS):
                sl = slice(h * B_HEAD, (h + 1) * B_HEAD)
                rowof = lambda m, rr: m[rr:rr + 1, sl]
                S = s_ref[j * per_tile + i, h]
                for s in range(steps):
                    rr = i * steps + s
                    kq_cols = to_cols([rowof(k_t, rr), rowof(q_t, rr)])
                    k_col, q_col = kq_cols[:B_HEAD], kq_cols[B_HEAD:]
                    S = S * rowof(eg_t, rr)
                    u = jnp.sum(k_col * S, axis=0, keepdims=True)
                    S = S + k_col * (rowof(beta_t, rr) * (rowof(v_t, rr) - u))
                    out_rows[rr].append(jnp.sum(q_col * S, axis=0, keepdims=True))
                sout_ref[j * per_tile + i, h] = S
        o_s[pl.ds(r0, SUBLANES), :] = _rows_to_tile([jnp.concatenate(parts, axis=1) for parts in out_rows])
        return carry

    lax.fori_loop(0, rows // SUBLANES, tile, 0)
    for h in range(B_HEADS):
        sl = slice(h * B_HEAD, (h + 1) * B_HEAD)
        o_ref[:, sl] = _gdn_out(o_s[:, sl], nw_ref[...], z_ref[:, sl])


def _gdn_sample(qkv, hist, ab, z, state, layer, steps, params, seqs):
    n = ab.shape[0]
    nb = state.shape[1]
    rows = seqs * steps
    hist_rows = seqs * (CONV_K - 1)
    assert SUBLANES % steps == 0 and rows % SUBLANES == 0 and hist_rows % SUBLANES == 0
    row = lambda w: pl.BlockSpec((rows, w), lambda i: (i, 0))
    block = (None, seqs, B_HEADS, B_HEAD, B_HEAD)
    sspec = pl.BlockSpec(block, lambda i: (layer, i, 0, 0, 0))
    ospec = pl.BlockSpec(block, lambda i: (0, i, 0, 0, 0))
    return pl.pallas_call(
        functools.partial(_gdn_sample_body, steps=steps),
        grid=(nb // seqs,),
        in_specs=[row(CONV_CH), pl.BlockSpec((hist_rows, CONV_CH), lambda i: (i, 0)), row(LANES), row(B_WIDTH), sspec]
                 + [_const_spec(p.shape) for p in params],
        out_specs=[row(B_WIDTH), ospec, pl.BlockSpec((hist_rows, CONV_CH), lambda i: (i, 0))],
        out_shape=[jax.ShapeDtypeStruct((n, B_WIDTH), F32), jax.ShapeDtypeStruct((1,) + state.shape[1:], F32),
                   jax.ShapeDtypeStruct(hist.shape, F32)],
        scratch_shapes=[pltpu.VMEM((rows, B_WIDTH), F32) for _ in range(6)],
        compiler_params=_cparams(1),
        name="gdn_sample",
    )(qkv, hist, ab, z, state, *params)


def _cuts(widths):
    edges, total = [], 0
    for w in widths[:-1]:
        total += w
        edges.append(total)
    return edges


def _regroup_pa(a):
    r, wd, k, v, ad, gd = jnp.split(a, _cuts((A_WIDTH, A_RANK_W, A_WIDTH, A_WIDTH, A_RANK_A, A_RANK_G)), axis=-1)
    return jnp.concatenate([r, k, v, wd, ad, gd], axis=-1)


def _ungroup_pa(a):
    r, k, v, wd, ad, gd = jnp.split(a, _cuts((A_WIDTH, A_WIDTH, A_WIDTH, A_RANK_W, A_RANK_A, A_RANK_G)), axis=-1)
    return jnp.concatenate([r, wd, k, v, ad, gd], axis=-1)


def _token_tile(n, want):
    tm = want
    while n % tm:
        tm //= 2
    return tm


def kernel(x_prompt, x_sample, state_rwkv, state_rwkv_shift, state_delta, state_conv, ffn1_norm, ffn1_w_gate, ffn1_w_up, ffn1_w_down, mix_norm, w_in, rwkv_mu, rwkv_w0, rwkv_w2, rwkv_a0, rwkv_a2, rwkv_g2, rwkv_k_k, rwkv_k_a, rwkv_r_k, rwkv_lnx_w, rwkv_lnx_b, gdn_conv_w, gdn_A_log, gdn_dt_bias, gdn_norm_w, proj_a, proj_b, w_out, ffn2_norm, ffn2_w_gate, ffn2_w_up, ffn2_w_down, final_norm):
    depth = ffn1_norm.shape[0]
    assert depth == 1, "single-layer trunk"
    Bp, Tp, _ = x_prompt.shape
    Bs, Ts, _ = x_sample.shape
    l = 0
    row = lambda a: a.reshape(1, -1).astype(F32)

    wi = w_in[l].astype(BF16)
    o_b = A_PROJ
    w_all = jnp.concatenate([
        _regroup_pa(wi[:, :A_PROJ]),
        wi[:, o_b:o_b + CONV_CH],
        wi[:, o_b + CONV_CH + 2 * B_HEADS:o_b + B_PROJ],
        wi[:, o_b + B_PROJ:],
        jnp.pad(wi[:, o_b + CONV_CH:o_b + CONV_CH + 2 * B_HEADS], ((0, 0), (0, LANES - 2 * B_HEADS))),
    ], axis=1).astype(BF16)
    ffn1 = (row(ffn1_norm[l]), ffn1_w_gate[l].astype(BF16), ffn1_w_up[l].astype(BF16), ffn1_w_down[l].astype(BF16))
    ffn2 = (row(ffn2_norm[l]), ffn2_w_gate[l].astype(BF16), ffn2_w_up[l].astype(BF16), ffn2_w_down[l].astype(BF16))
    merge_w = (proj_a[l].astype(BF16), proj_b[l].astype(BF16), w_out[l].astype(BF16))
    zw = jnp.zeros((A_RANK_W, A_WIDTH), F32)
    w2a = jnp.concatenate([jnp.concatenate([rwkv_w2[l], zw], axis=1),
                           jnp.concatenate([zw, rwkv_a2[l]], axis=1)], axis=0)
    rwkv_params = (row(_regroup_pa(rwkv_mu[l])), row(rwkv_w0[l]), row(rwkv_a0[l]), row(rwkv_k_k[l]), row(rwkv_k_a[l]),
                   row(rwkv_r_k[l]), row(rwkv_lnx_w[l]), row(rwkv_lnx_b[l]), w2a, rwkv_g2[l].astype(F32))
    pad_lane = lambda a: jnp.pad(a.reshape(1, -1).astype(F32), ((0, 0), (0, LANES - a.size)))
    gdn_params = (gdn_conv_w[l].astype(F32), pad_lane(gdn_A_log[l]), pad_lane(gdn_dt_bias[l]), row(gdn_norm_w[l]))

    def trunk_front(x2):
        n = x2.shape[0]
        h = _ffn(x2, *ffn1, tm=_token_tile(n, 512))
        return (h,) + tuple(_proj(h, row(mix_norm[l]), w_all, tm=_token_tile(n, 512)))

    def trunk_back(h, oa, ob, gates):
        n = h.shape[0]
        return _tail(h, oa, ob, gates, *merge_w, *ffn2, row(final_norm), tm=_token_tile(n, 256))

    xp = x_prompt.reshape(Bp * Tp, D_MODEL)
    h, pa, qkv, z, gates, ab = trunk_front(xp)
    tt = _token_tile(Tp, 256)
    oa, ob, s_pairs, delta_p = _mix_prompt(pa, qkv, ab, z, Bp, Tp, rwkv_params, gdn_params, tt)
    y_prompt = trunk_back(h, oa, ob, gates).reshape(Bp, Tp, D_MODEL)
    sp = s_pairs.reshape(Bp, A_PAIRS, 2, A_HEAD, 2, A_HEAD)
    rwkv_p = jnp.stack([sp[:, :, 0, :, 0], sp[:, :, 1, :, 1]], axis=2).reshape(Bp, A_HEADS, A_HEAD, A_HEAD)
    shift_p = _ungroup_pa(pa.reshape(Bp, Tp, A_PROJ)[:, -1])
    conv_p = qkv.reshape(Bp, Tp, CONV_CH)[:, Tp - (CONV_K - 1):]

    xs = x_sample.reshape(Bs * Ts, D_MODEL)
    h, pa, qkv, z, gates, ab = trunk_front(xs)
    assert Ts & (Ts - 1) == 0 and Ts >= CONV_K - 1, "sample steps: power of two covering the conv history"
    tables = _rwkv_pair_tables(_regroup_pa(rwkv_mu[l]), rwkv_w0[l], rwkv_a0[l], rwkv_k_k[l], rwkv_k_a[l], rwkv_r_k[l],
                               rwkv_lnx_w[l], rwkv_lnx_b[l], rwkv_w2[l], rwkv_a2[l], rwkv_g2[l])
    oa, s_lanes, last_pa = _rwkv_lanes(pa, _regroup_pa(state_rwkv_shift[l].astype(F32)),
                                       jnp.transpose(state_rwkv.astype(F32), (0, 2, 3, 4, 1)), l, Ts, tables)
    rwkv_s = jnp.transpose(s_lanes, (0, 4, 1, 2, 3))
    shift_s = _ungroup_pa(last_pa)
    hist = state_conv[l].astype(F32).reshape(Bs * (CONV_K - 1), CONV_CH)
    ob, delta_s, new_hist = _gdn_sample(qkv, hist, ab, z, state_delta.astype(F32), l, Ts, gdn_params,
                                        seqs=_token_tile(Bs, 16))
    conv_s = new_hist.reshape(Bs, CONV_K - 1, CONV_CH)
    y_sample = trunk_back(h, oa, ob, gates).reshape(Bs, Ts, D_MODEL)

    add_depth = lambda a: a[None]
    return (y_prompt, y_sample,
            add_depth(rwkv_p), add_depth(shift_p), add_depth(delta_p), add_depth(conv_p),
            rwkv_s, add_depth(shift_s), delta_s, add_depth(conv_s))
```

```python
import functools

import jax
import jax.numpy as jnp
from jax import lax
from jax.experimental import pallas as pl
from jax.experimental.pallas import tpu as pltpu

F32 = jnp.float32
BF16 = jnp.bfloat16

D_MODEL = 1024
D_FF = 2816
RMS_EPS = 1e-6
A_HEAD = 64
A_HEADS = 8
A_WIDTH = A_HEADS * A_HEAD
A_RANK_W = 64
A_RANK_A = 64
A_RANK_G = 128
A_PROJ = 3 * A_WIDTH + A_RANK_W + A_RANK_A + A_RANK_G
A_LNX_EPS = 64e-5
A_PAIRS = A_HEADS // 2
B_HEADS = 4
B_HEAD = 128
B_WIDTH = B_HEADS * B_HEAD
CONV_K = 4
CONV_CH = 3 * B_WIDTH
B_PROJ = CONV_CH + 2 * B_HEADS + B_WIDTH
GATE_COLS = 2 * D_MODEL
LANES = 128
SUBLANES = 8
VMEM_LIMIT_BYTES = 56 * 1024 * 1024
CHUNK = 64
PA_R, PA_K, PA_V, PA_WA, PA_G = 0, A_WIDTH, 2 * A_WIDTH, 3 * A_WIDTH, 3 * A_WIDTH + A_RANK_W + A_RANK_A
PROJ_SPLITS = (A_PROJ, CONV_CH, B_WIDTH, GATE_COLS, LANES)


def _cparams(n_grid_dims):
    return pltpu.CompilerParams(dimension_semantics=("arbitrary",) * n_grid_dims,
                                vmem_limit_bytes=VMEM_LIMIT_BYTES)


def _const_spec(shape):
    nd = len(shape)
    return pl.BlockSpec(shape, lambda *_: (0,) * nd, pipeline_mode=pl.Buffered(1))


def _dot(a, b):
    return jnp.dot(a, b, preferred_element_type=F32)


def _dot_nt(a, b):
    return lax.dot_general(a, b, (((1,), (1,)), ((), ())), preferred_element_type=F32)


def _split3(x):
    hi = x.astype(BF16)
    rest = x - hi.astype(F32)
    mid = rest.astype(BF16)
    lo = (rest - mid.astype(F32)).astype(BF16)
    return hi, mid, lo


def _mm(a, b):
    return _dot(a.astype(BF16), b.astype(BF16))


def _mm_nt(a, b):
    return _dot_nt(a.astype(BF16), b.astype(BF16))


def _sel_mm(sel, x):
    return _dot(jnp.concatenate([sel, sel, sel], axis=1), jnp.concatenate(_split3(x), axis=0))


def _mm_sel(x, sel):
    return _dot(jnp.concatenate(_split3(x), axis=1), jnp.concatenate([sel, sel, sel], axis=0))


INV_BASE = 8


def _nilpotent_inverse(n, eye):
    width = n.shape[1]
    bi, bj = _iota(n.shape, 0), _iota(n.shape, 1)
    same = lambda size: _group(bi, size) == _group(bj, size)
    d = jnp.where(same(INV_BASE), n, 0.0)
    t = eye + d
    d = _mm(d, d)
    yield
    for _ in range(INV_BASE.bit_length() - 3):
        both = _mm(d, jnp.concatenate([t, d], axis=1))
        yield
        t = t + both[:, :width]
        d = both[:, width:]
    t = t + _mm(d, t)
    yield
    size = INV_BASE
    while size < CHUNK:
        coupling = jnp.where(same(2 * size) & jnp.logical_not(same(size)), n, 0.0)
        tb = t.astype(BF16)
        lt = _mm(coupling, tb)
        yield
        t = t + _mm(tb, lt)
        yield
        size *= 2
    return t


def _round_robin(chains):
    chains = list(chains)
    while chains:
        for chain in list(chains):
            try:
                next(chain)
            except StopIteration:
                chains.remove(chain)


def _rms(x, w):
    return x * lax.rsqrt(jnp.mean(x * x, axis=-1, keepdims=True) + RMS_EPS) * w


def _sigmoid(x):
    return 1.0 / (1.0 + jnp.exp(-x))


def _silu(x):
    return x * _sigmoid(x)


def _softplus(x):
    return jnp.maximum(x, 0.0) + jnp.log(1.0 + jnp.exp(-jnp.abs(x)))


def _iota(shape, dim):
    return lax.broadcasted_iota(jnp.int32, shape, dim)


def _group(idx, size):
    assert size & (size - 1) == 0
    return lax.shift_right_logical(idx, size.bit_length() - 1)


def _one_hot(cond):
    return jnp.where(cond, 1.0, 0.0).astype(BF16)


def _rows_to_tile(rows):
    rid = _iota((SUBLANES, 1), 0)
    tile = jnp.zeros((SUBLANES, rows[0].shape[1]), F32)
    for i, row in enumerate(rows):
        tile = jnp.where(rid == i, row, tile)
    return tile


def _swiglu_half_step(x, nw, wg_ref, wu_ref, wd_ref):
    xn = _rms(x, nw).astype(BF16)
    g = _dot(xn, wg_ref[...])
    u = _dot(xn, wu_ref[...])
    act = (_silu(g) * u).astype(BF16)
    return x + 0.5 * _dot(act, wd_ref[...])


def _ffn_body(x_ref, nw_ref, wg_ref, wu_ref, wd_ref, o_ref):
    o_ref[...] = _swiglu_half_step(x_ref[...], nw_ref[...], wg_ref, wu_ref, wd_ref)


def _ffn(x, nw, wg, wu, wd, tm):
    n = x.shape[0]
    return pl.pallas_call(
        _ffn_body,
        grid=(n // tm,),
        in_specs=[pl.BlockSpec((tm, D_MODEL), lambda i: (i, 0)),
                  _const_spec((1, D_MODEL)),
                  _const_spec((D_MODEL, D_FF)), _const_spec((D_MODEL, D_FF)), _const_spec((D_FF, D_MODEL))],
        out_specs=pl.BlockSpec((tm, D_MODEL), lambda i: (i, 0)),
        out_shape=jax.ShapeDtypeStruct((n, D_MODEL), F32),
        compiler_params=_cparams(1),
        name="ffn1",
    )(x, nw, wg, wu, wd)


def _proj_body(h_ref, nw_ref, w_ref, *o_refs):
    u = _rms(h_ref[...], nw_ref[...]).astype(BF16)
    off = 0
    for o_ref, width in zip(o_refs, PROJ_SPLITS):
        o_ref[...] = _dot(u, w_ref[:, off:off + width])
        off += width


def _proj(h, nw, w_all, tm):
    n = h.shape[0]
    cols = sum(PROJ_SPLITS)
    return pl.pallas_call(
        _proj_body,
        grid=(n // tm,),
        in_specs=[pl.BlockSpec((tm, D_MODEL), lambda i: (i, 0)),
                  _const_spec((1, D_MODEL)), _const_spec((D_MODEL, cols))],
        out_specs=[pl.BlockSpec((tm, w), lambda i: (i, 0)) for w in PROJ_SPLITS],
        out_shape=[jax.ShapeDtypeStruct((n, w), F32) for w in PROJ_SPLITS],
        compiler_params=_cparams(1),
        name="proj",
    )(h, nw, w_all)


def _tail_body(h_ref, oa_ref, ob_ref, gates_ref, pa_ref, pb_ref, wo_ref, nw_ref, wg_ref, wu_ref, wd_ref,
               fn_ref, o_ref):
    ma = _dot(oa_ref[...].astype(BF16), pa_ref[...])
    mb = _dot(ob_ref[...].astype(BF16), pb_ref[...])
    merged = _sigmoid(gates_ref[:, :D_MODEL]) * ma + _sigmoid(gates_ref[:, D_MODEL:]) * mb
    h = h_ref[...] + _dot(merged.astype(BF16), wo_ref[...])
    h = _swiglu_half_step(h, nw_ref[...], wg_ref, wu_ref, wd_ref)
    o_ref[...] = _rms(h, fn_ref[...])


def _tail(h, oa, ob, gates, proj_a, proj_b, w_out, nw, wg, wu, wd, fn, tm):
    n = h.shape[0]
    row = lambda w: pl.BlockSpec((tm, w), lambda i: (i, 0))
    return pl.pallas_call(
        _tail_body,
        grid=(n // tm,),
        in_specs=[row(D_MODEL), row(A_WIDTH), row(B_WIDTH), row(GATE_COLS),
                  _const_spec((A_WIDTH, D_MODEL)), _const_spec((B_WIDTH, D_MODEL)),
                  _const_spec((D_MODEL, D_MODEL)), _const_spec((1, D_MODEL)),
                  _const_spec((D_MODEL, D_FF)), _const_spec((D_MODEL, D_FF)), _const_spec((D_FF, D_MODEL)),
                  _const_spec((1, D_MODEL))],
        out_specs=row(D_MODEL),
        out_shape=jax.ShapeDtypeStruct((n, D_MODEL), F32),
        compiler_params=_cparams(1),
        name="tail",
    )(h, oa, ob, gates, proj_a, proj_b, w_out, nw, wg, wu, wd, fn)


def _rwkv_token_math(x, prev, mu, w0, a0, k_k, k_a, w2a, g2):
    pm = x + (prev - x) * mu
    r = pm[:, PA_R:PA_R + A_WIDTH]
    k = pm[:, PA_K:PA_K + A_WIDTH]
    v = pm[:, PA_V:PA_V + A_WIDTH]
    wa = pm[:, PA_WA:PA_WA + LANES]
    gd = pm[:, PA_G:PA_G + A_RANK_G]
    lane = _iota((1, LANES), 1)
    lora_in = jnp.where(lane < A_RANK_W, jnp.tanh(wa), wa)
    lora = _mm(lora_in, w2a)
    g = _mm(_sigmoid(gd), g2)
    yield
    w_log = -_softplus(-(w0 + lora[:, :A_WIDTH])) - 0.5
    log_decay = -jnp.exp(w_log)
    yield
    a = _sigmoid(a0 + lora[:, A_WIDTH:])
    kk_raw = k * k_k
    k_mod = k * (1.0 + (a - 1.0) * k_a)
    return r, k_mod, v, kk_raw, a, log_decay, g


def _pair_mask(rows_per_head):
    shape = (2 * rows_per_head, LANES)
    return _group(_iota(shape, 0), rows_per_head) == _group(_iota(shape, 1), A_HEAD)


def _rwkv_prompt_part(pa_ref, mu_ref, w0_ref, a0_ref, kk_ref, ka_ref, rk_ref, lnw_ref, lnb_ref, w2a_ref, g2_ref,
                      o_ref, carry_ref, state_ref, r_s, k_s, v_s, kkraw_s, a_s, cum_s, ld_s, g_s):
    tt = pa_ref.shape[0]
    C = CHUNK
    lower = _one_hot(_iota((C, C), 1) <= _iota((C, C), 0))

    def token_chain(r0):
        rows = slice(r0, r0 + C)
        x = pa_ref[rows, :]
        before = carry_ref[SUBLANES - 1:SUBLANES, :] if r0 == 0 else pa_ref[r0 - 1:r0, :]
        prev = jnp.where(_iota((C, 1), 0) == 0, before, pltpu.roll(x, 1, axis=0))
        if r0 + C == tt:
            carry_ref[...] = x[C - SUBLANES:, :]
        r, k_mod, v, kk_raw, a, log_decay, g = yield from _rwkv_token_math(
            x, prev, mu_ref[...], w0_ref[...], a0_ref[...], kk_ref[...], ka_ref[...], w2a_ref[...], g2_ref[...])
        r_s[rows, :] = r
        k_s[rows, :] = k_mod
        v_s[rows, :] = v
        kkraw_s[rows, :] = kk_raw
        a_s[rows, :] = a
        g_s[rows, :] = g
        ld_s[rows, :] = log_decay
        yield
        cum_s[rows, :] = sum(_dot(lower, piece) for piece in _split3(log_decay))
        yield

    mask = _pair_mask(C)
    i2, j2 = _iota((2 * C, 2 * C), 0), _iota((2 * C, 2 * C), 1)
    strict = i2 > j2
    incl = i2 >= j2
    eye = jnp.where(i2 == j2, 1.0, 0.0)
    dup = lambda m: jnp.concatenate([m, m], axis=0)
    stack = lambda m: jnp.where(mask, dup(m), 0.0)

    def solve_chain(p, r0, stash):
        sl = slice(p * LANES, (p + 1) * LANES)
        ld = lambda ref: ref[pl.ds(r0, C), sl]
        r_p, k_p, v_p, a_p, cum, ldec = ld(r_s), ld(k_s), ld(v_s), ld(a_s), ld(cum_s), ld(ld_s)
        einc = jnp.exp(cum)
        eex = jnp.exp(cum - ldec)
        einv = jnp.exp(-cum)
        etail = jnp.exp(cum[C - 1:C, :] - cum)
        kks = stack(ld(kkraw_s))
        kks = kks * jnp.minimum(lax.rsqrt(jnp.sum(kks * kks, axis=-1, keepdims=True)), 1e12)
        As = -kks * dup(eex)
        Bs = kks * dup(a_p * einv)
        Bh = kks * dup(a_p * etail)
        Ks = stack(k_p * einv)
        Kh = stack(k_p * etail)
        Rs = stack(r_p * einc)
        Vs = stack(v_p)
        AR = jnp.concatenate([As, Rs], axis=0).astype(BF16)
        Vb = Vs.astype(BF16)
        G = _mm_nt(AR, jnp.concatenate([Bs, Ks], axis=0))
        yield
        Aab = jnp.where(strict, G[:2 * C, :2 * C], 0.0)
        Aak = jnp.where(strict, G[:2 * C, 2 * C:], 0.0)
        Arb = jnp.where(incl, G[2 * C:, :2 * C], 0.0)
        Ark = jnp.where(incl, G[2 * C:, 2 * C:], 0.0)
        Y = _mm(Aak, Vb)
        yield
        T = yield from _nilpotent_inverse(Aab, eye)
        WU = _mm(T, jnp.concatenate([AR[:2 * C], Y.astype(BF16)], axis=1))
        yield
        bonus = jnp.sum(stack(r_p * k_p * rk_ref[:, sl]), axis=-1, keepdims=True) * Vs
        stash[p] = dict(WU=WU, R=AR[2 * C:], Vs=Vs, bonus=bonus,
                        Aro=jnp.concatenate([Arb, Ark], axis=1).astype(BF16),
                        BKh=jnp.concatenate([Bh, Kh], axis=0).astype(BF16), decay=einc[C - 1:C, :])

    def state_chain(p, r0, stash):
        sl = slice(p * LANES, (p + 1) * LANES)
        s = stash[p]
        S = state_ref[p]
        Sb = S.astype(BF16)
        W = _mm_nt(s["WU"][:, :LANES], Sb) + s["WU"][:, LANES:]
        yield
        WV = jnp.concatenate([W, s["Vs"]], axis=0)
        O = _dot_nt(s["R"], Sb) + _mm(s["Aro"], WV)
        state_ref[p] = S * s["decay"] + _mm(WV.T, s["BKh"])
        yield
        mean = jnp.sum(O, axis=-1, keepdims=True) * (1.0 / A_HEAD)
        cen = jnp.where(mask, O - mean, 0.0)
        var = jnp.sum(cen * cen, axis=-1, keepdims=True) * (1.0 / A_HEAD)
        normed = jnp.where(mask, cen * lax.rsqrt(var + A_LNX_EPS) * lnw_ref[:, sl] + lnb_ref[:, sl], 0.0)
        full = normed + s["bonus"]
        o_ref[pl.ds(r0, C), sl] = (full[:C] + full[C:]) * g_s[pl.ds(r0, C), sl]

    return (lambda r0: [token_chain(r0)],
            lambda r0, stash: [solve_chain(p, r0, stash) for p in range(A_PAIRS)],
            lambda r0, stash: [state_chain(p, r0, stash) for p in range(A_PAIRS)])


CHUNKS_IN_FLIGHT = 2
N_RWKV_PARAMS = 10
N_GDN_PARAMS = 4
N_RWKV_SCRATCH = 8
N_GDN_SCRATCH = 5


def _mix_prompt_body(pa_ref, qkv_ref, ab_ref, z_ref, *refs):
    refs = list(refs)
    take = lambda n: [refs.pop(0) for _ in range(n)]
    rwkv_prm, gdn_prm = take(N_RWKV_PARAMS), take(N_GDN_PARAMS)
    oa_ref, ob_ref, sfa_ref, sfb_ref = take(4)
    carry_a, state_a, carry_b, state_b = take(4)
    rwkv_scr, gdn_scr = take(N_RWKV_SCRATCH), take(N_GDN_SCRATCH)
    t = pl.program_id(1)
    tt = pa_ref.shape[0]

    @pl.when(t == 0)
    def _():
        for ref in (carry_a, state_a, carry_b, state_b):
            ref[...] = jnp.zeros_like(ref)

    rwkv_token, rwkv_solve, rwkv_state = _rwkv_prompt_part(pa_ref, *rwkv_prm, oa_ref, carry_a, state_a, *rwkv_scr)
    gdn_token, gdn_solve, gdn_state = _gdn_prompt_part(qkv_ref, ab_ref, z_ref, *gdn_prm, ob_ref, carry_b, state_b,
                                                       *gdn_scr)

    n_chunks = tt // CHUNK
    group = min(CHUNKS_IN_FLIGHT, n_chunks)
    n_groups = n_chunks // group
    stashes = [({}, {}) for _ in range(n_chunks)]
    chunks_of = lambda gi: range(gi * group, (gi + 1) * group) if 0 <= gi < n_groups else ()

    def in_sequence(per_chunk_chains):
        for chains in zip(*per_chunk_chains):
            for chain in chains:
                yield from chain

    for gi in range(n_groups + 2):
        chains = []
        for c in chunks_of(gi):
            chains += rwkv_token(c * CHUNK) + gdn_token(c * CHUNK)
        for c in chunks_of(gi - 1):
            chains += rwkv_solve(c * CHUNK, stashes[c][0]) + gdn_solve(c * CHUNK, stashes[c][1])
        state_chains = [rwkv_state(c * CHUNK, stashes[c][0]) + gdn_state(c * CHUNK, stashes[c][1])
                        for c in chunks_of(gi - 2)]
        if state_chains:
            chains += [in_sequence([per_chunk[i:i + 1] for per_chunk in state_chains])
                       for i in range(len(state_chains[0]))]
        _round_robin(chains)

    @pl.when(t == pl.num_programs(1) - 1)
    def _():
        sfa_ref[0] = state_a[...]
        sfb_ref[0] = state_b[...]


def _mix_prompt(pa, qkv, ab, z, B, T, rwkv_params, gdn_params, tt):
    n = pa.shape[0]
    nt = T // tt
    assert len(rwkv_params) == N_RWKV_PARAMS and len(gdn_params) == N_GDN_PARAMS
    rows = lambda w: pl.BlockSpec((tt, w), lambda b, t: (b * nt + t, 0))
    state = lambda: pl.BlockSpec((1, 4, LANES, LANES), lambda b, t: (b, 0, 0, 0))
    big = lambda: pltpu.VMEM((tt, A_WIDTH), F32)
    return pl.pallas_call(
        _mix_prompt_body,
        grid=(B, nt),
        in_specs=[rows(A_PROJ), rows(CONV_CH), rows(LANES), rows(B_WIDTH)]
                 + [_const_spec(p.shape) for p in rwkv_params + gdn_params],
        out_specs=[rows(A_WIDTH), rows(B_WIDTH), state(), state()],
        out_shape=[jax.ShapeDtypeStruct((n, A_WIDTH), F32), jax.ShapeDtypeStruct((n, B_WIDTH), F32),
                   jax.ShapeDtypeStruct((B, A_PAIRS, LANES, LANES), F32),
                   jax.ShapeDtypeStruct((B, B_HEADS, B_HEAD, B_HEAD), F32)],
        scratch_shapes=[pltpu.VMEM((SUBLANES, A_PROJ), F32), pltpu.VMEM((A_PAIRS, LANES, LANES), F32),
                        pltpu.VMEM((SUBLANES, CONV_CH), F32), pltpu.VMEM((B_HEADS, B_HEAD, B_HEAD), F32)]
                       + [big() for _ in range(N_RWKV_SCRATCH + N_GDN_SCRATCH)],
        compiler_params=_cparams(2),
        name="mix_prompt",
    )(pa, qkv, ab, z, *rwkv_params, *gdn_params)


def _first_step_rows(rows, seqs, steps, state_rows, offset=0):
    hist = state_rows.shape[0] // seqs
    r, c = _iota((rows, seqs * hist), 0), _iota((rows, seqs * hist), 1)
    t = r & (steps - 1)
    sel = _one_hot((c == _group(r, steps) * hist + offset + t) & (t < hist - offset))
    return sum(_dot(sel, piece) for piece in _split3(state_rows))


def _last_step_rows(x, seqs, steps, keep):
    r, c = _iota((seqs * keep, seqs * steps), 0), _iota((seqs * keep, seqs * steps), 1)
    i = (c & (steps - 1)) - (steps - keep)
    sel = _one_hot((i >= 0) & (r == _group(c, steps) * keep + i))
    return sum(_dot(sel, piece) for piece in _split3(x))


ROWP = dict(mu_r=0, mu_k=1, mu_v=2, w0=3, a0=4, k_k=5, k_a=6, r_k=7, lnw=8, lnb=9)
ROWP_ROWS = 16
VALUE_GROUP = SUBLANES


def _rwkv_lanes_body(par_ref, pak_ref, pav_ref, paw_ref, pag_ref, shr_ref, shk_ref, shv_ref, shw_ref, shg_ref,
                     pa_ref, s_ref, rowp_ref, shared_ref, w2a_ref, g2_ref,
                     o_ref, sout_ref, shift_out_ref,
                     tr_s, ot_s, *, steps):
    rows = par_ref.shape[0]
    B = rows // steps
    p = pl.program_id(0)
    rp = lambda name: rowp_ref[ROWP[name]:ROWP[name] + 1, :]

    @pl.when(p == 0)
    def _():
        shift_out_ref[...] = _last_step_rows(pa_ref[...], B, steps, 1)

    def lerp(x_ref, first_ref, mu):
        per_step = [x_ref[pl.ds(t, B, stride=steps), :] for t in range(steps)]
        x = jnp.concatenate(per_step, axis=0)
        prev = jnp.concatenate([first_ref[...]] + per_step[:-1], axis=0)
        return x + (prev - x) * mu

    r = lerp(par_ref, shr_ref, rp("mu_r"))
    k = lerp(pak_ref, shk_ref, rp("mu_k"))
    v = lerp(pav_ref, shv_ref, rp("mu_v"))
    wa = lerp(paw_ref, shw_ref, shared_ref[0:1, :])
    gd = lerp(pag_ref, shg_ref, shared_ref[1:2, :])
    lane = _iota((1, LANES), 1)
    lora = _mm(jnp.where(lane < A_RANK_W, jnp.tanh(wa), wa), w2a_ref[...])
    g = _mm(_sigmoid(gd), g2_ref[...])
    w_log = -_softplus(-(rp("w0") + lora[:, :LANES])) - 0.5
    decay = jnp.exp(-jnp.exp(w_log))
    a = _sigmoid(rp("a0") + lora[:, LANES:])
    k_mod = k * (1.0 + (a - 1.0) * rp("k_a"))
    hi, hj = _iota((LANES, LANES), 0), _iota((LANES, LANES), 1)
    pair_ones = _one_hot(_group(hi, A_HEAD) == _group(hj, A_HEAD))
    head_sum = lambda m: _mm_sel(m, pair_ones)
    kk_raw = k * rp("k_k")
    kk = kk_raw * jnp.minimum(lax.rsqrt(head_sum(kk_raw * kk_raw)), 1e12)
    names = ("nkk", "beta", "decay", "k", "r", "v")
    for idx, m in enumerate((-kk, kk * a, decay, k_mod, r, v)):
        for t in range(steps):
            tr_s[idx, t] = m[t * B:(t + 1) * B, :].T
    at = lambda name, t: tr_s.at[names.index(name), t]

    def group(gi, carry):
        j = gi // (A_HEAD // VALUE_GROUP)
        v0 = (gi % (A_HEAD // VALUE_GROUP)) * VALUE_GROUP
        keys = lambda name, t: at(name, t)[pl.ds(pl.multiple_of(j * A_HEAD, A_HEAD), A_HEAD), :]
        v_rows = [at("v", t)[pl.ds(pl.multiple_of(gi * VALUE_GROUP, VALUE_GROUP), VALUE_GROUP), :]
                  for t in range(steps)]
        outs = [[] for _ in range(steps)]
        for i in range(VALUE_GROUP):
            S = s_ref[j, v0 + i]
            for t in range(steps):
                sa = jnp.sum(S * keys("nkk", t), axis=0, keepdims=True)
                S = S * keys("decay", t) + sa * keys("beta", t) + v_rows[t][i:i + 1, :] * keys("k", t)
                outs[t].append(jnp.sum(S * keys("r", t), axis=0, keepdims=True))
            sout_ref[j, v0 + i] = S
        for t in range(steps):
            ot_s[t, pl.ds(pl.multiple_of(gi * VALUE_GROUP, VALUE_GROUP), VALUE_GROUP), :] = _rows_to_tile(outs[t])
        return carry

    lax.fori_loop(0, 2 * A_HEAD // VALUE_GROUP, group, 0)
    o = jnp.concatenate([ot_s[t].T for t in range(steps)], axis=0)
    mean = head_sum(o) * (1.0 / A_HEAD)
    cen = o - mean
    var = head_sum(cen * cen) * (1.0 / A_HEAD)
    o = cen * lax.rsqrt(var + A_LNX_EPS) * rp("lnw") + rp("lnb")
    o = (o + head_sum(r * k_mod * rp("r_k")) * v) * g
    for t in range(steps):
        o_ref[pl.ds(t, B, stride=steps), :] = o[t * B:(t + 1) * B, :]


def _rwkv_lanes(pa, shift, state_t, layer, steps, tables):
    rowp, shared, w2a_p, g2_p = tables
    n = pa.shape[0]
    B = state_t.shape[-1]
    assert n == B * steps and B == LANES and steps & (steps - 1) == 0
    col = lambda rows_, j: pl.BlockSpec((rows_, LANES), lambda p: (0, j(p)))
    groups = [lambda p: p, lambda p: A_PAIRS + p, lambda p: 2 * A_PAIRS + p,
              lambda p: 3 * A_PAIRS, lambda p: 3 * A_PAIRS + 1]
    block = (None, 2, A_HEAD, A_HEAD, B)
    return pl.pallas_call(
        functools.partial(_rwkv_lanes_body, steps=steps),
        grid=(A_PAIRS,),
        in_specs=[col(n, j) for j in groups] + [col(B, j) for j in groups]
                 + [_const_spec((n, A_PROJ)), pl.BlockSpec(block, lambda p: (layer, p, 0, 0, 0)),
                    pl.BlockSpec((None, ROWP_ROWS, LANES), lambda p: (p, 0, 0)), _const_spec(shared.shape),
                    pl.BlockSpec((None, LANES, 2 * LANES), lambda p: (p, 0, 0)),
                    pl.BlockSpec((None, LANES, LANES), lambda p: (p, 0, 0))],
        out_specs=[pl.BlockSpec((n, LANES), lambda p: (0, p)), pl.BlockSpec(block, lambda p: (0, p, 0, 0, 0)),
                   pl.BlockSpec((B, A_PROJ), lambda p: (0, 0))],
        out_shape=[jax.ShapeDtypeStruct((n, A_WIDTH), F32), jax.ShapeDtypeStruct((1,) + state_t.shape[1:], F32),
                   jax.ShapeDtypeStruct((B, A_PROJ), F32)],
        scratch_shapes=[pltpu.VMEM((6, steps, LANES, B), F32), pltpu.VMEM((steps, LANES, B), F32)],
        compiler_params=_cparams(1),
        name="rwkv_sample",
    )(*([pa] * 5), *([shift] * 5), pa, state_t, rowp, shared, w2a_p, g2_p)


def _rwkv_pair_tables(mu, w0, a0, k_k, k_a, r_k, lnw, lnb, w2, a2, g2):
    per_pair = lambda a: a.reshape(A_PAIRS, 1, LANES)
    rows = {"mu_r": mu[:A_WIDTH], "mu_k": mu[A_WIDTH:2 * A_WIDTH], "mu_v": mu[2 * A_WIDTH:3 * A_WIDTH],
            "w0": w0, "a0": a0, "k_k": k_k, "k_a": k_a, "r_k": r_k.reshape(-1), "lnw": lnw, "lnb": lnb}
    table = jnp.concatenate([per_pair(rows[name].astype(F32)) for name in sorted(ROWP, key=ROWP.get)]
                            + [jnp.zeros((A_PAIRS, ROWP_ROWS - len(ROWP), LANES), F32)], axis=1)
    shared = jnp.concatenate([mu[3 * A_WIDTH:3 * A_WIDTH + LANES].reshape(1, LANES),
                              mu[3 * A_WIDTH + LANES:].reshape(1, LANES),
                              jnp.zeros((SUBLANES - 2, LANES), F32)], axis=0).astype(F32)
    by_pair = lambda w: jnp.transpose(w.astype(F32).reshape(w.shape[0], A_PAIRS, LANES), (1, 0, 2))
    zeros = jnp.zeros((A_PAIRS, A_RANK_W, LANES), F32)
    w2a_p = jnp.concatenate([jnp.concatenate([by_pair(w2), zeros], axis=2),
                             jnp.concatenate([zeros, by_pair(a2)], axis=2)], axis=1)
    return table, shared, w2a_p, by_pair(g2)


def _gdn_qkv(conv):
    c = _silu(conv)
    qs, ks = [], []
    for h in range(B_HEADS):
        q = c[:, h * B_HEAD:(h + 1) * B_HEAD]
        k = c[:, B_WIDTH + h * B_HEAD:B_WIDTH + (h + 1) * B_HEAD]
        qs.append(q * (lax.rsqrt(jnp.sum(q * q, axis=-1, keepdims=True) + 1e-6) * (B_HEAD ** -0.5)))
        ks.append(k * lax.rsqrt(jnp.sum(k * k, axis=-1, keepdims=True) + 1e-6))
    q = jnp.concatenate(qs, axis=1)
    k = jnp.concatenate(ks, axis=1)
    v = c[:, 2 * B_WIDTH:]
    return q, k, v


def _gdn_gates(ab, alog, dtb):
    lane = _iota((1, LANES), 1)
    g = -jnp.exp(alog) * _softplus(ab + dtb)
    beta = _sigmoid(ab)
    gb = jnp.where(lane < B_HEADS, g, beta)
    si, sj = _iota((LANES, 2 * B_WIDTH), 0), _iota((LANES, 2 * B_WIDTH), 1)
    spread = _mm_sel(gb, _one_hot(si == _group(sj, B_HEAD)))
    return spread[:, :B_WIDTH], spread[:, B_WIDTH:]


def _gdn_out(o, norm_w, z):
    return o * lax.rsqrt(jnp.mean(o * o, axis=-1, keepdims=True) + RMS_EPS) * norm_w * _silu(z)


def _gdn_prompt_part(qkv_ref, ab_ref, z_ref, cw_ref, alog_ref, dtb_ref, nw_ref, o_ref,
                     carry_ref, state_ref, q_s, k_s, v_s, gc_s, beta_s):
    tt = qkv_ref.shape[0]
    C = CHUNK
    g, beta = _gdn_gates(ab_ref[...], alog_ref[...], dtb_ref[...])
    ri, ci = _iota((tt, tt), 0), _iota((tt, tt), 1)
    beta_s[...] = beta
    gc_s[...] = _sel_mm(_one_hot((_group(ri, C) == _group(ci, C)) & (ci <= ri)), g)

    def token_chain(r0):
        rows = slice(r0, r0 + C)
        x = qkv_ref[rows, :]
        before = carry_ref[...] if r0 == 0 else qkv_ref[r0 - SUBLANES:r0, :]
        if r0 + C == tt:
            carry_ref[...] = x[C - SUBLANES:, :]
        row8 = _iota((SUBLANES, 1), 0)

        def shift_rows(cur, halo, i):
            down = pltpu.roll(cur, i, axis=0)
            top = jnp.where(row8 < i, pltpu.roll(halo, i, axis=0), down[:SUBLANES])
            return jnp.concatenate([top, down[SUBLANES:]], axis=0)

        assert CONV_K == 4
        c0, c1, c2, c3 = (cw_ref[i:i + 1, :] for i in range(CONV_K))
        x1 = shift_rows(x, before, 1)
        yield
        older = x * c1 + x1 * c0
        older_halo = before * c1 + pltpu.roll(before, 1, axis=0) * c0
        conv = x * c3 + x1 * c2 + shift_rows(older, older_halo, 2)
        yield
        q, k, v = _gdn_qkv(conv)
        q_s[rows, :] = q
        k_s[rows, :] = k
        v_s[rows, :] = v
        yield

    i2, j2 = _iota((2 * C, 2 * C), 0), _iota((2 * C, 2 * C), 1)
    same_head = _group(i2, C) == _group(j2, C)
    strict = same_head & (i2 > j2)
    incl = same_head & (i2 >= j2)
    eye = jnp.where(i2 == j2, 1.0, 0.0)
    first = _iota((2 * C, 1), 0) < C

    def solve_chain(pr, r0, stash):
        sls = [slice(h * B_HEAD, (h + 1) * B_HEAD) for h in (2 * pr, 2 * pr + 1)]
        ld = lambda ref: jnp.concatenate([ref[pl.ds(r0, C), sl] for sl in sls], axis=0)
        q_h, k_h, v_h, gc_h, beta_h = ld(q_s), ld(k_s), ld(v_s), ld(gc_s), ld(beta_s)
        diff = gc_h - gc_h.T
        dm = jnp.where(incl, jnp.exp(jnp.where(incl, diff, 0.0)), 0.0)
        kb = k_h * beta_h
        QK = _mm_nt(jnp.concatenate([kb, q_h], axis=0), k_h)
        yield
        N = -jnp.where(strict, QK[:2 * C] * dm, 0.0)
        qk = QK[2 * C:] * dm
        egc = jnp.exp(gc_h)
        X = jnp.concatenate([v_h * beta_h, kb * egc], axis=1)
        T = yield from _nilpotent_inverse(N, eye)
        UW = _mm(T, X)
        yield
        g_last = jnp.where(first, gc_h[C - 1:C, :], gc_h[2 * C - 1:2 * C, :])
        stash[pr] = dict(u=UW[:, :B_HEAD], w=UW[:, B_HEAD:].astype(BF16), qd=(q_h * egc).astype(BF16),
                         qk=qk.astype(BF16), k_dec_t=(k_h * jnp.exp(g_last - gc_h)).T.astype(BF16),
                         decay=[jnp.exp(gc_h[(j + 1) * C - 1:(j + 1) * C, :]) for j in range(2)])

    def state_chain(pr, r0, stash):
        heads = (2 * pr, 2 * pr + 1)
        sls = [slice(h * B_HEAD, (h + 1) * B_HEAD) for h in heads]
        s = stash[pr]
        wS, qS, S_old = [], [], []
        for j, h in enumerate(heads):
            S = state_ref[h]
            rows = slice(j * C, (j + 1) * C)
            wq = _dot(jnp.concatenate([s["w"][rows], s["qd"][rows]], axis=0), S.astype(BF16))
            wS.append(wq[:C])
            qS.append(wq[C:])
            S_old.append(S)
        yield
        v_new = s["u"] - jnp.concatenate(wS, axis=0)
        o = jnp.concatenate(qS, axis=0) + _dot(s["qk"], v_new.astype(BF16))
        for j, h in enumerate(heads):
            mine = first if j == 0 else jnp.logical_not(first)
            rows = slice(j * C, (j + 1) * C)
            state_ref[h] = S_old[j] * s["decay"][j] + _mm(s["k_dec_t"], jnp.where(mine, v_new, 0.0))
            o_ref[pl.ds(r0, C), sls[j]] = _gdn_out(o[rows], nw_ref[...], z_ref[pl.ds(r0, C), sls[j]])
        yield

    return (lambda r0: [token_chain(r0)],
            lambda r0, stash: [solve_chain(pr, r0, stash) for pr in range(B_HEADS // 2)],
            lambda r0, stash: [state_chain(pr, r0, stash) for pr in range(B_HEADS // 2)])


def _gdn_sample_body(qkv_ref, hist_ref, ab_ref, z_ref, s_ref, cw_ref, alog_ref, dtb_ref, nw_ref,
                     o_ref, sout_ref, hist_out_ref, q_s, k_s, v_s, eg_s, beta_s, o_s, *, steps):
    rows = ab_ref.shape[0]
    seqs = rows // steps
    per_tile = SUBLANES // steps
    x = qkv_ref[...]
    hist_out_ref[...] = _last_step_rows(x, seqs, steps, CONV_K - 1)
    hist = hist_ref[...]
    t_idx = _iota((rows, 1), 0) & (steps - 1)
    conv = x * cw_ref[CONV_K - 1:CONV_K, :]
    for i in range(1, CONV_K):
        tap = jnp.where(t_idx >= i, pltpu.roll(x, i, axis=0),
                        _first_step_rows(rows, seqs, steps, hist, offset=CONV_K - 1 - i))
        conv = conv + tap * cw_ref[CONV_K - 1 - i:CONV_K - i, :]
    q, k, v = _gdn_qkv(conv)
    g, beta = _gdn_gates(ab_ref[...], alog_ref[...], dtb_ref[...])
    q_s[...] = q
    k_s[...] = k
    v_s[...] = v
    eg_s[...] = jnp.exp(g)
    beta_s[...] = beta
    diag = _iota((B_HEAD, B_HEAD), 0) == _iota((B_HEAD, B_HEAD), 1)
    ones = jnp.ones((B_HEAD, B_HEAD), BF16)

    def to_cols(tile_rows):
        diags = jnp.concatenate([jnp.where(diag, rv, 0.0) for rv in tile_rows], axis=0)
        return _mm(diags, ones)

    def tile(j, carry):
        r0 = pl.multiple_of(j * SUBLANES, SUBLANES)
        ld = lambda ref: ref[pl.ds(r0, SUBLANES), :]
        q_t, k_t, v_t, eg_t, beta_t = ld(q_s), ld(k_s), ld(v_s), ld(eg_s), ld(beta_s)
        out_rows = [[] for _ in range(SUBLANES)]
        for i in range(per_tile):
            for h in range(B_HEADS):
                sl = slice(h * B_HEAD, (h + 1) * B_HEAD)
                rowof = lambda m, rr: m[rr:rr + 1, sl]
                S = s_ref[j * per_tile + i, h]
                for s in range(steps):
                    rr = i * steps + s
                    kq_cols = to_cols([rowof(k_t, rr), rowof(q_t, rr)])
                    k_col, q_col = kq_cols[:B_HEAD], kq_cols[B_HEAD:]
                    S = S * rowof(eg_t, rr)
                    u = jnp.sum(k_col * S, axis=0, keepdims=True)
                    S = S + k_col * (rowof(beta_t, rr) * (rowof(v_t, rr) - u))
                    out_rows[rr].append(jnp.sum(q_col * S, axis=0, keepdims=True))
                sout_ref[j * per_tile + i, h] = S
        o_s[pl.ds(r0, SUBLANES), :] = _rows_to_tile([jnp.concatenate(parts, axis=1) for parts in out_rows])
        return carry

    lax.fori_loop(0, rows // SUBLANES, tile, 0)
    for h in range(B_HEADS):
        sl = slice(h * B_HEAD, (h + 1) * B_HEAD)
        o_ref[:, sl] = _gdn_out(o_s[:, sl], nw_ref[...], z_ref[:, sl])


def _gdn_sample(qkv, hist, ab, z, state, layer, steps, params, seqs):
    n = ab.shape[0]
    nb = state.shape[1]
    rows = seqs * steps
    hist_rows = seqs * (CONV_K - 1)
    assert SUBLANES % steps == 0 and rows % SUBLANES == 0 and hist_rows % SUBLANES == 0
    row = lambda w: pl.BlockSpec((rows, w), lambda i: (i, 0))
    block = (None, seqs, B_HEADS, B_HEAD, B_HEAD)
    sspec = pl.BlockSpec(block, lambda i: (layer, i, 0, 0, 0))
    ospec = pl.BlockSpec(block, lambda i: (0, i, 0, 0, 0))
    return pl.pallas_call(
        functools.partial(_gdn_sample_body, steps=steps),
        grid=(nb // seqs,),
        in_specs=[row(CONV_CH), pl.BlockSpec((hist_rows, CONV_CH), lambda i: (i, 0)), row(LANES), row(B_WIDTH), sspec]
                 + [_const_spec(p.shape) for p in params],
        out_specs=[row(B_WIDTH), ospec, pl.BlockSpec((hist_rows, CONV_CH), lambda i: (i, 0))],
        out_shape=[jax.ShapeDtypeStruct((n, B_WIDTH), F32), jax.ShapeDtypeStruct((1,) + state.shape[1:], F32),
                   jax.ShapeDtypeStruct(hist.shape, F32)],
        scratch_shapes=[pltpu.VMEM((rows, B_WIDTH), F32) for _ in range(6)],
        compiler_params=_cparams(1),
        name="gdn_sample",
    )(qkv, hist, ab, z, state, *params)


def _cuts(widths):
    edges, total = [], 0
    for w in widths[:-1]:
        total += w
        edges.append(total)
    return edges


def _regroup_pa(a):
    r, wd, k, v, ad, gd = jnp.split(a, _cuts((A_WIDTH, A_RANK_W, A_WIDTH, A_WIDTH, A_RANK_A, A_RANK_G)), axis=-1)
    return jnp.concatenate([r, k, v, wd, ad, gd], axis=-1)


def _ungroup_pa(a):
    r, k, v, wd, ad, gd = jnp.split(a, _cuts((A_WIDTH, A_WIDTH, A_WIDTH, A_RANK_W, A_RANK_A, A_RANK_G)), axis=-1)
    return jnp.concatenate([r, wd, k, v, ad, gd], axis=-1)


def _token_tile(n, want):
    tm = want
    while n % tm:
        tm //= 2
    return tm


def kernel(x_prompt, x_sample, state_rwkv, state_rwkv_shift, state_delta, state_conv, ffn1_norm, ffn1_w_gate, ffn1_w_up, ffn1_w_down, mix_norm, w_in, rwkv_mu, rwkv_w0, rwkv_w2, rwkv_a0, rwkv_a2, rwkv_g2, rwkv_k_k, rwkv_k_a, rwkv_r_k, rwkv_lnx_w, rwkv_lnx_b, gdn_conv_w, gdn_A_log, gdn_dt_bias, gdn_norm_w, proj_a, proj_b, w_out, ffn2_norm, ffn2_w_gate, ffn2_w_up, ffn2_w_down, final_norm):
    depth = ffn1_norm.shape[0]
    assert depth == 1, "single-layer trunk"
    Bp, Tp, _ = x_prompt.shape
    Bs, Ts, _ = x_sample.shape
    l = 0
    row = lambda a: a.reshape(1, -1).astype(F32)

    wi = w_in[l].astype(BF16)
    o_b = A_PROJ
    w_all = jnp.concatenate([
        _regroup_pa(wi[:, :A_PROJ]),
        wi[:, o_b:o_b + CONV_CH],
        wi[:, o_b + CONV_CH + 2 * B_HEADS:o_b + B_PROJ],
        wi[:, o_b + B_PROJ:],
        jnp.pad(wi[:, o_b + CONV_CH:o_b + CONV_CH + 2 * B_HEADS], ((0, 0), (0, LANES - 2 * B_HEADS))),
    ], axis=1).astype(BF16)
    ffn1 = (row(ffn1_norm[l]), ffn1_w_gate[l].astype(BF16), ffn1_w_up[l].astype(BF16), ffn1_w_down[l].astype(BF16))
    ffn2 = (row(ffn2_norm[l]), ffn2_w_gate[l].astype(BF16), ffn2_w_up[l].astype(BF16), ffn2_w_down[l].astype(BF16))
    merge_w = (proj_a[l].astype(BF16), proj_b[l].astype(BF16), w_out[l].astype(BF16))
    zw = jnp.zeros((A_RANK_W, A_WIDTH), F32)
    w2a = jnp.concatenate([jnp.concatenate([rwkv_w2[l], zw], axis=1),
                           jnp.concatenate([zw, rwkv_a2[l]], axis=1)], axis=0)
    rwkv_params = (row(_regroup_pa(rwkv_mu[l])), row(rwkv_w0[l]), row(rwkv_a0[l]), row(rwkv_k_k[l]), row(rwkv_k_a[l]),
                   row(rwkv_r_k[l]), row(rwkv_lnx_w[l]), row(rwkv_lnx_b[l]), w2a, rwkv_g2[l].astype(F32))
    pad_lane = lambda a: jnp.pad(a.reshape(1, -1).astype(F32), ((0, 0), (0, LANES - a.size)))
    gdn_params = (gdn_conv_w[l].astype(F32), pad_lane(gdn_A_log[l]), pad_lane(gdn_dt_bias[l]), row(gdn_norm_w[l]))

    def trunk_front(x2):
        n = x2.shape[0]
        h = _ffn(x2, *ffn1, tm=_token_tile(n, 512))
        return (h,) + tuple(_proj(h, row(mix_norm[l]), w_all, tm=_token_tile(n, 512)))

    def trunk_back(h, oa, ob, gates):
        n = h.shape[0]
        return _tail(h, oa, ob, gates, *merge_w, *ffn2, row(final_norm), tm=_token_tile(n, 256))

    xp = x_prompt.reshape(Bp * Tp, D_MODEL)
    h, pa, qkv, z, gates, ab = trunk_front(xp)
    tt = _token_tile(Tp, 256)
    oa, ob, s_pairs, delta_p = _mix_prompt(pa, qkv, ab, z, Bp, Tp, rwkv_params, gdn_params, tt)
    y_prompt = trunk_back(h, oa, ob, gates).reshape(Bp, Tp, D_MODEL)
    sp = s_pairs.reshape(Bp, A_PAIRS, 2, A_HEAD, 2, A_HEAD)
    rwkv_p = jnp.stack([sp[:, :, 0, :, 0], sp[:, :, 1, :, 1]], axis=2).reshape(Bp, A_HEADS, A_HEAD, A_HEAD)
    shift_p = _ungroup_pa(pa.reshape(Bp, Tp, A_PROJ)[:, -1])
    conv_p = qkv.reshape(Bp, Tp, CONV_CH)[:, Tp - (CONV_K - 1):]

    xs = x_sample.reshape(Bs * Ts, D_MODEL)
    h, pa, qkv, z, gates, ab = trunk_front(xs)
    assert Ts & (Ts - 1) == 0 and Ts >= CONV_K - 1, "sample steps: power of two covering the conv history"
    tables = _rwkv_pair_tables(_regroup_pa(rwkv_mu[l]), rwkv_w0[l], rwkv_a0[l], rwkv_k_k[l], rwkv_k_a[l], rwkv_r_k[l],
                               rwkv_lnx_w[l], rwkv_lnx_b[l], rwkv_w2[l], rwkv_a2[l], rwkv_g2[l])
    oa, s_lanes, last_pa = _rwkv_lanes(pa, _regroup_pa(state_rwkv_shift[l].astype(F32)),
                                       jnp.transpose(state_rwkv.astype(F32), (0, 2, 3, 4, 1)), l, Ts, tables)
    rwkv_s = jnp.transpose(s_lanes, (0, 4, 1, 2, 3))
    shift_s = _ungroup_pa(last_pa)
    hist = state_conv[l].astype(F32).reshape(Bs * (CONV_K - 1), CONV_CH)
    ob, delta_s, new_hist = _gdn_sample(qkv, hist, ab, z, state_delta.astype(F32), l, Ts, gdn_params,
                                        seqs=_token_tile(Bs, 16))
    conv_s = new_hist.reshape(Bs, CONV_K - 1, CONV_CH)
    y_sample = trunk_back(h, oa, ob, gates).reshape(Bs, Ts, D_MODEL)

    add_depth = lambda a: a[None]
    return (y_prompt, y_sample,
            add_depth(rwkv_p), add_depth(shift_p), add_depth(delta_p), add_depth(conv_p),
            rwkv_s, add_depth(shift_s), delta_s, add_depth(conv_s))
```

```python
import functools

import jax
import jax.numpy as jnp
from jax import lax
from jax.experimental import pallas as pl
from jax.experimental.pallas import tpu as pltpu

F32 = jnp.float32
BF16 = jnp.bfloat16

D_MODEL = 1024
D_FF = 2816
RMS_EPS = 1e-6
A_HEAD = 64
A_HEADS = 8
A_WIDTH = A_HEADS * A_HEAD
A_RANK_W = 64
A_RANK_A = 64
A_RANK_G = 128
A_PROJ = 3 * A_WIDTH + A_RANK_W + A_RANK_A + A_RANK_G
A_LNX_EPS = 64e-5
A_PAIRS = A_HEADS // 2
B_HEADS = 4
B_HEAD = 128
B_WIDTH = B_HEADS * B_HEAD
CONV_K = 4
CONV_CH = 3 * B_WIDTH
B_PROJ = CONV_CH + 2 * B_HEADS + B_WIDTH
GATE_COLS = 2 * D_MODEL
LANES = 128
SUBLANES = 8
MXU_DIM = 256
VMEM_LIMIT_BYTES = 56 * 1024 * 1024
CHUNK = 64
PA_R, PA_K, PA_V, PA_WA, PA_G = 0, A_WIDTH, 2 * A_WIDTH, 3 * A_WIDTH, 3 * A_WIDTH + A_RANK_W + A_RANK_A
PROJ_SPLITS = (A_PROJ, CONV_CH, B_WIDTH, GATE_COLS, LANES)


def _cparams(n_grid_dims):
    return pltpu.CompilerParams(dimension_semantics=("arbitrary",) * n_grid_dims,
                                vmem_limit_bytes=VMEM_LIMIT_BYTES)


def _const_spec(shape):
    nd = len(shape)
    return pl.BlockSpec(shape, lambda *_: (0,) * nd, pipeline_mode=pl.Buffered(1))


def _dot(a, b):
    return jnp.dot(a, b, preferred_element_type=F32)


def _dot_nt(a, b):
    return lax.dot_general(a, b, (((1,), (1,)), ((), ())), preferred_element_type=F32)


def _split3(x):
    hi = x.astype(BF16)
    rest = x - hi.astype(F32)
    mid = rest.astype(BF16)
    lo = (rest - mid.astype(F32)).astype(BF16)
    return hi, mid, lo


def _mm(a, b):
    return _dot(a.astype(BF16), b.astype(BF16))


def _mm_nt(a, b):
    return _dot_nt(a.astype(BF16), b.astype(BF16))


def _sel_mm(sel, x):
    return _dot(jnp.concatenate([sel, sel, sel], axis=1), jnp.concatenate(_split3(x), axis=0))


def _mm_sel(x, sel):
    return _dot(jnp.concatenate(_split3(x), axis=1), jnp.concatenate([sel, sel, sel], axis=0))


INV_BASE = 8


def _nilpotent_inverse(n, eye):
    width = n.shape[1]
    bi, bj = _iota(n.shape, 0), _iota(n.shape, 1)
    same = lambda size: _group(bi, size) == _group(bj, size)
    d = jnp.where(same(INV_BASE), n, 0.0)
    t = eye + d
    d = _mm(d, d)
    yield
    for _ in range(INV_BASE.bit_length() - 3):
        both = _mm(d, jnp.concatenate([t, d], axis=1))
        yield
        t = t + both[:, :width]
        d = both[:, width:]
    t = t + _mm(d, t)
    yield
    size = INV_BASE
    while size < CHUNK:
        coupling = jnp.where(same(2 * size) & jnp.logical_not(same(size)), n, 0.0)
        tb = t.astype(BF16)
        lt = _mm(coupling, tb)
        yield
        t = t + _mm(tb, lt)
        yield
        size *= 2
    return t


def _round_robin(chains):
    chains = list(chains)
    while chains:
        for chain in list(chains):
            try:
                next(chain)
            except StopIteration:
                chains.remove(chain)


def _rms(x, w):
    return x * lax.rsqrt(jnp.mean(x * x, axis=-1, keepdims=True) + RMS_EPS) * w


def _sigmoid(x):
    return 1.0 / (1.0 + jnp.exp(-x))


def _silu(x):
    return x * _sigmoid(x)


def _softplus(x):
    return jnp.maximum(x, 0.0) + jnp.log(1.0 + jnp.exp(-jnp.abs(x)))


def _iota(shape, dim):
    return lax.broadcasted_iota(jnp.int32, shape, dim)


def _group(idx, size):
    assert size & (size - 1) == 0
    return lax.shift_right_logical(idx, size.bit_length() - 1)


def _one_hot(cond):
    return jnp.where(cond, 1.0, 0.0).astype(BF16)


def _rows_to_tile(rows):
    rid = _iota((SUBLANES, 1), 0)
    tile = jnp.zeros((SUBLANES, rows[0].shape[1]), F32)
    for i, row in enumerate(rows):
        tile = jnp.where(rid == i, row, tile)
    return tile


def _swiglu_half_step(x, nw, wg_ref, wu_ref, wd_ref, slabs=(D_FF,)):
    xn = _rms(x, nw).astype(BF16)
    assert sum(slabs) == D_FF and all(w % MXU_DIM == 0 for w in slabs)
    y = None
    start = 0
    for width in slabs:
        cols = slice(start, start + width)
        start += width
        g = _dot(xn, wg_ref[:, cols])
        u = _dot(xn, wu_ref[:, cols])
        act = (_silu(g) * u).astype(BF16)
        part = _dot(act, wd_ref[cols, :])
        y = part if y is None else y + part
    return x + 0.5 * y


def _ffn_body(x_ref, nw_ref, wg_ref, wu_ref, wd_ref, o_ref):
    o_ref[...] = _swiglu_half_step(x_ref[...], nw_ref[...], wg_ref, wu_ref, wd_ref)


def _ffn(x, nw, wg, wu, wd, tm):
    n = x.shape[0]
    return pl.pallas_call(
        _ffn_body,
        grid=(n // tm,),
        in_specs=[pl.BlockSpec((tm, D_MODEL), lambda i: (i, 0)),
                  _const_spec((1, D_MODEL)),
                  _const_spec((D_MODEL, D_FF)), _const_spec((D_MODEL, D_FF)), _const_spec((D_FF, D_MODEL))],
        out_specs=pl.BlockSpec((tm, D_MODEL), lambda i: (i, 0)),
        out_shape=jax.ShapeDtypeStruct((n, D_MODEL), F32),
        compiler_params=_cparams(1),
        name="ffn1",
    )(x, nw, wg, wu, wd)


def _proj_body(h_ref, nw_ref, wpa_ref, wqa_ref, wzg_ref, pa_ref, qkv_ref, z_ref, gates_ref, ab_ref):
    u = _rms(h_ref[...], nw_ref[...]).astype(BF16)
    pa_ref[...] = _dot(u, wpa_ref[...])
    qkv_ref[...] = _dot(u, wqa_ref[:, :CONV_CH])
    ab_ref[...] = _dot(u, wqa_ref[:, CONV_CH:])
    z_ref[...] = _dot(u, wzg_ref[:, :B_WIDTH])
    gates_ref[...] = _dot(u, wzg_ref[:, B_WIDTH:])


def _proj(h, nw, w_pa, w_qa, w_zg, tm):
    n = h.shape[0]
    assert w_qa.shape[1] == CONV_CH + LANES and w_zg.shape[1] == B_WIDTH + GATE_COLS
    return pl.pallas_call(
        _proj_body,
        grid=(n // tm,),
        in_specs=[pl.BlockSpec((tm, D_MODEL), lambda i: (i, 0)), _const_spec((1, D_MODEL)),
                  _const_spec(w_pa.shape), _const_spec(w_qa.shape), _const_spec(w_zg.shape)],
        out_specs=[pl.BlockSpec((tm, w), lambda i: (i, 0)) for w in PROJ_SPLITS],
        out_shape=[jax.ShapeDtypeStruct((n, w), F32) for w in PROJ_SPLITS],
        compiler_params=_cparams(1),
        name="proj",
    )(h, nw, w_pa, w_qa, w_zg)


TAIL_FF_SLABS = (6 * MXU_DIM, 5 * MXU_DIM)


def _tail_body(h_ref, oa_ref, ob_ref, gates_ref, pa_ref, pb_ref, wo_ref, nw_ref, wg_ref, wu_ref, wd_ref,
               fn_ref, o_ref):
    ma = _dot(oa_ref[...].astype(BF16), pa_ref[...])
    mb = _dot(ob_ref[...].astype(BF16), pb_ref[...])
    merged = _sigmoid(gates_ref[:, :D_MODEL]) * ma + _sigmoid(gates_ref[:, D_MODEL:]) * mb
    h = h_ref[...] + _dot(merged.astype(BF16), wo_ref[...])
    h = _swiglu_half_step(h, nw_ref[...], wg_ref, wu_ref, wd_ref, slabs=TAIL_FF_SLABS)
    o_ref[...] = _rms(h, fn_ref[...])


def _tail(h, oa, ob, gates, proj_a, proj_b, w_out, nw, wg, wu, wd, fn, tm):
    n = h.shape[0]
    row = lambda w: pl.BlockSpec((tm, w), lambda i: (i, 0))
    return pl.pallas_call(
        _tail_body,
        grid=(n // tm,),
        in_specs=[row(D_MODEL), row(A_WIDTH), row(B_WIDTH), row(GATE_COLS),
                  _const_spec((A_WIDTH, D_MODEL)), _const_spec((B_WIDTH, D_MODEL)),
                  _const_spec((D_MODEL, D_MODEL)), _const_spec((1, D_MODEL)),
                  _const_spec((D_MODEL, D_FF)), _const_spec((D_MODEL, D_FF)), _const_spec((D_FF, D_MODEL)),
                  _const_spec((1, D_MODEL))],
        out_specs=row(D_MODEL),
        out_shape=jax.ShapeDtypeStruct((n, D_MODEL), F32),
        compiler_params=_cparams(1),
        name="tail",
    )(h, oa, ob, gates, proj_a, proj_b, w_out, nw, wg, wu, wd, fn)


def _rwkv_token_math(x, prev, mu, w0, a0, k_k, k_a, w2a, g2):
    pm = x + (prev - x) * mu
    r = pm[:, PA_R:PA_R + A_WIDTH]
    k = pm[:, PA_K:PA_K + A_WIDTH]
    v = pm[:, PA_V:PA_V + A_WIDTH]
    wa = pm[:, PA_WA:PA_WA + LANES]
    gd = pm[:, PA_G:PA_G + A_RANK_G]
    lane = _iota((1, LANES), 1)
    lora_in = jnp.where(lane < A_RANK_W, jnp.tanh(wa), wa)
    lora = _mm(lora_in, w2a)
    g = _mm(_sigmoid(gd), g2)
    yield
    w_log = -_softplus(-(w0 + lora[:, :A_WIDTH])) - 0.5
    log_decay = -jnp.exp(w_log)
    yield
    a = _sigmoid(a0 + lora[:, A_WIDTH:])
    kk_raw = k * k_k
    k_mod = k * (1.0 + (a - 1.0) * k_a)
    return r, k_mod, v, kk_raw, a, log_decay, g


def _pair_mask(rows_per_head):
    shape = (2 * rows_per_head, LANES)
    return _group(_iota(shape, 0), rows_per_head) == _group(_iota(shape, 1), A_HEAD)


def _rwkv_prompt_part(pa_ref, mu_ref, w0_ref, a0_ref, kk_ref, ka_ref, rk_ref, lnw_ref, lnb_ref, w2a_ref, g2_ref,
                      o_ref, carry_ref, state_ref, r_s, k_s, v_s, kkraw_s, a_s, cum_s, ld_s, g_s):
    tt = pa_ref.shape[0]
    C = CHUNK
    lower = _one_hot(_iota((C, C), 1) <= _iota((C, C), 0))

    def token_chain(r0):
        rows = slice(r0, r0 + C)
        x = pa_ref[rows, :]
        before = carry_ref[SUBLANES - 1:SUBLANES, :] if r0 == 0 else pa_ref[r0 - 1:r0, :]
        prev = jnp.where(_iota((C, 1), 0) == 0, before, pltpu.roll(x, 1, axis=0))
        if r0 + C == tt:
            carry_ref[...] = x[C - SUBLANES:, :]
        r, k_mod, v, kk_raw, a, log_decay, g = yield from _rwkv_token_math(
            x, prev, mu_ref[...], w0_ref[...], a0_ref[...], kk_ref[...], ka_ref[...], w2a_ref[...], g2_ref[...])
        r_s[rows, :] = r
        k_s[rows, :] = k_mod
        v_s[rows, :] = v
        kkraw_s[rows, :] = kk_raw
        a_s[rows, :] = a
        g_s[rows, :] = g
        ld_s[rows, :] = log_decay
        yield
        cum_s[rows, :] = sum(_dot(lower, piece) for piece in _split3(log_decay))
        yield

    mask = _pair_mask(C)
    i2, j2 = _iota((2 * C, 2 * C), 0), _iota((2 * C, 2 * C), 1)
    strict = i2 > j2
    incl = i2 >= j2
    eye = jnp.where(i2 == j2, 1.0, 0.0)
    dup = lambda m: jnp.concatenate([m, m], axis=0)
    stack = lambda m: jnp.where(mask, dup(m), 0.0)

    def solve_chain(p, r0, stash):
        sl = slice(p * LANES, (p + 1) * LANES)
        ld = lambda ref: ref[pl.ds(r0, C), sl]
        r_p, k_p, v_p, a_p, cum, ldec = ld(r_s), ld(k_s), ld(v_s), ld(a_s), ld(cum_s), ld(ld_s)
        einc = jnp.exp(cum)
        eex = jnp.exp(cum - ldec)
        einv = jnp.exp(-cum)
        etail = jnp.exp(cum[C - 1:C, :] - cum)
        kks = stack(ld(kkraw_s))
        kks = kks * jnp.minimum(lax.rsqrt(jnp.sum(kks * kks, axis=-1, keepdims=True)), 1e12)
        As = -kks * dup(eex)
        Bs = kks * dup(a_p * einv)
        Bh = kks * dup(a_p * etail)
        Ks = stack(k_p * einv)
        Kh = stack(k_p * etail)
        Rs = stack(r_p * einc)
        Vs = stack(v_p)
        AR = jnp.concatenate([As, Rs], axis=0).astype(BF16)
        Vb = Vs.astype(BF16)
        G = _mm_nt(AR, jnp.concatenate([Bs, Ks], axis=0))
        yield
        Aab = jnp.where(strict, G[:2 * C, :2 * C], 0.0)
        Aak = jnp.where(strict, G[:2 * C, 2 * C:], 0.0)
        Arb = jnp.where(incl, G[2 * C:, :2 * C], 0.0)
        Ark = jnp.where(incl, G[2 * C:, 2 * C:], 0.0)
        Y = _mm(Aak, Vb)
        yield
        T = yield from _nilpotent_inverse(Aab, eye)
        WU = _mm(T, jnp.concatenate([AR[:2 * C], Y.astype(BF16)], axis=1))
        yield
        bonus = jnp.sum(stack(r_p * k_p * rk_ref[:, sl]), axis=-1, keepdims=True) * Vs
        stash[p] = dict(WU=WU, R=AR[2 * C:], Vs=Vs, bonus=bonus,
                        Aro=jnp.concatenate([Arb, Ark], axis=1).astype(BF16),
                        BKh=jnp.concatenate([Bh, Kh], axis=0).astype(BF16), decay=einc[C - 1:C, :])

    def state_chain(p, r0, stash):
        sl = slice(p * LANES, (p + 1) * LANES)
        s = stash[p]
        S = state_ref[p]
        Sb = S.astype(BF16)
        W = _mm_nt(s["WU"][:, :LANES], Sb) + s["WU"][:, LANES:]
        yield
        WV = jnp.concatenate([W, s["Vs"]], axis=0)
        O = _dot_nt(s["R"], Sb) + _mm(s["Aro"], WV)
        state_ref[p] = S * s["decay"] + _mm(WV.T, s["BKh"])
        yield
        mean = jnp.sum(O, axis=-1, keepdims=True) * (1.0 / A_HEAD)
        cen = jnp.where(mask, O - mean, 0.0)
        var = jnp.sum(cen * cen, axis=-1, keepdims=True) * (1.0 / A_HEAD)
        normed = jnp.where(mask, cen * lax.rsqrt(var + A_LNX_EPS) * lnw_ref[:, sl] + lnb_ref[:, sl], 0.0)
        full = normed + s["bonus"]
        o_ref[pl.ds(r0, C), sl] = (full[:C] + full[C:]) * g_s[pl.ds(r0, C), sl]

    return (lambda r0: [token_chain(r0)],
            lambda r0, stash: [solve_chain(p, r0, stash) for p in range(A_PAIRS)],
            lambda r0, stash: [state_chain(p, r0, stash) for p in range(A_PAIRS)])


CHUNKS_IN_FLIGHT = 2
N_RWKV_PARAMS = 10
N_GDN_PARAMS = 4
N_RWKV_SCRATCH = 8
N_GDN_SCRATCH = 5


def _mix_prompt_body(pa_ref, qkv_ref, ab_ref, z_ref, *refs):
    refs = list(refs)
    take = lambda n: [refs.pop(0) for _ in range(n)]
    rwkv_prm, gdn_prm = take(N_RWKV_PARAMS), take(N_GDN_PARAMS)
    oa_ref, ob_ref, sfa_ref, sfb_ref = take(4)
    carry_a, state_a, carry_b, state_b = take(4)
    rwkv_scr, gdn_scr = take(N_RWKV_SCRATCH), take(N_GDN_SCRATCH)
    t = pl.program_id(1)
    tt = pa_ref.shape[0]

    @pl.when(t == 0)
    def _():
        for ref in (carry_a, state_a, carry_b, state_b):
            ref[...] = jnp.zeros_like(ref)

    rwkv_token, rwkv_solve, rwkv_state = _rwkv_prompt_part(pa_ref, *rwkv_prm, oa_ref, carry_a, state_a, *rwkv_scr)
    gdn_token, gdn_solve, gdn_state = _gdn_prompt_part(qkv_ref, ab_ref, z_ref, *gdn_prm, ob_ref, carry_b, state_b,
                                                       *gdn_scr)

    n_chunks = tt // CHUNK
    group = min(CHUNKS_IN_FLIGHT, n_chunks)
    n_groups = n_chunks // group
    stashes = [({}, {}) for _ in range(n_chunks)]
    chunks_of = lambda gi: range(gi * group, (gi + 1) * group) if 0 <= gi < n_groups else ()

    def in_sequence(per_chunk_chains):
        for chains in zip(*per_chunk_chains):
            for chain in chains:
                yield from chain

    for gi in range(n_groups + 2):
        chains = []
        for c in chunks_of(gi):
            chains += rwkv_token(c * CHUNK) + gdn_token(c * CHUNK)
        for c in chunks_of(gi - 1):
            chains += rwkv_solve(c * CHUNK, stashes[c][0]) + gdn_solve(c * CHUNK, stashes[c][1])
        state_chains = [rwkv_state(c * CHUNK, stashes[c][0]) + gdn_state(c * CHUNK, stashes[c][1])
                        for c in chunks_of(gi - 2)]
        if state_chains:
            chains += [in_sequence([per_chunk[i:i + 1] for per_chunk in state_chains])
                       for i in range(len(state_chains[0]))]
        _round_robin(chains)

    @pl.when(t == pl.num_programs(1) - 1)
    def _():
        sfa_ref[0] = state_a[...]
        sfb_ref[0] = state_b[...]


def _mix_prompt(pa, qkv, ab, z, B, T, rwkv_params, gdn_params, tt):
    n = pa.shape[0]
    nt = T // tt
    assert len(rwkv_params) == N_RWKV_PARAMS and len(gdn_params) == N_GDN_PARAMS
    rows = lambda w: pl.BlockSpec((tt, w), lambda b, t: (b * nt + t, 0))
    state = lambda: pl.BlockSpec((1, 4, LANES, LANES), lambda b, t: (b, 0, 0, 0))
    big = lambda: pltpu.VMEM((tt, A_WIDTH), F32)
    return pl.pallas_call(
        _mix_prompt_body,
        grid=(B, nt),
        in_specs=[rows(A_PROJ), rows(CONV_CH), rows(LANES), rows(B_WIDTH)]
                 + [_const_spec(p.shape) for p in rwkv_params + gdn_params],
        out_specs=[rows(A_WIDTH), rows(B_WIDTH), state(), state()],
        out_shape=[jax.ShapeDtypeStruct((n, A_WIDTH), F32), jax.ShapeDtypeStruct((n, B_WIDTH), F32),
                   jax.ShapeDtypeStruct((B, A_PAIRS, LANES, LANES), F32),
                   jax.ShapeDtypeStruct((B, B_HEADS, B_HEAD, B_HEAD), F32)],
        scratch_shapes=[pltpu.VMEM((SUBLANES, A_PROJ), F32), pltpu.VMEM((A_PAIRS, LANES, LANES), F32),
                        pltpu.VMEM((SUBLANES, CONV_CH), F32), pltpu.VMEM((B_HEADS, B_HEAD, B_HEAD), F32)]
                       + [big() for _ in range(N_RWKV_SCRATCH + N_GDN_SCRATCH)],
        compiler_params=_cparams(2),
        name="mix_prompt",
    )(pa, qkv, ab, z, *rwkv_params, *gdn_params)


def _first_step_rows(rows, seqs, steps, state_rows, offset=0):
    hist = state_rows.shape[0] // seqs
    r, c = _iota((rows, seqs * hist), 0), _iota((rows, seqs * hist), 1)
    t = r & (steps - 1)
    sel = _one_hot((c == _group(r, steps) * hist + offset + t) & (t < hist - offset))
    return sum(_dot(sel, piece) for piece in _split3(state_rows))


def _last_step_rows(x, seqs, steps, keep):
    r, c = _iota((seqs * keep, seqs * steps), 0), _iota((seqs * keep, seqs * steps), 1)
    i = (c & (steps - 1)) - (steps - keep)
    sel = _one_hot((i >= 0) & (r == _group(c, steps) * keep + i))
    return sum(_dot(sel, piece) for piece in _split3(x))


ROWP = dict(mu_r=0, mu_k=1, mu_v=2, w0=3, a0=4, k_k=5, k_a=6, r_k=7, lnw=8, lnb=9)
ROWP_ROWS = 16
VALUE_GROUP = SUBLANES


def _rwkv_lanes_body(par_ref, pak_ref, pav_ref, paw_ref, pag_ref, shr_ref, shk_ref, shv_ref, shw_ref, shg_ref,
                     pa_ref, s_ref, rowp_ref, shared_ref, w2a_ref, g2_ref,
                     o_ref, sout_ref, shift_out_ref,
                     tr_s, ot_s, *, steps):
    rows = par_ref.shape[0]
    B = rows // steps
    p = pl.program_id(0)
    rp = lambda name: rowp_ref[ROWP[name]:ROWP[name] + 1, :]

    @pl.when(p == 0)
    def _():
        shift_out_ref[...] = _last_step_rows(pa_ref[...], B, steps, 1)

    def lerp(x_ref, first_ref, mu):
        per_step = [x_ref[pl.ds(t, B, stride=steps), :] for t in range(steps)]
        x = jnp.concatenate(per_step, axis=0)
        prev = jnp.concatenate([first_ref[...]] + per_step[:-1], axis=0)
        return x + (prev - x) * mu

    r = lerp(par_ref, shr_ref, rp("mu_r"))
    k = lerp(pak_ref, shk_ref, rp("mu_k"))
    v = lerp(pav_ref, shv_ref, rp("mu_v"))
    wa = lerp(paw_ref, shw_ref, shared_ref[0:1, :])
    gd = lerp(pag_ref, shg_ref, shared_ref[1:2, :])
    lane = _iota((1, LANES), 1)
    lora = _mm(jnp.where(lane < A_RANK_W, jnp.tanh(wa), wa), w2a_ref[...])
    g = _mm(_sigmoid(gd), g2_ref[...])
    w_log = -_softplus(-(rp("w0") + lora[:, :LANES])) - 0.5
    decay = jnp.exp(-jnp.exp(w_log))
    a = _sigmoid(rp("a0") + lora[:, LANES:])
    k_mod = k * (1.0 + (a - 1.0) * rp("k_a"))
    hi, hj = _iota((LANES, LANES), 0), _iota((LANES, LANES), 1)
    pair_ones = _one_hot(_group(hi, A_HEAD) == _group(hj, A_HEAD))
    head_sum = lambda m: _mm_sel(m, pair_ones)
    kk_raw = k * rp("k_k")
    kk = kk_raw * jnp.minimum(lax.rsqrt(head_sum(kk_raw * kk_raw)), 1e12)
    names = ("nkk", "beta", "decay", "k", "r", "v")
    for idx, m in enumerate((-kk, kk * a, decay, k_mod, r, v)):
        for t in range(steps):
            tr_s[idx, t] = m[t * B:(t + 1) * B, :].T
    at = lambda name, t: tr_s.at[names.index(name), t]

    def group(gi, carry):
        j = gi // (A_HEAD // VALUE_GROUP)
        v0 = (gi % (A_HEAD // VALUE_GROUP)) * VALUE_GROUP
        keys = lambda name, t: at(name, t)[pl.ds(pl.multiple_of(j * A_HEAD, A_HEAD), A_HEAD), :]
        v_rows = [at("v", t)[pl.ds(pl.multiple_of(gi * VALUE_GROUP, VALUE_GROUP), VALUE_GROUP), :]
                  for t in range(steps)]
        outs = [[] for _ in range(steps)]
        for i in range(VALUE_GROUP):
            S = s_ref[j, v0 + i]
            for t in range(steps):
                sa = jnp.sum(S * keys("nkk", t), axis=0, keepdims=True)
                S = S * keys("decay", t) + sa * keys("beta", t) + v_rows[t][i:i + 1, :] * keys("k", t)
                outs[t].append(jnp.sum(S * keys("r", t), axis=0, keepdims=True))
            sout_ref[j, v0 + i] = S
        for t in range(steps):
            ot_s[t, pl.ds(pl.multiple_of(gi * VALUE_GROUP, VALUE_GROUP), VALUE_GROUP), :] = _rows_to_tile(outs[t])
        return carry

    lax.fori_loop(0, 2 * A_HEAD // VALUE_GROUP, group, 0)
    o = jnp.concatenate([ot_s[t].T for t in range(steps)], axis=0)
    mean = head_sum(o) * (1.0 / A_HEAD)
    cen = o - mean
    var = head_sum(cen * cen) * (1.0 / A_HEAD)
    o = cen * lax.rsqrt(var + A_LNX_EPS) * rp("lnw") + rp("lnb")
    o = (o + head_sum(r * k_mod * rp("r_k")) * v) * g
    for t in range(steps):
        o_ref[pl.ds(t, B, stride=steps), :] = o[t * B:(t + 1) * B, :]


def _rwkv_lanes(pa, shift, state_t, layer, steps, tables):
    rowp, shared, w2a_p, g2_p = tables
    n = pa.shape[0]
    B = state_t.shape[-1]
    assert n == B * steps and B == LANES and steps & (steps - 1) == 0
    col = lambda rows_, j: pl.BlockSpec((rows_, LANES), lambda p: (0, j(p)))
    groups = [lambda p: p, lambda p: A_PAIRS + p, lambda p: 2 * A_PAIRS + p,
              lambda p: 3 * A_PAIRS, lambda p: 3 * A_PAIRS + 1]
    block = (None, 2, A_HEAD, A_HEAD, B)
    return pl.pallas_call(
        functools.partial(_rwkv_lanes_body, steps=steps),
        grid=(A_PAIRS,),
        in_specs=[col(n, j) for j in groups] + [col(B, j) for j in groups]
                 + [_const_spec((n, A_PROJ)), pl.BlockSpec(block, lambda p: (layer, p, 0, 0, 0)),
                    pl.BlockSpec((None, ROWP_ROWS, LANES), lambda p: (p, 0, 0)), _const_spec(shared.shape),
                    pl.BlockSpec((None, LANES, 2 * LANES), lambda p: (p, 0, 0)),
                    pl.BlockSpec((None, LANES, LANES), lambda p: (p, 0, 0))],
        out_specs=[pl.BlockSpec((n, LANES), lambda p: (0, p)), pl.BlockSpec(block, lambda p: (0, p, 0, 0, 0)),
                   pl.BlockSpec((B, A_PROJ), lambda p: (0, 0))],
        out_shape=[jax.ShapeDtypeStruct((n, A_WIDTH), F32), jax.ShapeDtypeStruct((1,) + state_t.shape[1:], F32),
                   jax.ShapeDtypeStruct((B, A_PROJ), F32)],
        scratch_shapes=[pltpu.VMEM((6, steps, LANES, B), F32), pltpu.VMEM((steps, LANES, B), F32)],
        compiler_params=_cparams(1),
        name="rwkv_sample",
    )(*([pa] * 5), *([shift] * 5), pa, state_t, rowp, shared, w2a_p, g2_p)


def _rwkv_pair_tables(mu, w0, a0, k_k, k_a, r_k, lnw, lnb, w2, a2, g2):
    per_pair = lambda a: a.reshape(A_PAIRS, 1, LANES)
    rows = {"mu_r": mu[:A_WIDTH], "mu_k": mu[A_WIDTH:2 * A_WIDTH], "mu_v": mu[2 * A_WIDTH:3 * A_WIDTH],
            "w0": w0, "a0": a0, "k_k": k_k, "k_a": k_a, "r_k": r_k.reshape(-1), "lnw": lnw, "lnb": lnb}
    table = jnp.concatenate([per_pair(rows[name].astype(F32)) for name in sorted(ROWP, key=ROWP.get)]
                            + [jnp.zeros((A_PAIRS, ROWP_ROWS - len(ROWP), LANES), F32)], axis=1)
    shared = jnp.concatenate([mu[3 * A_WIDTH:3 * A_WIDTH + LANES].reshape(1, LANES),
                              mu[3 * A_WIDTH + LANES:].reshape(1, LANES),
                              jnp.zeros((SUBLANES - 2, LANES), F32)], axis=0).astype(F32)
    by_pair = lambda w: jnp.transpose(w.astype(F32).reshape(w.shape[0], A_PAIRS, LANES), (1, 0, 2))
    zeros = jnp.zeros((A_PAIRS, A_RANK_W, LANES), F32)
    w2a_p = jnp.concatenate([jnp.concatenate([by_pair(w2), zeros], axis=2),
                             jnp.concatenate([zeros, by_pair(a2)], axis=2)], axis=1)
    return table, shared, w2a_p, by_pair(g2)


def _gdn_qkv(conv):
    c = _silu(conv)
    qs, ks = [], []
    for h in range(B_HEADS):
        q = c[:, h * B_HEAD:(h + 1) * B_HEAD]
        k = c[:, B_WIDTH + h * B_HEAD:B_WIDTH + (h + 1) * B_HEAD]
        qs.append(q * (lax.rsqrt(jnp.sum(q * q, axis=-1, keepdims=True) + 1e-6) * (B_HEAD ** -0.5)))
        ks.append(k * lax.rsqrt(jnp.sum(k * k, axis=-1, keepdims=True) + 1e-6))
    q = jnp.concatenate(qs, axis=1)
    k = jnp.concatenate(ks, axis=1)
    v = c[:, 2 * B_WIDTH:]
    return q, k, v


def _gdn_gates(ab, alog, dtb):
    lane = _iota((1, LANES), 1)
    g = -jnp.exp(alog) * _softplus(ab + dtb)
    beta = _sigmoid(ab)
    gb = jnp.where(lane < B_HEADS, g, beta)
    si, sj = _iota((LANES, 2 * B_WIDTH), 0), _iota((LANES, 2 * B_WIDTH), 1)
    spread = _mm_sel(gb, _one_hot(si == _group(sj, B_HEAD)))
    return spread[:, :B_WIDTH], spread[:, B_WIDTH:]


def _gdn_out(o, norm_w, z):
    return o * lax.rsqrt(jnp.mean(o * o, axis=-1, keepdims=True) + RMS_EPS) * norm_w * _silu(z)


def _gdn_prompt_part(qkv_ref, ab_ref, z_ref, cw_ref, alog_ref, dtb_ref, nw_ref, o_ref,
                     carry_ref, state_ref, q_s, k_s, v_s, gc_s, beta_s):
    tt = qkv_ref.shape[0]
    C = CHUNK
    g, beta = _gdn_gates(ab_ref[...], alog_ref[...], dtb_ref[...])
    ri, ci = _iota((tt, tt), 0), _iota((tt, tt), 1)
    beta_s[...] = beta
    gc_s[...] = _sel_mm(_one_hot((_group(ri, C) == _group(ci, C)) & (ci <= ri)), g)

    def token_chain(r0):
        rows = slice(r0, r0 + C)
        x = qkv_ref[rows, :]
        before = carry_ref[...] if r0 == 0 else qkv_ref[r0 - SUBLANES:r0, :]
        if r0 + C == tt:
            carry_ref[...] = x[C - SUBLANES:, :]
        row8 = _iota((SUBLANES, 1), 0)

        def shift_rows(cur, halo, i):
            down = pltpu.roll(cur, i, axis=0)
            top = jnp.where(row8 < i, pltpu.roll(halo, i, axis=0), down[:SUBLANES])
            return jnp.concatenate([top, down[SUBLANES:]], axis=0)

        assert CONV_K == 4
        c0, c1, c2, c3 = (cw_ref[i:i + 1, :] for i in range(CONV_K))
        x1 = shift_rows(x, before, 1)
        yield
        older = x * c1 + x1 * c0
        older_halo = before * c1 + pltpu.roll(before, 1, axis=0) * c0
        conv = x * c3 + x1 * c2 + shift_rows(older, older_halo, 2)
        yield
        q, k, v = _gdn_qkv(conv)
        q_s[rows, :] = q
        k_s[rows, :] = k
        v_s[rows, :] = v
        yield

    i2, j2 = _iota((2 * C, 2 * C), 0), _iota((2 * C, 2 * C), 1)
    same_head = _group(i2, C) == _group(j2, C)
    strict = same_head & (i2 > j2)
    incl = same_head & (i2 >= j2)
    eye = jnp.where(i2 == j2, 1.0, 0.0)
    first = _iota((2 * C, 1), 0) < C

    def solve_chain(pr, r0, stash):
        sls = [slice(h * B_HEAD, (h + 1) * B_HEAD) for h in (2 * pr, 2 * pr + 1)]
        ld = lambda ref: jnp.concatenate([ref[pl.ds(r0, C), sl] for sl in sls], axis=0)
        q_h, k_h, v_h, gc_h, beta_h = ld(q_s), ld(k_s), ld(v_s), ld(gc_s), ld(beta_s)
        diff = gc_h - gc_h.T
        dm = jnp.where(incl, jnp.exp(jnp.where(incl, diff, 0.0)), 0.0)
        kb = k_h * beta_h
        QK = _mm_nt(jnp.concatenate([kb, q_h], axis=0), k_h)
        yield
        N = -jnp.where(strict, QK[:2 * C] * dm, 0.0)
        qk = QK[2 * C:] * dm
        egc = jnp.exp(gc_h)
        X = jnp.concatenate([v_h * beta_h, kb * egc], axis=1)
        T = yield from _nilpotent_inverse(N, eye)
        UW = _mm(T, X)
        yield
        g_last = jnp.where(first, gc_h[C - 1:C, :], gc_h[2 * C - 1:2 * C, :])
        stash[pr] = dict(u=UW[:, :B_HEAD], w=UW[:, B_HEAD:].astype(BF16), qd=(q_h * egc).astype(BF16),
                         qk=qk.astype(BF16), k_dec_t=(k_h * jnp.exp(g_last - gc_h)).T.astype(BF16),
                         decay=[jnp.exp(gc_h[(j + 1) * C - 1:(j + 1) * C, :]) for j in range(2)])

    def state_chain(pr, r0, stash):
        heads = (2 * pr, 2 * pr + 1)
        sls = [slice(h * B_HEAD, (h + 1) * B_HEAD) for h in heads]
        s = stash[pr]
        wS, qS, S_old = [], [], []
        for j, h in enumerate(heads):
            S = state_ref[h]
            rows = slice(j * C, (j + 1) * C)
            wq = _dot(jnp.concatenate([s["w"][rows], s["qd"][rows]], axis=0), S.astype(BF16))
            wS.append(wq[:C])
            qS.append(wq[C:])
            S_old.append(S)
        yield
        v_new = s["u"] - jnp.concatenate(wS, axis=0)
        o = jnp.concatenate(qS, axis=0) + _dot(s["qk"], v_new.astype(BF16))
        for j, h in enumerate(heads):
            mine = first if j == 0 else jnp.logical_not(first)
            rows = slice(j * C, (j + 1) * C)
            state_ref[h] = S_old[j] * s["decay"][j] + _mm(s["k_dec_t"], jnp.where(mine, v_new, 0.0))
            o_ref[pl.ds(r0, C), sls[j]] = _gdn_out(o[rows], nw_ref[...], z_ref[pl.ds(r0, C), sls[j]])
        yield

    return (lambda r0: [token_chain(r0)],
            lambda r0, stash: [solve_chain(pr, r0, stash) for pr in range(B_HEADS // 2)],
            lambda r0, stash: [state_chain(pr, r0, stash) for pr in range(B_HEADS // 2)])


def _gdn_sample_body(qkv_ref, hist_ref, ab_ref, z_ref, s_ref, cw_ref, alog_ref, dtb_ref, nw_ref,
                     o_ref, sout_ref, hist_out_ref, q_s, k_s, v_s, eg_s, beta_s, o_s, *, steps):
    rows = ab_ref.shape[0]
    seqs = rows // steps
    per_tile = SUBLANES // steps
    x = qkv_ref[...]
    hist_out_ref[...] = _last_step_rows(x, seqs, steps, CONV_K - 1)
    hist = hist_ref[...]
    t_idx = _iota((rows, 1), 0) & (steps - 1)
    conv = x * cw_ref[CONV_K - 1:CONV_K, :]
    for i in range(1, CONV_K):
        tap = jnp.where(t_idx >= i, pltpu.roll(x, i, axis=0),
                        _first_step_rows(rows, seqs, steps, hist, offset=CONV_K - 1 - i))
        conv = conv + tap * cw_ref[CONV_K - 1 - i:CONV_K - i, :]
    q, k, v = _gdn_qkv(conv)
    g, beta = _gdn_gates(ab_ref[...], alog_ref[...], dtb_ref[...])
    q_s[...] = q
    k_s[...] = k
    v_s[...] = v
    eg_s[...] = jnp.exp(g)
    beta_s[...] = beta
    diag = _iota((B_HEAD, B_HEAD), 0) == _iota((B_HEAD, B_HEAD), 1)
    ones = jnp.ones((B_HEAD, B_HEAD), BF16)

    def to_cols(tile_rows):
        diags = jnp.concatenate([jnp.where(diag, rv, 0.0) for rv in tile_rows], axis=0)
        return _mm(diags, ones)

    def tile(j, carry):
        r0 = pl.multiple_of(j * SUBLANES, SUBLANES)
        ld = lambda ref: ref[pl.ds(r0, SUBLANES), :]
        q_t, k_t, v_t, eg_t, beta_t = ld(q_s), ld(k_s), ld(v_s), ld(eg_s), ld(beta_s)
        out_rows = [[] for _ in range(SUBLANES)]
        for i in range(per_tile):
            for h in range(B_HEADS):
                sl = slice(h * B_HEAD, (h + 1) * B_HEAD)
                rowof = lambda m, rr: m[rr:rr + 1, sl]
                S = s_ref[j * per_tile + i, h]
                for s in range(steps):
                    rr = i * steps + s
                    kq_cols = to_cols([rowof(k_t, rr), rowof(q_t, rr)])
                    k_col, q_col = kq_cols[:B_HEAD], kq_cols[B_HEAD:]
                    S = S * rowof(eg_t, rr)
                    u = jnp.sum(k_col * S, axis=0, keepdims=True)
                    S = S + k_col * (rowof(beta_t, rr) * (rowof(v_t, rr) - u))
                    out_rows[rr].append(jnp.sum(q_col * S, axis=0, keepdims=True))
                sout_ref[j * per_tile + i, h] = S
        o_s[pl.ds(r0, SUBLANES), :] = _rows_to_tile([jnp.concatenate(parts, axis=1) for parts in out_rows])
        return carry

    lax.fori_loop(0, rows // SUBLANES, tile, 0)
    for h in range(B_HEADS):
        sl = slice(h * B_HEAD, (h + 1) * B_HEAD)
        o_ref[:, sl] = _gdn_out(o_s[:, sl], nw_ref[...], z_ref[:, sl])


def _gdn_sample(qkv, hist, ab, z, state, layer, steps, params, seqs):
    n = ab.shape[0]
    nb = state.shape[1]
    rows = seqs * steps
    hist_rows = seqs * (CONV_K - 1)
    assert SUBLANES % steps == 0 and rows % SUBLANES == 0 and hist_rows % SUBLANES == 0
    row = lambda w: pl.BlockSpec((rows, w), lambda i: (i, 0))
    block = (None, seqs, B_HEADS, B_HEAD, B_HEAD)
    sspec = pl.BlockSpec(block, lambda i: (layer, i, 0, 0, 0))
    ospec = pl.BlockSpec(block, lambda i: (0, i, 0, 0, 0))
    return pl.pallas_call(
        functools.partial(_gdn_sample_body, steps=steps),
        grid=(nb // seqs,),
        in_specs=[row(CONV_CH), pl.BlockSpec((hist_rows, CONV_CH), lambda i: (i, 0)), row(LANES), row(B_WIDTH), sspec]
                 + [_const_spec(p.shape) for p in params],
        out_specs=[row(B_WIDTH), ospec, pl.BlockSpec((hist_rows, CONV_CH), lambda i: (i, 0))],
        out_shape=[jax.ShapeDtypeStruct((n, B_WIDTH), F32), jax.ShapeDtypeStruct((1,) + state.shape[1:], F32),
                   jax.ShapeDtypeStruct(hist.shape, F32)],
        scratch_shapes=[pltpu.VMEM((rows, B_WIDTH), F32) for _ in range(6)],
        compiler_params=_cparams(1),
        name="gdn_sample",
    )(qkv, hist, ab, z, state, *params)


def _cuts(widths):
    edges, total = [], 0
    for w in widths[:-1]:
        total += w
        edges.append(total)
    return edges


def _regroup_pa(a):
    r, wd, k, v, ad, gd = jnp.split(a, _cuts((A_WIDTH, A_RANK_W, A_WIDTH, A_WIDTH, A_RANK_A, A_RANK_G)), axis=-1)
    return jnp.concatenate([r, k, v, wd, ad, gd], axis=-1)


def _ungroup_pa(a):
    r, k, v, wd, ad, gd = jnp.split(a, _cuts((A_WIDTH, A_WIDTH, A_WIDTH, A_RANK_W, A_RANK_A, A_RANK_G)), axis=-1)
    return jnp.concatenate([r, wd, k, v, ad, gd], axis=-1)


def _token_tile(n, want):
    tm = want
    while n % tm:
        tm //= 2
    return tm


def kernel(x_prompt, x_sample, state_rwkv, state_rwkv_shift, state_delta, state_conv, ffn1_norm, ffn1_w_gate, ffn1_w_up, ffn1_w_down, mix_norm, w_in, rwkv_mu, rwkv_w0, rwkv_w2, rwkv_a0, rwkv_a2, rwkv_g2, rwkv_k_k, rwkv_k_a, rwkv_r_k, rwkv_lnx_w, rwkv_lnx_b, gdn_conv_w, gdn_A_log, gdn_dt_bias, gdn_norm_w, proj_a, proj_b, w_out, ffn2_norm, ffn2_w_gate, ffn2_w_up, ffn2_w_down, final_norm):
    depth = ffn1_norm.shape[0]
    assert depth == 1, "single-layer trunk"
    Bp, Tp, _ = x_prompt.shape
    Bs, Ts, _ = x_sample.shape
    l = 0
    row = lambda a: a.reshape(1, -1).astype(F32)

    wi = w_in[l].astype(BF16)
    o_b = A_PROJ
    proj_w = (_regroup_pa(wi[:, :A_PROJ]),
              wi[:, o_b:o_b + CONV_CH + LANES],
              wi[:, o_b + CONV_CH + 2 * B_HEADS:])
    ffn1 = (row(ffn1_norm[l]), ffn1_w_gate[l].astype(BF16), ffn1_w_up[l].astype(BF16), ffn1_w_down[l].astype(BF16))
    ffn2 = (row(ffn2_norm[l]), ffn2_w_gate[l].astype(BF16), ffn2_w_up[l].astype(BF16), ffn2_w_down[l].astype(BF16))
    merge_w = (proj_a[l].astype(BF16), proj_b[l].astype(BF16), w_out[l].astype(BF16))
    zw = jnp.zeros((A_RANK_W, A_WIDTH), F32)
    w2a = jnp.concatenate([jnp.concatenate([rwkv_w2[l], zw], axis=1),
                           jnp.concatenate([zw, rwkv_a2[l]], axis=1)], axis=0)
    rwkv_params = (row(_regroup_pa(rwkv_mu[l])), row(rwkv_w0[l]), row(rwkv_a0[l]), row(rwkv_k_k[l]), row(rwkv_k_a[l]),
                   row(rwkv_r_k[l]), row(rwkv_lnx_w[l]), row(rwkv_lnx_b[l]), w2a, rwkv_g2[l].astype(F32))
    pad_lane = lambda a: jnp.pad(a.reshape(1, -1).astype(F32), ((0, 0), (0, LANES - a.size)))
    gdn_params = (gdn_conv_w[l].astype(F32), pad_lane(gdn_A_log[l]), pad_lane(gdn_dt_bias[l]), row(gdn_norm_w[l]))

    def trunk_front(x2):
        n = x2.shape[0]
        h = _ffn(x2, *ffn1, tm=_token_tile(n, 512))
        return (h,) + tuple(_proj(h, row(mix_norm[l]), *proj_w, tm=_token_tile(n, 512)))

    def trunk_back(h, oa, ob, gates):
        n = h.shape[0]
        return _tail(h, oa, ob, gates, *merge_w, *ffn2, row(final_norm), tm=_token_tile(n, 512))

    xp = x_prompt.reshape(Bp * Tp, D_MODEL)
    h, pa, qkv, z, gates, ab = trunk_front(xp)
    tt = _token_tile(Tp, 256)
    oa, ob, s_pairs, delta_p = _mix_prompt(pa, qkv, ab, z, Bp, Tp, rwkv_params, gdn_params, tt)
    y_prompt = trunk_back(h, oa, ob, gates).reshape(Bp, Tp, D_MODEL)
    sp = s_pairs.reshape(Bp, A_PAIRS, 2, A_HEAD, 2, A_HEAD)
    rwkv_p = jnp.stack([sp[:, :, 0, :, 0], sp[:, :, 1, :, 1]], axis=2).reshape(Bp, A_HEADS, A_HEAD, A_HEAD)
    shift_p = _ungroup_pa(pa.reshape(Bp, Tp, A_PROJ)[:, -1])
    conv_p = qkv.reshape(Bp, Tp, CONV_CH)[:, Tp - (CONV_K - 1):]

    xs = x_sample.reshape(Bs * Ts, D_MODEL)
    h, pa, qkv, z, gates, ab = trunk_front(xs)
    assert Ts & (Ts - 1) == 0 and Ts >= CONV_K - 1, "sample steps: power of two covering the conv history"
    tables = _rwkv_pair_tables(_regroup_pa(rwkv_mu[l]), rwkv_w0[l], rwkv_a0[l], rwkv_k_k[l], rwkv_k_a[l], rwkv_r_k[l],
                               rwkv_lnx_w[l], rwkv_lnx_b[l], rwkv_w2[l], rwkv_a2[l], rwkv_g2[l])
    oa, s_lanes, last_pa = _rwkv_lanes(pa, _regroup_pa(state_rwkv_shift[l].astype(F32)),
                                       jnp.transpose(state_rwkv.astype(F32), (0, 2, 3, 4, 1)), l, Ts, tables)
    rwkv_s = jnp.transpose(s_lanes, (0, 4, 1, 2, 3))
    shift_s = _ungroup_pa(last_pa)
    hist = state_conv[l].astype(F32).reshape(Bs * (CONV_K - 1), CONV_CH)
    ob, delta_s, new_hist = _gdn_sample(qkv, hist, ab, z, state_delta.astype(F32), l, Ts, gdn_params,
                                        seqs=_token_tile(Bs, 16))
    conv_s = new_hist.reshape(Bs, CONV_K - 1, CONV_CH)
    y_sample = trunk_back(h, oa, ob, gates).reshape(Bs, Ts, D_MODEL)

    add_depth = lambda a: a[None]
    return (y_prompt, y_sample,
            add_depth(rwkv_p), add_depth(shift_p), add_depth(delta_p), add_depth(conv_p),
            rwkv_s, add_depth(shift_s), delta_s, add_depth(conv_s))
```

```python
import functools

import jax
import jax.numpy as jnp
from jax import lax
from jax.experimental import pallas as pl
from jax.experimental.pallas import tpu as pltpu

F32 = jnp.float32
BF16 = jnp.bfloat16

D_MODEL = 1024
D_FF = 2816
RMS_EPS = 1e-6
A_HEAD = 64
A_HEADS = 8
A_WIDTH = A_HEADS * A_HEAD
A_RANK_W = 64
A_RANK_A = 64
A_RANK_G = 128
A_PROJ = 3 * A_WIDTH + A_RANK_W + A_RANK_A + A_RANK_G
A_LNX_EPS = 64e-5
A_PAIRS = A_HEADS // 2
B_HEADS = 4
B_HEAD = 128
B_WIDTH = B_HEADS * B_HEAD
CONV_K = 4
CONV_CH = 3 * B_WIDTH
B_PROJ = CONV_CH + 2 * B_HEADS + B_WIDTH
GATE_COLS = 2 * D_MODEL
LANES = 128
SUBLANES = 8
MXU_DIM = 256
VMEM_LIMIT_BYTES = 56 * 1024 * 1024
CHUNK = 64
PA_R, PA_K, PA_V, PA_WA, PA_G = 0, A_WIDTH, 2 * A_WIDTH, 3 * A_WIDTH, 3 * A_WIDTH + A_RANK_W + A_RANK_A
PROJ_SPLITS = (A_PROJ, CONV_CH, B_WIDTH, GATE_COLS, LANES)


def _cparams(n_grid_dims):
    return pltpu.CompilerParams(dimension_semantics=("arbitrary",) * n_grid_dims,
                                vmem_limit_bytes=VMEM_LIMIT_BYTES)


def _const_spec(shape):
    nd = len(shape)
    return pl.BlockSpec(shape, lambda *_: (0,) * nd, pipeline_mode=pl.Buffered(1))


def _dot(a, b):
    return jnp.dot(a, b, preferred_element_type=F32)


def _dot_nt(a, b):
    return lax.dot_general(a, b, (((1,), (1,)), ((), ())), preferred_element_type=F32)


def _split3(x):
    hi = x.astype(BF16)
    rest = x - hi.astype(F32)
    mid = rest.astype(BF16)
    lo = (rest - mid.astype(F32)).astype(BF16)
    return hi, mid, lo


def _mm(a, b):
    return _dot(a.astype(BF16), b.astype(BF16))


def _mm_nt(a, b):
    return _dot_nt(a.astype(BF16), b.astype(BF16))


def _sel_mm(sel, x):
    return _dot(jnp.concatenate([sel, sel, sel], axis=1), jnp.concatenate(_split3(x), axis=0))


def _mm_sel(x, sel):
    return _dot(jnp.concatenate(_split3(x), axis=1), jnp.concatenate([sel, sel, sel], axis=0))


INV_BASE = 8


def _nilpotent_inverse(n, eye):
    width = n.shape[1]
    bi, bj = _iota(n.shape, 0), _iota(n.shape, 1)
    same = lambda size: _group(bi, size) == _group(bj, size)
    d = jnp.where(same(INV_BASE), n, 0.0)
    t = eye + d
    d = _mm(d, d)
    yield
    for _ in range(INV_BASE.bit_length() - 3):
        both = _mm(d, jnp.concatenate([t, d], axis=1))
        yield
        t = t + both[:, :width]
        d = both[:, width:]
    t = t + _mm(d, t)
    yield
    size = INV_BASE
    while size < CHUNK:
        coupling = jnp.where(same(2 * size) & jnp.logical_not(same(size)), n, 0.0)
        tb = t.astype(BF16)
        lt = _mm(coupling, tb)
        yield
        t = t + _mm(tb, lt)
        yield
        size *= 2
    return t


def _round_robin(chains):
    chains = list(chains)
    while chains:
        for chain in list(chains):
            try:
                next(chain)
            except StopIteration:
                chains.remove(chain)


def _rms(x, w):
    return x * lax.rsqrt(jnp.mean(x * x, axis=-1, keepdims=True) + RMS_EPS) * w


def _sigmoid(x):
    return 1.0 / (1.0 + jnp.exp(-x))


def _silu(x):
    return x * _sigmoid(x)


def _softplus(x):
    return jnp.maximum(x, 0.0) + jnp.log(1.0 + jnp.exp(-jnp.abs(x)))


def _iota(shape, dim):
    return lax.broadcasted_iota(jnp.int32, shape, dim)


def _group(idx, size):
    assert size & (size - 1) == 0
    return lax.shift_right_logical(idx, size.bit_length() - 1)


def _one_hot(cond):
    return jnp.where(cond, 1.0, 0.0).astype(BF16)


def _rows_to_tile(rows):
    rid = _iota((SUBLANES, 1), 0)
    tile = jnp.zeros((SUBLANES, rows[0].shape[1]), F32)
    for i, row in enumerate(rows):
        tile = jnp.where(rid == i, row, tile)
    return tile


def _swiglu_half_step(x, nw, wg_ref, wu_ref, wd_ref, slabs=(D_FF,)):
    xn = _rms(x, nw).astype(BF16)
    assert sum(slabs) == D_FF and all(w % MXU_DIM == 0 for w in slabs)
    y = None
    start = 0
    for width in slabs:
        cols = slice(start, start + width)
        start += width
        g = _dot(xn, wg_ref[:, cols])
        u = _dot(xn, wu_ref[:, cols])
        act = (_silu(g) * u).astype(BF16)
        part = _dot(act, wd_ref[cols, :])
        y = part if y is None else y + part
    return x + 0.5 * y


def _ffn_body(x_ref, nw_ref, wg_ref, wu_ref, wd_ref, o_ref):
    o_ref[...] = _swiglu_half_step(x_ref[...], nw_ref[...], wg_ref, wu_ref, wd_ref)


def _ffn(x, nw, wg, wu, wd, tm):
    n = x.shape[0]
    return pl.pallas_call(
        _ffn_body,
        grid=(n // tm,),
        in_specs=[pl.BlockSpec((tm, D_MODEL), lambda i: (i, 0)),
                  _const_spec((1, D_MODEL)),
                  _const_spec((D_MODEL, D_FF)), _const_spec((D_MODEL, D_FF)), _const_spec((D_FF, D_MODEL))],
        out_specs=pl.BlockSpec((tm, D_MODEL), lambda i: (i, 0)),
        out_shape=jax.ShapeDtypeStruct((n, D_MODEL), F32),
        compiler_params=_cparams(1),
        name="ffn1",
    )(x, nw, wg, wu, wd)


def _proj_body(h_ref, nw_ref, wpa_ref, wqa_ref, wzg_ref, pa_ref, qkv_ref, z_ref, gates_ref, ab_ref):
    u = _rms(h_ref[...], nw_ref[...]).astype(BF16)
    pa_ref[...] = _dot(u, wpa_ref[...])
    qkv_ref[...] = _dot(u, wqa_ref[:, :CONV_CH])
    ab_ref[...] = _dot(u, wqa_ref[:, CONV_CH:])
    z_ref[...] = _dot(u, wzg_ref[:, :B_WIDTH])
    gates_ref[...] = _dot(u, wzg_ref[:, B_WIDTH:])


def _proj(h, nw, w_pa, w_qa, w_zg, tm):
    n = h.shape[0]
    assert w_qa.shape[1] == CONV_CH + LANES and w_zg.shape[1] == B_WIDTH + GATE_COLS
    return pl.pallas_call(
        _proj_body,
        grid=(n // tm,),
        in_specs=[pl.BlockSpec((tm, D_MODEL), lambda i: (i, 0)), _const_spec((1, D_MODEL)),
                  _const_spec(w_pa.shape), _const_spec(w_qa.shape), _const_spec(w_zg.shape)],
        out_specs=[pl.BlockSpec((tm, w), lambda i: (i, 0)) for w in PROJ_SPLITS],
        out_shape=[jax.ShapeDtypeStruct((n, w), F32) for w in PROJ_SPLITS],
        compiler_params=_cparams(1),
        name="proj",
    )(h, nw, w_pa, w_qa, w_zg)


TAIL_FF_SLABS = (6 * MXU_DIM, 5 * MXU_DIM)


def _tail_body(h_ref, oa_ref, ob_ref, gates_ref, pa_ref, pb_ref, wo_ref, nw_ref, wg_ref, wu_ref, wd_ref,
               fn_ref, o_ref):
    ma = _dot(oa_ref[...].astype(BF16), pa_ref[...])
    mb = _dot(ob_ref[...].astype(BF16), pb_ref[...])
    merged = _sigmoid(gates_ref[:, :D_MODEL]) * ma + _sigmoid(gates_ref[:, D_MODEL:]) * mb
    h = h_ref[...] + _dot(merged.astype(BF16), wo_ref[...])
    h = _swiglu_half_step(h, nw_ref[...], wg_ref, wu_ref, wd_ref, slabs=TAIL_FF_SLABS)
    o_ref[...] = _rms(h, fn_ref[...])


def _tail(h, oa, ob, gates, proj_a, proj_b, w_out, nw, wg, wu, wd, fn, tm):
    n = h.shape[0]
    row = lambda w: pl.BlockSpec((tm, w), lambda i: (i, 0))
    return pl.pallas_call(
        _tail_body,
        grid=(n // tm,),
        in_specs=[row(D_MODEL), row(A_WIDTH), row(B_WIDTH), row(GATE_COLS),
                  _const_spec((A_WIDTH, D_MODEL)), _const_spec((B_WIDTH, D_MODEL)),
                  _const_spec((D_MODEL, D_MODEL)), _const_spec((1, D_MODEL)),
                  _const_spec((D_MODEL, D_FF)), _const_spec((D_MODEL, D_FF)), _const_spec((D_FF, D_MODEL)),
                  _const_spec((1, D_MODEL))],
        out_specs=row(D_MODEL),
        out_shape=jax.ShapeDtypeStruct((n, D_MODEL), F32),
        compiler_params=_cparams(1),
        name="tail",
    )(h, oa, ob, gates, proj_a, proj_b, w_out, nw, wg, wu, wd, fn)


def _rwkv_token_math(x, prev, mu, w0, a0, k_k, k_a, w2a, g2):
    pm = x + (prev - x) * mu
    r = pm[:, PA_R:PA_R + A_WIDTH]
    k = pm[:, PA_K:PA_K + A_WIDTH]
    v = pm[:, PA_V:PA_V + A_WIDTH]
    wa = pm[:, PA_WA:PA_WA + LANES]
    gd = pm[:, PA_G:PA_G + A_RANK_G]
    lane = _iota((1, LANES), 1)
    lora_in = jnp.where(lane < A_RANK_W, jnp.tanh(wa), wa)
    lora = _mm(lora_in, w2a)
    g = _mm(_sigmoid(gd), g2)
    yield
    w_log = -_softplus(-(w0 + lora[:, :A_WIDTH])) - 0.5
    log_decay = -jnp.exp(w_log)
    yield
    a = _sigmoid(a0 + lora[:, A_WIDTH:])
    kk_raw = k * k_k
    k_mod = k * (1.0 + (a - 1.0) * k_a)
    return r, k_mod, v, kk_raw, a, log_decay, g


def _pair_mask(rows_per_head):
    shape = (2 * rows_per_head, LANES)
    return _group(_iota(shape, 0), rows_per_head) == _group(_iota(shape, 1), A_HEAD)


def _rwkv_prompt_part(pa_ref, mu_ref, w0_ref, a0_ref, kk_ref, ka_ref, rk_ref, lnw_ref, lnb_ref, w2a_ref, g2_ref,
                      o_ref, carry_ref, state_ref, r_s, k_s, v_s, kkraw_s, a_s, cum_s, ld_s, g_s):
    tt = pa_ref.shape[0]
    C = CHUNK
    lower = _one_hot(_iota((C, C), 1) <= _iota((C, C), 0))

    def token_chain(r0):
        rows = slice(r0, r0 + C)
        x = pa_ref[rows, :]
        before = carry_ref[SUBLANES - 1:SUBLANES, :] if r0 == 0 else pa_ref[r0 - 1:r0, :]
        prev = jnp.where(_iota((C, 1), 0) == 0, before, pltpu.roll(x, 1, axis=0))
        if r0 + C == tt:
            carry_ref[...] = x[C - SUBLANES:, :]
        r, k_mod, v, kk_raw, a, log_decay, g = yield from _rwkv_token_math(
            x, prev, mu_ref[...], w0_ref[...], a0_ref[...], kk_ref[...], ka_ref[...], w2a_ref[...], g2_ref[...])
        r_s[rows, :] = r
        k_s[rows, :] = k_mod
        v_s[rows, :] = v
        kkraw_s[rows, :] = kk_raw
        a_s[rows, :] = a
        g_s[rows, :] = g
        ld_s[rows, :] = log_decay
        yield
        cum_s[rows, :] = sum(_dot(lower, piece) for piece in _split3(log_decay))
        yield

    mask = _pair_mask(C)
    i2, j2 = _iota((2 * C, 2 * C), 0), _iota((2 * C, 2 * C), 1)
    strict = i2 > j2
    incl = i2 >= j2
    eye = jnp.where(i2 == j2, 1.0, 0.0)
    dup = lambda m: jnp.concatenate([m, m], axis=0)
    stack = lambda m: jnp.where(mask, dup(m), 0.0)

    def solve_chain(p, r0, stash):
        sl = slice(p * LANES, (p + 1) * LANES)
        ld = lambda ref: ref[pl.ds(r0, C), sl]
        r_p, k_p, v_p, a_p, cum, ldec = ld(r_s), ld(k_s), ld(v_s), ld(a_s), ld(cum_s), ld(ld_s)
        einc = jnp.exp(cum)
        eex = jnp.exp(cum - ldec)
        einv = jnp.exp(-cum)
        etail = jnp.exp(cum[C - 1:C, :] - cum)
        kks = stack(ld(kkraw_s))
        kks = kks * jnp.minimum(lax.rsqrt(jnp.sum(kks * kks, axis=-1, keepdims=True)), 1e12)
        As = -kks * dup(eex)
        Bs = kks * dup(a_p * einv)
        Bh = kks * dup(a_p * etail)
        Ks = stack(k_p * einv)
        Kh = stack(k_p * etail)
        Rs = stack(r_p * einc)
        Vs = stack(v_p)
        AR = jnp.concatenate([As, Rs], axis=0).astype(BF16)
        Vb = Vs.astype(BF16)
        G = _mm_nt(AR, jnp.concatenate([Bs, Ks], axis=0))
        yield
        Aab = jnp.where(strict, G[:2 * C, :2 * C], 0.0)
        Aak = jnp.where(strict, G[:2 * C, 2 * C:], 0.0)
        Arb = jnp.where(incl, G[2 * C:, :2 * C], 0.0)
        Ark = jnp.where(incl, G[2 * C:, 2 * C:], 0.0)
        Y = _mm(Aak, Vb)
        yield
        T = yield from _nilpotent_inverse(Aab, eye)
        WU = _mm(T, jnp.concatenate([AR[:2 * C], Y.astype(BF16)], axis=1))
        yield
        bonus = jnp.sum(stack(r_p * k_p * rk_ref[:, sl]), axis=-1, keepdims=True) * Vs
        stash[p] = dict(WU=WU, R=AR[2 * C:], Vs=Vs, bonus=bonus,
                        Aro=jnp.concatenate([Arb, Ark], axis=1).astype(BF16),
                        BKh=jnp.concatenate([Bh, Kh], axis=0).astype(BF16), decay=einc[C - 1:C, :])

    def state_chain(p, r0, stash):
        sl = slice(p * LANES, (p + 1) * LANES)
        s = stash[p]
        S = state_ref[p]
        Sb = S.astype(BF16)
        W = _mm_nt(s["WU"][:, :LANES], Sb) + s["WU"][:, LANES:]
        yield
        WV = jnp.concatenate([W, s["Vs"]], axis=0)
        O = _dot_nt(s["R"], Sb) + _mm(s["Aro"], WV)
        state_ref[p] = S * s["decay"] + _mm(WV.T, s["BKh"])
        yield
        mean = jnp.sum(O, axis=-1, keepdims=True) * (1.0 / A_HEAD)
        cen = jnp.where(mask, O - mean, 0.0)
        var = jnp.sum(cen * cen, axis=-1, keepdims=True) * (1.0 / A_HEAD)
        normed = jnp.where(mask, cen * lax.rsqrt(var + A_LNX_EPS) * lnw_ref[:, sl] + lnb_ref[:, sl], 0.0)
        full = normed + s["bonus"]
        o_ref[pl.ds(r0, C), sl] = (full[:C] + full[C:]) * g_s[pl.ds(r0, C), sl]

    return (lambda r0: [token_chain(r0)],
            lambda r0, stash: [solve_chain(p, r0, stash) for p in range(A_PAIRS)],
            lambda r0, stash: [state_chain(p, r0, stash) for p in range(A_PAIRS)])


CHUNKS_IN_FLIGHT = 2
N_RWKV_PARAMS = 10
N_GDN_PARAMS = 4
N_RWKV_SCRATCH = 8
N_GDN_SCRATCH = 5


def _mix_prompt_body(pa_ref, qkv_ref, ab_ref, z_ref, *refs):
    refs = list(refs)
    take = lambda n: [refs.pop(0) for _ in range(n)]
    rwkv_prm, gdn_prm = take(N_RWKV_PARAMS), take(N_GDN_PARAMS)
    oa_ref, ob_ref, sfa_ref, sfb_ref = take(4)
    carry_a, state_a, carry_b, state_b = take(4)
    rwkv_scr, gdn_scr = take(N_RWKV_SCRATCH), take(N_GDN_SCRATCH)
    t = pl.program_id(1)
    tt = pa_ref.shape[0]

    @pl.when(t == 0)
    def _():
        for ref in (carry_a, state_a, carry_b, state_b):
            ref[...] = jnp.zeros_like(ref)

    rwkv_token, rwkv_solve, rwkv_state = _rwkv_prompt_part(pa_ref, *rwkv_prm, oa_ref, carry_a, state_a, *rwkv_scr)
    gdn_token, gdn_solve, gdn_state = _gdn_prompt_part(qkv_ref, ab_ref, z_ref, *gdn_prm, ob_ref, carry_b, state_b,
                                                       *gdn_scr)

    n_chunks = tt // CHUNK
    group = min(CHUNKS_IN_FLIGHT, n_chunks)
    n_groups = n_chunks // group
    stashes = [({}, {}) for _ in range(n_chunks)]
    chunks_of = lambda gi: range(gi * group, (gi + 1) * group) if 0 <= gi < n_groups else ()

    def in_sequence(per_chunk_chains):
        for chains in zip(*per_chunk_chains):
            for chain in chains:
                yield from chain

    for gi in range(n_groups + 2):
        chains = []
        for c in chunks_of(gi):
            chains += rwkv_token(c * CHUNK) + gdn_token(c * CHUNK)
        for c in chunks_of(gi - 1):
            chains += rwkv_solve(c * CHUNK, stashes[c][0]) + gdn_solve(c * CHUNK, stashes[c][1])
        state_chains = [rwkv_state(c * CHUNK, stashes[c][0]) + gdn_state(c * CHUNK, stashes[c][1])
                        for c in chunks_of(gi - 2)]
        if state_chains:
            chains += [in_sequence([per_chunk[i:i + 1] for per_chunk in state_chains])
                       for i in range(len(state_chains[0]))]
        _round_robin(chains)

    @pl.when(t == pl.num_programs(1) - 1)
    def _():
        sfa_ref[0] = state_a[...]
        sfb_ref[0] = state_b[...]


def _mix_prompt(pa, qkv, ab, z, B, T, rwkv_params, gdn_params, tt):
    n = pa.shape[0]
    nt = T // tt
    assert len(rwkv_params) == N_RWKV_PARAMS and len(gdn_params) == N_GDN_PARAMS
    rows = lambda w: pl.BlockSpec((tt, w), lambda b, t: (b * nt + t, 0))
    state = lambda: pl.BlockSpec((1, 4, LANES, LANES), lambda b, t: (b, 0, 0, 0))
    big = lambda: pltpu.VMEM((tt, A_WIDTH), F32)
    return pl.pallas_call(
        _mix_prompt_body,
        grid=(B, nt),
        in_specs=[rows(A_PROJ), rows(CONV_CH), rows(LANES), rows(B_WIDTH)]
                 + [_const_spec(p.shape) for p in rwkv_params + gdn_params],
        out_specs=[rows(A_WIDTH), rows(B_WIDTH), state(), state()],
        out_shape=[jax.ShapeDtypeStruct((n, A_WIDTH), F32), jax.ShapeDtypeStruct((n, B_WIDTH), F32),
                   jax.ShapeDtypeStruct((B, A_PAIRS, LANES, LANES), F32),
                   jax.ShapeDtypeStruct((B, B_HEADS, B_HEAD, B_HEAD), F32)],
        scratch_shapes=[pltpu.VMEM((SUBLANES, A_PROJ), F32), pltpu.VMEM((A_PAIRS, LANES, LANES), F32),
                        pltpu.VMEM((SUBLANES, CONV_CH), F32), pltpu.VMEM((B_HEADS, B_HEAD, B_HEAD), F32)]
                       + [big() for _ in range(N_RWKV_SCRATCH + N_GDN_SCRATCH)],
        compiler_params=_cparams(2),
        name="mix_prompt",
    )(pa, qkv, ab, z, *rwkv_params, *gdn_params)


def _first_step_rows(rows, seqs, steps, state_rows, offset=0):
    hist = state_rows.shape[0] // seqs
    r, c = _iota((rows, seqs * hist), 0), _iota((rows, seqs * hist), 1)
    t = r & (steps - 1)
    sel = _one_hot((c == _group(r, steps) * hist + offset + t) & (t < hist - offset))
    return sum(_dot(sel, piece) for piece in _split3(state_rows))


def _last_step_rows(x, seqs, steps, keep):
    r, c = _iota((seqs * keep, seqs * steps), 0), _iota((seqs * keep, seqs * steps), 1)
    i = (c & (steps - 1)) - (steps - keep)
    sel = _one_hot((i >= 0) & (r == _group(c, steps) * keep + i))
    return sum(_dot(sel, piece) for piece in _split3(x))


ROWP = dict(mu_r=0, mu_k=1, mu_v=2, w0=3, a0=4, k_k=5, k_a=6, r_k=7, lnw=8, lnb=9)
ROWP_ROWS = 16
VALUE_GROUP = SUBLANES


def _rwkv_lanes_body(par_ref, pak_ref, pav_ref, paw_ref, pag_ref, shr_ref, shk_ref, shv_ref, shw_ref, shg_ref,
                     pa_ref, s_ref, rowp_ref, shared_ref, w2a_ref, g2_ref,
                     o_ref, sout_ref, shift_out_ref,
                     tr_s, ot_s, *, steps):
    rows = par_ref.shape[0]
    B = rows // steps
    p = pl.program_id(0)
    rp = lambda name: rowp_ref[ROWP[name]:ROWP[name] + 1, :]

    @pl.when(p == 0)
    def _():
        shift_out_ref[...] = _last_step_rows(pa_ref[...], B, steps, 1)

    def lerp(x_ref, first_ref, mu):
        per_step = [x_ref[pl.ds(t, B, stride=steps), :] for t in range(steps)]
        x = jnp.concatenate(per_step, axis=0)
        prev = jnp.concatenate([first_ref[...]] + per_step[:-1], axis=0)
        return x + (prev - x) * mu

    r = lerp(par_ref, shr_ref, rp("mu_r"))
    k = lerp(pak_ref, shk_ref, rp("mu_k"))
    v = lerp(pav_ref, shv_ref, rp("mu_v"))
    wa = lerp(paw_ref, shw_ref, shared_ref[0:1, :])
    gd = lerp(pag_ref, shg_ref, shared_ref[1:2, :])
    lane = _iota((1, LANES), 1)
    lora = _mm(jnp.where(lane < A_RANK_W, jnp.tanh(wa), wa), w2a_ref[...])
    g = _mm(_sigmoid(gd), g2_ref[...])
    w_log = -_softplus(-(rp("w0") + lora[:, :LANES])) - 0.5
    decay = jnp.exp(-jnp.exp(w_log))
    a = _sigmoid(rp("a0") + lora[:, LANES:])
    k_mod = k * (1.0 + (a - 1.0) * rp("k_a"))
    hi, hj = _iota((LANES, LANES), 0), _iota((LANES, LANES), 1)
    pair_ones = _one_hot(_group(hi, A_HEAD) == _group(hj, A_HEAD))
    head_sum = lambda m: _mm_sel(m, pair_ones)
    kk_raw = k * rp("k_k")
    kk = kk_raw * jnp.minimum(lax.rsqrt(head_sum(kk_raw * kk_raw)), 1e12)
    names = ("nkk", "beta", "decay", "k", "r", "v")
    for idx, m in enumerate((-kk, kk * a, decay, k_mod, r, v)):
        for t in range(steps):
            tr_s[idx, t] = m[t * B:(t + 1) * B, :].T
    at = lambda name, t: tr_s.at[names.index(name), t]

    def group(gi, carry):
        j = gi // (A_HEAD // VALUE_GROUP)
        v0 = (gi % (A_HEAD // VALUE_GROUP)) * VALUE_GROUP
        keys = lambda name, t: at(name, t)[pl.ds(pl.multiple_of(j * A_HEAD, A_HEAD), A_HEAD), :]
        v_rows = [at("v", t)[pl.ds(pl.multiple_of(gi * VALUE_GROUP, VALUE_GROUP), VALUE_GROUP), :]
                  for t in range(steps)]
        outs = [[] for _ in range(steps)]
        for i in range(VALUE_GROUP):
            S = s_ref[j, v0 + i]
            for t in range(steps):
                sa = jnp.sum(S * keys("nkk", t), axis=0, keepdims=True)
                S = S * keys("decay", t) + sa * keys("beta", t) + v_rows[t][i:i + 1, :] * keys("k", t)
                outs[t].append(jnp.sum(S * keys("r", t), axis=0, keepdims=True))
            sout_ref[j, v0 + i] = S
        for t in range(steps):
            ot_s[t, pl.ds(pl.multiple_of(gi * VALUE_GROUP, VALUE_GROUP), VALUE_GROUP), :] = _rows_to_tile(outs[t])
        return carry

    lax.fori_loop(0, 2 * A_HEAD // VALUE_GROUP, group, 0)
    o = jnp.concatenate([ot_s[t].T for t in range(steps)], axis=0)
    mean = head_sum(o) * (1.0 / A_HEAD)
    cen = o - mean
    var = head_sum(cen * cen) * (1.0 / A_HEAD)
    o = cen * lax.rsqrt(var + A_LNX_EPS) * rp("lnw") + rp("lnb")
    o = (o + head_sum(r * k_mod * rp("r_k")) * v) * g
    for t in range(steps):
        o_ref[pl.ds(t, B, stride=steps), :] = o[t * B:(t + 1) * B, :]


def _rwkv_lanes(pa, shift, state_t, layer, steps, tables):
    rowp, shared, w2a_p, g2_p = tables
    n = pa.shape[0]
    B = state_t.shape[-1]
    assert n == B * steps and B == LANES and steps & (steps - 1) == 0
    col = lambda rows_, j: pl.BlockSpec((rows_, LANES), lambda p: (0, j(p)))
    groups = [lambda p: p, lambda p: A_PAIRS + p, lambda p: 2 * A_PAIRS + p,
              lambda p: 3 * A_PAIRS, lambda p: 3 * A_PAIRS + 1]
    block = (None, 2, A_HEAD, A_HEAD, B)
    return pl.pallas_call(
        functools.partial(_rwkv_lanes_body, steps=steps),
        grid=(A_PAIRS,),
        in_specs=[col(n, j) for j in groups] + [col(B, j) for j in groups]
                 + [_const_spec((n, A_PROJ)), pl.BlockSpec(block, lambda p: (layer, p, 0, 0, 0)),
                    pl.BlockSpec((None, ROWP_ROWS, LANES), lambda p: (p, 0, 0)), _const_spec(shared.shape),
                    pl.BlockSpec((None, LANES, 2 * LANES), lambda p: (p, 0, 0)),
                    pl.BlockSpec((None, LANES, LANES), lambda p: (p, 0, 0))],
        out_specs=[pl.BlockSpec((n, LANES), lambda p: (0, p)), pl.BlockSpec(block, lambda p: (0, p, 0, 0, 0)),
                   pl.BlockSpec((B, A_PROJ), lambda p: (0, 0))],
        out_shape=[jax.ShapeDtypeStruct((n, A_WIDTH), F32), jax.ShapeDtypeStruct((1,) + state_t.shape[1:], F32),
                   jax.ShapeDtypeStruct((B, A_PROJ), F32)],
        scratch_shapes=[pltpu.VMEM((6, steps, LANES, B), F32), pltpu.VMEM((steps, LANES, B), F32)],
        compiler_params=_cparams(1),
        name="rwkv_sample",
    )(*([pa] * 5), *([shift] * 5), pa, state_t, rowp, shared, w2a_p, g2_p)


def _rwkv_pair_tables(mu, w0, a0, k_k, k_a, r_k, lnw, lnb, w2, a2, g2):
    per_pair = lambda a: a.reshape(A_PAIRS, 1, LANES)
    rows = {"mu_r": mu[:A_WIDTH], "mu_k": mu[A_WIDTH:2 * A_WIDTH], "mu_v": mu[2 * A_WIDTH:3 * A_WIDTH],
            "w0": w0, "a0": a0, "k_k": k_k, "k_a": k_a, "r_k": r_k.reshape(-1), "lnw": lnw, "lnb": lnb}
    table = jnp.concatenate([per_pair(rows[name].astype(F32)) for name in sorted(ROWP, key=ROWP.get)]
                            + [jnp.zeros((A_PAIRS, ROWP_ROWS - len(ROWP), LANES), F32)], axis=1)
    shared = jnp.concatenate([mu[3 * A_WIDTH:3 * A_WIDTH + LANES].reshape(1, LANES),
                              mu[3 * A_WIDTH + LANES:].reshape(1, LANES),
                              jnp.zeros((SUBLANES - 2, LANES), F32)], axis=0).astype(F32)
    by_pair = lambda w: jnp.transpose(w.astype(F32).reshape(w.shape[0], A_PAIRS, LANES), (1, 0, 2))
    zeros = jnp.zeros((A_PAIRS, A_RANK_W, LANES), F32)
    w2a_p = jnp.concatenate([jnp.concatenate([by_pair(w2), zeros], axis=2),
                             jnp.concatenate([zeros, by_pair(a2)], axis=2)], axis=1)
    return table, shared, w2a_p, by_pair(g2)


def _gdn_qkv(conv):
    c = _silu(conv)
    qs, ks = [], []
    for h in range(B_HEADS):
        q = c[:, h * B_HEAD:(h + 1) * B_HEAD]
        k = c[:, B_WIDTH + h * B_HEAD:B_WIDTH + (h + 1) * B_HEAD]
        qs.append(q * (lax.rsqrt(jnp.sum(q * q, axis=-1, keepdims=True) + 1e-6) * (B_HEAD ** -0.5)))
        ks.append(k * lax.rsqrt(jnp.sum(k * k, axis=-1, keepdims=True) + 1e-6))
    q = jnp.concatenate(qs, axis=1)
    k = jnp.concatenate(ks, axis=1)
    v = c[:, 2 * B_WIDTH:]
    return q, k, v


def _gdn_gates(ab, alog, dtb):
    lane = _iota((1, LANES), 1)
    g = -jnp.exp(alog) * _softplus(ab + dtb)
    beta = _sigmoid(ab)
    gb = jnp.where(lane < B_HEADS, g, beta)
    si, sj = _iota((LANES, 2 * B_WIDTH), 0), _iota((LANES, 2 * B_WIDTH), 1)
    spread = _mm_sel(gb, _one_hot(si == _group(sj, B_HEAD)))
    return spread[:, :B_WIDTH], spread[:, B_WIDTH:]


def _gdn_out(o, norm_w, z):
    return o * lax.rsqrt(jnp.mean(o * o, axis=-1, keepdims=True) + RMS_EPS) * norm_w * _silu(z)


def _gdn_prompt_part(qkv_ref, ab_ref, z_ref, cw_ref, alog_ref, dtb_ref, nw_ref, o_ref,
                     carry_ref, state_ref, q_s, k_s, v_s, gc_s, beta_s):
    tt = qkv_ref.shape[0]
    C = CHUNK
    g, beta = _gdn_gates(ab_ref[...], alog_ref[...], dtb_ref[...])
    ri, ci = _iota((tt, tt), 0), _iota((tt, tt), 1)
    beta_s[...] = beta
    gc_s[...] = _sel_mm(_one_hot((_group(ri, C) == _group(ci, C)) & (ci <= ri)), g)

    def token_chain(r0):
        rows = slice(r0, r0 + C)
        x = qkv_ref[rows, :]
        before = carry_ref[...] if r0 == 0 else qkv_ref[r0 - SUBLANES:r0, :]
        if r0 + C == tt:
            carry_ref[...] = x[C - SUBLANES:, :]
        row8 = _iota((SUBLANES, 1), 0)

        def shift_rows(cur, halo, i):
            down = pltpu.roll(cur, i, axis=0)
            top = jnp.where(row8 < i, pltpu.roll(halo, i, axis=0), down[:SUBLANES])
            return jnp.concatenate([top, down[SUBLANES:]], axis=0)

        assert CONV_K == 4
        c0, c1, c2, c3 = (cw_ref[i:i + 1, :] for i in range(CONV_K))
        x1 = shift_rows(x, before, 1)
        yield
        older = x * c1 + x1 * c0
        older_halo = before * c1 + pltpu.roll(before, 1, axis=0) * c0
        conv = x * c3 + x1 * c2 + shift_rows(older, older_halo, 2)
        yield
        q, k, v = _gdn_qkv(conv)
        q_s[rows, :] = q
        k_s[rows, :] = k
        v_s[rows, :] = v
        yield

    i2, j2 = _iota((2 * C, 2 * C), 0), _iota((2 * C, 2 * C), 1)
    same_head = _group(i2, C) == _group(j2, C)
    strict = same_head & (i2 > j2)
    incl = same_head & (i2 >= j2)
    eye = jnp.where(i2 == j2, 1.0, 0.0)
    first = _iota((2 * C, 1), 0) < C

    def solve_chain(pr, r0, stash):
        sls = [slice(h * B_HEAD, (h + 1) * B_HEAD) for h in (2 * pr, 2 * pr + 1)]
        ld = lambda ref: jnp.concatenate([ref[pl.ds(r0, C), sl] for sl in sls], axis=0)
        q_h, k_h, v_h, gc_h, beta_h = ld(q_s), ld(k_s), ld(v_s), ld(gc_s), ld(beta_s)
        diff = gc_h - gc_h.T
        dm = jnp.where(incl, jnp.exp(jnp.where(incl, diff, 0.0)), 0.0)
        kb = k_h * beta_h
        QK = _mm_nt(jnp.concatenate([kb, q_h], axis=0), k_h)
        yield
        N = -jnp.where(strict, QK[:2 * C] * dm, 0.0)
        qk = QK[2 * C:] * dm
        egc = jnp.exp(gc_h)
        X = jnp.concatenate([v_h * beta_h, kb * egc], axis=1)
        T = yield from _nilpotent_inverse(N, eye)
        UW = _mm(T, X)
        yield
        g_last = jnp.where(first, gc_h[C - 1:C, :], gc_h[2 * C - 1:2 * C, :])
        stash[pr] = dict(u=UW[:, :B_HEAD], w=UW[:, B_HEAD:].astype(BF16), qd=(q_h * egc).astype(BF16),
                         qk=qk.astype(BF16), k_dec_t=(k_h * jnp.exp(g_last - gc_h)).T.astype(BF16),
                         decay=[jnp.exp(gc_h[(j + 1) * C - 1:(j + 1) * C, :]) for j in range(2)])

    def state_chain(pr, r0, stash):
        heads = (2 * pr, 2 * pr + 1)
        sls = [slice(h * B_HEAD, (h + 1) * B_HEAD) for h in heads]
        s = stash[pr]
        wS, qS, S_old = [], [], []
        for j, h in enumerate(heads):
            S = state_ref[h]
            rows = slice(j * C, (j + 1) * C)
            wq = _dot(jnp.concatenate([s["w"][rows], s["qd"][rows]], axis=0), S.astype(BF16))
            wS.append(wq[:C])
            qS.append(wq[C:])
            S_old.append(S)
        yield
        v_new = s["u"] - jnp.concatenate(wS, axis=0)
        o = jnp.concatenate(qS, axis=0) + _dot(s["qk"], v_new.astype(BF16))
        for j, h in enumerate(heads):
            mine = first if j == 0 else jnp.logical_not(first)
            rows = slice(j * C, (j + 1) * C)
            state_ref[h] = S_old[j] * s["decay"][j] + _mm(s["k_dec_t"], jnp.where(mine, v_new, 0.0))
            o_ref[pl.ds(r0, C), sls[j]] = _gdn_out(o[rows], nw_ref[...], z_ref[pl.ds(r0, C), sls[j]])
        yield

    return (lambda r0: [token_chain(r0)],
            lambda r0, stash: [solve_chain(pr, r0, stash) for pr in range(B_HEADS // 2)],
            lambda r0, stash: [state_chain(pr, r0, stash) for pr in range(B_HEADS // 2)])


def _gdn_chunk_body(qkv_ref, hist_ref, ab_ref, z_ref, s_ref, cw_ref, alog_ref, dtb_ref, nw_ref,
                    o_ref, sout_ref, hist_out_ref, *, steps):
    rows = ab_ref.shape[0]
    seqs = rows // steps
    x = qkv_ref[...]
    hist_out_ref[...] = _last_step_rows(x, seqs, steps, CONV_K - 1)
    hist = hist_ref[...]
    t_idx = _iota((rows, 1), 0) & (steps - 1)
    conv = x * cw_ref[CONV_K - 1:CONV_K, :]
    for i in range(1, CONV_K):
        tap = jnp.where(t_idx >= i, pltpu.roll(x, i, axis=0),
                        _first_step_rows(rows, seqs, steps, hist, offset=CONV_K - 1 - i))
        conv = conv + tap * cw_ref[CONV_K - 1 - i:CONV_K - i, :]
    q, k, v = _gdn_qkv(conv)
    g, beta = _gdn_gates(ab_ref[...], alog_ref[...], dtb_ref[...])
    i2, j2 = _iota((rows, rows), 0), _iota((rows, rows), 1)
    same = _group(i2, steps) == _group(j2, steps)
    strict = same & (i2 > j2)
    incl = same & (i2 >= j2)
    eye = jnp.where(i2 == j2, 1.0, 0.0)
    gc = _sel_mm(_one_hot(incl), g)
    g_end = _sel_mm(_one_hot(j2 == (i2 | (steps - 1))), gc)
    row = _iota((rows, 1), 0)
    pair_rows = 2 * SUBLANES
    first_half = (_iota((pair_rows, 1), 0) & (SUBLANES - 1)) < steps
    assert 2 * steps == SUBLANES

    def head_chain(h):
        sl = slice(h * B_HEAD, (h + 1) * B_HEAD)
        q_h, k_h, v_h, gc_h, beta_h, ge_h = q[:, sl], k[:, sl], v[:, sl], gc[:, sl], beta[:, sl], g_end[:, sl]
        dm = jnp.where(incl, jnp.exp(jnp.where(incl, gc_h - gc_h.T, 0.0)), 0.0)
        kb = k_h * beta_h
        QK = _mm_nt(jnp.concatenate([kb, q_h], axis=0), k_h)
        yield
        N = -jnp.where(strict, QK[:rows] * dm, 0.0)
        qk = QK[rows:] * dm
        assert steps == 4
        N2 = _mm(N, N)
        yield
        T = eye + N
        T = T + _mm(T, N2)
        yield
        egc = jnp.exp(gc_h)
        UW = _mm(T, jnp.concatenate([v_h * beta_h, kb * egc], axis=1))
        yield
        w = UW[:, B_HEAD:].astype(BF16)
        qd = (q_h * egc).astype(BF16)
        k_dec_t = (k_h * jnp.exp(ge_h - gc_h)).T.astype(BF16)
        decay = jnp.exp(ge_h)
        wS, qS = [], []
        for m in range(rows // SUBLANES):
            tile = slice(m * SUBLANES, (m + 1) * SUBLANES)
            lhs = jnp.concatenate([w[tile], qd[tile]], axis=0)
            res = jnp.where(first_half, _dot(lhs, s_ref[2 * m, h].astype(BF16)),
                            _dot(lhs, s_ref[2 * m + 1, h].astype(BF16)))
            wS.append(res[:SUBLANES])
            qS.append(res[SUBLANES:])
            yield
        v_new = UW[:, :B_HEAD] - jnp.concatenate(wS, axis=0)
        o = jnp.concatenate(qS, axis=0) + _mm(qk, v_new)
        o_ref[:, sl] = _gdn_out(o, nw_ref[...], z_ref[:, sl])
        yield
        for b in range(seqs):
            mine = _group(row, steps) == b
            sout_ref[b, h] = (s_ref[b, h] * decay[b * steps:b * steps + 1, :]
                              + _mm(k_dec_t, jnp.where(mine, v_new, 0.0)))
            yield

    _round_robin(head_chain(h) for h in range(B_HEADS))


def _gdn_sample(qkv, hist, ab, z, state, layer, steps, params, seqs):
    n = ab.shape[0]
    nb = state.shape[1]
    rows = seqs * steps
    hist_rows = seqs * (CONV_K - 1)
    assert SUBLANES % steps == 0 and rows % SUBLANES == 0 and hist_rows % SUBLANES == 0
    row = lambda w: pl.BlockSpec((rows, w), lambda i: (i, 0))
    block = (None, seqs, B_HEADS, B_HEAD, B_HEAD)
    sspec = pl.BlockSpec(block, lambda i: (layer, i, 0, 0, 0))
    ospec = pl.BlockSpec(block, lambda i: (0, i, 0, 0, 0))
    return pl.pallas_call(
        functools.partial(_gdn_chunk_body, steps=steps),
        grid=(nb // seqs,),
        in_specs=[row(CONV_CH), pl.BlockSpec((hist_rows, CONV_CH), lambda i: (i, 0)), row(LANES), row(B_WIDTH), sspec]
                 + [_const_spec(p.shape) for p in params],
        out_specs=[row(B_WIDTH), ospec, pl.BlockSpec((hist_rows, CONV_CH), lambda i: (i, 0))],
        out_shape=[jax.ShapeDtypeStruct((n, B_WIDTH), F32), jax.ShapeDtypeStruct((1,) + state.shape[1:], F32),
                   jax.ShapeDtypeStruct(hist.shape, F32)],
        compiler_params=_cparams(1),
        name="gdn_sample",
    )(qkv, hist, ab, z, state, *params)


def _cuts(widths):
    edges, total = [], 0
    for w in widths[:-1]:
        total += w
        edges.append(total)
    return edges


def _regroup_pa(a):
    r, wd, k, v, ad, gd = jnp.split(a, _cuts((A_WIDTH, A_RANK_W, A_WIDTH, A_WIDTH, A_RANK_A, A_RANK_G)), axis=-1)
    return jnp.concatenate([r, k, v, wd, ad, gd], axis=-1)


def _ungroup_pa(a):
    r, k, v, wd, ad, gd = jnp.split(a, _cuts((A_WIDTH, A_WIDTH, A_WIDTH, A_RANK_W, A_RANK_A, A_RANK_G)), axis=-1)
    return jnp.concatenate([r, wd, k, v, ad, gd], axis=-1)


def _token_tile(n, want):
    tm = want
    while n % tm:
        tm //= 2
    return tm


def kernel(x_prompt, x_sample, state_rwkv, state_rwkv_shift, state_delta, state_conv, ffn1_norm, ffn1_w_gate, ffn1_w_up, ffn1_w_down, mix_norm, w_in, rwkv_mu, rwkv_w0, rwkv_w2, rwkv_a0, rwkv_a2, rwkv_g2, rwkv_k_k, rwkv_k_a, rwkv_r_k, rwkv_lnx_w, rwkv_lnx_b, gdn_conv_w, gdn_A_log, gdn_dt_bias, gdn_norm_w, proj_a, proj_b, w_out, ffn2_norm, ffn2_w_gate, ffn2_w_up, ffn2_w_down, final_norm):
    depth = ffn1_norm.shape[0]
    assert depth == 1, "single-layer trunk"
    Bp, Tp, _ = x_prompt.shape
    Bs, Ts, _ = x_sample.shape
    l = 0
    row = lambda a: a.reshape(1, -1).astype(F32)

    wi = w_in[l].astype(BF16)
    o_b = A_PROJ
    proj_w = (_regroup_pa(wi[:, :A_PROJ]),
              wi[:, o_b:o_b + CONV_CH + LANES],
              wi[:, o_b + CONV_CH + 2 * B_HEADS:])
    ffn1 = (row(ffn1_norm[l]), ffn1_w_gate[l].astype(BF16), ffn1_w_up[l].astype(BF16), ffn1_w_down[l].astype(BF16))
    ffn2 = (row(ffn2_norm[l]), ffn2_w_gate[l].astype(BF16), ffn2_w_up[l].astype(BF16), ffn2_w_down[l].astype(BF16))
    merge_w = (proj_a[l].astype(BF16), proj_b[l].astype(BF16), w_out[l].astype(BF16))
    zw = jnp.zeros((A_RANK_W, A_WIDTH), F32)
    w2a = jnp.concatenate([jnp.concatenate([rwkv_w2[l], zw], axis=1),
                           jnp.concatenate([zw, rwkv_a2[l]], axis=1)], axis=0)
    rwkv_params = (row(_regroup_pa(rwkv_mu[l])), row(rwkv_w0[l]), row(rwkv_a0[l]), row(rwkv_k_k[l]), row(rwkv_k_a[l]),
                   row(rwkv_r_k[l]), row(rwkv_lnx_w[l]), row(rwkv_lnx_b[l]), w2a, rwkv_g2[l].astype(F32))
    pad_lane = lambda a: jnp.pad(a.reshape(1, -1).astype(F32), ((0, 0), (0, LANES - a.size)))
    gdn_params = (gdn_conv_w[l].astype(F32), pad_lane(gdn_A_log[l]), pad_lane(gdn_dt_bias[l]), row(gdn_norm_w[l]))

    def trunk_front(x2):
        n = x2.shape[0]
        h = _ffn(x2, *ffn1, tm=_token_tile(n, 512))
        return (h,) + tuple(_proj(h, row(mix_norm[l]), *proj_w, tm=_token_tile(n, 512)))

    def trunk_back(h, oa, ob, gates):
        n = h.shape[0]
        return _tail(h, oa, ob, gates, *merge_w, *ffn2, row(final_norm), tm=_token_tile(n, 512))

    xp = x_prompt.reshape(Bp * Tp, D_MODEL)
    h, pa, qkv, z, gates, ab = trunk_front(xp)
    tt = _token_tile(Tp, 256)
    oa, ob, s_pairs, delta_p = _mix_prompt(pa, qkv, ab, z, Bp, Tp, rwkv_params, gdn_params, tt)
    y_prompt = trunk_back(h, oa, ob, gates).reshape(Bp, Tp, D_MODEL)
    sp = s_pairs.reshape(Bp, A_PAIRS, 2, A_HEAD, 2, A_HEAD)
    rwkv_p = jnp.stack([sp[:, :, 0, :, 0], sp[:, :, 1, :, 1]], axis=2).reshape(Bp, A_HEADS, A_HEAD, A_HEAD)
    shift_p = _ungroup_pa(pa.reshape(Bp, Tp, A_PROJ)[:, -1])
    conv_p = qkv.reshape(Bp, Tp, CONV_CH)[:, Tp - (CONV_K - 1):]

    xs = x_sample.reshape(Bs * Ts, D_MODEL)
    h, pa, qkv, z, gates, ab = trunk_front(xs)
    assert Ts & (Ts - 1) == 0 and Ts >= CONV_K - 1, "sample steps: power of two covering the conv history"
    tables = _rwkv_pair_tables(_regroup_pa(rwkv_mu[l]), rwkv_w0[l], rwkv_a0[l], rwkv_k_k[l], rwkv_k_a[l], rwkv_r_k[l],
                               rwkv_lnx_w[l], rwkv_lnx_b[l], rwkv_w2[l], rwkv_a2[l], rwkv_g2[l])
    oa, s_lanes, last_pa = _rwkv_lanes(pa, _regroup_pa(state_rwkv_shift[l].astype(F32)),
                                       jnp.transpose(state_rwkv.astype(F32), (0, 2, 3, 4, 1)), l, Ts, tables)
    rwkv_s = jnp.transpose(s_lanes, (0, 4, 1, 2, 3))
    shift_s = _ungroup_pa(last_pa)
    hist = state_conv[l].astype(F32).reshape(Bs * (CONV_K - 1), CONV_CH)
    ob, delta_s, new_hist = _gdn_sample(qkv, hist, ab, z, state_delta.astype(F32), l, Ts, gdn_params,
                                        seqs=_token_tile(Bs, 32))
    conv_s = new_hist.reshape(Bs, CONV_K - 1, CONV_CH)
    y_sample = trunk_back(h, oa, ob, gates).reshape(Bs, Ts, D_MODEL)

    add_depth = lambda a: a[None]
    return (y_prompt, y_sample,
            add_depth(rwkv_p), add_depth(shift_p), add_depth(delta_p), add_depth(conv_p),
            rwkv_s, add_depth(shift_s), delta_s, add_depth(conv_s))
```

```python
import functools

import jax
import jax.numpy as jnp
from jax import lax
from jax.experimental import pallas as pl
from jax.experimental.pallas import tpu as pltpu

F32 = jnp.float32
BF16 = jnp.bfloat16

D_MODEL = 1024
D_FF = 2816
RMS_EPS = 1e-6
A_HEAD = 64
A_HEADS = 8
A_WIDTH = A_HEADS * A_HEAD
A_RANK_W = 64
A_RANK_A = 64
A_RANK_G = 128
A_PROJ = 3 * A_WIDTH + A_RANK_W + A_RANK_A + A_RANK_G
A_LNX_EPS = 64e-5
A_PAIRS = A_HEADS // 2
B_HEADS = 4
B_HEAD = 128
B_WIDTH = B_HEADS * B_HEAD
CONV_K = 4
CONV_CH = 3 * B_WIDTH
B_PROJ = CONV_CH + 2 * B_HEADS + B_WIDTH
GATE_COLS = 2 * D_MODEL
LANES = 128
SUBLANES = 8
MXU_DIM = 256
VMEM_LIMIT_BYTES = 56 * 1024 * 1024
CHUNK = 64
PA_R, PA_K, PA_V, PA_WA, PA_G = 0, A_WIDTH, 2 * A_WIDTH, 3 * A_WIDTH, 3 * A_WIDTH + A_RANK_W + A_RANK_A
PROJ_SPLITS = (A_PROJ, CONV_CH, B_WIDTH, GATE_COLS, LANES)


def _cparams(n_grid_dims):
    return pltpu.CompilerParams(dimension_semantics=("arbitrary",) * n_grid_dims,
                                vmem_limit_bytes=VMEM_LIMIT_BYTES)


def _const_spec(shape):
    nd = len(shape)
    return pl.BlockSpec(shape, lambda *_: (0,) * nd, pipeline_mode=pl.Buffered(1))


def _dot(a, b):
    return jnp.dot(a, b, preferred_element_type=F32)


def _dot_nt(a, b):
    return lax.dot_general(a, b, (((1,), (1,)), ((), ())), preferred_element_type=F32)


def _split3(x):
    hi = x.astype(BF16)
    rest = x - hi.astype(F32)
    mid = rest.astype(BF16)
    lo = (rest - mid.astype(F32)).astype(BF16)
    return hi, mid, lo


def _mm(a, b):
    return _dot(a.astype(BF16), b.astype(BF16))


def _mm_nt(a, b):
    return _dot_nt(a.astype(BF16), b.astype(BF16))


def _sel_mm(sel, x):
    return _dot(jnp.concatenate([sel, sel, sel], axis=1), jnp.concatenate(_split3(x), axis=0))


def _mm_sel(x, sel):
    return _dot(jnp.concatenate(_split3(x), axis=1), jnp.concatenate([sel, sel, sel], axis=0))


INV_BASE = 8


def _nilpotent_inverse(n, eye):
    width = n.shape[1]
    bi, bj = _iota(n.shape, 0), _iota(n.shape, 1)
    same = lambda size: _group(bi, size) == _group(bj, size)
    d = jnp.where(same(INV_BASE), n, 0.0)
    t = eye + d
    d = _mm(d, d)
    yield
    for _ in range(INV_BASE.bit_length() - 3):
        both = _mm(d, jnp.concatenate([t, d], axis=1))
        yield
        t = t + both[:, :width]
        d = both[:, width:]
    t = t + _mm(d, t)
    yield
    size = INV_BASE
    while size < CHUNK:
        coupling = jnp.where(same(2 * size) & jnp.logical_not(same(size)), n, 0.0)
        tb = t.astype(BF16)
        lt = _mm(coupling, tb)
        yield
        t = t + _mm(tb, lt)
        yield
        size *= 2
    return t


def _round_robin(chains):
    chains = list(chains)
    while chains:
        for chain in list(chains):
            try:
                next(chain)
            except StopIteration:
                chains.remove(chain)


def _rms(x, w):
    return x * lax.rsqrt(jnp.mean(x * x, axis=-1, keepdims=True) + RMS_EPS) * w


def _sigmoid(x):
    return 1.0 / (1.0 + jnp.exp(-x))


def _silu(x):
    return x * _sigmoid(x)


def _softplus(x):
    return jnp.maximum(x, 0.0) + jnp.log(1.0 + jnp.exp(-jnp.abs(x)))


def _iota(shape, dim):
    return lax.broadcasted_iota(jnp.int32, shape, dim)


def _group(idx, size):
    assert size & (size - 1) == 0
    return lax.shift_right_logical(idx, size.bit_length() - 1)


def _one_hot(cond):
    return jnp.where(cond, 1.0, 0.0).astype(BF16)


def _rows_to_tile(rows):
    rid = _iota((SUBLANES, 1), 0)
    tile = jnp.zeros((SUBLANES, rows[0].shape[1]), F32)
    for i, row in enumerate(rows):
        tile = jnp.where(rid == i, row, tile)
    return tile


def _swiglu_half_step(x, nw, wg_ref, wu_ref, wd_ref, slabs=(D_FF,)):
    xn = _rms(x, nw).astype(BF16)
    assert sum(slabs) == D_FF and all(w % MXU_DIM == 0 for w in slabs)
    y = None
    start = 0
    for width in slabs:
        cols = slice(start, start + width)
        start += width
        g = _dot(xn, wg_ref[:, cols])
        u = _dot(xn, wu_ref[:, cols])
        act = (_silu(g) * u).astype(BF16)
        part = _dot(act, wd_ref[cols, :])
        y = part if y is None else y + part
    return x + 0.5 * y


def _ffn_body(xa_ref, xb_ref, nw_ref, wg_ref, wu_ref, wd_ref, o_ref, *, blocks_a):
    x = jnp.where(pl.program_id(0) < blocks_a, xa_ref[...], xb_ref[...])
    o_ref[...] = _swiglu_half_step(x, nw_ref[...], wg_ref, wu_ref, wd_ref)


def _ffn(xa, xb, nw, wg, wu, wd, tm):
    na, nb = xa.shape[0], xb.shape[0]
    assert na % tm == 0 and nb % tm == 0
    blocks_a = na // tm
    return pl.pallas_call(
        functools.partial(_ffn_body, blocks_a=blocks_a),
        grid=((na + nb) // tm,),
        in_specs=[pl.BlockSpec((tm, D_MODEL), lambda i: (jnp.minimum(i, blocks_a - 1), 0)),
                  pl.BlockSpec((tm, D_MODEL), lambda i: (jnp.maximum(i - blocks_a, 0), 0)),
                  _const_spec((1, D_MODEL)),
                  _const_spec((D_MODEL, D_FF)), _const_spec((D_MODEL, D_FF)), _const_spec((D_FF, D_MODEL))],
        out_specs=pl.BlockSpec((tm, D_MODEL), lambda i: (i, 0)),
        out_shape=jax.ShapeDtypeStruct((na + nb, D_MODEL), F32),
        compiler_params=_cparams(1),
        name="ffn1",
    )(xa, xb, nw, wg, wu, wd)


def _proj_body(h_ref, nw_ref, wpa_ref, wqa_ref, wzg_ref, pa_ref, qkv_ref, z_ref, gates_ref, ab_ref):
    u = _rms(h_ref[...], nw_ref[...]).astype(BF16)
    pa_ref[...] = _dot(u, wpa_ref[...])
    qkv_ref[...] = _dot(u, wqa_ref[:, :CONV_CH])
    ab_ref[...] = _dot(u, wqa_ref[:, CONV_CH:])
    z_ref[...] = _dot(u, wzg_ref[:, :B_WIDTH])
    gates_ref[...] = _dot(u, wzg_ref[:, B_WIDTH:])


def _proj(h, nw, w_pa, w_qa, w_zg, tm):
    n = h.shape[0]
    assert w_qa.shape[1] == CONV_CH + LANES and w_zg.shape[1] == B_WIDTH + GATE_COLS
    return pl.pallas_call(
        _proj_body,
        grid=(n // tm,),
        in_specs=[pl.BlockSpec((tm, D_MODEL), lambda i: (i, 0)), _const_spec((1, D_MODEL)),
                  _const_spec(w_pa.shape), _const_spec(w_qa.shape), _const_spec(w_zg.shape)],
        out_specs=[pl.BlockSpec((tm, w), lambda i: (i, 0)) for w in PROJ_SPLITS],
        out_shape=[jax.ShapeDtypeStruct((n, w), F32) for w in PROJ_SPLITS],
        compiler_params=_cparams(1),
        name="proj",
    )(h, nw, w_pa, w_qa, w_zg)


TAIL_FF_SLABS = (6 * MXU_DIM, 5 * MXU_DIM)


def _tail_body(h_ref, oa_ref, ob_ref, gates_ref, pa_ref, pb_ref, wo_ref, nw_ref, wg_ref, wu_ref, wd_ref,
               fn_ref, o_ref):
    ma = _dot(oa_ref[...].astype(BF16), pa_ref[...])
    mb = _dot(ob_ref[...].astype(BF16), pb_ref[...])
    merged = _sigmoid(gates_ref[:, :D_MODEL]) * ma + _sigmoid(gates_ref[:, D_MODEL:]) * mb
    h = h_ref[...] + _dot(merged.astype(BF16), wo_ref[...])
    h = _swiglu_half_step(h, nw_ref[...], wg_ref, wu_ref, wd_ref, slabs=TAIL_FF_SLABS)
    o_ref[...] = _rms(h, fn_ref[...])


def _tail(h, oa, ob, gates, proj_a, proj_b, w_out, nw, wg, wu, wd, fn, tm, first_row):
    n = oa.shape[0]
    assert n % tm == 0 and first_row % tm == 0
    row = lambda w: pl.BlockSpec((tm, w), lambda i: (i, 0))
    stream = lambda w: pl.BlockSpec((tm, w), lambda i: (i + first_row // tm, 0))
    return pl.pallas_call(
        _tail_body,
        grid=(n // tm,),
        in_specs=[stream(D_MODEL), row(A_WIDTH), row(B_WIDTH), stream(GATE_COLS),
                  _const_spec((A_WIDTH, D_MODEL)), _const_spec((B_WIDTH, D_MODEL)),
                  _const_spec((D_MODEL, D_MODEL)), _const_spec((1, D_MODEL)),
                  _const_spec((D_MODEL, D_FF)), _const_spec((D_MODEL, D_FF)), _const_spec((D_FF, D_MODEL)),
                  _const_spec((1, D_MODEL))],
        out_specs=row(D_MODEL),
        out_shape=jax.ShapeDtypeStruct((n, D_MODEL), F32),
        compiler_params=_cparams(1),
        name="tail",
    )(h, oa, ob, gates, proj_a, proj_b, w_out, nw, wg, wu, wd, fn)


def _rwkv_token_math(x, prev, mu, w0, a0, k_k, k_a, w2a, g2):
    pm = x + (prev - x) * mu
    r = pm[:, PA_R:PA_R + A_WIDTH]
    k = pm[:, PA_K:PA_K + A_WIDTH]
    v = pm[:, PA_V:PA_V + A_WIDTH]
    wa = pm[:, PA_WA:PA_WA + LANES]
    gd = pm[:, PA_G:PA_G + A_RANK_G]
    lane = _iota((1, LANES), 1)
    lora_in = jnp.where(lane < A_RANK_W, jnp.tanh(wa), wa)
    lora = _mm(lora_in, w2a)
    g = _mm(_sigmoid(gd), g2)
    yield
    w_log = -_softplus(-(w0 + lora[:, :A_WIDTH])) - 0.5
    log_decay = -jnp.exp(w_log)
    yield
    a = _sigmoid(a0 + lora[:, A_WIDTH:])
    kk_raw = k * k_k
    k_mod = k * (1.0 + (a - 1.0) * k_a)
    return r, k_mod, v, kk_raw, a, log_decay, g


def _pair_mask(rows_per_head):
    shape = (2 * rows_per_head, LANES)
    return _group(_iota(shape, 0), rows_per_head) == _group(_iota(shape, 1), A_HEAD)


def _rwkv_prompt_part(pa_ref, mu_ref, w0_ref, a0_ref, kk_ref, ka_ref, rk_ref, lnw_ref, lnb_ref, w2a_ref, g2_ref,
                      o_ref, carry_ref, state_ref, r_s, k_s, v_s, kkraw_s, a_s, cum_s, ld_s, g_s):
    tt = pa_ref.shape[0]
    C = CHUNK
    lower = _one_hot(_iota((C, C), 1) <= _iota((C, C), 0))

    def token_chain(r0):
        rows = slice(r0, r0 + C)
        x = pa_ref[rows, :]
        before = carry_ref[SUBLANES - 1:SUBLANES, :] if r0 == 0 else pa_ref[r0 - 1:r0, :]
        prev = jnp.where(_iota((C, 1), 0) == 0, before, pltpu.roll(x, 1, axis=0))
        if r0 + C == tt:
            carry_ref[...] = x[C - SUBLANES:, :]
        r, k_mod, v, kk_raw, a, log_decay, g = yield from _rwkv_token_math(
            x, prev, mu_ref[...], w0_ref[...], a0_ref[...], kk_ref[...], ka_ref[...], w2a_ref[...], g2_ref[...])
        r_s[rows, :] = r
        k_s[rows, :] = k_mod
        v_s[rows, :] = v
        kkraw_s[rows, :] = kk_raw
        a_s[rows, :] = a
        g_s[rows, :] = g
        ld_s[rows, :] = log_decay
        yield
        cum_s[rows, :] = sum(_dot(lower, piece) for piece in _split3(log_decay))
        yield

    mask = _pair_mask(C)
    i2, j2 = _iota((2 * C, 2 * C), 0), _iota((2 * C, 2 * C), 1)
    strict = i2 > j2
    incl = i2 >= j2
    eye = jnp.where(i2 == j2, 1.0, 0.0)
    dup = lambda m: jnp.concatenate([m, m], axis=0)
    stack = lambda m: jnp.where(mask, dup(m), 0.0)

    def solve_chain(p, r0, stash):
        sl = slice(p * LANES, (p + 1) * LANES)
        ld = lambda ref: ref[pl.ds(r0, C), sl]
        r_p, k_p, v_p, a_p, cum, ldec = ld(r_s), ld(k_s), ld(v_s), ld(a_s), ld(cum_s), ld(ld_s)
        einc = jnp.exp(cum)
        eex = jnp.exp(cum - ldec)
        einv = jnp.exp(-cum)
        etail = jnp.exp(cum[C - 1:C, :] - cum)
        kks = stack(ld(kkraw_s))
        kks = kks * jnp.minimum(lax.rsqrt(jnp.sum(kks * kks, axis=-1, keepdims=True)), 1e12)
        As = -kks * dup(eex)
        Bs = kks * dup(a_p * einv)
        Bh = kks * dup(a_p * etail)
        Ks = stack(k_p * einv)
        Kh = stack(k_p * etail)
        Rs = stack(r_p * einc)
        Vs = stack(v_p)
        AR = jnp.concatenate([As, Rs], axis=0).astype(BF16)
        Vb = Vs.astype(BF16)
        G = _mm_nt(AR, jnp.concatenate([Bs, Ks], axis=0))
        yield
        Aab = jnp.where(strict, G[:2 * C, :2 * C], 0.0)
        Aak = jnp.where(strict, G[:2 * C, 2 * C:], 0.0)
        Arb = jnp.where(incl, G[2 * C:, :2 * C], 0.0)
        Ark = jnp.where(incl, G[2 * C:, 2 * C:], 0.0)
        Y = _mm(Aak, Vb)
        yield
        T = yield from _nilpotent_inverse(Aab, eye)
        WU = _mm(T, jnp.concatenate([AR[:2 * C], Y.astype(BF16)], axis=1))
        yield
        bonus = jnp.sum(stack(r_p * k_p * rk_ref[:, sl]), axis=-1, keepdims=True) * Vs
        stash[p] = dict(WU=WU, R=AR[2 * C:], Vs=Vs, bonus=bonus,
                        Aro=jnp.concatenate([Arb, Ark], axis=1).astype(BF16),
                        BKh=jnp.concatenate([Bh, Kh], axis=0).astype(BF16), decay=einc[C - 1:C, :])

    def state_chain(p, r0, stash):
        sl = slice(p * LANES, (p + 1) * LANES)
        s = stash[p]
        S = state_ref[p]
        Sb = S.astype(BF16)
        W = _mm_nt(s["WU"][:, :LANES], Sb) + s["WU"][:, LANES:]
        yield
        WV = jnp.concatenate([W, s["Vs"]], axis=0)
        O = _dot_nt(s["R"], Sb) + _mm(s["Aro"], WV)
        state_ref[p] = S * s["decay"] + _mm(WV.T, s["BKh"])
        yield
        mean = jnp.sum(O, axis=-1, keepdims=True) * (1.0 / A_HEAD)
        cen = jnp.where(mask, O - mean, 0.0)
        var = jnp.sum(cen * cen, axis=-1, keepdims=True) * (1.0 / A_HEAD)
        normed = jnp.where(mask, cen * lax.rsqrt(var + A_LNX_EPS) * lnw_ref[:, sl] + lnb_ref[:, sl], 0.0)
        full = normed + s["bonus"]
        o_ref[pl.ds(r0, C), sl] = (full[:C] + full[C:]) * g_s[pl.ds(r0, C), sl]

    return (lambda r0: [token_chain(r0)],
            lambda r0, stash: [solve_chain(p, r0, stash) for p in range(A_PAIRS)],
            lambda r0, stash: [state_chain(p, r0, stash) for p in range(A_PAIRS)])


CHUNKS_IN_FLIGHT = 2
N_RWKV_PARAMS = 10
N_GDN_PARAMS = 4
N_RWKV_SCRATCH = 8
N_GDN_SCRATCH = 5


def _mix_prompt_body(pa_ref, qkv_ref, ab_ref, z_ref, *refs):
    refs = list(refs)
    take = lambda n: [refs.pop(0) for _ in range(n)]
    rwkv_prm, gdn_prm = take(N_RWKV_PARAMS), take(N_GDN_PARAMS)
    oa_ref, ob_ref, sfa_ref, sfb_ref = take(4)
    carry_a, state_a, carry_b, state_b = take(4)
    rwkv_scr, gdn_scr = take(N_RWKV_SCRATCH), take(N_GDN_SCRATCH)
    t = pl.program_id(1)
    tt = pa_ref.shape[0]

    @pl.when(t == 0)
    def _():
        for ref in (carry_a, state_a, carry_b, state_b):
            ref[...] = jnp.zeros_like(ref)

    rwkv_token, rwkv_solve, rwkv_state = _rwkv_prompt_part(pa_ref, *rwkv_prm, oa_ref, carry_a, state_a, *rwkv_scr)
    gdn_token, gdn_solve, gdn_state = _gdn_prompt_part(qkv_ref, ab_ref, z_ref, *gdn_prm, ob_ref, carry_b, state_b,
                                                       *gdn_scr)

    n_chunks = tt // CHUNK
    group = min(CHUNKS_IN_FLIGHT, n_chunks)
    n_groups = n_chunks // group
    stashes = [({}, {}) for _ in range(n_chunks)]
    chunks_of = lambda gi: range(gi * group, (gi + 1) * group) if 0 <= gi < n_groups else ()

    def in_sequence(per_chunk_chains):
        for chains in zip(*per_chunk_chains):
            for chain in chains:
                yield from chain

    for gi in range(n_groups + 2):
        chains = []
        for c in chunks_of(gi):
            chains += rwkv_token(c * CHUNK) + gdn_token(c * CHUNK)
        for c in chunks_of(gi - 1):
            chains += rwkv_solve(c * CHUNK, stashes[c][0]) + gdn_solve(c * CHUNK, stashes[c][1])
        state_chains = [rwkv_state(c * CHUNK, stashes[c][0]) + gdn_state(c * CHUNK, stashes[c][1])
                        for c in chunks_of(gi - 2)]
        if state_chains:
            chains += [in_sequence([per_chunk[i:i + 1] for per_chunk in state_chains])
                       for i in range(len(state_chains[0]))]
        _round_robin(chains)

    @pl.when(t == pl.num_programs(1) - 1)
    def _():
        sfa_ref[0] = state_a[...]
        sfb_ref[0] = state_b[...]


def _mix_prompt(pa, qkv, ab, z, B, T, rwkv_params, gdn_params, tt):
    n = B * T
    nt = T // tt
    assert len(rwkv_params) == N_RWKV_PARAMS and len(gdn_params) == N_GDN_PARAMS
    rows = lambda w: pl.BlockSpec((tt, w), lambda b, t: (b * nt + t, 0))
    state = lambda: pl.BlockSpec((1, 4, LANES, LANES), lambda b, t: (b, 0, 0, 0))
    big = lambda: pltpu.VMEM((tt, A_WIDTH), F32)
    return pl.pallas_call(
        _mix_prompt_body,
        grid=(B, nt),
        in_specs=[rows(A_PROJ), rows(CONV_CH), rows(LANES), rows(B_WIDTH)]
                 + [_const_spec(p.shape) for p in rwkv_params + gdn_params],
        out_specs=[rows(A_WIDTH), rows(B_WIDTH), state(), state()],
        out_shape=[jax.ShapeDtypeStruct((n, A_WIDTH), F32), jax.ShapeDtypeStruct((n, B_WIDTH), F32),
                   jax.ShapeDtypeStruct((B, A_PAIRS, LANES, LANES), F32),
                   jax.ShapeDtypeStruct((B, B_HEADS, B_HEAD, B_HEAD), F32)],
        scratch_shapes=[pltpu.VMEM((SUBLANES, A_PROJ), F32), pltpu.VMEM((A_PAIRS, LANES, LANES), F32),
                        pltpu.VMEM((SUBLANES, CONV_CH), F32), pltpu.VMEM((B_HEADS, B_HEAD, B_HEAD), F32)]
                       + [big() for _ in range(N_RWKV_SCRATCH + N_GDN_SCRATCH)],
        compiler_params=_cparams(2),
        name="mix_prompt",
    )(pa, qkv, ab, z, *rwkv_params, *gdn_params)


def _first_step_rows(rows, seqs, steps, state_rows, offset=0):
    hist = state_rows.shape[0] // seqs
    r, c = _iota((rows, seqs * hist), 0), _iota((rows, seqs * hist), 1)
    t = r & (steps - 1)
    sel = _one_hot((c == _group(r, steps) * hist + offset + t) & (t < hist - offset))
    return sum(_dot(sel, piece) for piece in _split3(state_rows))


def _last_step_rows(x, seqs, steps, keep):
    r, c = _iota((seqs * keep, seqs * steps), 0), _iota((seqs * keep, seqs * steps), 1)
    i = (c & (steps - 1)) - (steps - keep)
    sel = _one_hot((i >= 0) & (r == _group(c, steps) * keep + i))
    return sum(_dot(sel, piece) for piece in _split3(x))


ROWP = dict(mu_r=0, mu_k=1, mu_v=2, w0=3, a0=4, k_k=5, k_a=6, r_k=7, lnw=8, lnb=9)
ROWP_ROWS = 16
VALUE_GROUP = SUBLANES


def _rwkv_lanes_body(par_ref, pak_ref, pav_ref, paw_ref, pag_ref, shr_ref, shk_ref, shv_ref, shw_ref, shg_ref,
                     pa_ref, s_ref, rowp_ref, shared_ref, w2a_ref, g2_ref,
                     o_ref, sout_ref, shift_out_ref,
                     tr_s, ot_s, *, steps):
    rows = par_ref.shape[0]
    B = rows // steps
    p = pl.program_id(0)
    rp = lambda name: rowp_ref[ROWP[name]:ROWP[name] + 1, :]

    @pl.when(p == 0)
    def _():
        shift_out_ref[...] = _last_step_rows(pa_ref[...], B, steps, 1)

    def lerp(x_ref, first_ref, mu):
        per_step = [x_ref[pl.ds(t, B, stride=steps), :] for t in range(steps)]
        x = jnp.concatenate(per_step, axis=0)
        prev = jnp.concatenate([first_ref[...]] + per_step[:-1], axis=0)
        return x + (prev - x) * mu

    r = lerp(par_ref, shr_ref, rp("mu_r"))
    k = lerp(pak_ref, shk_ref, rp("mu_k"))
    v = lerp(pav_ref, shv_ref, rp("mu_v"))
    wa = lerp(paw_ref, shw_ref, shared_ref[0:1, :])
    gd = lerp(pag_ref, shg_ref, shared_ref[1:2, :])
    lane = _iota((1, LANES), 1)
    lora = _mm(jnp.where(lane < A_RANK_W, jnp.tanh(wa), wa), w2a_ref[...])
    g = _mm(_sigmoid(gd), g2_ref[...])
    w_log = -_softplus(-(rp("w0") + lora[:, :LANES])) - 0.5
    decay = jnp.exp(-jnp.exp(w_log))
    a = _sigmoid(rp("a0") + lora[:, LANES:])
    k_mod = k * (1.0 + (a - 1.0) * rp("k_a"))
    hi, hj = _iota((LANES, LANES), 0), _iota((LANES, LANES), 1)
    pair_ones = _one_hot(_group(hi, A_HEAD) == _group(hj, A_HEAD))
    head_sum = lambda m: _mm_sel(m, pair_ones)
    kk_raw = k * rp("k_k")
    kk = kk_raw * jnp.minimum(lax.rsqrt(head_sum(kk_raw * kk_raw)), 1e12)
    names = ("nkk", "beta", "decay", "k", "r", "v")
    for idx, m in enumerate((-kk, kk * a, decay, k_mod, r, v)):
        for t in range(steps):
            tr_s[idx, t] = m[t * B:(t + 1) * B, :].T
    at = lambda name, t: tr_s.at[names.index(name), t]

    def group(gi, carry):
        j = gi // (A_HEAD // VALUE_GROUP)
        v0 = (gi % (A_HEAD // VALUE_GROUP)) * VALUE_GROUP
        keys = lambda name, t: at(name, t)[pl.ds(pl.multiple_of(j * A_HEAD, A_HEAD), A_HEAD), :]
        v_rows = [at("v", t)[pl.ds(pl.multiple_of(gi * VALUE_GROUP, VALUE_GROUP), VALUE_GROUP), :]
                  for t in range(steps)]
        outs = [[] for _ in range(steps)]
        for i in range(VALUE_GROUP):
            S = s_ref[j, v0 + i]
            for t in range(steps):
                sa = jnp.sum(S * keys("nkk", t), axis=0, keepdims=True)
                S = S * keys("decay", t) + sa * keys("beta", t) + v_rows[t][i:i + 1, :] * keys("k", t)
                outs[t].append(jnp.sum(S * keys("r", t), axis=0, keepdims=True))
            sout_ref[j, v0 + i] = S
        for t in range(steps):
            ot_s[t, pl.ds(pl.multiple_of(gi * VALUE_GROUP, VALUE_GROUP), VALUE_GROUP), :] = _rows_to_tile(outs[t])
        return carry

    lax.fori_loop(0, 2 * A_HEAD // VALUE_GROUP, group, 0)
    o = jnp.concatenate([ot_s[t].T for t in range(steps)], axis=0)
    mean = head_sum(o) * (1.0 / A_HEAD)
    cen = o - mean
    var = head_sum(cen * cen) * (1.0 / A_HEAD)
    o = cen * lax.rsqrt(var + A_LNX_EPS) * rp("lnw") + rp("lnb")
    o = (o + head_sum(r * k_mod * rp("r_k")) * v) * g
    for t in range(steps):
        o_ref[pl.ds(t, B, stride=steps), :] = o[t * B:(t + 1) * B, :]


def _rwkv_lanes(pa, first_row, shift, state_t, layer, steps, tables):
    rowp, shared, w2a_p, g2_p = tables
    B = state_t.shape[-1]
    n = B * steps
    assert first_row % n == 0 and B == LANES and steps & (steps - 1) == 0
    rb = first_row // n
    col = lambda rows_, j, r: pl.BlockSpec((rows_, LANES), lambda p: (r, j(p)))
    groups = [lambda p: p, lambda p: A_PAIRS + p, lambda p: 2 * A_PAIRS + p,
              lambda p: 3 * A_PAIRS, lambda p: 3 * A_PAIRS + 1]
    block = (None, 2, A_HEAD, A_HEAD, B)
    return pl.pallas_call(
        functools.partial(_rwkv_lanes_body, steps=steps),
        grid=(A_PAIRS,),
        in_specs=[col(n, j, rb) for j in groups] + [col(B, j, 0) for j in groups]
                 + [pl.BlockSpec((n, A_PROJ), lambda p: (rb, 0), pipeline_mode=pl.Buffered(1)),
                    pl.BlockSpec(block, lambda p: (layer, p, 0, 0, 0)),
                    pl.BlockSpec((None, ROWP_ROWS, LANES), lambda p: (p, 0, 0)), _const_spec(shared.shape),
                    pl.BlockSpec((None, LANES, 2 * LANES), lambda p: (p, 0, 0)),
                    pl.BlockSpec((None, LANES, LANES), lambda p: (p, 0, 0))],
        out_specs=[pl.BlockSpec((n, LANES), lambda p: (0, p)), pl.BlockSpec(block, lambda p: (0, p, 0, 0, 0)),
                   pl.BlockSpec((B, A_PROJ), lambda p: (0, 0))],
        out_shape=[jax.ShapeDtypeStruct((n, A_WIDTH), F32), jax.ShapeDtypeStruct((1,) + state_t.shape[1:], F32),
                   jax.ShapeDtypeStruct((B, A_PROJ), F32)],
        scratch_shapes=[pltpu.VMEM((6, steps, LANES, B), F32), pltpu.VMEM((steps, LANES, B), F32)],
        compiler_params=_cparams(1),
        name="rwkv_sample",
    )(*([pa] * 5), *([shift] * 5), pa, state_t, rowp, shared, w2a_p, g2_p)


def _rwkv_pair_tables(mu, w0, a0, k_k, k_a, r_k, lnw, lnb, w2, a2, g2):
    per_pair = lambda a: a.reshape(A_PAIRS, 1, LANES)
    rows = {"mu_r": mu[:A_WIDTH], "mu_k": mu[A_WIDTH:2 * A_WIDTH], "mu_v": mu[2 * A_WIDTH:3 * A_WIDTH],
            "w0": w0, "a0": a0, "k_k": k_k, "k_a": k_a, "r_k": r_k.reshape(-1), "lnw": lnw, "lnb": lnb}
    table = jnp.concatenate([per_pair(rows[name].astype(F32)) for name in sorted(ROWP, key=ROWP.get)]
                            + [jnp.zeros((A_PAIRS, ROWP_ROWS - len(ROWP), LANES), F32)], axis=1)
    shared = jnp.concatenate([mu[3 * A_WIDTH:3 * A_WIDTH + LANES].reshape(1, LANES),
                              mu[3 * A_WIDTH + LANES:].reshape(1, LANES),
                              jnp.zeros((SUBLANES - 2, LANES), F32)], axis=0).astype(F32)
    by_pair = lambda w: jnp.transpose(w.astype(F32).reshape(w.shape[0], A_PAIRS, LANES), (1, 0, 2))
    zeros = jnp.zeros((A_PAIRS, A_RANK_W, LANES), F32)
    w2a_p = jnp.concatenate([jnp.concatenate([by_pair(w2), zeros], axis=2),
                             jnp.concatenate([zeros, by_pair(a2)], axis=2)], axis=1)
    return table, shared, w2a_p, by_pair(g2)


def _gdn_qkv(conv):
    c = _silu(conv)
    qs, ks = [], []
    for h in range(B_HEADS):
        q = c[:, h * B_HEAD:(h + 1) * B_HEAD]
        k = c[:, B_WIDTH + h * B_HEAD:B_WIDTH + (h + 1) * B_HEAD]
        qs.append(q * (lax.rsqrt(jnp.sum(q * q, axis=-1, keepdims=True) + 1e-6) * (B_HEAD ** -0.5)))
        ks.append(k * lax.rsqrt(jnp.sum(k * k, axis=-1, keepdims=True) + 1e-6))
    q = jnp.concatenate(qs, axis=1)
    k = jnp.concatenate(ks, axis=1)
    v = c[:, 2 * B_WIDTH:]
    return q, k, v


def _gdn_gates(ab, alog, dtb):
    lane = _iota((1, LANES), 1)
    g = -jnp.exp(alog) * _softplus(ab + dtb)
    beta = _sigmoid(ab)
    gb = jnp.where(lane < B_HEADS, g, beta)
    si, sj = _iota((LANES, 2 * B_WIDTH), 0), _iota((LANES, 2 * B_WIDTH), 1)
    spread = _mm_sel(gb, _one_hot(si == _group(sj, B_HEAD)))
    return spread[:, :B_WIDTH], spread[:, B_WIDTH:]


def _gdn_out(o, norm_w, z):
    return o * lax.rsqrt(jnp.mean(o * o, axis=-1, keepdims=True) + RMS_EPS) * norm_w * _silu(z)


def _gdn_prompt_part(qkv_ref, ab_ref, z_ref, cw_ref, alog_ref, dtb_ref, nw_ref, o_ref,
                     carry_ref, state_ref, q_s, k_s, v_s, gc_s, beta_s):
    tt = qkv_ref.shape[0]
    C = CHUNK
    g, beta = _gdn_gates(ab_ref[...], alog_ref[...], dtb_ref[...])
    ri, ci = _iota((tt, tt), 0), _iota((tt, tt), 1)
    beta_s[...] = beta
    gc_s[...] = _sel_mm(_one_hot((_group(ri, C) == _group(ci, C)) & (ci <= ri)), g)

    def token_chain(r0):
        rows = slice(r0, r0 + C)
        x = qkv_ref[rows, :]
        before = carry_ref[...] if r0 == 0 else qkv_ref[r0 - SUBLANES:r0, :]
        if r0 + C == tt:
            carry_ref[...] = x[C - SUBLANES:, :]
        row8 = _iota((SUBLANES, 1), 0)

        def shift_rows(cur, halo, i):
            down = pltpu.roll(cur, i, axis=0)
            top = jnp.where(row8 < i, pltpu.roll(halo, i, axis=0), down[:SUBLANES])
            return jnp.concatenate([top, down[SUBLANES:]], axis=0)

        assert CONV_K == 4
        c0, c1, c2, c3 = (cw_ref[i:i + 1, :] for i in range(CONV_K))
        x1 = shift_rows(x, before, 1)
        yield
        older = x * c1 + x1 * c0
        older_halo = before * c1 + pltpu.roll(before, 1, axis=0) * c0
        conv = x * c3 + x1 * c2 + shift_rows(older, older_halo, 2)
        yield
        q, k, v = _gdn_qkv(conv)
        q_s[rows, :] = q
        k_s[rows, :] = k
        v_s[rows, :] = v
        yield

    i2, j2 = _iota((2 * C, 2 * C), 0), _iota((2 * C, 2 * C), 1)
    same_head = _group(i2, C) == _group(j2, C)
    strict = same_head & (i2 > j2)
    incl = same_head & (i2 >= j2)
    eye = jnp.where(i2 == j2, 1.0, 0.0)
    first = _iota((2 * C, 1), 0) < C

    def solve_chain(pr, r0, stash):
        sls = [slice(h * B_HEAD, (h + 1) * B_HEAD) for h in (2 * pr, 2 * pr + 1)]
        ld = lambda ref: jnp.concatenate([ref[pl.ds(r0, C), sl] for sl in sls], axis=0)
        q_h, k_h, v_h, gc_h, beta_h = ld(q_s), ld(k_s), ld(v_s), ld(gc_s), ld(beta_s)
        diff = gc_h - gc_h.T
        dm = jnp.where(incl, jnp.exp(jnp.where(incl, diff, 0.0)), 0.0)
        kb = k_h * beta_h
        QK = _mm_nt(jnp.concatenate([kb, q_h], axis=0), k_h)
        yield
        N = -jnp.where(strict, QK[:2 * C] * dm, 0.0)
        qk = QK[2 * C:] * dm
        egc = jnp.exp(gc_h)
        X = jnp.concatenate([v_h * beta_h, kb * egc], axis=1)
        T = yield from _nilpotent_inverse(N, eye)
        UW = _mm(T, X)
        yield
        g_last = jnp.where(first, gc_h[C - 1:C, :], gc_h[2 * C - 1:2 * C, :])
        stash[pr] = dict(u=UW[:, :B_HEAD], w=UW[:, B_HEAD:].astype(BF16), qd=(q_h * egc).astype(BF16),
                         qk=qk.astype(BF16), k_dec_t=(k_h * jnp.exp(g_last - gc_h)).T.astype(BF16),
                         decay=[jnp.exp(gc_h[(j + 1) * C - 1:(j + 1) * C, :]) for j in range(2)])

    def state_chain(pr, r0, stash):
        heads = (2 * pr, 2 * pr + 1)
        sls = [slice(h * B_HEAD, (h + 1) * B_HEAD) for h in heads]
        s = stash[pr]
        wS, qS, S_old = [], [], []
        for j, h in enumerate(heads):
            S = state_ref[h]
            rows = slice(j * C, (j + 1) * C)
            wq = _dot(jnp.concatenate([s["w"][rows], s["qd"][rows]], axis=0), S.astype(BF16))
            wS.append(wq[:C])
            qS.append(wq[C:])
            S_old.append(S)
        yield
        v_new = s["u"] - jnp.concatenate(wS, axis=0)
        o = jnp.concatenate(qS, axis=0) + _dot(s["qk"], v_new.astype(BF16))
        for j, h in enumerate(heads):
            mine = first if j == 0 else jnp.logical_not(first)
            rows = slice(j * C, (j + 1) * C)
            state_ref[h] = S_old[j] * s["decay"][j] + _mm(s["k_dec_t"], jnp.where(mine, v_new, 0.0))
            o_ref[pl.ds(r0, C), sls[j]] = _gdn_out(o[rows], nw_ref[...], z_ref[pl.ds(r0, C), sls[j]])
        yield

    return (lambda r0: [token_chain(r0)],
            lambda r0, stash: [solve_chain(pr, r0, stash) for pr in range(B_HEADS // 2)],
            lambda r0, stash: [state_chain(pr, r0, stash) for pr in range(B_HEADS // 2)])


def _gdn_chunk_body(qkv_ref, hist_ref, ab_ref, z_ref, s_ref, cw_ref, alog_ref, dtb_ref, nw_ref,
                    o_ref, sout_ref, hist_out_ref, *, steps):
    rows = ab_ref.shape[0]
    seqs = rows // steps
    x = qkv_ref[...]
    hist_out_ref[...] = _last_step_rows(x, seqs, steps, CONV_K - 1)
    hist = hist_ref[...]
    t_idx = _iota((rows, 1), 0) & (steps - 1)
    conv = x * cw_ref[CONV_K - 1:CONV_K, :]
    for i in range(1, CONV_K):
        tap = jnp.where(t_idx >= i, pltpu.roll(x, i, axis=0),
                        _first_step_rows(rows, seqs, steps, hist, offset=CONV_K - 1 - i))
        conv = conv + tap * cw_ref[CONV_K - 1 - i:CONV_K - i, :]
    q, k, v = _gdn_qkv(conv)
    g, beta = _gdn_gates(ab_ref[...], alog_ref[...], dtb_ref[...])
    i2, j2 = _iota((rows, rows), 0), _iota((rows, rows), 1)
    same = _group(i2, steps) == _group(j2, steps)
    strict = same & (i2 > j2)
    incl = same & (i2 >= j2)
    eye = jnp.where(i2 == j2, 1.0, 0.0)
    gc = _sel_mm(_one_hot(incl), g)
    g_end = _sel_mm(_one_hot(j2 == (i2 | (steps - 1))), gc)
    row = _iota((rows, 1), 0)
    pair_rows = 2 * SUBLANES
    first_half = (_iota((pair_rows, 1), 0) & (SUBLANES - 1)) < steps
    assert 2 * steps == SUBLANES

    def head_chain(h):
        sl = slice(h * B_HEAD, (h + 1) * B_HEAD)
        q_h, k_h, v_h, gc_h, beta_h, ge_h = q[:, sl], k[:, sl], v[:, sl], gc[:, sl], beta[:, sl], g_end[:, sl]
        dm = jnp.where(incl, jnp.exp(jnp.where(incl, gc_h - gc_h.T, 0.0)), 0.0)
        kb = k_h * beta_h
        QK = _mm_nt(jnp.concatenate([kb, q_h], axis=0), k_h)
        yield
        N = -jnp.where(strict, QK[:rows] * dm, 0.0)
        qk = QK[rows:] * dm
        assert steps == 4
        N2 = _mm(N, N)
        yield
        T = eye + N
        T = T + _mm(T, N2)
        yield
        egc = jnp.exp(gc_h)
        UW = _mm(T, jnp.concatenate([v_h * beta_h, kb * egc], axis=1))
        yield
        w = UW[:, B_HEAD:].astype(BF16)
        qd = (q_h * egc).astype(BF16)
        k_dec_t = (k_h * jnp.exp(ge_h - gc_h)).T.astype(BF16)
        decay = jnp.exp(ge_h)
        wS, qS = [], []
        for m in range(rows // SUBLANES):
            tile = slice(m * SUBLANES, (m + 1) * SUBLANES)
            lhs = jnp.concatenate([w[tile], qd[tile]], axis=0)
            res = jnp.where(first_half, _dot(lhs, s_ref[2 * m, h].astype(BF16)),
                            _dot(lhs, s_ref[2 * m + 1, h].astype(BF16)))
            wS.append(res[:SUBLANES])
            qS.append(res[SUBLANES:])
            yield
        v_new = UW[:, :B_HEAD] - jnp.concatenate(wS, axis=0)
        o = jnp.concatenate(qS, axis=0) + _mm(qk, v_new)
        o_ref[:, sl] = _gdn_out(o, nw_ref[...], z_ref[:, sl])
        yield
        for b in range(seqs):
            mine = _group(row, steps) == b
            sout_ref[b, h] = (s_ref[b, h] * decay[b * steps:b * steps + 1, :]
                              + _mm(k_dec_t, jnp.where(mine, v_new, 0.0)))
            yield

    _round_robin(head_chain(h) for h in range(B_HEADS))


def _gdn_sample(qkv, ab, z, first_row, hist, state, layer, steps, params, seqs):
    nb = state.shape[1]
    n = nb * steps
    rows = seqs * steps
    hist_rows = seqs * (CONV_K - 1)
    assert SUBLANES % steps == 0 and rows % SUBLANES == 0 and hist_rows % SUBLANES == 0 and first_row % rows == 0
    row = lambda w: pl.BlockSpec((rows, w), lambda i: (i + first_row // rows, 0))
    block = (None, seqs, B_HEADS, B_HEAD, B_HEAD)
    sspec = pl.BlockSpec(block, lambda i: (layer, i, 0, 0, 0))
    ospec = pl.BlockSpec(block, lambda i: (0, i, 0, 0, 0))
    return pl.pallas_call(
        functools.partial(_gdn_chunk_body, steps=steps),
        grid=(nb // seqs,),
        in_specs=[row(CONV_CH), pl.BlockSpec((hist_rows, CONV_CH), lambda i: (i, 0)), row(LANES), row(B_WIDTH), sspec]
                 + [_const_spec(p.shape) for p in params],
        out_specs=[pl.BlockSpec((rows, B_WIDTH), lambda i: (i, 0)), ospec,
                   pl.BlockSpec((hist_rows, CONV_CH), lambda i: (i, 0))],
        out_shape=[jax.ShapeDtypeStruct((n, B_WIDTH), F32), jax.ShapeDtypeStruct((1,) + state.shape[1:], F32),
                   jax.ShapeDtypeStruct(hist.shape, F32)],
        compiler_params=_cparams(1),
        name="gdn_sample",
    )(qkv, hist, ab, z, state, *params)


def _cuts(widths):
    edges, total = [], 0
    for w in widths[:-1]:
        total += w
        edges.append(total)
    return edges


def _regroup_pa(a):
    r, wd, k, v, ad, gd = jnp.split(a, _cuts((A_WIDTH, A_RANK_W, A_WIDTH, A_WIDTH, A_RANK_A, A_RANK_G)), axis=-1)
    return jnp.concatenate([r, k, v, wd, ad, gd], axis=-1)


def _ungroup_pa(a):
    r, k, v, wd, ad, gd = jnp.split(a, _cuts((A_WIDTH, A_WIDTH, A_WIDTH, A_RANK_W, A_RANK_A, A_RANK_G)), axis=-1)
    return jnp.concatenate([r, wd, k, v, ad, gd], axis=-1)


def _token_tile(n, want):
    tm = want
    while n % tm:
        tm //= 2
    return tm


def kernel(x_prompt, x_sample, state_rwkv, state_rwkv_shift, state_delta, state_conv, ffn1_norm, ffn1_w_gate, ffn1_w_up, ffn1_w_down, mix_norm, w_in, rwkv_mu, rwkv_w0, rwkv_w2, rwkv_a0, rwkv_a2, rwkv_g2, rwkv_k_k, rwkv_k_a, rwkv_r_k, rwkv_lnx_w, rwkv_lnx_b, gdn_conv_w, gdn_A_log, gdn_dt_bias, gdn_norm_w, proj_a, proj_b, w_out, ffn2_norm, ffn2_w_gate, ffn2_w_up, ffn2_w_down, final_norm):
    depth = ffn1_norm.shape[0]
    assert depth == 1, "single-layer trunk"
    Bp, Tp, _ = x_prompt.shape
    Bs, Ts, _ = x_sample.shape
    l = 0
    row = lambda a: a.reshape(1, -1).astype(F32)

    wi = w_in[l].astype(BF16)
    o_b = A_PROJ
    proj_w = (_regroup_pa(wi[:, :A_PROJ]),
              wi[:, o_b:o_b + CONV_CH + LANES],
              wi[:, o_b + CONV_CH + 2 * B_HEADS:])
    ffn1 = (row(ffn1_norm[l]), ffn1_w_gate[l].astype(BF16), ffn1_w_up[l].astype(BF16), ffn1_w_down[l].astype(BF16))
    ffn2 = (row(ffn2_norm[l]), ffn2_w_gate[l].astype(BF16), ffn2_w_up[l].astype(BF16), ffn2_w_down[l].astype(BF16))
    merge_w = (proj_a[l].astype(BF16), proj_b[l].astype(BF16), w_out[l].astype(BF16))
    zw = jnp.zeros((A_RANK_W, A_WIDTH), F32)
    w2a = jnp.concatenate([jnp.concatenate([rwkv_w2[l], zw], axis=1),
                           jnp.concatenate([zw, rwkv_a2[l]], axis=1)], axis=0)
    rwkv_params = (row(_regroup_pa(rwkv_mu[l])), row(rwkv_w0[l]), row(rwkv_a0[l]), row(rwkv_k_k[l]), row(rwkv_k_a[l]),
                   row(rwkv_r_k[l]), row(rwkv_lnx_w[l]), row(rwkv_lnx_b[l]), w2a, rwkv_g2[l].astype(F32))
    pad_lane = lambda a: jnp.pad(a.reshape(1, -1).astype(F32), ((0, 0), (0, LANES - a.size)))
    gdn_params = (gdn_conv_w[l].astype(F32), pad_lane(gdn_A_log[l]), pad_lane(gdn_dt_bias[l]), row(gdn_norm_w[l]))

    n_p, n_s = Bp * Tp, Bs * Ts
    tm = _token_tile(n_s, _token_tile(n_p, 512))
    h = _ffn(x_prompt.reshape(n_p, D_MODEL), x_sample.reshape(n_s, D_MODEL), *ffn1, tm=tm)
    pa, qkv, z, gates, ab = _proj(h, row(mix_norm[l]), *proj_w, tm=tm)

    def trunk_back(oa, ob, first_row):
        return _tail(h, oa, ob, gates, *merge_w, *ffn2, row(final_norm), tm=tm, first_row=first_row)

    tt = _token_tile(Tp, 256)
    oa, ob, s_pairs, delta_p = _mix_prompt(pa, qkv, ab, z, Bp, Tp, rwkv_params, gdn_params, tt)
    y_prompt = trunk_back(oa, ob, 0).reshape(Bp, Tp, D_MODEL)
    sp = s_pairs.reshape(Bp, A_PAIRS, 2, A_HEAD, 2, A_HEAD)
    rwkv_p = jnp.stack([sp[:, :, 0, :, 0], sp[:, :, 1, :, 1]], axis=2).reshape(Bp, A_HEADS, A_HEAD, A_HEAD)
    shift_p = _ungroup_pa(pa[Tp - 1:n_p:Tp])
    conv_p = jnp.stack([qkv[Tp - (CONV_K - 1) + i:n_p:Tp] for i in range(CONV_K - 1)], axis=1)

    assert Ts & (Ts - 1) == 0 and Ts >= CONV_K - 1, "sample steps: power of two covering the conv history"
    tables = _rwkv_pair_tables(_regroup_pa(rwkv_mu[l]), rwkv_w0[l], rwkv_a0[l], rwkv_k_k[l], rwkv_k_a[l], rwkv_r_k[l],
                               rwkv_lnx_w[l], rwkv_lnx_b[l], rwkv_w2[l], rwkv_a2[l], rwkv_g2[l])
    oa, s_lanes, last_pa = _rwkv_lanes(pa, n_p, _regroup_pa(state_rwkv_shift[l].astype(F32)),
                                       jnp.transpose(state_rwkv.astype(F32), (0, 2, 3, 4, 1)), l, Ts, tables)
    rwkv_s = jnp.transpose(s_lanes, (0, 4, 1, 2, 3))
    shift_s = _ungroup_pa(last_pa)
    hist = state_conv[l].astype(F32).reshape(Bs * (CONV_K - 1), CONV_CH)
    ob, delta_s, new_hist = _gdn_sample(qkv, ab, z, n_p, hist, state_delta.astype(F32), l, Ts, gdn_params,
                                        seqs=_token_tile(Bs, 32))
    conv_s = new_hist.reshape(Bs, CONV_K - 1, CONV_CH)
    y_sample = trunk_back(oa, ob, n_p).reshape(Bs, Ts, D_MODEL)

    add_depth = lambda a: a[None]
    return (y_prompt, y_sample,
            add_depth(rwkv_p), add_depth(shift_p), add_depth(delta_p), add_depth(conv_p),
            rwkv_s, add_depth(shift_s), delta_s, add_depth(conv_s))
```

```python
import functools

import jax
import jax.numpy as jnp
from jax import lax
from jax.experimental import pallas as pl
from jax.experimental.pallas import tpu as pltpu

F32 = jnp.float32
BF16 = jnp.bfloat16

D_MODEL = 1024
D_FF = 2816
RMS_EPS = 1e-6
A_HEAD = 64
A_HEADS = 8
A_WIDTH = A_HEADS * A_HEAD
A_RANK_W = 64
A_RANK_A = 64
A_RANK_G = 128
A_PROJ = 3 * A_WIDTH + A_RANK_W + A_RANK_A + A_RANK_G
A_LNX_EPS = 64e-5
A_PAIRS = A_HEADS // 2
B_HEADS = 4
B_HEAD = 128
B_WIDTH = B_HEADS * B_HEAD
CONV_K = 4
CONV_CH = 3 * B_WIDTH
B_PROJ = CONV_CH + 2 * B_HEADS + B_WIDTH
GATE_COLS = 2 * D_MODEL
LANES = 128
SUBLANES = 8
MXU_DIM = 256
VMEM_LIMIT_BYTES = 56 * 1024 * 1024
CHUNK = 64
PA_R, PA_K, PA_V, PA_WA, PA_G = 0, A_WIDTH, 2 * A_WIDTH, 3 * A_WIDTH, 3 * A_WIDTH + A_RANK_W + A_RANK_A
PROJ_SPLITS = (A_PROJ, CONV_CH, B_WIDTH, GATE_COLS, LANES)


def _cparams(n_grid_dims):
    return pltpu.CompilerParams(dimension_semantics=("arbitrary",) * n_grid_dims,
                                vmem_limit_bytes=VMEM_LIMIT_BYTES)


def _const_spec(shape):
    nd = len(shape)
    return pl.BlockSpec(shape, lambda *_: (0,) * nd, pipeline_mode=pl.Buffered(1))


def _dot(a, b):
    return jnp.dot(a, b, preferred_element_type=F32)


def _dot_nt(a, b):
    return lax.dot_general(a, b, (((1,), (1,)), ((), ())), preferred_element_type=F32)


def _split3(x):
    hi = x.astype(BF16)
    rest = x - hi.astype(F32)
    mid = rest.astype(BF16)
    lo = (rest - mid.astype(F32)).astype(BF16)
    return hi, mid, lo


def _mm(a, b):
    return _dot(a.astype(BF16), b.astype(BF16))


def _mm_nt(a, b):
    return _dot_nt(a.astype(BF16), b.astype(BF16))


def _sel_mm(sel, x):
    return _dot(jnp.concatenate([sel, sel, sel], axis=1), jnp.concatenate(_split3(x), axis=0))


def _mm_sel(x, sel):
    return _dot(jnp.concatenate(_split3(x), axis=1), jnp.concatenate([sel, sel, sel], axis=0))


INV_BASE = 8


def _nilpotent_inverse(n, eye):
    width = n.shape[1]
    bi, bj = _iota(n.shape, 0), _iota(n.shape, 1)
    same = lambda size: _group(bi, size) == _group(bj, size)
    d = jnp.where(same(INV_BASE), n, 0.0)
    t = eye + d
    d = _mm(d, d)
    yield
    for _ in range(INV_BASE.bit_length() - 3):
        both = _mm(d, jnp.concatenate([t, d], axis=1))
        yield
        t = t + both[:, :width]
        d = both[:, width:]
    t = t + _mm(d, t)
    yield
    size = INV_BASE
    while size < CHUNK:
        coupling = jnp.where(same(2 * size) & jnp.logical_not(same(size)), n, 0.0)
        tb = t.astype(BF16)
        lt = _mm(coupling, tb)
        yield
        t = t + _mm(tb, lt)
        yield
        size *= 2
    return t


def _round_robin(chains):
    chains = list(chains)
    while chains:
        for chain in list(chains):
            try:
                next(chain)
            except StopIteration:
                chains.remove(chain)


def _rms(x, w):
    return x * lax.rsqrt(jnp.mean(x * x, axis=-1, keepdims=True) + RMS_EPS) * w


def _sigmoid(x):
    return 1.0 / (1.0 + jnp.exp(-x))


def _silu(x):
    return x * _sigmoid(x)


def _softplus(x):
    return jnp.maximum(x, 0.0) + jnp.log(1.0 + jnp.exp(-jnp.abs(x)))


def _iota(shape, dim):
    return lax.broadcasted_iota(jnp.int32, shape, dim)


def _group(idx, size):
    assert size & (size - 1) == 0
    return lax.shift_right_logical(idx, size.bit_length() - 1)


def _one_hot(cond):
    return jnp.where(cond, 1.0, 0.0).astype(BF16)


def _rows_to_tile(rows):
    rid = _iota((SUBLANES, 1), 0)
    tile = jnp.zeros((SUBLANES, rows[0].shape[1]), F32)
    for i, row in enumerate(rows):
        tile = jnp.where(rid == i, row, tile)
    return tile


def _swiglu_half_step(x, nw, wg_ref, wu_ref, wd_ref, slabs=(D_FF,)):
    xn = _rms(x, nw).astype(BF16)
    assert sum(slabs) == D_FF and all(w % MXU_DIM == 0 for w in slabs)
    y = None
    start = 0
    for width in slabs:
        cols = slice(start, start + width)
        start += width
        g = _dot(xn, wg_ref[:, cols])
        u = _dot(xn, wu_ref[:, cols])
        act = (_silu(g) * u).astype(BF16)
        part = _dot(act, wd_ref[cols, :])
        y = part if y is None else y + part
    return x + 0.5 * y


def _ffn_body(xa_ref, xb_ref, nw_ref, wg_ref, wu_ref, wd_ref, o_ref, *, blocks_a):
    x = jnp.where(pl.program_id(0) < blocks_a, xa_ref[...], xb_ref[...])
    o_ref[...] = _swiglu_half_step(x, nw_ref[...], wg_ref, wu_ref, wd_ref)


def _ffn(xa, xb, nw, wg, wu, wd, tm):
    na, nb = xa.shape[0], xb.shape[0]
    assert na % tm == 0 and nb % tm == 0
    blocks_a = na // tm
    return pl.pallas_call(
        functools.partial(_ffn_body, blocks_a=blocks_a),
        grid=((na + nb) // tm,),
        in_specs=[pl.BlockSpec((tm, D_MODEL), lambda i: (jnp.minimum(i, blocks_a - 1), 0)),
                  pl.BlockSpec((tm, D_MODEL), lambda i: (jnp.maximum(i - blocks_a, 0), 0)),
                  _const_spec((1, D_MODEL)),
                  _const_spec((D_MODEL, D_FF)), _const_spec((D_MODEL, D_FF)), _const_spec((D_FF, D_MODEL))],
        out_specs=pl.BlockSpec((tm, D_MODEL), lambda i: (i, 0)),
        out_shape=jax.ShapeDtypeStruct((na + nb, D_MODEL), F32),
        compiler_params=_cparams(1),
        name="ffn1",
    )(xa, xb, nw, wg, wu, wd)


def _proj_body(h_ref, nw_ref, wpa_ref, wqa_ref, wzg_ref, pa_ref, qkv_ref, z_ref, gates_ref, ab_ref):
    u = _rms(h_ref[...], nw_ref[...]).astype(BF16)
    pa_ref[...] = _dot(u, wpa_ref[...])
    qkv_ref[...] = _dot(u, wqa_ref[:, :CONV_CH])
    ab_ref[...] = _dot(u, wqa_ref[:, CONV_CH:])
    z_ref[...] = _dot(u, wzg_ref[:, :B_WIDTH])
    gates_ref[...] = _dot(u, wzg_ref[:, B_WIDTH:])


def _proj(h, nw, w_pa, w_qa, w_zg, tm):
    n = h.shape[0]
    assert w_qa.shape[1] == CONV_CH + LANES and w_zg.shape[1] == B_WIDTH + GATE_COLS
    return pl.pallas_call(
        _proj_body,
        grid=(n // tm,),
        in_specs=[pl.BlockSpec((tm, D_MODEL), lambda i: (i, 0)), _const_spec((1, D_MODEL)),
                  _const_spec(w_pa.shape), _const_spec(w_qa.shape), _const_spec(w_zg.shape)],
        out_specs=[pl.BlockSpec((tm, w), lambda i: (i, 0)) for w in PROJ_SPLITS],
        out_shape=[jax.ShapeDtypeStruct((n, w), F32) for w in PROJ_SPLITS],
        compiler_params=_cparams(1),
        name="proj",
    )(h, nw, w_pa, w_qa, w_zg)


TAIL_FF_SLABS = (6 * MXU_DIM, 5 * MXU_DIM)


def _tail_body(h_ref, oa_ref, ob_ref, gates_ref, pa_ref, pb_ref, wo_ref, nw_ref, wg_ref, wu_ref, wd_ref,
               fn_ref, o_ref):
    ma = _dot(oa_ref[...].astype(BF16), pa_ref[...])
    mb = _dot(ob_ref[...].astype(BF16), pb_ref[...])
    merged = _sigmoid(gates_ref[:, :D_MODEL]) * ma + _sigmoid(gates_ref[:, D_MODEL:]) * mb
    h = h_ref[...] + _dot(merged.astype(BF16), wo_ref[...])
    h = _swiglu_half_step(h, nw_ref[...], wg_ref, wu_ref, wd_ref, slabs=TAIL_FF_SLABS)
    o_ref[...] = _rms(h, fn_ref[...])


def _tail(h, oa, ob, gates, proj_a, proj_b, w_out, nw, wg, wu, wd, fn, tm, first_row):
    n = oa.shape[0]
    assert n % tm == 0 and first_row % tm == 0
    row = lambda w: pl.BlockSpec((tm, w), lambda i: (i, 0))
    stream = lambda w: pl.BlockSpec((tm, w), lambda i: (i + first_row // tm, 0))
    return pl.pallas_call(
        _tail_body,
        grid=(n // tm,),
        in_specs=[stream(D_MODEL), row(A_WIDTH), row(B_WIDTH), stream(GATE_COLS),
                  _const_spec((A_WIDTH, D_MODEL)), _const_spec((B_WIDTH, D_MODEL)),
                  _const_spec((D_MODEL, D_MODEL)), _const_spec((1, D_MODEL)),
                  _const_spec((D_MODEL, D_FF)), _const_spec((D_MODEL, D_FF)), _const_spec((D_FF, D_MODEL)),
                  _const_spec((1, D_MODEL))],
        out_specs=row(D_MODEL),
        out_shape=jax.ShapeDtypeStruct((n, D_MODEL), F32),
        compiler_params=_cparams(1),
        name="tail",
    )(h, oa, ob, gates, proj_a, proj_b, w_out, nw, wg, wu, wd, fn)


def _rwkv_token_math(x, prev, mu, w0, a0, k_k, k_a, w2a, g2):
    pm = x + (prev - x) * mu
    r = pm[:, PA_R:PA_R + A_WIDTH]
    k = pm[:, PA_K:PA_K + A_WIDTH]
    v = pm[:, PA_V:PA_V + A_WIDTH]
    wa = pm[:, PA_WA:PA_WA + LANES]
    gd = pm[:, PA_G:PA_G + A_RANK_G]
    lane = _iota((1, LANES), 1)
    lora_in = jnp.where(lane < A_RANK_W, jnp.tanh(wa), wa)
    lora = _mm(lora_in, w2a)
    g = _mm(_sigmoid(gd), g2)
    yield
    w_log = -_softplus(-(w0 + lora[:, :A_WIDTH])) - 0.5
    log_decay = -jnp.exp(w_log)
    yield
    a = _sigmoid(a0 + lora[:, A_WIDTH:])
    kk_raw = k * k_k
    k_mod = k * (1.0 + (a - 1.0) * k_a)
    return r, k_mod, v, kk_raw, a, log_decay, g


def _pair_mask(rows_per_head):
    shape = (2 * rows_per_head, LANES)
    return _group(_iota(shape, 0), rows_per_head) == _group(_iota(shape, 1), A_HEAD)


def _rwkv_prompt_part(pa_ref, mu_ref, w0_ref, a0_ref, kk_ref, ka_ref, rk_ref, lnw_ref, lnb_ref, w2a_ref, g2_ref,
                      o_ref, carry_ref, state_ref, r_s, k_s, v_s, kkraw_s, a_s, cum_s, ld_s, g_s):
    tt = pa_ref.shape[0]
    C = CHUNK
    lower = _one_hot(_iota((C, C), 1) <= _iota((C, C), 0))

    def token_chain(r0):
        rows = slice(r0, r0 + C)
        x = pa_ref[rows, :]
        before = carry_ref[SUBLANES - 1:SUBLANES, :] if r0 == 0 else pa_ref[r0 - 1:r0, :]
        prev = jnp.where(_iota((C, 1), 0) == 0, before, pltpu.roll(x, 1, axis=0))
        if r0 + C == tt:
            carry_ref[...] = x[C - SUBLANES:, :]
        r, k_mod, v, kk_raw, a, log_decay, g = yield from _rwkv_token_math(
            x, prev, mu_ref[...], w0_ref[...], a0_ref[...], kk_ref[...], ka_ref[...], w2a_ref[...], g2_ref[...])
        r_s[rows, :] = r
        k_s[rows, :] = k_mod
        v_s[rows, :] = v
        kkraw_s[rows, :] = kk_raw
        a_s[rows, :] = a
        g_s[rows, :] = g
        ld_s[rows, :] = log_decay
        yield
        cum_s[rows, :] = sum(_dot(lower, piece) for piece in _split3(log_decay))
        yield

    mask = _pair_mask(C)
    i2, j2 = _iota((2 * C, 2 * C), 0), _iota((2 * C, 2 * C), 1)
    strict = i2 > j2
    incl = i2 >= j2
    eye = jnp.where(i2 == j2, 1.0, 0.0)
    dup = lambda m: jnp.concatenate([m, m], axis=0)
    stack = lambda m: jnp.where(mask, dup(m), 0.0)

    def solve_chain(p, r0, stash):
        sl = slice(p * LANES, (p + 1) * LANES)
        ld = lambda ref: ref[pl.ds(r0, C), sl]
        r_p, k_p, v_p, a_p, cum, ldec = ld(r_s), ld(k_s), ld(v_s), ld(a_s), ld(cum_s), ld(ld_s)
        einc = jnp.exp(cum)
        eex = jnp.exp(cum - ldec)
        einv = jnp.exp(-cum)
        etail = jnp.exp(cum[C - 1:C, :] - cum)
        kks = stack(ld(kkraw_s))
        kks = kks * jnp.minimum(lax.rsqrt(jnp.sum(kks * kks, axis=-1, keepdims=True)), 1e12)
        As = -kks * dup(eex)
        Bs = kks * dup(a_p * einv)
        Bh = kks * dup(a_p * etail)
        Ks = stack(k_p * einv)
        Kh = stack(k_p * etail)
        Rs = stack(r_p * einc)
        Vs = stack(v_p)
        AR = jnp.concatenate([As, Rs], axis=0).astype(BF16)
        Vb = Vs.astype(BF16)
        G = _mm_nt(AR, jnp.concatenate([Bs, Ks], axis=0))
        yield
        Aab = jnp.where(strict, G[:2 * C, :2 * C], 0.0)
        Aak = jnp.where(strict, G[:2 * C, 2 * C:], 0.0)
        Arb = jnp.where(incl, G[2 * C:, :2 * C], 0.0)
        Ark = jnp.where(incl, G[2 * C:, 2 * C:], 0.0)
        Y = _mm(Aak, Vb)
        yield
        T = yield from _nilpotent_inverse(Aab, eye)
        WU = _mm(T, jnp.concatenate([AR[:2 * C], Y.astype(BF16)], axis=1))
        yield
        bonus = jnp.sum(stack(r_p * k_p * rk_ref[:, sl]), axis=-1, keepdims=True) * Vs
        stash[p] = dict(WU=WU, R=AR[2 * C:], Vs=Vs, bonus=bonus,
                        Aro=jnp.concatenate([Arb, Ark], axis=1).astype(BF16),
                        BKh=jnp.concatenate([Bh, Kh], axis=0).astype(BF16), decay=einc[C - 1:C, :])

    def state_chain(p, r0, stash):
        sl = slice(p * LANES, (p + 1) * LANES)
        s = stash[p]
        S = state_ref[p]
        Sb = S.astype(BF16)
        W = _mm_nt(s["WU"][:, :LANES], Sb) + s["WU"][:, LANES:]
        yield
        WV = jnp.concatenate([W, s["Vs"]], axis=0)
        O = _dot_nt(s["R"], Sb) + _mm(s["Aro"], WV)
        state_ref[p] = S * s["decay"] + _mm(WV.T, s["BKh"])
        yield
        mean = jnp.sum(O, axis=-1, keepdims=True) * (1.0 / A_HEAD)
        cen = jnp.where(mask, O - mean, 0.0)
        var = jnp.sum(cen * cen, axis=-1, keepdims=True) * (1.0 / A_HEAD)
        normed = jnp.where(mask, cen * lax.rsqrt(var + A_LNX_EPS) * lnw_ref[:, sl] + lnb_ref[:, sl], 0.0)
        full = normed + s["bonus"]
        o_ref[pl.ds(r0, C), sl] = (full[:C] + full[C:]) * g_s[pl.ds(r0, C), sl]

    return (lambda r0: [token_chain(r0)],
            lambda r0, stash: [solve_chain(p, r0, stash) for p in range(A_PAIRS)],
            lambda r0, stash: [state_chain(p, r0, stash) for p in range(A_PAIRS)])


CHUNKS_IN_FLIGHT = 2
N_RWKV_PARAMS = 10
N_GDN_PARAMS = 4
N_RWKV_SCRATCH = 8
N_GDN_SCRATCH = 5


def _mix_prompt_body(pa_ref, qkv_ref, ab_ref, z_ref, *refs):
    refs = list(refs)
    take = lambda n: [refs.pop(0) for _ in range(n)]
    rwkv_prm, gdn_prm = take(N_RWKV_PARAMS), take(N_GDN_PARAMS)
    oa_ref, ob_ref, sfa_ref, sfb_ref, tail_a_ref, tail_b_ref = take(6)
    carry_a, state_a, carry_b, state_b = take(4)
    rwkv_scr, gdn_scr = take(N_RWKV_SCRATCH), take(N_GDN_SCRATCH)
    t = pl.program_id(1)
    tt = pa_ref.shape[0]

    @pl.when(t == 0)
    def _():
        for ref in (carry_a, state_a, carry_b, state_b):
            ref[...] = jnp.zeros_like(ref)

    rwkv_token, rwkv_solve, rwkv_state = _rwkv_prompt_part(pa_ref, *rwkv_prm, oa_ref, carry_a, state_a, *rwkv_scr)
    gdn_token, gdn_solve, gdn_state = _gdn_prompt_part(qkv_ref, ab_ref, z_ref, *gdn_prm, ob_ref, carry_b, state_b,
                                                       *gdn_scr)

    n_chunks = tt // CHUNK
    group = min(CHUNKS_IN_FLIGHT, n_chunks)
    n_groups = n_chunks // group
    stashes = [({}, {}) for _ in range(n_chunks)]
    chunks_of = lambda gi: range(gi * group, (gi + 1) * group) if 0 <= gi < n_groups else ()

    def in_sequence(per_chunk_chains):
        for chains in zip(*per_chunk_chains):
            for chain in chains:
                yield from chain

    for gi in range(n_groups + 2):
        chains = []
        for c in chunks_of(gi):
            chains += rwkv_token(c * CHUNK) + gdn_token(c * CHUNK)
        for c in chunks_of(gi - 1):
            chains += rwkv_solve(c * CHUNK, stashes[c][0]) + gdn_solve(c * CHUNK, stashes[c][1])
        state_chains = [rwkv_state(c * CHUNK, stashes[c][0]) + gdn_state(c * CHUNK, stashes[c][1])
                        for c in chunks_of(gi - 2)]
        if state_chains:
            chains += [in_sequence([per_chunk[i:i + 1] for per_chunk in state_chains])
                       for i in range(len(state_chains[0]))]
        _round_robin(chains)

    @pl.when(t == pl.num_programs(1) - 1)
    def _():
        sfa_ref[0] = state_a[...]
        sfb_ref[0] = state_b[...]
        tail_a_ref[0] = pa_ref[tt - SUBLANES:, :]
        tail_b_ref[0] = qkv_ref[tt - SUBLANES:, :]


def _mix_prompt(pa, qkv, ab, z, B, T, rwkv_params, gdn_params, tt):
    n = B * T
    nt = T // tt
    assert len(rwkv_params) == N_RWKV_PARAMS and len(gdn_params) == N_GDN_PARAMS
    rows = lambda w: pl.BlockSpec((tt, w), lambda b, t: (b * nt + t, 0))
    state = lambda: pl.BlockSpec((1, 4, LANES, LANES), lambda b, t: (b, 0, 0, 0))
    big = lambda: pltpu.VMEM((tt, A_WIDTH), F32)
    return pl.pallas_call(
        _mix_prompt_body,
        grid=(B, nt),
        in_specs=[rows(A_PROJ), rows(CONV_CH), rows(LANES), rows(B_WIDTH)]
                 + [_const_spec(p.shape) for p in rwkv_params + gdn_params],
        out_specs=[rows(A_WIDTH), rows(B_WIDTH), state(), state(),
                   pl.BlockSpec((1, SUBLANES, A_PROJ), lambda b, t: (b, 0, 0)),
                   pl.BlockSpec((1, SUBLANES, CONV_CH), lambda b, t: (b, 0, 0))],
        out_shape=[jax.ShapeDtypeStruct((n, A_WIDTH), F32), jax.ShapeDtypeStruct((n, B_WIDTH), F32),
                   jax.ShapeDtypeStruct((B, A_PAIRS, LANES, LANES), F32),
                   jax.ShapeDtypeStruct((B, B_HEADS, B_HEAD, B_HEAD), F32),
                   jax.ShapeDtypeStruct((B, SUBLANES, A_PROJ), F32),
                   jax.ShapeDtypeStruct((B, SUBLANES, CONV_CH), F32)],
        scratch_shapes=[pltpu.VMEM((SUBLANES, A_PROJ), F32), pltpu.VMEM((A_PAIRS, LANES, LANES), F32),
                        pltpu.VMEM((SUBLANES, CONV_CH), F32), pltpu.VMEM((B_HEADS, B_HEAD, B_HEAD), F32)]
                       + [big() for _ in range(N_RWKV_SCRATCH + N_GDN_SCRATCH)],
        compiler_params=_cparams(2),
        name="mix_prompt",
    )(pa, qkv, ab, z, *rwkv_params, *gdn_params)


def _first_step_rows(rows, seqs, steps, state_rows, offset=0):
    hist = state_rows.shape[0] // seqs
    r, c = _iota((rows, seqs * hist), 0), _iota((rows, seqs * hist), 1)
    t = r & (steps - 1)
    sel = _one_hot((c == _group(r, steps) * hist + offset + t) & (t < hist - offset))
    return sum(_dot(sel, piece) for piece in _split3(state_rows))


def _last_step_rows(x, seqs, steps, keep):
    r, c = _iota((seqs * keep, seqs * steps), 0), _iota((seqs * keep, seqs * steps), 1)
    i = (c & (steps - 1)) - (steps - keep)
    sel = _one_hot((i >= 0) & (r == _group(c, steps) * keep + i))
    return sum(_dot(sel, piece) for piece in _split3(x))


ROWP = dict(mu_r=0, mu_k=1, mu_v=2, w0=3, a0=4, k_k=5, k_a=6, r_k=7, lnw=8, lnb=9)
ROWP_ROWS = 16
VALUE_GROUP = SUBLANES


def _rwkv_lanes_body(par_ref, pak_ref, pav_ref, paw_ref, pag_ref, shr_ref, shk_ref, shv_ref, shw_ref, shg_ref,
                     pa_ref, s_ref, rowp_ref, shared_ref, w2a_ref, g2_ref,
                     o_ref, sout_ref, shift_out_ref,
                     tr_s, ot_s, *, steps):
    rows = par_ref.shape[0]
    B = rows // steps
    p = pl.program_id(0)
    rp = lambda name: rowp_ref[ROWP[name]:ROWP[name] + 1, :]

    @pl.when(p == 0)
    def _():
        shift_out_ref[...] = _last_step_rows(pa_ref[...], B, steps, 1)

    def lerp(x_ref, first_ref, mu):
        per_step = [x_ref[pl.ds(t, B, stride=steps), :] for t in range(steps)]
        x = jnp.concatenate(per_step, axis=0)
        prev = jnp.concatenate([first_ref[...]] + per_step[:-1], axis=0)
        return x + (prev - x) * mu

    r = lerp(par_ref, shr_ref, rp("mu_r"))
    k = lerp(pak_ref, shk_ref, rp("mu_k"))
    v = lerp(pav_ref, shv_ref, rp("mu_v"))
    wa = lerp(paw_ref, shw_ref, shared_ref[0:1, :])
    gd = lerp(pag_ref, shg_ref, shared_ref[1:2, :])
    lane = _iota((1, LANES), 1)
    lora = _mm(jnp.where(lane < A_RANK_W, jnp.tanh(wa), wa), w2a_ref[...])
    g = _mm(_sigmoid(gd), g2_ref[...])
    w_log = -_softplus(-(rp("w0") + lora[:, :LANES])) - 0.5
    decay = jnp.exp(-jnp.exp(w_log))
    a = _sigmoid(rp("a0") + lora[:, LANES:])
    k_mod = k * (1.0 + (a - 1.0) * rp("k_a"))
    hi, hj = _iota((LANES, LANES), 0), _iota((LANES, LANES), 1)
    pair_ones = _one_hot(_group(hi, A_HEAD) == _group(hj, A_HEAD))
    head_sum = lambda m: _mm_sel(m, pair_ones)
    kk_raw = k * rp("k_k")
    kk = kk_raw * jnp.minimum(lax.rsqrt(head_sum(kk_raw * kk_raw)), 1e12)
    names = ("nkk", "beta", "decay", "k", "r", "v")
    for idx, m in enumerate((-kk, kk * a, decay, k_mod, r, v)):
        for t in range(steps):
            tr_s[idx, t] = m[t * B:(t + 1) * B, :].T
    at = lambda name, t: tr_s.at[names.index(name), t]

    def group(gi, carry):
        j = gi // (A_HEAD // VALUE_GROUP)
        v0 = (gi % (A_HEAD // VALUE_GROUP)) * VALUE_GROUP
        keys = lambda name, t: at(name, t)[pl.ds(pl.multiple_of(j * A_HEAD, A_HEAD), A_HEAD), :]
        v_rows = [at("v", t)[pl.ds(pl.multiple_of(gi * VALUE_GROUP, VALUE_GROUP), VALUE_GROUP), :]
                  for t in range(steps)]
        outs = [[] for _ in range(steps)]
        for i in range(VALUE_GROUP):
            S = s_ref[j, v0 + i]
            for t in range(steps):
                sa = jnp.sum(S * keys("nkk", t), axis=0, keepdims=True)
                S = S * keys("decay", t) + sa * keys("beta", t) + v_rows[t][i:i + 1, :] * keys("k", t)
                outs[t].append(jnp.sum(S * keys("r", t), axis=0, keepdims=True))
            sout_ref[j, v0 + i] = S
        for t in range(steps):
            ot_s[t, pl.ds(pl.multiple_of(gi * VALUE_GROUP, VALUE_GROUP), VALUE_GROUP), :] = _rows_to_tile(outs[t])
        return carry

    lax.fori_loop(0, 2 * A_HEAD // VALUE_GROUP, group, 0)
    o = jnp.concatenate([ot_s[t].T for t in range(steps)], axis=0)
    mean = head_sum(o) * (1.0 / A_HEAD)
    cen = o - mean
    var = head_sum(cen * cen) * (1.0 / A_HEAD)
    o = cen * lax.rsqrt(var + A_LNX_EPS) * rp("lnw") + rp("lnb")
    o = (o + head_sum(r * k_mod * rp("r_k")) * v) * g
    for t in range(steps):
        o_ref[pl.ds(t, B, stride=steps), :] = o[t * B:(t + 1) * B, :]


def _rwkv_lanes(pa, first_row, shift, state_t, layer, steps, tables):
    rowp, shared, w2a_p, g2_p = tables
    B = state_t.shape[-1]
    n = B * steps
    assert first_row % n == 0 and B == LANES and steps & (steps - 1) == 0
    rb = first_row // n
    col = lambda rows_, j, r: pl.BlockSpec((rows_, LANES), lambda p: (r, j(p)))
    groups = [lambda p: p, lambda p: A_PAIRS + p, lambda p: 2 * A_PAIRS + p,
              lambda p: 3 * A_PAIRS, lambda p: 3 * A_PAIRS + 1]
    block = (None, 2, A_HEAD, A_HEAD, B)
    return pl.pallas_call(
        functools.partial(_rwkv_lanes_body, steps=steps),
        grid=(A_PAIRS,),
        in_specs=[col(n, j, rb) for j in groups] + [col(B, j, 0) for j in groups]
                 + [pl.BlockSpec((n, A_PROJ), lambda p: (rb, 0), pipeline_mode=pl.Buffered(1)),
                    pl.BlockSpec(block, lambda p: (layer, p, 0, 0, 0)),
                    pl.BlockSpec((None, ROWP_ROWS, LANES), lambda p: (p, 0, 0)), _const_spec(shared.shape),
                    pl.BlockSpec((None, LANES, 2 * LANES), lambda p: (p, 0, 0)),
                    pl.BlockSpec((None, LANES, LANES), lambda p: (p, 0, 0))],
        out_specs=[pl.BlockSpec((n, LANES), lambda p: (0, p)), pl.BlockSpec(block, lambda p: (0, p, 0, 0, 0)),
                   pl.BlockSpec((B, A_PROJ), lambda p: (0, 0))],
        out_shape=[jax.ShapeDtypeStruct((n, A_WIDTH), F32), jax.ShapeDtypeStruct((1,) + state_t.shape[1:], F32),
                   jax.ShapeDtypeStruct((B, A_PROJ), F32)],
        scratch_shapes=[pltpu.VMEM((6, steps, LANES, B), F32), pltpu.VMEM((steps, LANES, B), F32)],
        compiler_params=_cparams(1),
        name="rwkv_sample",
    )(*([pa] * 5), *([shift] * 5), pa, state_t, rowp, shared, w2a_p, g2_p)


def _rwkv_pair_tables(mu, w0, a0, k_k, k_a, r_k, lnw, lnb, w2, a2, g2):
    per_pair = lambda a: a.reshape(A_PAIRS, 1, LANES)
    rows = {"mu_r": mu[:A_WIDTH], "mu_k": mu[A_WIDTH:2 * A_WIDTH], "mu_v": mu[2 * A_WIDTH:3 * A_WIDTH],
            "w0": w0, "a0": a0, "k_k": k_k, "k_a": k_a, "r_k": r_k.reshape(-1), "lnw": lnw, "lnb": lnb}
    table = jnp.concatenate([per_pair(rows[name].astype(F32)) for name in sorted(ROWP, key=ROWP.get)]
                            + [jnp.zeros((A_PAIRS, ROWP_ROWS - len(ROWP), LANES), F32)], axis=1)
    shared = jnp.concatenate([mu[3 * A_WIDTH:3 * A_WIDTH + LANES].reshape(1, LANES),
                              mu[3 * A_WIDTH + LANES:].reshape(1, LANES),
                              jnp.zeros((SUBLANES - 2, LANES), F32)], axis=0).astype(F32)
    by_pair = lambda w: jnp.transpose(w.astype(F32).reshape(w.shape[0], A_PAIRS, LANES), (1, 0, 2))
    zeros = jnp.zeros((A_PAIRS, A_RANK_W, LANES), F32)
    w2a_p = jnp.concatenate([jnp.concatenate([by_pair(w2), zeros], axis=2),
                             jnp.concatenate([zeros, by_pair(a2)], axis=2)], axis=1)
    return table, shared, w2a_p, by_pair(g2)


def _gdn_qkv(conv):
    c = _silu(conv)
    qs, ks = [], []
    for h in range(B_HEADS):
        q = c[:, h * B_HEAD:(h + 1) * B_HEAD]
        k = c[:, B_WIDTH + h * B_HEAD:B_WIDTH + (h + 1) * B_HEAD]
        qs.append(q * (lax.rsqrt(jnp.sum(q * q, axis=-1, keepdims=True) + 1e-6) * (B_HEAD ** -0.5)))
        ks.append(k * lax.rsqrt(jnp.sum(k * k, axis=-1, keepdims=True) + 1e-6))
    q = jnp.concatenate(qs, axis=1)
    k = jnp.concatenate(ks, axis=1)
    v = c[:, 2 * B_WIDTH:]
    return q, k, v


def _gdn_gates(ab, alog, dtb):
    lane = _iota((1, LANES), 1)
    g = -jnp.exp(alog) * _softplus(ab + dtb)
    beta = _sigmoid(ab)
    gb = jnp.where(lane < B_HEADS, g, beta)
    si, sj = _iota((LANES, 2 * B_WIDTH), 0), _iota((LANES, 2 * B_WIDTH), 1)
    spread = _mm_sel(gb, _one_hot(si == _group(sj, B_HEAD)))
    return spread[:, :B_WIDTH], spread[:, B_WIDTH:]


def _gdn_out(o, norm_w, z):
    return o * lax.rsqrt(jnp.mean(o * o, axis=-1, keepdims=True) + RMS_EPS) * norm_w * _silu(z)


def _gdn_prompt_part(qkv_ref, ab_ref, z_ref, cw_ref, alog_ref, dtb_ref, nw_ref, o_ref,
                     carry_ref, state_ref, q_s, k_s, v_s, gc_s, beta_s):
    tt = qkv_ref.shape[0]
    C = CHUNK
    g, beta = _gdn_gates(ab_ref[...], alog_ref[...], dtb_ref[...])
    ri, ci = _iota((tt, tt), 0), _iota((tt, tt), 1)
    beta_s[...] = beta
    gc_s[...] = _sel_mm(_one_hot((_group(ri, C) == _group(ci, C)) & (ci <= ri)), g)

    def token_chain(r0):
        rows = slice(r0, r0 + C)
        x = qkv_ref[rows, :]
        before = carry_ref[...] if r0 == 0 else qkv_ref[r0 - SUBLANES:r0, :]
        if r0 + C == tt:
            carry_ref[...] = x[C - SUBLANES:, :]
        row8 = _iota((SUBLANES, 1), 0)

        def shift_rows(cur, halo, i):
            down = pltpu.roll(cur, i, axis=0)
            top = jnp.where(row8 < i, pltpu.roll(halo, i, axis=0), down[:SUBLANES])
            return jnp.concatenate([top, down[SUBLANES:]], axis=0)

        assert CONV_K == 4
        c0, c1, c2, c3 = (cw_ref[i:i + 1, :] for i in range(CONV_K))
        x1 = shift_rows(x, before, 1)
        yield
        older = x * c1 + x1 * c0
        older_halo = before * c1 + pltpu.roll(before, 1, axis=0) * c0
        conv = x * c3 + x1 * c2 + shift_rows(older, older_halo, 2)
        yield
        q, k, v = _gdn_qkv(conv)
        q_s[rows, :] = q
        k_s[rows, :] = k
        v_s[rows, :] = v
        yield

    i2, j2 = _iota((2 * C, 2 * C), 0), _iota((2 * C, 2 * C), 1)
    same_head = _group(i2, C) == _group(j2, C)
    strict = same_head & (i2 > j2)
    incl = same_head & (i2 >= j2)
    eye = jnp.where(i2 == j2, 1.0, 0.0)
    first = _iota((2 * C, 1), 0) < C

    def solve_chain(pr, r0, stash):
        sls = [slice(h * B_HEAD, (h + 1) * B_HEAD) for h in (2 * pr, 2 * pr + 1)]
        ld = lambda ref: jnp.concatenate([ref[pl.ds(r0, C), sl] for sl in sls], axis=0)
        q_h, k_h, v_h, gc_h, beta_h = ld(q_s), ld(k_s), ld(v_s), ld(gc_s), ld(beta_s)
        diff = gc_h - gc_h.T
        dm = jnp.where(incl, jnp.exp(jnp.where(incl, diff, 0.0)), 0.0)
        kb = k_h * beta_h
        QK = _mm_nt(jnp.concatenate([kb, q_h], axis=0), k_h)
        yield
        N = -jnp.where(strict, QK[:2 * C] * dm, 0.0)
        qk = QK[2 * C:] * dm
        egc = jnp.exp(gc_h)
        X = jnp.concatenate([v_h * beta_h, kb * egc], axis=1)
        T = yield from _nilpotent_inverse(N, eye)
        UW = _mm(T, X)
        yield
        g_last = jnp.where(first, gc_h[C - 1:C, :], gc_h[2 * C - 1:2 * C, :])
        stash[pr] = dict(u=UW[:, :B_HEAD], w=UW[:, B_HEAD:].astype(BF16), qd=(q_h * egc).astype(BF16),
                         qk=qk.astype(BF16), k_dec_t=(k_h * jnp.exp(g_last - gc_h)).T.astype(BF16),
                         decay=[jnp.exp(gc_h[(j + 1) * C - 1:(j + 1) * C, :]) for j in range(2)])

    def state_chain(pr, r0, stash):
        heads = (2 * pr, 2 * pr + 1)
        sls = [slice(h * B_HEAD, (h + 1) * B_HEAD) for h in heads]
        s = stash[pr]
        wS, qS, S_old = [], [], []
        for j, h in enumerate(heads):
            S = state_ref[h]
            rows = slice(j * C, (j + 1) * C)
            wq = _dot(jnp.concatenate([s["w"][rows], s["qd"][rows]], axis=0), S.astype(BF16))
            wS.append(wq[:C])
            qS.append(wq[C:])
            S_old.append(S)
        yield
        v_new = s["u"] - jnp.concatenate(wS, axis=0)
        o = jnp.concatenate(qS, axis=0) + _dot(s["qk"], v_new.astype(BF16))
        for j, h in enumerate(heads):
            mine = first if j == 0 else jnp.logical_not(first)
            rows = slice(j * C, (j + 1) * C)
            state_ref[h] = S_old[j] * s["decay"][j] + _mm(s["k_dec_t"], jnp.where(mine, v_new, 0.0))
            o_ref[pl.ds(r0, C), sls[j]] = _gdn_out(o[rows], nw_ref[...], z_ref[pl.ds(r0, C), sls[j]])
        yield

    return (lambda r0: [token_chain(r0)],
            lambda r0, stash: [solve_chain(pr, r0, stash) for pr in range(B_HEADS // 2)],
            lambda r0, stash: [state_chain(pr, r0, stash) for pr in range(B_HEADS // 2)])


def _gdn_chunk_body(qkv_ref, hist_ref, ab_ref, z_ref, s_ref, cw_ref, alog_ref, dtb_ref, nw_ref,
                    o_ref, sout_ref, hist_out_ref, *, steps):
    rows = ab_ref.shape[0]
    seqs = rows // steps
    x = qkv_ref[...]
    hist_out_ref[...] = _last_step_rows(x, seqs, steps, CONV_K - 1)
    hist = hist_ref[...]
    t_idx = _iota((rows, 1), 0) & (steps - 1)
    conv = x * cw_ref[CONV_K - 1:CONV_K, :]
    for i in range(1, CONV_K):
        tap = jnp.where(t_idx >= i, pltpu.roll(x, i, axis=0),
                        _first_step_rows(rows, seqs, steps, hist, offset=CONV_K - 1 - i))
        conv = conv + tap * cw_ref[CONV_K - 1 - i:CONV_K - i, :]
    q, k, v = _gdn_qkv(conv)
    g, beta = _gdn_gates(ab_ref[...], alog_ref[...], dtb_ref[...])
    i2, j2 = _iota((rows, rows), 0), _iota((rows, rows), 1)
    same = _group(i2, steps) == _group(j2, steps)
    strict = same & (i2 > j2)
    incl = same & (i2 >= j2)
    eye = jnp.where(i2 == j2, 1.0, 0.0)
    gc = _sel_mm(_one_hot(incl), g)
    g_end = _sel_mm(_one_hot(j2 == (i2 | (steps - 1))), gc)
    row = _iota((rows, 1), 0)
    pair_rows = 2 * SUBLANES
    first_half = (_iota((pair_rows, 1), 0) & (SUBLANES - 1)) < steps
    assert 2 * steps == SUBLANES

    def head_chain(h):
        sl = slice(h * B_HEAD, (h + 1) * B_HEAD)
        q_h, k_h, v_h, gc_h, beta_h, ge_h = q[:, sl], k[:, sl], v[:, sl], gc[:, sl], beta[:, sl], g_end[:, sl]
        dm = jnp.where(incl, jnp.exp(jnp.where(incl, gc_h - gc_h.T, 0.0)), 0.0)
        kb = k_h * beta_h
        QK = _mm_nt(jnp.concatenate([kb, q_h], axis=0), k_h)
        yield
        N = -jnp.where(strict, QK[:rows] * dm, 0.0)
        qk = QK[rows:] * dm
        assert steps == 4
        N2 = _mm(N, N)
        yield
        T = eye + N
        T = T + _mm(T, N2)
        yield
        egc = jnp.exp(gc_h)
        UW = _mm(T, jnp.concatenate([v_h * beta_h, kb * egc], axis=1))
        yield
        w = UW[:, B_HEAD:].astype(BF16)
        qd = (q_h * egc).astype(BF16)
        k_dec_t = (k_h * jnp.exp(ge_h - gc_h)).T.astype(BF16)
        decay = jnp.exp(ge_h)
        wS, qS = [], []
        for m in range(rows // SUBLANES):
            tile = slice(m * SUBLANES, (m + 1) * SUBLANES)
            lhs = jnp.concatenate([w[tile], qd[tile]], axis=0)
            res = jnp.where(first_half, _dot(lhs, s_ref[2 * m, h].astype(BF16)),
                            _dot(lhs, s_ref[2 * m + 1, h].astype(BF16)))
            wS.append(res[:SUBLANES])
            qS.append(res[SUBLANES:])
            yield
        v_new = UW[:, :B_HEAD] - jnp.concatenate(wS, axis=0)
        o = jnp.concatenate(qS, axis=0) + _mm(qk, v_new)
        o_ref[:, sl] = _gdn_out(o, nw_ref[...], z_ref[:, sl])
        yield
        for b in range(seqs):
            mine = _group(row, steps) == b
            sout_ref[b, h] = (s_ref[b, h] * decay[b * steps:b * steps + 1, :]
                              + _mm(k_dec_t, jnp.where(mine, v_new, 0.0)))
            yield

    _round_robin(head_chain(h) for h in range(B_HEADS))


def _gdn_sample(qkv, ab, z, first_row, hist, state, layer, steps, params, seqs):
    nb = state.shape[1]
    n = nb * steps
    rows = seqs * steps
    hist_rows = seqs * (CONV_K - 1)
    assert SUBLANES % steps == 0 and rows % SUBLANES == 0 and hist_rows % SUBLANES == 0 and first_row % rows == 0
    row = lambda w: pl.BlockSpec((rows, w), lambda i: (i + first_row // rows, 0))
    block = (None, seqs, B_HEADS, B_HEAD, B_HEAD)
    sspec = pl.BlockSpec(block, lambda i: (layer, i, 0, 0, 0))
    ospec = pl.BlockSpec(block, lambda i: (0, i, 0, 0, 0))
    return pl.pallas_call(
        functools.partial(_gdn_chunk_body, steps=steps),
        grid=(nb // seqs,),
        in_specs=[row(CONV_CH), pl.BlockSpec((hist_rows, CONV_CH), lambda i: (i, 0)), row(LANES), row(B_WIDTH), sspec]
                 + [_const_spec(p.shape) for p in params],
        out_specs=[pl.BlockSpec((rows, B_WIDTH), lambda i: (i, 0)), ospec,
                   pl.BlockSpec((hist_rows, CONV_CH), lambda i: (i, 0))],
        out_shape=[jax.ShapeDtypeStruct((n, B_WIDTH), F32), jax.ShapeDtypeStruct((1,) + state.shape[1:], F32),
                   jax.ShapeDtypeStruct(hist.shape, F32)],
        compiler_params=_cparams(1),
        name="gdn_sample",
    )(qkv, hist, ab, z, state, *params)


def _cuts(widths):
    edges, total = [], 0
    for w in widths[:-1]:
        total += w
        edges.append(total)
    return edges


def _regroup_pa(a):
    r, wd, k, v, ad, gd = jnp.split(a, _cuts((A_WIDTH, A_RANK_W, A_WIDTH, A_WIDTH, A_RANK_A, A_RANK_G)), axis=-1)
    return jnp.concatenate([r, k, v, wd, ad, gd], axis=-1)


def _ungroup_pa(a):
    r, k, v, wd, ad, gd = jnp.split(a, _cuts((A_WIDTH, A_WIDTH, A_WIDTH, A_RANK_W, A_RANK_A, A_RANK_G)), axis=-1)
    return jnp.concatenate([r, wd, k, v, ad, gd], axis=-1)


def _token_tile(n, want):
    tm = want
    while n % tm:
        tm //= 2
    return tm


def kernel(x_prompt, x_sample, state_rwkv, state_rwkv_shift, state_delta, state_conv, ffn1_norm, ffn1_w_gate, ffn1_w_up, ffn1_w_down, mix_norm, w_in, rwkv_mu, rwkv_w0, rwkv_w2, rwkv_a0, rwkv_a2, rwkv_g2, rwkv_k_k, rwkv_k_a, rwkv_r_k, rwkv_lnx_w, rwkv_lnx_b, gdn_conv_w, gdn_A_log, gdn_dt_bias, gdn_norm_w, proj_a, proj_b, w_out, ffn2_norm, ffn2_w_gate, ffn2_w_up, ffn2_w_down, final_norm):
    depth = ffn1_norm.shape[0]
    assert depth == 1, "single-layer trunk"
    Bp, Tp, _ = x_prompt.shape
    Bs, Ts, _ = x_sample.shape
    l = 0
    row = lambda a: a.reshape(1, -1).astype(F32)

    wi = w_in[l].astype(BF16)
    o_b = A_PROJ
    proj_w = (_regroup_pa(wi[:, :A_PROJ]),
              wi[:, o_b:o_b + CONV_CH + LANES],
              wi[:, o_b + CONV_CH + 2 * B_HEADS:])
    ffn1 = (row(ffn1_norm[l]), ffn1_w_gate[l].astype(BF16), ffn1_w_up[l].astype(BF16), ffn1_w_down[l].astype(BF16))
    ffn2 = (row(ffn2_norm[l]), ffn2_w_gate[l].astype(BF16), ffn2_w_up[l].astype(BF16), ffn2_w_down[l].astype(BF16))
    merge_w = (proj_a[l].astype(BF16), proj_b[l].astype(BF16), w_out[l].astype(BF16))
    zw = jnp.zeros((A_RANK_W, A_WIDTH), F32)
    w2a = jnp.concatenate([jnp.concatenate([rwkv_w2[l], zw], axis=1),
                           jnp.concatenate([zw, rwkv_a2[l]], axis=1)], axis=0)
    rwkv_params = (row(_regroup_pa(rwkv_mu[l])), row(rwkv_w0[l]), row(rwkv_a0[l]), row(rwkv_k_k[l]), row(rwkv_k_a[l]),
                   row(rwkv_r_k[l]), row(rwkv_lnx_w[l]), row(rwkv_lnx_b[l]), w2a, rwkv_g2[l].astype(F32))
    pad_lane = lambda a: jnp.pad(a.reshape(1, -1).astype(F32), ((0, 0), (0, LANES - a.size)))
    gdn_params = (gdn_conv_w[l].astype(F32), pad_lane(gdn_A_log[l]), pad_lane(gdn_dt_bias[l]), row(gdn_norm_w[l]))

    n_p, n_s = Bp * Tp, Bs * Ts
    tm = _token_tile(n_s, _token_tile(n_p, 512))
    h = _ffn(x_prompt.reshape(n_p, D_MODEL), x_sample.reshape(n_s, D_MODEL), *ffn1, tm=tm)
    pa, qkv, z, gates, ab = _proj(h, row(mix_norm[l]), *proj_w, tm=tm)

    def trunk_back(oa, ob, first_row):
        return _tail(h, oa, ob, gates, *merge_w, *ffn2, row(final_norm), tm=tm, first_row=first_row)

    tt = _token_tile(Tp, 256)
    oa, ob, s_pairs, delta_p, pa_tail, qkv_tail = _mix_prompt(pa, qkv, ab, z, Bp, Tp, rwkv_params, gdn_params, tt)
    y_prompt = trunk_back(oa, ob, 0).reshape(Bp, Tp, D_MODEL)
    sp = s_pairs.reshape(Bp, A_PAIRS, 2, A_HEAD, 2, A_HEAD)
    rwkv_p = jnp.stack([sp[:, :, 0, :, 0], sp[:, :, 1, :, 1]], axis=2).reshape(Bp, A_HEADS, A_HEAD, A_HEAD)
    shift_p = _ungroup_pa(pa_tail[:, -1])
    conv_p = qkv_tail[:, SUBLANES - (CONV_K - 1):]

    assert Ts & (Ts - 1) == 0 and Ts >= CONV_K - 1, "sample steps: power of two covering the conv history"
    tables = _rwkv_pair_tables(_regroup_pa(rwkv_mu[l]), rwkv_w0[l], rwkv_a0[l], rwkv_k_k[l], rwkv_k_a[l], rwkv_r_k[l],
                               rwkv_lnx_w[l], rwkv_lnx_b[l], rwkv_w2[l], rwkv_a2[l], rwkv_g2[l])
    oa, s_lanes, last_pa = _rwkv_lanes(pa, n_p, _regroup_pa(state_rwkv_shift[l].astype(F32)),
                                       jnp.transpose(state_rwkv.astype(F32), (0, 2, 3, 4, 1)), l, Ts, tables)
    rwkv_s = jnp.transpose(s_lanes, (0, 4, 1, 2, 3))
    shift_s = _ungroup_pa(last_pa)
    hist = state_conv[l].astype(F32).reshape(Bs * (CONV_K - 1), CONV_CH)
    ob, delta_s, new_hist = _gdn_sample(qkv, ab, z, n_p, hist, state_delta.astype(F32), l, Ts, gdn_params,
                                        seqs=_token_tile(Bs, 32))
    conv_s = new_hist.reshape(Bs, CONV_K - 1, CONV_CH)
    y_sample = trunk_back(oa, ob, n_p).reshape(Bs, Ts, D_MODEL)

    add_depth = lambda a: a[None]
    return (y_prompt, y_sample,
            add_depth(rwkv_p), add_depth(shift_p), add_depth(delta_p), add_depth(conv_p),
            rwkv_s, add_depth(shift_s), delta_s, add_depth(conv_s))
```

```python
import functools

import jax
import jax.numpy as jnp
from jax import lax
from jax.experimental import pallas as pl
from jax.experimental.pallas import tpu as pltpu

F32 = jnp.float32
BF16 = jnp.bfloat16

D_MODEL = 1024
D_FF = 2816
RMS_EPS = 1e-6
A_HEAD = 64
A_HEADS = 8
A_WIDTH = A_HEADS * A_HEAD
A_RANK_W = 64
A_RANK_A = 64
A_RANK_G = 128
A_PROJ = 3 * A_WIDTH + A_RANK_W + A_RANK_A + A_RANK_G
A_LNX_EPS = 64e-5
A_PAIRS = A_HEADS // 2
B_HEADS = 4
B_HEAD = 128
B_WIDTH = B_HEADS * B_HEAD
CONV_K = 4
CONV_CH = 3 * B_WIDTH
B_PROJ = CONV_CH + 2 * B_HEADS + B_WIDTH
GATE_COLS = 2 * D_MODEL
LANES = 128
SUBLANES = 8
MXU_DIM = 256
VMEM_LIMIT_BYTES = 56 * 1024 * 1024
CHUNK = 64
PA_R, PA_K, PA_V, PA_WA, PA_G = 0, A_WIDTH, 2 * A_WIDTH, 3 * A_WIDTH, 3 * A_WIDTH + A_RANK_W + A_RANK_A
PROJ_SPLITS = (A_PROJ, CONV_CH, B_WIDTH, GATE_COLS, LANES)


def _cparams(n_grid_dims):
    return pltpu.CompilerParams(dimension_semantics=("arbitrary",) * n_grid_dims,
                                vmem_limit_bytes=VMEM_LIMIT_BYTES)


def _const_spec(shape):
    nd = len(shape)
    return pl.BlockSpec(shape, lambda *_: (0,) * nd, pipeline_mode=pl.Buffered(1))


def _dot(a, b):
    return jnp.dot(a, b, preferred_element_type=F32)


def _dot_nt(a, b):
    return lax.dot_general(a, b, (((1,), (1,)), ((), ())), preferred_element_type=F32)


def _split3(x):
    hi = x.astype(BF16)
    rest = x - hi.astype(F32)
    mid = rest.astype(BF16)
    lo = (rest - mid.astype(F32)).astype(BF16)
    return hi, mid, lo


def _mm(a, b):
    return _dot(a.astype(BF16), b.astype(BF16))


def _mm_nt(a, b):
    return _dot_nt(a.astype(BF16), b.astype(BF16))


def _sel_mm(sel, x):
    return _dot(jnp.concatenate([sel, sel, sel], axis=1), jnp.concatenate(_split3(x), axis=0))


def _mm_sel(x, sel):
    return _dot(jnp.concatenate(_split3(x), axis=1), jnp.concatenate([sel, sel, sel], axis=0))


INV_BASE = 8


def _nilpotent_inverse(n, eye):
    width = n.shape[1]
    bi, bj = _iota(n.shape, 0), _iota(n.shape, 1)
    same = lambda size: _group(bi, size) == _group(bj, size)
    d = jnp.where(same(INV_BASE), n, 0.0)
    t = eye + d
    d = _mm(d, d)
    yield
    for _ in range(INV_BASE.bit_length() - 3):
        both = _mm(d, jnp.concatenate([t, d], axis=1))
        yield
        t = t + both[:, :width]
        d = both[:, width:]
    t = t + _mm(d, t)
    yield
    size = INV_BASE
    while size < CHUNK:
        coupling = jnp.where(same(2 * size) & jnp.logical_not(same(size)), n, 0.0)
        tb = t.astype(BF16)
        lt = _mm(coupling, tb)
        yield
        t = t + _mm(tb, lt)
        yield
        size *= 2
    return t


def _round_robin(chains):
    chains = list(chains)
    while chains:
        for chain in list(chains):
            try:
                next(chain)
            except StopIteration:
                chains.remove(chain)


def _rms(x, w):
    return x * lax.rsqrt(jnp.mean(x * x, axis=-1, keepdims=True) + RMS_EPS) * w


def _sigmoid(x):
    return 1.0 / (1.0 + jnp.exp(-x))


def _silu(x):
    return x * _sigmoid(x)


def _softplus(x):
    return jnp.maximum(x, 0.0) + jnp.log(1.0 + jnp.exp(-jnp.abs(x)))


def _iota(shape, dim):
    return lax.broadcasted_iota(jnp.int32, shape, dim)


def _group(idx, size):
    assert size & (size - 1) == 0
    return lax.shift_right_logical(idx, size.bit_length() - 1)


def _one_hot(cond):
    return jnp.where(cond, 1.0, 0.0).astype(BF16)


def _rows_to_tile(rows):
    rid = _iota((SUBLANES, 1), 0)
    tile = jnp.zeros((SUBLANES, rows[0].shape[1]), F32)
    for i, row in enumerate(rows):
        tile = jnp.where(rid == i, row, tile)
    return tile


def _swiglu_half_step(x, nw, wg_ref, wu_ref, wd_ref, slabs=(D_FF,)):
    xn = _rms(x, nw).astype(BF16)
    assert sum(slabs) == D_FF and all(w % MXU_DIM == 0 for w in slabs)
    y = None
    start = 0
    for width in slabs:
        cols = slice(start, start + width)
        start += width
        g = _dot(xn, wg_ref[:, cols])
        u = _dot(xn, wu_ref[:, cols])
        act = (_silu(g) * u).astype(BF16)
        part = _dot(act, wd_ref[cols, :])
        y = part if y is None else y + part
    return x + 0.5 * y


def _ffn_body(xa_ref, xb_ref, nw_ref, wg_ref, wu_ref, wd_ref, o_ref, *, blocks_a):
    x = jnp.where(pl.program_id(0) < blocks_a, xa_ref[...], xb_ref[...])
    o_ref[...] = _swiglu_half_step(x, nw_ref[...], wg_ref, wu_ref, wd_ref)


def _ffn(xa, xb, nw, wg, wu, wd, tm):
    na, nb = xa.shape[0], xb.shape[0]
    assert na % tm == 0 and nb % tm == 0
    blocks_a = na // tm
    return pl.pallas_call(
        functools.partial(_ffn_body, blocks_a=blocks_a),
        grid=((na + nb) // tm,),
        in_specs=[pl.BlockSpec((tm, D_MODEL), lambda i: (jnp.minimum(i, blocks_a - 1), 0)),
                  pl.BlockSpec((tm, D_MODEL), lambda i: (jnp.maximum(i - blocks_a, 0), 0)),
                  _const_spec((1, D_MODEL)),
                  _const_spec((D_MODEL, D_FF)), _const_spec((D_MODEL, D_FF)), _const_spec((D_FF, D_MODEL))],
        out_specs=pl.BlockSpec((tm, D_MODEL), lambda i: (i, 0)),
        out_shape=jax.ShapeDtypeStruct((na + nb, D_MODEL), F32),
        compiler_params=_cparams(1),
        name="ffn1",
    )(xa, xb, nw, wg, wu, wd)


def _proj_body(h_ref, nw_ref, wpa_ref, wqa_ref, wzg_ref, pa_ref, qkv_ref, z_ref, gates_ref, ab_ref):
    u = _rms(h_ref[...], nw_ref[...]).astype(BF16)
    pa_ref[...] = _dot(u, wpa_ref[...])
    qkv_ref[...] = _dot(u, wqa_ref[:, :CONV_CH])
    ab_ref[...] = _dot(u, wqa_ref[:, CONV_CH:])
    z_ref[...] = _dot(u, wzg_ref[:, :B_WIDTH])
    gates_ref[...] = _dot(u, wzg_ref[:, B_WIDTH:])


def _proj(h, nw, w_pa, w_qa, w_zg, tm):
    n = h.shape[0]
    assert w_qa.shape[1] == CONV_CH + LANES and w_zg.shape[1] == B_WIDTH + GATE_COLS
    return pl.pallas_call(
        _proj_body,
        grid=(n // tm,),
        in_specs=[pl.BlockSpec((tm, D_MODEL), lambda i: (i, 0)), _const_spec((1, D_MODEL)),
                  _const_spec(w_pa.shape), _const_spec(w_qa.shape), _const_spec(w_zg.shape)],
        out_specs=[pl.BlockSpec((tm, w), lambda i: (i, 0)) for w in PROJ_SPLITS],
        out_shape=[jax.ShapeDtypeStruct((n, w), F32) for w in PROJ_SPLITS],
        compiler_params=_cparams(1),
        name="proj",
    )(h, nw, w_pa, w_qa, w_zg)


TAIL_FF_SLABS = (6 * MXU_DIM, 5 * MXU_DIM)


def _tail_body(h_ref, oa_ref, ob_ref, gates_ref, pa_ref, pb_ref, wo_ref, nw_ref, wg_ref, wu_ref, wd_ref,
               fn_ref, o_ref):
    ma = _dot(oa_ref[...].astype(BF16), pa_ref[...])
    mb = _dot(ob_ref[...].astype(BF16), pb_ref[...])
    merged = _sigmoid(gates_ref[:, :D_MODEL]) * ma + _sigmoid(gates_ref[:, D_MODEL:]) * mb
    h = h_ref[...] + _dot(merged.astype(BF16), wo_ref[...])
    h = _swiglu_half_step(h, nw_ref[...], wg_ref, wu_ref, wd_ref, slabs=TAIL_FF_SLABS)
    o_ref[...] = _rms(h, fn_ref[...])


def _tail(h, oa, ob, gates, proj_a, proj_b, w_out, nw, wg, wu, wd, fn, tm, first_row):
    n = oa.shape[0]
    assert n % tm == 0 and first_row % tm == 0
    row = lambda w: pl.BlockSpec((tm, w), lambda i: (i, 0))
    stream = lambda w: pl.BlockSpec((tm, w), lambda i: (i + first_row // tm, 0))
    return pl.pallas_call(
        _tail_body,
        grid=(n // tm,),
        in_specs=[stream(D_MODEL), row(A_WIDTH), row(B_WIDTH), stream(GATE_COLS),
                  _const_spec((A_WIDTH, D_MODEL)), _const_spec((B_WIDTH, D_MODEL)),
                  _const_spec((D_MODEL, D_MODEL)), _const_spec((1, D_MODEL)),
                  _const_spec((D_MODEL, D_FF)), _const_spec((D_MODEL, D_FF)), _const_spec((D_FF, D_MODEL)),
                  _const_spec((1, D_MODEL))],
        out_specs=row(D_MODEL),
        out_shape=jax.ShapeDtypeStruct((n, D_MODEL), F32),
        compiler_params=_cparams(1),
        name="tail",
    )(h, oa, ob, gates, proj_a, proj_b, w_out, nw, wg, wu, wd, fn)


def _rwkv_token_math(x, prev, mu, w0, a0, k_k, k_a, w2a, g2):
    pm = x + (prev - x) * mu
    r = pm[:, PA_R:PA_R + A_WIDTH]
    k = pm[:, PA_K:PA_K + A_WIDTH]
    v = pm[:, PA_V:PA_V + A_WIDTH]
    wa = pm[:, PA_WA:PA_WA + LANES]
    gd = pm[:, PA_G:PA_G + A_RANK_G]
    lane = _iota((1, LANES), 1)
    lora_in = jnp.where(lane < A_RANK_W, jnp.tanh(wa), wa)
    lora = _mm(lora_in, w2a)
    g = _mm(_sigmoid(gd), g2)
    yield
    w_log = -_softplus(-(w0 + lora[:, :A_WIDTH])) - 0.5
    log_decay = -jnp.exp(w_log)
    yield
    a = _sigmoid(a0 + lora[:, A_WIDTH:])
    kk_raw = k * k_k
    k_mod = k * (1.0 + (a - 1.0) * k_a)
    return r, k_mod, v, kk_raw, a, log_decay, g


def _pair_mask(rows_per_head):
    shape = (2 * rows_per_head, LANES)
    return _group(_iota(shape, 0), rows_per_head) == _group(_iota(shape, 1), A_HEAD)


def _rwkv_prompt_part(pa_ref, mu_ref, w0_ref, a0_ref, kk_ref, ka_ref, rk_ref, lnw_ref, lnb_ref, w2a_ref, g2_ref,
                      o_ref, carry_ref, state_ref, r_s, k_s, v_s, kkraw_s, a_s, cum_s, ld_s, g_s):
    tt = pa_ref.shape[0]
    C = CHUNK
    lower = _one_hot(_iota((C, C), 1) <= _iota((C, C), 0))

    def token_chain(r0):
        rows = slice(r0, r0 + C)
        x = pa_ref[rows, :]
        before = carry_ref[SUBLANES - 1:SUBLANES, :] if r0 == 0 else pa_ref[r0 - 1:r0, :]
        prev = jnp.where(_iota((C, 1), 0) == 0, before, pltpu.roll(x, 1, axis=0))
        if r0 + C == tt:
            carry_ref[...] = x[C - SUBLANES:, :]
        r, k_mod, v, kk_raw, a, log_decay, g = yield from _rwkv_token_math(
            x, prev, mu_ref[...], w0_ref[...], a0_ref[...], kk_ref[...], ka_ref[...], w2a_ref[...], g2_ref[...])
        r_s[rows, :] = r
        k_s[rows, :] = k_mod
        v_s[rows, :] = v
        kkraw_s[rows, :] = kk_raw
        a_s[rows, :] = a
        g_s[rows, :] = g
        ld_s[rows, :] = log_decay
        yield
        cum_s[rows, :] = sum(_dot(lower, piece) for piece in _split3(log_decay))
        yield

    mask = _pair_mask(C)
    i2, j2 = _iota((2 * C, 2 * C), 0), _iota((2 * C, 2 * C), 1)
    strict = i2 > j2
    incl = i2 >= j2
    eye = jnp.where(i2 == j2, 1.0, 0.0)
    dup = lambda m: jnp.concatenate([m, m], axis=0)
    stack = lambda m: jnp.where(mask, dup(m), 0.0)
    lnb_stacked = [jnp.where(mask, lnb_ref[:, p * LANES:(p + 1) * LANES], 0.0) for p in range(A_PAIRS)]

    def solve_chain(p, r0, stash):
        sl = slice(p * LANES, (p + 1) * LANES)
        ld = lambda ref: ref[pl.ds(r0, C), sl]
        r_p, k_p, v_p, a_p, cum, ldec = ld(r_s), ld(k_s), ld(v_s), ld(a_s), ld(cum_s), ld(ld_s)
        einc = jnp.exp(cum)
        eex = jnp.exp(cum - ldec)
        einv = jnp.exp(-cum)
        etail = jnp.exp(cum[C - 1:C, :] - cum)
        kks = stack(ld(kkraw_s))
        kks = kks * jnp.minimum(lax.rsqrt(jnp.sum(kks * kks, axis=-1, keepdims=True)), 1e12)
        As = kks * dup(-eex)
        Bs = kks * dup(a_p * einv)
        Bh = kks * dup(a_p * etail)
        Ks = stack(k_p * einv)
        Kh = stack(k_p * etail)
        Rs = stack(r_p * einc)
        Vs = stack(v_p)
        AR = jnp.concatenate([As, Rs], axis=0).astype(BF16)
        Vb = Vs.astype(BF16)
        G = _mm_nt(AR, jnp.concatenate([Bs, Ks], axis=0))
        yield
        Aab = jnp.where(strict, G[:2 * C, :2 * C], 0.0)
        Aak = jnp.where(strict, G[:2 * C, 2 * C:], 0.0)
        Arb = jnp.where(incl, G[2 * C:, :2 * C], 0.0)
        Ark = jnp.where(incl, G[2 * C:, 2 * C:], 0.0)
        Y = _mm(Aak, Vb)
        yield
        T = yield from _nilpotent_inverse(Aab, eye)
        WU = _mm(T, jnp.concatenate([AR[:2 * C], Y.astype(BF16)], axis=1))
        yield
        bonus = jnp.sum(stack(r_p * k_p * rk_ref[:, sl]), axis=-1, keepdims=True) * Vs
        stash[p] = dict(WU=WU, R=AR[2 * C:], Vs=Vs, bonus=bonus,
                        Aro=jnp.concatenate([Arb, Ark], axis=1).astype(BF16),
                        BKh=jnp.concatenate([Bh, Kh], axis=0).astype(BF16), decay=einc[C - 1:C, :])

    def state_chain(p, r0, stash):
        sl = slice(p * LANES, (p + 1) * LANES)
        s = stash[p]
        S = state_ref[p]
        Sb = S.astype(BF16)
        W = _mm_nt(s["WU"][:, :LANES], Sb) + s["WU"][:, LANES:]
        yield
        WV = jnp.concatenate([W, s["Vs"]], axis=0)
        O = _dot_nt(s["R"], Sb) + _mm(s["Aro"], WV)
        state_ref[p] = S * s["decay"] + _mm(WV.T, s["BKh"])
        yield
        mean = jnp.sum(O, axis=-1, keepdims=True) * (1.0 / A_HEAD)
        cen = jnp.where(mask, O - mean, 0.0)
        var = jnp.sum(cen * cen, axis=-1, keepdims=True) * (1.0 / A_HEAD)
        normed = cen * lax.rsqrt(var + A_LNX_EPS) * lnw_ref[:, sl] + lnb_stacked[p]
        full = normed + s["bonus"]
        o_ref[pl.ds(r0, C), sl] = (full[:C] + full[C:]) * g_s[pl.ds(r0, C), sl]

    return (lambda r0: [token_chain(r0)],
            lambda r0, stash: [solve_chain(p, r0, stash) for p in range(A_PAIRS)],
            lambda r0, stash: [state_chain(p, r0, stash) for p in range(A_PAIRS)])


CHUNKS_IN_FLIGHT = 2
N_RWKV_PARAMS = 10
N_GDN_PARAMS = 4
N_RWKV_SCRATCH = 8
N_GDN_SCRATCH = 5


def _mix_prompt_body(pa_ref, qkv_ref, ab_ref, z_ref, *refs):
    refs = list(refs)
    take = lambda n: [refs.pop(0) for _ in range(n)]
    rwkv_prm, gdn_prm = take(N_RWKV_PARAMS), take(N_GDN_PARAMS)
    oa_ref, ob_ref, sfa_ref, sfb_ref, tail_a_ref, tail_b_ref = take(6)
    carry_a, state_a, carry_b, state_b = take(4)
    rwkv_scr, gdn_scr = take(N_RWKV_SCRATCH), take(N_GDN_SCRATCH)
    t = pl.program_id(1)
    tt = pa_ref.shape[0]

    @pl.when(t == 0)
    def _():
        for ref in (carry_a, state_a, carry_b, state_b):
            ref[...] = jnp.zeros_like(ref)

    rwkv_token, rwkv_solve, rwkv_state = _rwkv_prompt_part(pa_ref, *rwkv_prm, oa_ref, carry_a, state_a, *rwkv_scr)
    gdn_token, gdn_solve, gdn_state = _gdn_prompt_part(qkv_ref, ab_ref, z_ref, *gdn_prm, ob_ref, carry_b, state_b,
                                                       *gdn_scr)

    n_chunks = tt // CHUNK
    group = min(CHUNKS_IN_FLIGHT, n_chunks)
    n_groups = n_chunks // group
    stashes = [({}, {}) for _ in range(n_chunks)]
    chunks_of = lambda gi: range(gi * group, (gi + 1) * group) if 0 <= gi < n_groups else ()

    def in_sequence(per_chunk_chains):
        for chains in zip(*per_chunk_chains):
            for chain in chains:
                yield from chain

    for gi in range(n_groups + 2):
        chains = []
        for c in chunks_of(gi):
            chains += rwkv_token(c * CHUNK) + gdn_token(c * CHUNK)
        for c in chunks_of(gi - 1):
            chains += rwkv_solve(c * CHUNK, stashes[c][0]) + gdn_solve(c * CHUNK, stashes[c][1])
        state_chains = [rwkv_state(c * CHUNK, stashes[c][0]) + gdn_state(c * CHUNK, stashes[c][1])
                        for c in chunks_of(gi - 2)]
        if state_chains:
            chains += [in_sequence([per_chunk[i:i + 1] for per_chunk in state_chains])
                       for i in range(len(state_chains[0]))]
        _round_robin(chains)

    @pl.when(t == pl.num_programs(1) - 1)
    def _():
        sfa_ref[0] = state_a[...]
        sfb_ref[0] = state_b[...]
        tail_a_ref[0] = pa_ref[tt - SUBLANES:, :]
        tail_b_ref[0] = qkv_ref[tt - SUBLANES:, :]


def _mix_prompt(pa, qkv, ab, z, B, T, rwkv_params, gdn_params, tt):
    n = B * T
    nt = T // tt
    assert len(rwkv_params) == N_RWKV_PARAMS and len(gdn_params) == N_GDN_PARAMS
    rows = lambda w: pl.BlockSpec((tt, w), lambda b, t: (b * nt + t, 0))
    assert A_PAIRS == B_HEADS and B_HEAD == LANES
    state = lambda: pl.BlockSpec((1, A_PAIRS, LANES, LANES), lambda b, t: (b, 0, 0, 0))
    big = lambda: pltpu.VMEM((tt, A_WIDTH), F32)
    return pl.pallas_call(
        _mix_prompt_body,
        grid=(B, nt),
        in_specs=[rows(A_PROJ), rows(CONV_CH), rows(LANES), rows(B_WIDTH)]
                 + [_const_spec(p.shape) for p in rwkv_params + gdn_params],
        out_specs=[rows(A_WIDTH), rows(B_WIDTH), state(), state(),
                   pl.BlockSpec((1, SUBLANES, A_PROJ), lambda b, t: (b, 0, 0)),
                   pl.BlockSpec((1, SUBLANES, CONV_CH), lambda b, t: (b, 0, 0))],
        out_shape=[jax.ShapeDtypeStruct((n, A_WIDTH), F32), jax.ShapeDtypeStruct((n, B_WIDTH), F32),
                   jax.ShapeDtypeStruct((B, A_PAIRS, LANES, LANES), F32),
                   jax.ShapeDtypeStruct((B, B_HEADS, B_HEAD, B_HEAD), F32),
                   jax.ShapeDtypeStruct((B, SUBLANES, A_PROJ), F32),
                   jax.ShapeDtypeStruct((B, SUBLANES, CONV_CH), F32)],
        scratch_shapes=[pltpu.VMEM((SUBLANES, A_PROJ), F32), pltpu.VMEM((A_PAIRS, LANES, LANES), F32),
                        pltpu.VMEM((SUBLANES, CONV_CH), F32), pltpu.VMEM((B_HEADS, B_HEAD, B_HEAD), F32)]
                       + [big() for _ in range(N_RWKV_SCRATCH + N_GDN_SCRATCH)],
        compiler_params=_cparams(2),
        name="mix_prompt",
    )(pa, qkv, ab, z, *rwkv_params, *gdn_params)


def _first_step_rows(rows, seqs, steps, state_rows, offset=0):
    hist = state_rows.shape[0] // seqs
    r, c = _iota((rows, seqs * hist), 0), _iota((rows, seqs * hist), 1)
    t = r & (steps - 1)
    sel = _one_hot((c == _group(r, steps) * hist + offset + t) & (t < hist - offset))
    return sum(_dot(sel, piece) for piece in _split3(state_rows))


def _last_step_rows(x, seqs, steps, keep):
    r, c = _iota((seqs * keep, seqs * steps), 0), _iota((seqs * keep, seqs * steps), 1)
    i = (c & (steps - 1)) - (steps - keep)
    sel = _one_hot((i >= 0) & (r == _group(c, steps) * keep + i))
    return sum(_dot(sel, piece) for piece in _split3(x))


ROWP = dict(mu_r=0, mu_k=1, mu_v=2, w0=3, a0=4, k_k=5, k_a=6, r_k=7, lnw=8, lnb=9)
ROWP_ROWS = 16
VALUE_GROUP = SUBLANES


def _rwkv_lanes_body(par_ref, pak_ref, pav_ref, paw_ref, pag_ref, shr_ref, shk_ref, shv_ref, shw_ref, shg_ref,
                     pa_ref, s_ref, rowp_ref, shared_ref, w2a_ref, g2_ref,
                     o_ref, sout_ref, shift_out_ref,
                     tr_s, ot_s, *, steps):
    rows = par_ref.shape[0]
    B = rows // steps
    p = pl.program_id(0)
    rp = lambda name: rowp_ref[ROWP[name]:ROWP[name] + 1, :]

    @pl.when(p == 0)
    def _():
        shift_out_ref[...] = _last_step_rows(pa_ref[...], B, steps, 1)

    def lerp(x_ref, first_ref, mu):
        per_step = [x_ref[pl.ds(t, B, stride=steps), :] for t in range(steps)]
        x = jnp.concatenate(per_step, axis=0)
        prev = jnp.concatenate([first_ref[...]] + per_step[:-1], axis=0)
        return x + (prev - x) * mu

    r = lerp(par_ref, shr_ref, rp("mu_r"))
    k = lerp(pak_ref, shk_ref, rp("mu_k"))
    v = lerp(pav_ref, shv_ref, rp("mu_v"))
    wa = lerp(paw_ref, shw_ref, shared_ref[0:1, :])
    gd = lerp(pag_ref, shg_ref, shared_ref[1:2, :])
    lane = _iota((1, LANES), 1)
    lora = _mm(jnp.where(lane < A_RANK_W, jnp.tanh(wa), wa), w2a_ref[...])
    g = _mm(_sigmoid(gd), g2_ref[...])
    w_log = -_softplus(-(rp("w0") + lora[:, :LANES])) - 0.5
    decay = jnp.exp(-jnp.exp(w_log))
    a = _sigmoid(rp("a0") + lora[:, LANES:])
    k_mod = k * (1.0 + (a - 1.0) * rp("k_a"))
    hi, hj = _iota((LANES, LANES), 0), _iota((LANES, LANES), 1)
    pair_ones = _one_hot(_group(hi, A_HEAD) == _group(hj, A_HEAD))
    head_sum = lambda m: _mm_sel(m, pair_ones)
    kk_raw = k * rp("k_k")
    kk = kk_raw * jnp.minimum(lax.rsqrt(head_sum(kk_raw * kk_raw)), 1e12)
    names = ("nkk", "beta", "decay", "k", "r", "v")
    for idx, m in enumerate((-kk, kk * a, decay, k_mod, r, v)):
        for t in range(steps):
            tr_s[idx, t] = m[t * B:(t + 1) * B, :].T
    at = lambda name, t: tr_s.at[names.index(name), t]

    def group(gi, carry):
        j = gi // (A_HEAD // VALUE_GROUP)
        v0 = (gi % (A_HEAD // VALUE_GROUP)) * VALUE_GROUP
        keys = lambda name, t: at(name, t)[pl.ds(pl.multiple_of(j * A_HEAD, A_HEAD), A_HEAD), :]
        v_rows = [at("v", t)[pl.ds(pl.multiple_of(gi * VALUE_GROUP, VALUE_GROUP), VALUE_GROUP), :]
                  for t in range(steps)]
        outs = [[] for _ in range(steps)]
        for i in range(VALUE_GROUP):
            S = s_ref[j, v0 + i]
            for t in range(steps):
                sa = jnp.sum(S * keys("nkk", t), axis=0, keepdims=True)
                S = S * keys("decay", t) + sa * keys("beta", t) + v_rows[t][i:i + 1, :] * keys("k", t)
                outs[t].append(jnp.sum(S * keys("r", t), axis=0, keepdims=True))
            sout_ref[j, v0 + i] = S
        for t in range(steps):
            ot_s[t, pl.ds(pl.multiple_of(gi * VALUE_GROUP, VALUE_GROUP), VALUE_GROUP), :] = _rows_to_tile(outs[t])
        return carry

    lax.fori_loop(0, 2 * A_HEAD // VALUE_GROUP, group, 0)
    o = jnp.concatenate([ot_s[t].T for t in range(steps)], axis=0)
    mean = head_sum(o) * (1.0 / A_HEAD)
    cen = o - mean
    var = head_sum(cen * cen) * (1.0 / A_HEAD)
    o = cen * lax.rsqrt(var + A_LNX_EPS) * rp("lnw") + rp("lnb")
    o = (o + head_sum(r * k_mod * rp("r_k")) * v) * g
    for t in range(steps):
        o_ref[pl.ds(t, B, stride=steps), :] = o[t * B:(t + 1) * B, :]


def _rwkv_lanes(pa, first_row, shift, state_t, layer, steps, tables):
    rowp, shared, w2a_p, g2_p = tables
    B = state_t.shape[-1]
    n = B * steps
    assert first_row % n == 0 and B == LANES and steps & (steps - 1) == 0
    rb = first_row // n
    col = lambda rows_, j, r: pl.BlockSpec((rows_, LANES), lambda p: (r, j(p)))
    groups = [lambda p: p, lambda p: A_PAIRS + p, lambda p: 2 * A_PAIRS + p,
              lambda p: 3 * A_PAIRS, lambda p: 3 * A_PAIRS + 1]
    block = (None, 2, A_HEAD, A_HEAD, B)
    return pl.pallas_call(
        functools.partial(_rwkv_lanes_body, steps=steps),
        grid=(A_PAIRS,),
        in_specs=[col(n, j, rb) for j in groups] + [col(B, j, 0) for j in groups]
                 + [pl.BlockSpec((n, A_PROJ), lambda p: (rb, 0), pipeline_mode=pl.Buffered(1)),
                    pl.BlockSpec(block, lambda p: (layer, p, 0, 0, 0)),
                    pl.BlockSpec((None, ROWP_ROWS, LANES), lambda p: (p, 0, 0)), _const_spec(shared.shape),
                    pl.BlockSpec((None, LANES, 2 * LANES), lambda p: (p, 0, 0)),
                    pl.BlockSpec((None, LANES, LANES), lambda p: (p, 0, 0))],
        out_specs=[pl.BlockSpec((n, LANES), lambda p: (0, p)), pl.BlockSpec(block, lambda p: (0, p, 0, 0, 0)),
                   pl.BlockSpec((B, A_PROJ), lambda p: (0, 0))],
        out_shape=[jax.ShapeDtypeStruct((n, A_WIDTH), F32), jax.ShapeDtypeStruct((1,) + state_t.shape[1:], F32),
                   jax.ShapeDtypeStruct((B, A_PROJ), F32)],
        scratch_shapes=[pltpu.VMEM((6, steps, LANES, B), F32), pltpu.VMEM((steps, LANES, B), F32)],
        compiler_params=_cparams(1),
        name="rwkv_sample",
    )(*([pa] * 5), *([shift] * 5), pa, state_t, rowp, shared, w2a_p, g2_p)


def _rwkv_pair_tables(mu, w0, a0, k_k, k_a, r_k, lnw, lnb, w2, a2, g2):
    per_pair = lambda a: a.reshape(A_PAIRS, 1, LANES)
    rows = {"mu_r": mu[:A_WIDTH], "mu_k": mu[A_WIDTH:2 * A_WIDTH], "mu_v": mu[2 * A_WIDTH:3 * A_WIDTH],
            "w0": w0, "a0": a0, "k_k": k_k, "k_a": k_a, "r_k": r_k.reshape(-1), "lnw": lnw, "lnb": lnb}
    table = jnp.concatenate([per_pair(rows[name].astype(F32)) for name in sorted(ROWP, key=ROWP.get)]
                            + [jnp.zeros((A_PAIRS, ROWP_ROWS - len(ROWP), LANES), F32)], axis=1)
    shared = jnp.concatenate([mu[3 * A_WIDTH:3 * A_WIDTH + LANES].reshape(1, LANES),
                              mu[3 * A_WIDTH + LANES:].reshape(1, LANES),
                              jnp.zeros((SUBLANES - 2, LANES), F32)], axis=0).astype(F32)
    by_pair = lambda w: jnp.transpose(w.astype(F32).reshape(w.shape[0], A_PAIRS, LANES), (1, 0, 2))
    zeros = jnp.zeros((A_PAIRS, A_RANK_W, LANES), F32)
    w2a_p = jnp.concatenate([jnp.concatenate([by_pair(w2), zeros], axis=2),
                             jnp.concatenate([zeros, by_pair(a2)], axis=2)], axis=1)
    return table, shared, w2a_p, by_pair(g2)


def _gdn_qkv(conv):
    c = _silu(conv)
    qs, ks = [], []
    for h in range(B_HEADS):
        q = c[:, h * B_HEAD:(h + 1) * B_HEAD]
        k = c[:, B_WIDTH + h * B_HEAD:B_WIDTH + (h + 1) * B_HEAD]
        qs.append(q * (lax.rsqrt(jnp.sum(q * q, axis=-1, keepdims=True) + 1e-6) * (B_HEAD ** -0.5)))
        ks.append(k * lax.rsqrt(jnp.sum(k * k, axis=-1, keepdims=True) + 1e-6))
    q = jnp.concatenate(qs, axis=1)
    k = jnp.concatenate(ks, axis=1)
    v = c[:, 2 * B_WIDTH:]
    return q, k, v


def _gdn_gates(ab, alog, dtb):
    lane = _iota((1, LANES), 1)
    g = -jnp.exp(alog) * _softplus(ab + dtb)
    beta = _sigmoid(ab)
    gb = jnp.where(lane < B_HEADS, g, beta)
    si, sj = _iota((LANES, 2 * B_WIDTH), 0), _iota((LANES, 2 * B_WIDTH), 1)
    spread = _mm_sel(gb, _one_hot(si == _group(sj, B_HEAD)))
    return spread[:, :B_WIDTH], spread[:, B_WIDTH:]


def _gdn_out(o, norm_w, z):
    return o * lax.rsqrt(jnp.mean(o * o, axis=-1, keepdims=True) + RMS_EPS) * norm_w * _silu(z)


def _gdn_prompt_part(qkv_ref, ab_ref, z_ref, cw_ref, alog_ref, dtb_ref, nw_ref, o_ref,
                     carry_ref, state_ref, q_s, k_s, v_s, gc_s, beta_s):
    tt = qkv_ref.shape[0]
    C = CHUNK
    g, beta = _gdn_gates(ab_ref[...], alog_ref[...], dtb_ref[...])
    ri, ci = _iota((tt, tt), 0), _iota((tt, tt), 1)
    beta_s[...] = beta
    gc_s[...] = _sel_mm(_one_hot((_group(ri, C) == _group(ci, C)) & (ci <= ri)), g)

    def token_chain(r0):
        rows = slice(r0, r0 + C)
        x = qkv_ref[rows, :]
        before = carry_ref[...] if r0 == 0 else qkv_ref[r0 - SUBLANES:r0, :]
        if r0 + C == tt:
            carry_ref[...] = x[C - SUBLANES:, :]
        row8 = _iota((SUBLANES, 1), 0)

        def shift_rows(cur, halo, i):
            down = pltpu.roll(cur, i, axis=0)
            top = jnp.where(row8 < i, pltpu.roll(halo, i, axis=0), down[:SUBLANES])
            return jnp.concatenate([top, down[SUBLANES:]], axis=0)

        assert CONV_K == 4
        c0, c1, c2, c3 = (cw_ref[i:i + 1, :] for i in range(CONV_K))
        x1 = shift_rows(x, before, 1)
        yield
        older = x * c1 + x1 * c0
        older_halo = before * c1 + pltpu.roll(before, 1, axis=0) * c0
        conv = x * c3 + x1 * c2 + shift_rows(older, older_halo, 2)
        yield
        q, k, v = _gdn_qkv(conv)
        q_s[rows, :] = q
        k_s[rows, :] = k
        v_s[rows, :] = v
        yield

    i2, j2 = _iota((2 * C, 2 * C), 0), _iota((2 * C, 2 * C), 1)
    same_head = _group(i2, C) == _group(j2, C)
    strict = same_head & (i2 > j2)
    incl = same_head & (i2 >= j2)
    eye = jnp.where(i2 == j2, 1.0, 0.0)
    first = _iota((2 * C, 1), 0) < C

    def solve_chain(pr, r0, stash):
        sls = [slice(h * B_HEAD, (h + 1) * B_HEAD) for h in (2 * pr, 2 * pr + 1)]
        ld = lambda ref: jnp.concatenate([ref[pl.ds(r0, C), sl] for sl in sls], axis=0)
        q_h, k_h, v_h, gc_h, beta_h = ld(q_s), ld(k_s), ld(v_s), ld(gc_s), ld(beta_s)
        diff = gc_h - gc_h.T
        dm = jnp.where(incl, jnp.exp(jnp.where(incl, diff, 0.0)), 0.0)
        kb = k_h * beta_h
        QK = _mm_nt(jnp.concatenate([kb, q_h], axis=0), k_h)
        yield
        N = -jnp.where(strict, QK[:2 * C] * dm, 0.0)
        qk = QK[2 * C:] * dm
        egc = jnp.exp(gc_h)
        X = jnp.concatenate([v_h * beta_h, kb * egc], axis=1)
        T = yield from _nilpotent_inverse(N, eye)
        UW = _mm(T, X)
        yield
        g_last = jnp.where(first, gc_h[C - 1:C, :], gc_h[2 * C - 1:2 * C, :])
        stash[pr] = dict(u=UW[:, :B_HEAD], w=UW[:, B_HEAD:].astype(BF16), qd=(q_h * egc).astype(BF16),
                         qk=qk.astype(BF16), k_dec_t=(k_h * jnp.exp(g_last - gc_h)).T.astype(BF16),
                         decay=[jnp.exp(gc_h[(j + 1) * C - 1:(j + 1) * C, :]) for j in range(2)])

    def state_chain(pr, r0, stash):
        heads = (2 * pr, 2 * pr + 1)
        sls = [slice(h * B_HEAD, (h + 1) * B_HEAD) for h in heads]
        s = stash[pr]
        wS, qS, S_old = [], [], []
        for j, h in enumerate(heads):
            S = state_ref[h]
            rows = slice(j * C, (j + 1) * C)
            wq = _dot(jnp.concatenate([s["w"][rows], s["qd"][rows]], axis=0), S.astype(BF16))
            wS.append(wq[:C])
            qS.append(wq[C:])
            S_old.append(S)
        yield
        v_new = s["u"] - jnp.concatenate(wS, axis=0)
        o = jnp.concatenate(qS, axis=0) + _dot(s["qk"], v_new.astype(BF16))
        for j, h in enumerate(heads):
            mine = first if j == 0 else jnp.logical_not(first)
            rows = slice(j * C, (j + 1) * C)
            state_ref[h] = S_old[j] * s["decay"][j] + _mm(s["k_dec_t"], jnp.where(mine, v_new, 0.0))
            o_ref[pl.ds(r0, C), sls[j]] = _gdn_out(o[rows], nw_ref[...], z_ref[pl.ds(r0, C), sls[j]])
        yield

    return (lambda r0: [token_chain(r0)],
            lambda r0, stash: [solve_chain(pr, r0, stash) for pr in range(B_HEADS // 2)],
            lambda r0, stash: [state_chain(pr, r0, stash) for pr in range(B_HEADS // 2)])


def _gdn_chunk_body(qkv_ref, hist_ref, ab_ref, z_ref, s_ref, cw_ref, alog_ref, dtb_ref, nw_ref,
                    o_ref, sout_ref, hist_out_ref, *, steps):
    rows = ab_ref.shape[0]
    seqs = rows // steps
    x = qkv_ref[...]
    hist_out_ref[...] = _last_step_rows(x, seqs, steps, CONV_K - 1)
    hist = hist_ref[...]
    t_idx = _iota((rows, 1), 0) & (steps - 1)
    conv = x * cw_ref[CONV_K - 1:CONV_K, :]
    for i in range(1, CONV_K):
        tap = jnp.where(t_idx >= i, pltpu.roll(x, i, axis=0),
                        _first_step_rows(rows, seqs, steps, hist, offset=CONV_K - 1 - i))
        conv = conv + tap * cw_ref[CONV_K - 1 - i:CONV_K - i, :]
    q, k, v = _gdn_qkv(conv)
    g, beta = _gdn_gates(ab_ref[...], alog_ref[...], dtb_ref[...])
    i2, j2 = _iota((rows, rows), 0), _iota((rows, rows), 1)
    same = _group(i2, steps) == _group(j2, steps)
    strict = same & (i2 > j2)
    incl = same & (i2 >= j2)
    eye = jnp.where(i2 == j2, 1.0, 0.0)
    gc = _sel_mm(_one_hot(incl), g)
    g_end = _sel_mm(_one_hot(j2 == (i2 | (steps - 1))), gc)
    row = _iota((rows, 1), 0)
    pair_rows = 2 * SUBLANES
    first_half = (_iota((pair_rows, 1), 0) & (SUBLANES - 1)) < steps
    assert 2 * steps == SUBLANES

    def head_chain(h):
        sl = slice(h * B_HEAD, (h + 1) * B_HEAD)
        q_h, k_h, v_h, gc_h, beta_h, ge_h = q[:, sl], k[:, sl], v[:, sl], gc[:, sl], beta[:, sl], g_end[:, sl]
        dm = jnp.where(incl, jnp.exp(jnp.where(incl, gc_h - gc_h.T, 0.0)), 0.0)
        kb = k_h * beta_h
        QK = _mm_nt(jnp.concatenate([kb, q_h], axis=0), k_h)
        yield
        N = -jnp.where(strict, QK[:rows] * dm, 0.0)
        qk = QK[rows:] * dm
        assert steps == 4
        N2 = _mm(N, N)
        yield
        T = eye + N
        T = T + _mm(T, N2)
        yield
        egc = jnp.exp(gc_h)
        UW = _mm(T, jnp.concatenate([v_h * beta_h, kb * egc], axis=1))
        yield
        w = UW[:, B_HEAD:].astype(BF16)
        qd = (q_h * egc).astype(BF16)
        k_dec_t = (k_h * jnp.exp(ge_h - gc_h)).T.astype(BF16)
        decay = jnp.exp(ge_h)
        wS, qS = [], []
        for m in range(rows // SUBLANES):
            tile = slice(m * SUBLANES, (m + 1) * SUBLANES)
            lhs = jnp.concatenate([w[tile], qd[tile]], axis=0)
            res = jnp.where(first_half, _dot(lhs, s_ref[2 * m, h].astype(BF16)),
                            _dot(lhs, s_ref[2 * m + 1, h].astype(BF16)))
            wS.append(res[:SUBLANES])
            qS.append(res[SUBLANES:])
            yield
        v_new = UW[:, :B_HEAD] - jnp.concatenate(wS, axis=0)
        o = jnp.concatenate(qS, axis=0) + _mm(qk, v_new)
        o_ref[:, sl] = _gdn_out(o, nw_ref[...], z_ref[:, sl])
        yield
        for b in range(seqs):
            mine = _group(row, steps) == b
            sout_ref[b, h] = (s_ref[b, h] * decay[b * steps:b * steps + 1, :]
                              + _mm(k_dec_t, jnp.where(mine, v_new, 0.0)))
            yield

    _round_robin(head_chain(h) for h in range(B_HEADS))


def _gdn_sample(qkv, ab, z, first_row, hist, state, layer, steps, params, seqs):
    nb = state.shape[1]
    n = nb * steps
    rows = seqs * steps
    hist_rows = seqs * (CONV_K - 1)
    assert SUBLANES % steps == 0 and rows % SUBLANES == 0 and hist_rows % SUBLANES == 0 and first_row % rows == 0
    row = lambda w: pl.BlockSpec((rows, w), lambda i: (i + first_row // rows, 0))
    block = (None, seqs, B_HEADS, B_HEAD, B_HEAD)
    sspec = pl.BlockSpec(block, lambda i: (layer, i, 0, 0, 0))
    ospec = pl.BlockSpec(block, lambda i: (0, i, 0, 0, 0))
    return pl.pallas_call(
        functools.partial(_gdn_chunk_body, steps=steps),
        grid=(nb // seqs,),
        in_specs=[row(CONV_CH), pl.BlockSpec((hist_rows, CONV_CH), lambda i: (i, 0)), row(LANES), row(B_WIDTH), sspec]
                 + [_const_spec(p.shape) for p in params],
        out_specs=[pl.BlockSpec((rows, B_WIDTH), lambda i: (i, 0)), ospec,
                   pl.BlockSpec((hist_rows, CONV_CH), lambda i: (i, 0))],
        out_shape=[jax.ShapeDtypeStruct((n, B_WIDTH), F32), jax.ShapeDtypeStruct((1,) + state.shape[1:], F32),
                   jax.ShapeDtypeStruct(hist.shape, F32)],
        compiler_params=_cparams(1),
        name="gdn_sample",
    )(qkv, hist, ab, z, state, *params)


def _cuts(widths):
    edges, total = [], 0
    for w in widths[:-1]:
        total += w
        edges.append(total)
    return edges


def _regroup_pa(a):
    r, wd, k, v, ad, gd = jnp.split(a, _cuts((A_WIDTH, A_RANK_W, A_WIDTH, A_WIDTH, A_RANK_A, A_RANK_G)), axis=-1)
    return jnp.concatenate([r, k, v, wd, ad, gd], axis=-1)


def _ungroup_pa(a):
    r, k, v, wd, ad, gd = jnp.split(a, _cuts((A_WIDTH, A_WIDTH, A_WIDTH, A_RANK_W, A_RANK_A, A_RANK_G)), axis=-1)
    return jnp.concatenate([r, wd, k, v, ad, gd], axis=-1)


def _token_tile(n, want):
    tm = want
    while n % tm:
        tm //= 2
    return tm


def kernel(x_prompt, x_sample, state_rwkv, state_rwkv_shift, state_delta, state_conv, ffn1_norm, ffn1_w_gate, ffn1_w_up, ffn1_w_down, mix_norm, w_in, rwkv_mu, rwkv_w0, rwkv_w2, rwkv_a0, rwkv_a2, rwkv_g2, rwkv_k_k, rwkv_k_a, rwkv_r_k, rwkv_lnx_w, rwkv_lnx_b, gdn_conv_w, gdn_A_log, gdn_dt_bias, gdn_norm_w, proj_a, proj_b, w_out, ffn2_norm, ffn2_w_gate, ffn2_w_up, ffn2_w_down, final_norm):
    depth = ffn1_norm.shape[0]
    assert depth == 1, "single-layer trunk"
    Bp, Tp, _ = x_prompt.shape
    Bs, Ts, _ = x_sample.shape
    l = 0
    row = lambda a: a.reshape(1, -1).astype(F32)

    wi = w_in[l].astype(BF16)
    o_b = A_PROJ
    proj_w = (_regroup_pa(wi[:, :A_PROJ]),
              wi[:, o_b:o_b + CONV_CH + LANES],
              wi[:, o_b + CONV_CH + 2 * B_HEADS:])
    ffn1 = (row(ffn1_norm[l]), ffn1_w_gate[l].astype(BF16), ffn1_w_up[l].astype(BF16), ffn1_w_down[l].astype(BF16))
    ffn2 = (row(ffn2_norm[l]), ffn2_w_gate[l].astype(BF16), ffn2_w_up[l].astype(BF16), ffn2_w_down[l].astype(BF16))
    merge_w = (proj_a[l].astype(BF16), proj_b[l].astype(BF16), w_out[l].astype(BF16))
    zw = jnp.zeros((A_RANK_W, A_WIDTH), F32)
    w2a = jnp.concatenate([jnp.concatenate([rwkv_w2[l], zw], axis=1),
                           jnp.concatenate([zw, rwkv_a2[l]], axis=1)], axis=0)
    rwkv_params = (row(_regroup_pa(rwkv_mu[l])), row(rwkv_w0[l]), row(rwkv_a0[l]), row(rwkv_k_k[l]), row(rwkv_k_a[l]),
                   row(rwkv_r_k[l]), row(rwkv_lnx_w[l]), row(rwkv_lnx_b[l]), w2a, rwkv_g2[l].astype(F32))
    pad_lane = lambda a: jnp.pad(a.reshape(1, -1).astype(F32), ((0, 0), (0, LANES - a.size)))
    gdn_params = (gdn_conv_w[l].astype(F32), pad_lane(gdn_A_log[l]), pad_lane(gdn_dt_bias[l]), row(gdn_norm_w[l]))

    n_p, n_s = Bp * Tp, Bs * Ts
    tm = _token_tile(n_s, _token_tile(n_p, 512))
    h = _ffn(x_prompt.reshape(n_p, D_MODEL), x_sample.reshape(n_s, D_MODEL), *ffn1, tm=tm)
    pa, qkv, z, gates, ab = _proj(h, row(mix_norm[l]), *proj_w, tm=tm)

    def trunk_back(oa, ob, first_row):
        return _tail(h, oa, ob, gates, *merge_w, *ffn2, row(final_norm), tm=tm, first_row=first_row)

    tt = _token_tile(Tp, 256)
    oa, ob, s_pairs, delta_p, pa_tail, qkv_tail = _mix_prompt(pa, qkv, ab, z, Bp, Tp, rwkv_params, gdn_params, tt)
    y_prompt = trunk_back(oa, ob, 0).reshape(Bp, Tp, D_MODEL)
    sp = s_pairs.reshape(Bp, A_PAIRS, 2, A_HEAD, 2, A_HEAD)
    rwkv_p = jnp.stack([sp[:, :, 0, :, 0], sp[:, :, 1, :, 1]], axis=2).reshape(Bp, A_HEADS, A_HEAD, A_HEAD)
    shift_p = _ungroup_pa(pa_tail[:, -1])
    conv_p = qkv_tail[:, SUBLANES - (CONV_K - 1):]

    assert Ts & (Ts - 1) == 0 and Ts >= CONV_K - 1, "sample steps: power of two covering the conv history"
    tables = _rwkv_pair_tables(_regroup_pa(rwkv_mu[l]), rwkv_w0[l], rwkv_a0[l], rwkv_k_k[l], rwkv_k_a[l], rwkv_r_k[l],
                               rwkv_lnx_w[l], rwkv_lnx_b[l], rwkv_w2[l], rwkv_a2[l], rwkv_g2[l])
    oa, s_lanes, last_pa = _rwkv_lanes(pa, n_p, _regroup_pa(state_rwkv_shift[l].astype(F32)),
                                       jnp.transpose(state_rwkv.astype(F32), (0, 2, 3, 4, 1)), l, Ts, tables)
    rwkv_s = jnp.transpose(s_lanes, (0, 4, 1, 2, 3))
    shift_s = _ungroup_pa(last_pa)
    hist = state_conv[l].astype(F32).reshape(Bs * (CONV_K - 1), CONV_CH)
    ob, delta_s, new_hist = _gdn_sample(qkv, ab, z, n_p, hist, state_delta.astype(F32), l, Ts, gdn_params,
                                        seqs=_token_tile(Bs, 32))
    conv_s = new_hist.reshape(Bs, CONV_K - 1, CONV_CH)
    y_sample = trunk_back(oa, ob, n_p).reshape(Bs, Ts, D_MODEL)

    add_depth = lambda a: a[None]
    return (y_prompt, y_sample,
            add_depth(rwkv_p), add_depth(shift_p), add_depth(delta_p), add_depth(conv_p),
            rwkv_s, add_depth(shift_s), delta_s, add_depth(conv_s))
```

```python
import functools

import jax
import jax.numpy as jnp
from jax import lax
from jax.experimental import pallas as pl
from jax.experimental.pallas import tpu as pltpu

F32 = jnp.float32
BF16 = jnp.bfloat16

D_MODEL = 1024
D_FF = 2816
RMS_EPS = 1e-6
A_HEAD = 64
A_HEADS = 8
A_WIDTH = A_HEADS * A_HEAD
A_RANK_W = 64
A_RANK_A = 64
A_RANK_G = 128
A_PROJ = 3 * A_WIDTH + A_RANK_W + A_RANK_A + A_RANK_G
A_LNX_EPS = 64e-5
A_PAIRS = A_HEADS // 2
B_HEADS = 4
B_HEAD = 128
B_WIDTH = B_HEADS * B_HEAD
CONV_K = 4
CONV_CH = 3 * B_WIDTH
B_PROJ = CONV_CH + 2 * B_HEADS + B_WIDTH
GATE_COLS = 2 * D_MODEL
LANES = 128
SUBLANES = 8
MXU_DIM = 256
VMEM_LIMIT_BYTES = 56 * 1024 * 1024
CHUNK = 64
PA_R, PA_K, PA_V, PA_WA, PA_G = 0, A_WIDTH, 2 * A_WIDTH, 3 * A_WIDTH, 3 * A_WIDTH + A_RANK_W + A_RANK_A
PROJ_SPLITS = (A_PROJ, CONV_CH, B_WIDTH, GATE_COLS, LANES)


def _cparams(n_grid_dims):
    return pltpu.CompilerParams(dimension_semantics=("arbitrary",) * n_grid_dims,
                                vmem_limit_bytes=VMEM_LIMIT_BYTES)


def _const_spec(shape):
    nd = len(shape)
    return pl.BlockSpec(shape, lambda *_: (0,) * nd, pipeline_mode=pl.Buffered(1))


def _dot(a, b):
    return jnp.dot(a, b, preferred_element_type=F32)


def _dot_nt(a, b):
    return lax.dot_general(a, b, (((1,), (1,)), ((), ())), preferred_element_type=F32)


def _split3(x):
    hi = x.astype(BF16)
    rest = x - hi.astype(F32)
    mid = rest.astype(BF16)
    lo = (rest - mid.astype(F32)).astype(BF16)
    return hi, mid, lo


def _mm(a, b):
    return _dot(a.astype(BF16), b.astype(BF16))


def _mm_nt(a, b):
    return _dot_nt(a.astype(BF16), b.astype(BF16))


def _sel_mm(sel, x):
    return _dot(jnp.concatenate([sel, sel, sel], axis=1), jnp.concatenate(_split3(x), axis=0))


def _mm_sel(x, sel):
    return _dot(jnp.concatenate(_split3(x), axis=1), jnp.concatenate([sel, sel, sel], axis=0))


INV_BASE = 8


def _nilpotent_inverse(n, eye):
    width = n.shape[1]
    bi, bj = _iota(n.shape, 0), _iota(n.shape, 1)
    same = lambda size: _group(bi, size) == _group(bj, size)
    d = jnp.where(same(INV_BASE), n, 0.0)
    t = eye + d
    d = _mm(d, d)
    yield
    for _ in range(INV_BASE.bit_length() - 3):
        both = _mm(d, jnp.concatenate([t, d], axis=1))
        yield
        t = t + both[:, :width]
        d = both[:, width:]
    t = t + _mm(d, t)
    yield
    size = INV_BASE
    while size < CHUNK:
        coupling = jnp.where(same(2 * size) & jnp.logical_not(same(size)), n, 0.0)
        tb = t.astype(BF16)
        lt = _mm(coupling, tb)
        yield
        t = t + _mm(tb, lt)
        yield
        size *= 2
    return t


def _round_robin(chains):
    chains = list(chains)
    while chains:
        for chain in list(chains):
            try:
                next(chain)
            except StopIteration:
                chains.remove(chain)


def _rms(x, w):
    return x * lax.rsqrt(jnp.mean(x * x, axis=-1, keepdims=True) + RMS_EPS) * w


def _sigmoid(x):
    return 1.0 / (1.0 + jnp.exp(-x))


def _silu(x):
    return x * _sigmoid(x)


def _softplus(x):
    return jnp.maximum(x, 0.0) + jnp.log(1.0 + jnp.exp(-jnp.abs(x)))


def _iota(shape, dim):
    return lax.broadcasted_iota(jnp.int32, shape, dim)


def _group(idx, size):
    assert size & (size - 1) == 0
    return lax.shift_right_logical(idx, size.bit_length() - 1)


def _one_hot(cond):
    return jnp.where(cond, 1.0, 0.0).astype(BF16)


def _rows_to_tile(rows):
    rid = _iota((SUBLANES, 1), 0)
    tile = jnp.zeros((SUBLANES, rows[0].shape[1]), F32)
    for i, row in enumerate(rows):
        tile = jnp.where(rid == i, row, tile)
    return tile


def _swiglu_half_step(x, nw, wg_ref, wu_ref, wd_ref, slabs=(D_FF,)):
    xn = _rms(x, nw).astype(BF16)
    assert sum(slabs) == D_FF and all(w % MXU_DIM == 0 for w in slabs)
    y = None
    start = 0
    for width in slabs:
        cols = slice(start, start + width)
        start += width
        g = _dot(xn, wg_ref[:, cols])
        u = _dot(xn, wu_ref[:, cols])
        act = (_silu(g) * u).astype(BF16)
        part = _dot(act, wd_ref[cols, :])
        y = part if y is None else y + part
    return x + 0.5 * y


def _head_body(xa_ref, xb_ref, nw1_ref, wg_ref, wu_ref, wd_ref, nw2_ref, wpa_ref, wqa_ref, wzg_ref,
               h_ref, pa_ref, qkv_ref, z_ref, gates_ref, ab_ref, *, blocks_a):
    x = jnp.where(pl.program_id(0) < blocks_a, xa_ref[...], xb_ref[...])
    h = _swiglu_half_step(x, nw1_ref[...], wg_ref, wu_ref, wd_ref, slabs=TAIL_FF_SLABS)
    h_ref[...] = h
    u = _rms(h, nw2_ref[...]).astype(BF16)
    pa_ref[...] = _dot(u, wpa_ref[...])
    qkv_ref[...] = _dot(u, wqa_ref[:, :CONV_CH])
    ab_ref[...] = _dot(u, wqa_ref[:, CONV_CH:])
    z_ref[...] = _dot(u, wzg_ref[:, :B_WIDTH])
    gates_ref[...] = _dot(u, wzg_ref[:, B_WIDTH:])


def _head(xa, xb, nw1, wg, wu, wd, nw2, w_pa, w_qa, w_zg, tm):
    na, nb = xa.shape[0], xb.shape[0]
    assert na % tm == 0 and nb % tm == 0
    assert w_qa.shape[1] == CONV_CH + LANES and w_zg.shape[1] == B_WIDTH + GATE_COLS
    blocks_a = na // tm
    n = na + nb
    widths = (D_MODEL,) + PROJ_SPLITS
    return pl.pallas_call(
        functools.partial(_head_body, blocks_a=blocks_a),
        grid=(n // tm,),
        in_specs=[pl.BlockSpec((tm, D_MODEL), lambda i: (jnp.minimum(i, blocks_a - 1), 0)),
                  pl.BlockSpec((tm, D_MODEL), lambda i: (jnp.maximum(i - blocks_a, 0), 0)),
                  _const_spec((1, D_MODEL)),
                  _const_spec((D_MODEL, D_FF)), _const_spec((D_MODEL, D_FF)), _const_spec((D_FF, D_MODEL)),
                  _const_spec((1, D_MODEL)), _const_spec(w_pa.shape), _const_spec(w_qa.shape), _const_spec(w_zg.shape)],
        out_specs=[pl.BlockSpec((tm, w), lambda i: (i, 0)) for w in widths],
        out_shape=[jax.ShapeDtypeStruct((n, w), F32) for w in widths],
        compiler_params=_cparams(1),
        name="head",
    )(xa, xb, nw1, wg, wu, wd, nw2, w_pa, w_qa, w_zg)


TAIL_FF_SLABS = (6 * MXU_DIM, 5 * MXU_DIM)


def _tail_body(h_ref, oa_ref, ob_ref, gates_ref, pa_ref, pb_ref, wo_ref, nw_ref, wg_ref, wu_ref, wd_ref,
               fn_ref, o_ref):
    ma = _dot(oa_ref[...].astype(BF16), pa_ref[...])
    mb = _dot(ob_ref[...].astype(BF16), pb_ref[...])
    merged = _sigmoid(gates_ref[:, :D_MODEL]) * ma + _sigmoid(gates_ref[:, D_MODEL:]) * mb
    h = h_ref[...] + _dot(merged.astype(BF16), wo_ref[...])
    h = _swiglu_half_step(h, nw_ref[...], wg_ref, wu_ref, wd_ref, slabs=TAIL_FF_SLABS)
    o_ref[...] = _rms(h, fn_ref[...])


def _tail(h, oa, ob, gates, proj_a, proj_b, w_out, nw, wg, wu, wd, fn, tm, first_row):
    n = oa.shape[0]
    assert n % tm == 0 and first_row % tm == 0
    row = lambda w: pl.BlockSpec((tm, w), lambda i: (i, 0))
    stream = lambda w: pl.BlockSpec((tm, w), lambda i: (i + first_row // tm, 0))
    return pl.pallas_call(
        _tail_body,
        grid=(n // tm,),
        in_specs=[stream(D_MODEL), row(A_WIDTH), row(B_WIDTH), stream(GATE_COLS),
                  _const_spec((A_WIDTH, D_MODEL)), _const_spec((B_WIDTH, D_MODEL)),
                  _const_spec((D_MODEL, D_MODEL)), _const_spec((1, D_MODEL)),
                  _const_spec((D_MODEL, D_FF)), _const_spec((D_MODEL, D_FF)), _const_spec((D_FF, D_MODEL)),
                  _const_spec((1, D_MODEL))],
        out_specs=row(D_MODEL),
        out_shape=jax.ShapeDtypeStruct((n, D_MODEL), F32),
        compiler_params=_cparams(1),
        name="tail",
    )(h, oa, ob, gates, proj_a, proj_b, w_out, nw, wg, wu, wd, fn)


def _rwkv_token_math(x, prev, mu, w0, a0, k_k, k_a, w2a, g2):
    pm = x + (prev - x) * mu
    r = pm[:, PA_R:PA_R + A_WIDTH]
    k = pm[:, PA_K:PA_K + A_WIDTH]
    v = pm[:, PA_V:PA_V + A_WIDTH]
    wa = pm[:, PA_WA:PA_WA + LANES]
    gd = pm[:, PA_G:PA_G + A_RANK_G]
    lane = _iota((1, LANES), 1)
    lora_in = jnp.where(lane < A_RANK_W, jnp.tanh(wa), wa)
    lora = _mm(lora_in, w2a)
    g = _mm(_sigmoid(gd), g2)
    yield
    w_log = -_softplus(-(w0 + lora[:, :A_WIDTH])) - 0.5
    log_decay = -jnp.exp(w_log)
    yield
    a = _sigmoid(a0 + lora[:, A_WIDTH:])
    kk_raw = k * k_k
    k_mod = k * (1.0 + (a - 1.0) * k_a)
    return r, k_mod, v, kk_raw, a, log_decay, g


def _pair_mask(rows_per_head):
    shape = (2 * rows_per_head, LANES)
    return _group(_iota(shape, 0), rows_per_head) == _group(_iota(shape, 1), A_HEAD)


def _rwkv_prompt_part(pa_ref, mu_ref, w0_ref, a0_ref, kk_ref, ka_ref, rk_ref, lnw_ref, lnb_ref, w2a_ref, g2_ref,
                      o_ref, carry_ref, state_ref, r_s, k_s, v_s, kkraw_s, a_s, cum_s, ld_s, g_s):
    tt = pa_ref.shape[0]
    C = CHUNK
    lower = _one_hot(_iota((C, C), 1) <= _iota((C, C), 0))

    def token_chain(r0):
        rows = slice(r0, r0 + C)
        x = pa_ref[rows, :]
        before = carry_ref[SUBLANES - 1:SUBLANES, :] if r0 == 0 else pa_ref[r0 - 1:r0, :]
        prev = jnp.where(_iota((C, 1), 0) == 0, before, pltpu.roll(x, 1, axis=0))
        if r0 + C == tt:
            carry_ref[...] = x[C - SUBLANES:, :]
        r, k_mod, v, kk_raw, a, log_decay, g = yield from _rwkv_token_math(
            x, prev, mu_ref[...], w0_ref[...], a0_ref[...], kk_ref[...], ka_ref[...], w2a_ref[...], g2_ref[...])
        r_s[rows, :] = r
        k_s[rows, :] = k_mod
        v_s[rows, :] = v
        kkraw_s[rows, :] = kk_raw
        a_s[rows, :] = a
        g_s[rows, :] = g
        ld_s[rows, :] = log_decay
        yield
        cum_s[rows, :] = sum(_dot(lower, piece) for piece in _split3(log_decay))
        yield

    mask = _pair_mask(C)
    i2, j2 = _iota((2 * C, 2 * C), 0), _iota((2 * C, 2 * C), 1)
    strict = i2 > j2
    incl = i2 >= j2
    eye = jnp.where(i2 == j2, 1.0, 0.0)
    dup = lambda m: jnp.concatenate([m, m], axis=0)
    stack = lambda m: jnp.where(mask, dup(m), 0.0)
    lnb_stacked = [jnp.where(mask, lnb_ref[:, p * LANES:(p + 1) * LANES], 0.0) for p in range(A_PAIRS)]

    def solve_chain(p, r0, stash):
        sl = slice(p * LANES, (p + 1) * LANES)
        ld = lambda ref: ref[pl.ds(r0, C), sl]
        r_p, k_p, v_p, a_p, cum, ldec = ld(r_s), ld(k_s), ld(v_s), ld(a_s), ld(cum_s), ld(ld_s)
        einc = jnp.exp(cum)
        eex = jnp.exp(cum - ldec)
        einv = jnp.exp(-cum)
        etail = jnp.exp(cum[C - 1:C, :] - cum)
        kks = stack(ld(kkraw_s))
        kks = kks * jnp.minimum(lax.rsqrt(jnp.sum(kks * kks, axis=-1, keepdims=True)), 1e12)
        As = kks * dup(-eex)
        Bs = kks * dup(a_p * einv)
        Bh = kks * dup(a_p * etail)
        Ks = stack(k_p * einv)
        Kh = stack(k_p * etail)
        Rs = stack(r_p * einc)
        Vs = stack(v_p)
        AR = jnp.concatenate([As, Rs], axis=0).astype(BF16)
        Vb = Vs.astype(BF16)
        G = _mm_nt(AR, jnp.concatenate([Bs, Ks], axis=0))
        yield
        Aab = jnp.where(strict, G[:2 * C, :2 * C], 0.0)
        Aak = jnp.where(strict, G[:2 * C, 2 * C:], 0.0)
        Arb = jnp.where(incl, G[2 * C:, :2 * C], 0.0)
        Ark = jnp.where(incl, G[2 * C:, 2 * C:], 0.0)
        Y = _mm(Aak, Vb)
        yield
        T = yield from _nilpotent_inverse(Aab, eye)
        WU = _mm(T, jnp.concatenate([AR[:2 * C], Y.astype(BF16)], axis=1))
        yield
        bonus = jnp.sum(stack(r_p * k_p * rk_ref[:, sl]), axis=-1, keepdims=True) * Vs
        stash[p] = dict(WU=WU, R=AR[2 * C:], Vs=Vs, bonus=bonus,
                        Aro=jnp.concatenate([Arb, Ark], axis=1).astype(BF16),
                        BKh=jnp.concatenate([Bh, Kh], axis=0).astype(BF16), decay=einc[C - 1:C, :])

    def state_chain(p, r0, stash):
        sl = slice(p * LANES, (p + 1) * LANES)
        s = stash[p]
        S = state_ref[p]
        Sb = S.astype(BF16)
        W = _mm_nt(s["WU"][:, :LANES], Sb) + s["WU"][:, LANES:]
        yield
        WV = jnp.concatenate([W, s["Vs"]], axis=0)
        O = _dot_nt(s["R"], Sb) + _mm(s["Aro"], WV)
        state_ref[p] = S * s["decay"] + _mm(WV.T, s["BKh"])
        yield
        mean = jnp.sum(O, axis=-1, keepdims=True) * (1.0 / A_HEAD)
        cen = jnp.where(mask, O - mean, 0.0)
        var = jnp.sum(cen * cen, axis=-1, keepdims=True) * (1.0 / A_HEAD)
        normed = cen * lax.rsqrt(var + A_LNX_EPS) * lnw_ref[:, sl] + lnb_stacked[p]
        full = normed + s["bonus"]
        o_ref[pl.ds(r0, C), sl] = (full[:C] + full[C:]) * g_s[pl.ds(r0, C), sl]

    return (lambda r0: [token_chain(r0)],
            lambda r0, stash: [solve_chain(p, r0, stash) for p in range(A_PAIRS)],
            lambda r0, stash: [state_chain(p, r0, stash) for p in range(A_PAIRS)])


CHUNKS_IN_FLIGHT = 2
N_RWKV_PARAMS = 10
N_GDN_PARAMS = 4
N_RWKV_SCRATCH = 8
N_GDN_SCRATCH = 5


def _mix_prompt_body(pa_ref, qkv_ref, ab_ref, z_ref, *refs):
    refs = list(refs)
    take = lambda n: [refs.pop(0) for _ in range(n)]
    rwkv_prm, gdn_prm = take(N_RWKV_PARAMS), take(N_GDN_PARAMS)
    oa_ref, ob_ref, sfa_ref, sfb_ref, tail_a_ref, tail_b_ref = take(6)
    carry_a, state_a, carry_b, state_b = take(4)
    rwkv_scr, gdn_scr = take(N_RWKV_SCRATCH), take(N_GDN_SCRATCH)
    t = pl.program_id(1)
    tt = pa_ref.shape[0]

    @pl.when(t == 0)
    def _():
        for ref in (carry_a, state_a, carry_b, state_b):
            ref[...] = jnp.zeros_like(ref)

    rwkv_token, rwkv_solve, rwkv_state = _rwkv_prompt_part(pa_ref, *rwkv_prm, oa_ref, carry_a, state_a, *rwkv_scr)
    gdn_token, gdn_solve, gdn_state = _gdn_prompt_part(qkv_ref, ab_ref, z_ref, *gdn_prm, ob_ref, carry_b, state_b,
                                                       *gdn_scr)

    n_chunks = tt // CHUNK
    group = min(CHUNKS_IN_FLIGHT, n_chunks)
    n_groups = n_chunks // group
    stashes = [({}, {}) for _ in range(n_chunks)]
    chunks_of = lambda gi: range(gi * group, (gi + 1) * group) if 0 <= gi < n_groups else ()

    def in_sequence(per_chunk_chains):
        for chains in zip(*per_chunk_chains):
            for chain in chains:
                yield from chain

    for gi in range(n_groups + 2):
        chains = []
        for c in chunks_of(gi):
            chains += rwkv_token(c * CHUNK) + gdn_token(c * CHUNK)
        for c in chunks_of(gi - 1):
            chains += rwkv_solve(c * CHUNK, stashes[c][0]) + gdn_solve(c * CHUNK, stashes[c][1])
        state_chains = [rwkv_state(c * CHUNK, stashes[c][0]) + gdn_state(c * CHUNK, stashes[c][1])
                        for c in chunks_of(gi - 2)]
        if state_chains:
            chains += [in_sequence([per_chunk[i:i + 1] for per_chunk in state_chains])
                       for i in range(len(state_chains[0]))]
        _round_robin(chains)

    @pl.when(t == pl.num_programs(1) - 1)
    def _():
        sfa_ref[0] = state_a[...]
        sfb_ref[0] = state_b[...]
        tail_a_ref[0] = pa_ref[tt - SUBLANES:, :]
        tail_b_ref[0] = qkv_ref[tt - SUBLANES:, :]


def _mix_prompt(pa, qkv, ab, z, B, T, rwkv_params, gdn_params, tt):
    n = B * T
    nt = T // tt
    assert len(rwkv_params) == N_RWKV_PARAMS and len(gdn_params) == N_GDN_PARAMS
    rows = lambda w: pl.BlockSpec((tt, w), lambda b, t: (b * nt + t, 0))
    assert A_PAIRS == B_HEADS and B_HEAD == LANES
    state = lambda: pl.BlockSpec((1, A_PAIRS, LANES, LANES), lambda b, t: (b, 0, 0, 0))
    big = lambda: pltpu.VMEM((tt, A_WIDTH), F32)
    return pl.pallas_call(
        _mix_prompt_body,
        grid=(B, nt),
        in_specs=[rows(A_PROJ), rows(CONV_CH), rows(LANES), rows(B_WIDTH)]
                 + [_const_spec(p.shape) for p in rwkv_params + gdn_params],
        out_specs=[rows(A_WIDTH), rows(B_WIDTH), state(), state(),
                   pl.BlockSpec((1, SUBLANES, A_PROJ), lambda b, t: (b, 0, 0)),
                   pl.BlockSpec((1, SUBLANES, CONV_CH), lambda b, t: (b, 0, 0))],
        out_shape=[jax.ShapeDtypeStruct((n, A_WIDTH), F32), jax.ShapeDtypeStruct((n, B_WIDTH), F32),
                   jax.ShapeDtypeStruct((B, A_PAIRS, LANES, LANES), F32),
                   jax.ShapeDtypeStruct((B, B_HEADS, B_HEAD, B_HEAD), F32),
                   jax.ShapeDtypeStruct((B, SUBLANES, A_PROJ), F32),
                   jax.ShapeDtypeStruct((B, SUBLANES, CONV_CH), F32)],
        scratch_shapes=[pltpu.VMEM((SUBLANES, A_PROJ), F32), pltpu.VMEM((A_PAIRS, LANES, LANES), F32),
                        pltpu.VMEM((SUBLANES, CONV_CH), F32), pltpu.VMEM((B_HEADS, B_HEAD, B_HEAD), F32)]
                       + [big() for _ in range(N_RWKV_SCRATCH + N_GDN_SCRATCH)],
        compiler_params=_cparams(2),
        name="mix_prompt",
    )(pa, qkv, ab, z, *rwkv_params, *gdn_params)


def _first_step_rows(rows, seqs, steps, state_rows, offset=0):
    hist = state_rows.shape[0] // seqs
    r, c = _iota((rows, seqs * hist), 0), _iota((rows, seqs * hist), 1)
    t = r & (steps - 1)
    sel = _one_hot((c == _group(r, steps) * hist + offset + t) & (t < hist - offset))
    return sum(_dot(sel, piece) for piece in _split3(state_rows))


def _last_step_rows(x, seqs, steps, keep):
    r, c = _iota((seqs * keep, seqs * steps), 0), _iota((seqs * keep, seqs * steps), 1)
    i = (c & (steps - 1)) - (steps - keep)
    sel = _one_hot((i >= 0) & (r == _group(c, steps) * keep + i))
    return sum(_dot(sel, piece) for piece in _split3(x))


ROWP = dict(mu_r=0, mu_k=1, mu_v=2, w0=3, a0=4, k_k=5, k_a=6, r_k=7, lnw=8, lnb=9)
ROWP_ROWS = 16
VALUE_GROUP = SUBLANES


def _rwkv_lanes_body(par_ref, pak_ref, pav_ref, paw_ref, pag_ref, shr_ref, shk_ref, shv_ref, shw_ref, shg_ref,
                     pa_ref, s_ref, rowp_ref, shared_ref, w2a_ref, g2_ref,
                     o_ref, sout_ref, shift_out_ref,
                     tr_s, ot_s, *, steps):
    rows = par_ref.shape[0]
    B = rows // steps
    p = pl.program_id(0)
    rp = lambda name: rowp_ref[ROWP[name]:ROWP[name] + 1, :]

    @pl.when(p == 0)
    def _():
        shift_out_ref[...] = _last_step_rows(pa_ref[...], B, steps, 1)

    def lerp(x_ref, first_ref, mu):
        per_step = [x_ref[pl.ds(t, B, stride=steps), :] for t in range(steps)]
        x = jnp.concatenate(per_step, axis=0)
        prev = jnp.concatenate([first_ref[...]] + per_step[:-1], axis=0)
        return x + (prev - x) * mu

    r = lerp(par_ref, shr_ref, rp("mu_r"))
    k = lerp(pak_ref, shk_ref, rp("mu_k"))
    v = lerp(pav_ref, shv_ref, rp("mu_v"))
    wa = lerp(paw_ref, shw_ref, shared_ref[0:1, :])
    gd = lerp(pag_ref, shg_ref, shared_ref[1:2, :])
    lane = _iota((1, LANES), 1)
    lora = _mm(jnp.where(lane < A_RANK_W, jnp.tanh(wa), wa), w2a_ref[...])
    g = _mm(_sigmoid(gd), g2_ref[...])
    w_log = -_softplus(-(rp("w0") + lora[:, :LANES])) - 0.5
    decay = jnp.exp(-jnp.exp(w_log))
    a = _sigmoid(rp("a0") + lora[:, LANES:])
    k_mod = k * (1.0 + (a - 1.0) * rp("k_a"))
    hi, hj = _iota((LANES, LANES), 0), _iota((LANES, LANES), 1)
    pair_ones = _one_hot(_group(hi, A_HEAD) == _group(hj, A_HEAD))
    head_sum = lambda m: _mm_sel(m, pair_ones)
    kk_raw = k * rp("k_k")
    kk = kk_raw * jnp.minimum(lax.rsqrt(head_sum(kk_raw * kk_raw)), 1e12)
    names = ("nkk", "beta", "decay", "k", "r", "v")
    for idx, m in enumerate((-kk, kk * a, decay, k_mod, r, v)):
        for t in range(steps):
            tr_s[idx, t] = m[t * B:(t + 1) * B, :].T
    at = lambda name, t: tr_s.at[names.index(name), t]

    def group(gi, carry):
        j = gi // (A_HEAD // VALUE_GROUP)
        v0 = (gi % (A_HEAD // VALUE_GROUP)) * VALUE_GROUP
        keys = lambda name, t: at(name, t)[pl.ds(pl.multiple_of(j * A_HEAD, A_HEAD), A_HEAD), :]
        v_rows = [at("v", t)[pl.ds(pl.multiple_of(gi * VALUE_GROUP, VALUE_GROUP), VALUE_GROUP), :]
                  for t in range(steps)]
        outs = [[] for _ in range(steps)]
        for i in range(VALUE_GROUP):
            S = s_ref[j, v0 + i]
            for t in range(steps):
                sa = jnp.sum(S * keys("nkk", t), axis=0, keepdims=True)
                S = S * keys("decay", t) + sa * keys("beta", t) + v_rows[t][i:i + 1, :] * keys("k", t)
                outs[t].append(jnp.sum(S * keys("r", t), axis=0, keepdims=True))
            sout_ref[j, v0 + i] = S
        for t in range(steps):
            ot_s[t, pl.ds(pl.multiple_of(gi * VALUE_GROUP, VALUE_GROUP), VALUE_GROUP), :] = _rows_to_tile(outs[t])
        return carry

    lax.fori_loop(0, 2 * A_HEAD // VALUE_GROUP, group, 0)
    o = jnp.concatenate([ot_s[t].T for t in range(steps)], axis=0)
    mean = head_sum(o) * (1.0 / A_HEAD)
    cen = o - mean
    var = head_sum(cen * cen) * (1.0 / A_HEAD)
    o = cen * lax.rsqrt(var + A_LNX_EPS) * rp("lnw") + rp("lnb")
    o = (o + head_sum(r * k_mod * rp("r_k")) * v) * g
    for t in range(steps):
        o_ref[pl.ds(t, B, stride=steps), :] = o[t * B:(t + 1) * B, :]


def _rwkv_lanes(pa, first_row, shift, state_t, layer, steps, tables):
    rowp, shared, w2a_p, g2_p = tables
    B = state_t.shape[-1]
    n = B * steps
    assert first_row % n == 0 and B == LANES and steps & (steps - 1) == 0
    rb = first_row // n
    col = lambda rows_, j, r: pl.BlockSpec((rows_, LANES), lambda p: (r, j(p)))
    groups = [lambda p: p, lambda p: A_PAIRS + p, lambda p: 2 * A_PAIRS + p,
              lambda p: 3 * A_PAIRS, lambda p: 3 * A_PAIRS + 1]
    block = (None, 2, A_HEAD, A_HEAD, B)
    return pl.pallas_call(
        functools.partial(_rwkv_lanes_body, steps=steps),
        grid=(A_PAIRS,),
        in_specs=[col(n, j, rb) for j in groups] + [col(B, j, 0) for j in groups]
                 + [pl.BlockSpec((n, A_PROJ), lambda p: (rb, 0), pipeline_mode=pl.Buffered(1)),
                    pl.BlockSpec(block, lambda p: (layer, p, 0, 0, 0)),
                    pl.BlockSpec((None, ROWP_ROWS, LANES), lambda p: (p, 0, 0)), _const_spec(shared.shape),
                    pl.BlockSpec((None, LANES, 2 * LANES), lambda p: (p, 0, 0)),
                    pl.BlockSpec((None, LANES, LANES), lambda p: (p, 0, 0))],
        out_specs=[pl.BlockSpec((n, LANES), lambda p: (0, p)), pl.BlockSpec(block, lambda p: (0, p, 0, 0, 0)),
                   pl.BlockSpec((B, A_PROJ), lambda p: (0, 0))],
        out_shape=[jax.ShapeDtypeStruct((n, A_WIDTH), F32), jax.ShapeDtypeStruct((1,) + state_t.shape[1:], F32),
                   jax.ShapeDtypeStruct((B, A_PROJ), F32)],
        scratch_shapes=[pltpu.VMEM((6, steps, LANES, B), F32), pltpu.VMEM((steps, LANES, B), F32)],
        compiler_params=_cparams(1),
        name="rwkv_sample",
    )(*([pa] * 5), *([shift] * 5), pa, state_t, rowp, shared, w2a_p, g2_p)


def _rwkv_pair_tables(mu, w0, a0, k_k, k_a, r_k, lnw, lnb, w2, a2, g2):
    per_pair = lambda a: a.reshape(A_PAIRS, 1, LANES)
    rows = {"mu_r": mu[:A_WIDTH], "mu_k": mu[A_WIDTH:2 * A_WIDTH], "mu_v": mu[2 * A_WIDTH:3 * A_WIDTH],
            "w0": w0, "a0": a0, "k_k": k_k, "k_a": k_a, "r_k": r_k.reshape(-1), "lnw": lnw, "lnb": lnb}
    table = jnp.concatenate([per_pair(rows[name].astype(F32)) for name in sorted(ROWP, key=ROWP.get)]
                            + [jnp.zeros((A_PAIRS, ROWP_ROWS - len(ROWP), LANES), F32)], axis=1)
    shared = jnp.concatenate([mu[3 * A_WIDTH:3 * A_WIDTH + LANES].reshape(1, LANES),
                              mu[3 * A_WIDTH + LANES:].reshape(1, LANES),
                              jnp.zeros((SUBLANES - 2, LANES), F32)], axis=0).astype(F32)
    by_pair = lambda w: jnp.transpose(w.astype(F32).reshape(w.shape[0], A_PAIRS, LANES), (1, 0, 2))
    zeros = jnp.zeros((A_PAIRS, A_RANK_W, LANES), F32)
    w2a_p = jnp.concatenate([jnp.concatenate([by_pair(w2), zeros], axis=2),
                             jnp.concatenate([zeros, by_pair(a2)], axis=2)], axis=1)
    return table, shared, w2a_p, by_pair(g2)


def _gdn_qkv(conv):
    c = _silu(conv)
    qs, ks = [], []
    for h in range(B_HEADS):
        q = c[:, h * B_HEAD:(h + 1) * B_HEAD]
        k = c[:, B_WIDTH + h * B_HEAD:B_WIDTH + (h + 1) * B_HEAD]
        qs.append(q * (lax.rsqrt(jnp.sum(q * q, axis=-1, keepdims=True) + 1e-6) * (B_HEAD ** -0.5)))
        ks.append(k * lax.rsqrt(jnp.sum(k * k, axis=-1, keepdims=True) + 1e-6))
    q = jnp.concatenate(qs, axis=1)
    k = jnp.concatenate(ks, axis=1)
    v = c[:, 2 * B_WIDTH:]
    return q, k, v


def _gdn_gates(ab, alog, dtb):
    lane = _iota((1, LANES), 1)
    g = -jnp.exp(alog) * _softplus(ab + dtb)
    beta = _sigmoid(ab)
    gb = jnp.where(lane < B_HEADS, g, beta)
    si, sj = _iota((LANES, 2 * B_WIDTH), 0), _iota((LANES, 2 * B_WIDTH), 1)
    spread = _mm_sel(gb, _one_hot(si == _group(sj, B_HEAD)))
    return spread[:, :B_WIDTH], spread[:, B_WIDTH:]


def _gdn_out(o, norm_w, z):
    return o * lax.rsqrt(jnp.mean(o * o, axis=-1, keepdims=True) + RMS_EPS) * norm_w * _silu(z)


def _gdn_prompt_part(qkv_ref, ab_ref, z_ref, cw_ref, alog_ref, dtb_ref, nw_ref, o_ref,
                     carry_ref, state_ref, q_s, k_s, v_s, gc_s, beta_s):
    tt = qkv_ref.shape[0]
    C = CHUNK
    g, beta = _gdn_gates(ab_ref[...], alog_ref[...], dtb_ref[...])
    ri, ci = _iota((tt, tt), 0), _iota((tt, tt), 1)
    beta_s[...] = beta
    gc_s[...] = _sel_mm(_one_hot((_group(ri, C) == _group(ci, C)) & (ci <= ri)), g)

    def token_chain(r0):
        rows = slice(r0, r0 + C)
        x = qkv_ref[rows, :]
        before = carry_ref[...] if r0 == 0 else qkv_ref[r0 - SUBLANES:r0, :]
        if r0 + C == tt:
            carry_ref[...] = x[C - SUBLANES:, :]
        row8 = _iota((SUBLANES, 1), 0)

        def shift_rows(cur, halo, i):
            down = pltpu.roll(cur, i, axis=0)
            top = jnp.where(row8 < i, pltpu.roll(halo, i, axis=0), down[:SUBLANES])
            return jnp.concatenate([top, down[SUBLANES:]], axis=0)

        assert CONV_K == 4
        c0, c1, c2, c3 = (cw_ref[i:i + 1, :] for i in range(CONV_K))
        x1 = shift_rows(x, before, 1)
        yield
        older = x * c1 + x1 * c0
        older_halo = before * c1 + pltpu.roll(before, 1, axis=0) * c0
        conv = x * c3 + x1 * c2 + shift_rows(older, older_halo, 2)
        yield
        q, k, v = _gdn_qkv(conv)
        q_s[rows, :] = q
        k_s[rows, :] = k
        v_s[rows, :] = v
        yield

    i2, j2 = _iota((2 * C, 2 * C), 0), _iota((2 * C, 2 * C), 1)
    same_head = _group(i2, C) == _group(j2, C)
    strict = same_head & (i2 > j2)
    incl = same_head & (i2 >= j2)
    eye = jnp.where(i2 == j2, 1.0, 0.0)
    first = _iota((2 * C, 1), 0) < C

    def solve_chain(pr, r0, stash):
        sls = [slice(h * B_HEAD, (h + 1) * B_HEAD) for h in (2 * pr, 2 * pr + 1)]
        ld = lambda ref: jnp.concatenate([ref[pl.ds(r0, C), sl] for sl in sls], axis=0)
        q_h, k_h, v_h, gc_h, beta_h = ld(q_s), ld(k_s), ld(v_s), ld(gc_s), ld(beta_s)
        diff = gc_h - gc_h.T
        dm = jnp.where(incl, jnp.exp(jnp.where(incl, diff, 0.0)), 0.0)
        kb = k_h * beta_h
        QK = _mm_nt(jnp.concatenate([kb, q_h], axis=0), k_h)
        yield
        N = -jnp.where(strict, QK[:2 * C] * dm, 0.0)
        qk = QK[2 * C:] * dm
        egc = jnp.exp(gc_h)
        X = jnp.concatenate([v_h * beta_h, kb * egc], axis=1)
        T = yield from _nilpotent_inverse(N, eye)
        UW = _mm(T, X)
        yield
        g_last = jnp.where(first, gc_h[C - 1:C, :], gc_h[2 * C - 1:2 * C, :])
        stash[pr] = dict(u=UW[:, :B_HEAD], w=UW[:, B_HEAD:].astype(BF16), qd=(q_h * egc).astype(BF16),
                         qk=qk.astype(BF16), k_dec_t=(k_h * jnp.exp(g_last - gc_h)).T.astype(BF16),
                         decay=[jnp.exp(gc_h[(j + 1) * C - 1:(j + 1) * C, :]) for j in range(2)])

    def state_chain(pr, r0, stash):
        heads = (2 * pr, 2 * pr + 1)
        sls = [slice(h * B_HEAD, (h + 1) * B_HEAD) for h in heads]
        s = stash[pr]
        wS, qS, S_old = [], [], []
        for j, h in enumerate(heads):
            S = state_ref[h]
            rows = slice(j * C, (j + 1) * C)
            wq = _dot(jnp.concatenate([s["w"][rows], s["qd"][rows]], axis=0), S.astype(BF16))
            wS.append(wq[:C])
            qS.append(wq[C:])
            S_old.append(S)
        yield
        v_new = s["u"] - jnp.concatenate(wS, axis=0)
        o = jnp.concatenate(qS, axis=0) + _dot(s["qk"], v_new.astype(BF16))
        for j, h in enumerate(heads):
            mine = first if j == 0 else jnp.logical_not(first)
            rows = slice(j * C, (j + 1) * C)
            state_ref[h] = S_old[j] * s["decay"][j] + _mm(s["k_dec_t"], jnp.where(mine, v_new, 0.0))
            o_ref[pl.ds(r0, C), sls[j]] = _gdn_out(o[rows], nw_ref[...], z_ref[pl.ds(r0, C), sls[j]])
        yield

    return (lambda r0: [token_chain(r0)],
            lambda r0, stash: [solve_chain(pr, r0, stash) for pr in range(B_HEADS // 2)],
            lambda r0, stash: [state_chain(pr, r0, stash) for pr in range(B_HEADS // 2)])


def _gdn_chunk_body(qkv_ref, hist_ref, ab_ref, z_ref, s_ref, cw_ref, alog_ref, dtb_ref, nw_ref,
                    o_ref, sout_ref, hist_out_ref, *, steps):
    rows = ab_ref.shape[0]
    seqs = rows // steps
    x = qkv_ref[...]
    hist_out_ref[...] = _last_step_rows(x, seqs, steps, CONV_K - 1)
    hist = hist_ref[...]
    t_idx = _iota((rows, 1), 0) & (steps - 1)
    conv = x * cw_ref[CONV_K - 1:CONV_K, :]
    for i in range(1, CONV_K):
        tap = jnp.where(t_idx >= i, pltpu.roll(x, i, axis=0),
                        _first_step_rows(rows, seqs, steps, hist, offset=CONV_K - 1 - i))
        conv = conv + tap * cw_ref[CONV_K - 1 - i:CONV_K - i, :]
    q, k, v = _gdn_qkv(conv)
    g, beta = _gdn_gates(ab_ref[...], alog_ref[...], dtb_ref[...])
    i2, j2 = _iota((rows, rows), 0), _iota((rows, rows), 1)
    same = _group(i2, steps) == _group(j2, steps)
    strict = same & (i2 > j2)
    incl = same & (i2 >= j2)
    eye = jnp.where(i2 == j2, 1.0, 0.0)
    gc = _sel_mm(_one_hot(incl), g)
    g_end = _sel_mm(_one_hot(j2 == (i2 | (steps - 1))), gc)
    row = _iota((rows, 1), 0)
    pair_rows = 2 * SUBLANES
    first_half = (_iota((pair_rows, 1), 0) & (SUBLANES - 1)) < steps
    assert 2 * steps == SUBLANES

    def head_chain(h):
        sl = slice(h * B_HEAD, (h + 1) * B_HEAD)
        q_h, k_h, v_h, gc_h, beta_h, ge_h = q[:, sl], k[:, sl], v[:, sl], gc[:, sl], beta[:, sl], g_end[:, sl]
        dm = jnp.where(incl, jnp.exp(jnp.where(incl, gc_h - gc_h.T, 0.0)), 0.0)
        kb = k_h * beta_h
        QK = _mm_nt(jnp.concatenate([kb, q_h], axis=0), k_h)
        yield
        N = -jnp.where(strict, QK[:rows] * dm, 0.0)
        qk = QK[rows:] * dm
        assert steps == 4
        N2 = _mm(N, N)
        yield
        T = eye + N
        T = T + _mm(T, N2)
        yield
        egc = jnp.exp(gc_h)
        UW = _mm(T, jnp.concatenate([v_h * beta_h, kb * egc], axis=1))
        yield
        w = UW[:, B_HEAD:].astype(BF16)
        qd = (q_h * egc).astype(BF16)
        k_dec_t = (k_h * jnp.exp(ge_h - gc_h)).T.astype(BF16)
        decay = jnp.exp(ge_h)
        wS, qS = [], []
        for m in range(rows // SUBLANES):
            tile = slice(m * SUBLANES, (m + 1) * SUBLANES)
            lhs = jnp.concatenate([w[tile], qd[tile]], axis=0)
            res = jnp.where(first_half, _dot(lhs, s_ref[2 * m, h].astype(BF16)),
                            _dot(lhs, s_ref[2 * m + 1, h].astype(BF16)))
            wS.append(res[:SUBLANES])
            qS.append(res[SUBLANES:])
            yield
        v_new = UW[:, :B_HEAD] - jnp.concatenate(wS, axis=0)
        o = jnp.concatenate(qS, axis=0) + _mm(qk, v_new)
        o_ref[:, sl] = _gdn_out(o, nw_ref[...], z_ref[:, sl])
        yield
        for b in range(seqs):
            mine = _group(row, steps) == b
            sout_ref[b, h] = (s_ref[b, h] * decay[b * steps:b * steps + 1, :]
                              + _mm(k_dec_t, jnp.where(mine, v_new, 0.0)))
            yield

    _round_robin(head_chain(h) for h in range(B_HEADS))


def _gdn_sample(qkv, ab, z, first_row, hist, state, layer, steps, params, seqs):
    nb = state.shape[1]
    n = nb * steps
    rows = seqs * steps
    hist_rows = seqs * (CONV_K - 1)
    assert SUBLANES % steps == 0 and rows % SUBLANES == 0 and hist_rows % SUBLANES == 0 and first_row % rows == 0
    row = lambda w: pl.BlockSpec((rows, w), lambda i: (i + first_row // rows, 0))
    block = (None, seqs, B_HEADS, B_HEAD, B_HEAD)
    sspec = pl.BlockSpec(block, lambda i: (layer, i, 0, 0, 0))
    ospec = pl.BlockSpec(block, lambda i: (0, i, 0, 0, 0))
    return pl.pallas_call(
        functools.partial(_gdn_chunk_body, steps=steps),
        grid=(nb // seqs,),
        in_specs=[row(CONV_CH), pl.BlockSpec((hist_rows, CONV_CH), lambda i: (i, 0)), row(LANES), row(B_WIDTH), sspec]
                 + [_const_spec(p.shape) for p in params],
        out_specs=[pl.BlockSpec((rows, B_WIDTH), lambda i: (i, 0)), ospec,
                   pl.BlockSpec((hist_rows, CONV_CH), lambda i: (i, 0))],
        out_shape=[jax.ShapeDtypeStruct((n, B_WIDTH), F32), jax.ShapeDtypeStruct((1,) + state.shape[1:], F32),
                   jax.ShapeDtypeStruct(hist.shape, F32)],
        compiler_params=_cparams(1),
        name="gdn_sample",
    )(qkv, hist, ab, z, state, *params)


def _cuts(widths):
    edges, total = [], 0
    for w in widths[:-1]:
        total += w
        edges.append(total)
    return edges


def _regroup_pa(a):
    r, wd, k, v, ad, gd = jnp.split(a, _cuts((A_WIDTH, A_RANK_W, A_WIDTH, A_WIDTH, A_RANK_A, A_RANK_G)), axis=-1)
    return jnp.concatenate([r, k, v, wd, ad, gd], axis=-1)


def _ungroup_pa(a):
    r, k, v, wd, ad, gd = jnp.split(a, _cuts((A_WIDTH, A_WIDTH, A_WIDTH, A_RANK_W, A_RANK_A, A_RANK_G)), axis=-1)
    return jnp.concatenate([r, wd, k, v, ad, gd], axis=-1)


def _token_tile(n, want):
    tm = want
    while n % tm:
        tm //= 2
    return tm


def kernel(x_prompt, x_sample, state_rwkv, state_rwkv_shift, state_delta, state_conv, ffn1_norm, ffn1_w_gate, ffn1_w_up, ffn1_w_down, mix_norm, w_in, rwkv_mu, rwkv_w0, rwkv_w2, rwkv_a0, rwkv_a2, rwkv_g2, rwkv_k_k, rwkv_k_a, rwkv_r_k, rwkv_lnx_w, rwkv_lnx_b, gdn_conv_w, gdn_A_log, gdn_dt_bias, gdn_norm_w, proj_a, proj_b, w_out, ffn2_norm, ffn2_w_gate, ffn2_w_up, ffn2_w_down, final_norm):
    depth = ffn1_norm.shape[0]
    assert depth == 1, "single-layer trunk"
    Bp, Tp, _ = x_prompt.shape
    Bs, Ts, _ = x_sample.shape
    l = 0
    row = lambda a: a.reshape(1, -1).astype(F32)

    wi = w_in[l].astype(BF16)
    o_b = A_PROJ
    proj_w = (_regroup_pa(wi[:, :A_PROJ]),
              wi[:, o_b:o_b + CONV_CH + LANES],
              wi[:, o_b + CONV_CH + 2 * B_HEADS:])
    ffn1 = (row(ffn1_norm[l]), ffn1_w_gate[l].astype(BF16), ffn1_w_up[l].astype(BF16), ffn1_w_down[l].astype(BF16))
    ffn2 = (row(ffn2_norm[l]), ffn2_w_gate[l].astype(BF16), ffn2_w_up[l].astype(BF16), ffn2_w_down[l].astype(BF16))
    merge_w = (proj_a[l].astype(BF16), proj_b[l].astype(BF16), w_out[l].astype(BF16))
    zw = jnp.zeros((A_RANK_W, A_WIDTH), F32)
    w2a = jnp.concatenate([jnp.concatenate([rwkv_w2[l], zw], axis=1),
                           jnp.concatenate([zw, rwkv_a2[l]], axis=1)], axis=0)
    rwkv_params = (row(_regroup_pa(rwkv_mu[l])), row(rwkv_w0[l]), row(rwkv_a0[l]), row(rwkv_k_k[l]), row(rwkv_k_a[l]),
                   row(rwkv_r_k[l]), row(rwkv_lnx_w[l]), row(rwkv_lnx_b[l]), w2a, rwkv_g2[l].astype(F32))
    pad_lane = lambda a: jnp.pad(a.reshape(1, -1).astype(F32), ((0, 0), (0, LANES - a.size)))
    gdn_params = (gdn_conv_w[l].astype(F32), pad_lane(gdn_A_log[l]), pad_lane(gdn_dt_bias[l]), row(gdn_norm_w[l]))

    n_p, n_s = Bp * Tp, Bs * Ts
    tm = _token_tile(n_s, _token_tile(n_p, 512))
    h, pa, qkv, z, gates, ab = _head(x_prompt.reshape(n_p, D_MODEL), x_sample.reshape(n_s, D_MODEL), *ffn1,
                                     row(mix_norm[l]), *proj_w, tm=_token_tile(n_s, _token_tile(n_p, 256)))

    def trunk_back(oa, ob, first_row):
        return _tail(h, oa, ob, gates, *merge_w, *ffn2, row(final_norm), tm=tm, first_row=first_row)

    tt = _token_tile(Tp, 256)
    oa, ob, s_pairs, delta_p, pa_tail, qkv_tail = _mix_prompt(pa, qkv, ab, z, Bp, Tp, rwkv_params, gdn_params, tt)
    y_prompt = trunk_back(oa, ob, 0).reshape(Bp, Tp, D_MODEL)
    sp = s_pairs.reshape(Bp, A_PAIRS, 2, A_HEAD, 2, A_HEAD)
    rwkv_p = jnp.stack([sp[:, :, 0, :, 0], sp[:, :, 1, :, 1]], axis=2).reshape(Bp, A_HEADS, A_HEAD, A_HEAD)
    shift_p = _ungroup_pa(pa_tail[:, -1])
    conv_p = qkv_tail[:, SUBLANES - (CONV_K - 1):]

    assert Ts & (Ts - 1) == 0 and Ts >= CONV_K - 1, "sample steps: power of two covering the conv history"
    tables = _rwkv_pair_tables(_regroup_pa(rwkv_mu[l]), rwkv_w0[l], rwkv_a0[l], rwkv_k_k[l], rwkv_k_a[l], rwkv_r_k[l],
                               rwkv_lnx_w[l], rwkv_lnx_b[l], rwkv_w2[l], rwkv_a2[l], rwkv_g2[l])
    oa, s_lanes, last_pa = _rwkv_lanes(pa, n_p, _regroup_pa(state_rwkv_shift[l].astype(F32)),
                                       jnp.transpose(state_rwkv.astype(F32), (0, 2, 3, 4, 1)), l, Ts, tables)
    rwkv_s = jnp.transpose(s_lanes, (0, 4, 1, 2, 3))
    shift_s = _ungroup_pa(last_pa)
    hist = state_conv[l].astype(F32).reshape(Bs * (CONV_K - 1), CONV_CH)
    ob, delta_s, new_hist = _gdn_sample(qkv, ab, z, n_p, hist, state_delta.astype(F32), l, Ts, gdn_params,
                                        seqs=_token_tile(Bs, 32))
    conv_s = new_hist.reshape(Bs, CONV_K - 1, CONV_CH)
    y_sample = trunk_back(oa, ob, n_p).reshape(Bs, Ts, D_MODEL)

    add_depth = lambda a: a[None]
    return (y_prompt, y_sample,
            add_depth(rwkv_p), add_depth(shift_p), add_depth(delta_p), add_depth(conv_p),
            rwkv_s, add_depth(shift_s), delta_s, add_depth(conv_s))
```

```python
import functools

import jax
import jax.numpy as jnp
from jax import lax
from jax.experimental import pallas as pl
from jax.experimental.pallas import tpu as pltpu

F32 = jnp.float32
BF16 = jnp.bfloat16

D_MODEL = 1024
D_FF = 2816
RMS_EPS = 1e-6
A_HEAD = 64
A_HEADS = 8
A_WIDTH = A_HEADS * A_HEAD
A_RANK_W = 64
A_RANK_A = 64
A_RANK_G = 128
A_PROJ = 3 * A_WIDTH + A_RANK_W + A_RANK_A + A_RANK_G
A_LNX_EPS = 64e-5
A_PAIRS = A_HEADS // 2
B_HEADS = 4
B_HEAD = 128
B_WIDTH = B_HEADS * B_HEAD
CONV_K = 4
CONV_CH = 3 * B_WIDTH
B_PROJ = CONV_CH + 2 * B_HEADS + B_WIDTH
GATE_COLS = 2 * D_MODEL
LANES = 128
SUBLANES = 8
MXU_DIM = 256
VMEM_LIMIT_BYTES = 56 * 1024 * 1024
CHUNK = 64
PA_R, PA_K, PA_V, PA_WA, PA_G = 0, A_WIDTH, 2 * A_WIDTH, 3 * A_WIDTH, 3 * A_WIDTH + A_RANK_W + A_RANK_A
PROJ_SPLITS = (A_PROJ, CONV_CH, B_WIDTH, GATE_COLS, LANES)


def _cparams(n_grid_dims):
    return pltpu.CompilerParams(dimension_semantics=("arbitrary",) * n_grid_dims,
                                vmem_limit_bytes=VMEM_LIMIT_BYTES)


def _const_spec(shape):
    nd = len(shape)
    return pl.BlockSpec(shape, lambda *_: (0,) * nd, pipeline_mode=pl.Buffered(1))


def _dot(a, b):
    return jnp.dot(a, b, preferred_element_type=F32)


def _dot_nt(a, b):
    return lax.dot_general(a, b, (((1,), (1,)), ((), ())), preferred_element_type=F32)


def _split3(x):
    hi = x.astype(BF16)
    rest = x - hi.astype(F32)
    mid = rest.astype(BF16)
    lo = (rest - mid.astype(F32)).astype(BF16)
    return hi, mid, lo


def _mm(a, b):
    return _dot(a.astype(BF16), b.astype(BF16))


def _mm_nt(a, b):
    return _dot_nt(a.astype(BF16), b.astype(BF16))


def _sel_mm(sel, x):
    return _dot(jnp.concatenate([sel, sel, sel], axis=1), jnp.concatenate(_split3(x), axis=0))


def _mm_sel(x, sel):
    return _dot(jnp.concatenate(_split3(x), axis=1), jnp.concatenate([sel, sel, sel], axis=0))


INV_BASE = 8


def _nilpotent_inverse(n, eye):
    width = n.shape[1]
    bi, bj = _iota(n.shape, 0), _iota(n.shape, 1)
    same = lambda size: _group(bi, size) == _group(bj, size)
    d = jnp.where(same(INV_BASE), n, 0.0)
    t = eye + d
    d = _mm(d, d)
    yield
    for _ in range(INV_BASE.bit_length() - 3):
        both = _mm(d, jnp.concatenate([t, d], axis=1))
        yield
        t = t + both[:, :width]
        d = both[:, width:]
    t = t + _mm(d, t)
    yield
    size = INV_BASE
    while size < CHUNK:
        coupling = jnp.where(same(2 * size) & jnp.logical_not(same(size)), n, 0.0)
        tb = t.astype(BF16)
        lt = _mm(coupling, tb)
        yield
        t = t + _mm(tb, lt)
        yield
        size *= 2
    return t


def _round_robin(chains):
    chains = list(chains)
    while chains:
        for chain in list(chains):
            try:
                next(chain)
            except StopIteration:
                chains.remove(chain)


def _rms(x, w):
    return x * lax.rsqrt(jnp.mean(x * x, axis=-1, keepdims=True) + RMS_EPS) * w


def _sigmoid(x):
    return 1.0 / (1.0 + jnp.exp(-x))


def _silu(x):
    return x * _sigmoid(x)


def _softplus(x):
    return jnp.maximum(x, 0.0) + jnp.log(1.0 + jnp.exp(-jnp.abs(x)))


def _iota(shape, dim):
    return lax.broadcasted_iota(jnp.int32, shape, dim)


def _group(idx, size):
    assert size & (size - 1) == 0
    return lax.shift_right_logical(idx, size.bit_length() - 1)


def _one_hot(cond):
    return jnp.where(cond, 1.0, 0.0).astype(BF16)


def _rows_to_tile(rows):
    rid = _iota((SUBLANES, 1), 0)
    tile = jnp.zeros((SUBLANES, rows[0].shape[1]), F32)
    for i, row in enumerate(rows):
        tile = jnp.where(rid == i, row, tile)
    return tile


def _swiglu_half_step(x, nw, wg_ref, wu_ref, wd_ref, slabs=(D_FF,)):
    xn = _rms(x, nw).astype(BF16)
    assert sum(slabs) == D_FF and all(w % MXU_DIM == 0 for w in slabs)
    y = None
    start = 0
    for width in slabs:
        cols = slice(start, start + width)
        start += width
        g = _dot(xn, wg_ref[:, cols])
        u = _dot(xn, wu_ref[:, cols])
        act = (_silu(g) * u).astype(BF16)
        part = _dot(act, wd_ref[cols, :])
        y = part if y is None else y + part
    return x + 0.5 * y


def _head_body(xa_ref, xb_ref, nw1_ref, wg_ref, wu_ref, wd_ref, nw2_ref, wpa_ref, wqa_ref, wzg_ref,
               h_ref, pa_ref, qkv_ref, z_ref, gates_ref, ab_ref, *, blocks_a):
    x = jnp.where(pl.program_id(0) < blocks_a, xa_ref[...], xb_ref[...])
    h = _swiglu_half_step(x, nw1_ref[...], wg_ref, wu_ref, wd_ref, slabs=TAIL_FF_SLABS)
    h_ref[...] = h
    u = _rms(h, nw2_ref[...]).astype(BF16)
    pa_ref[...] = _dot_nt(u, wpa_ref[...])
    qkv_ref[...] = _dot_nt(u, wqa_ref[:CONV_CH, :])
    ab_ref[...] = _dot_nt(u, wqa_ref[CONV_CH:, :])
    z_ref[...] = _dot_nt(u, wzg_ref[:B_WIDTH, :])
    gates_ref[...] = _dot_nt(u, wzg_ref[B_WIDTH:, :])


def _head(xa, xb, nw1, wg, wu, wd, nw2, w_pa, w_qa, w_zg, tm):
    na, nb = xa.shape[0], xb.shape[0]
    assert na % tm == 0 and nb % tm == 0
    assert w_qa.shape[0] == CONV_CH + LANES and w_zg.shape[0] == B_WIDTH + GATE_COLS
    blocks_a = na // tm
    n = na + nb
    widths = (D_MODEL,) + PROJ_SPLITS
    return pl.pallas_call(
        functools.partial(_head_body, blocks_a=blocks_a),
        grid=(n // tm,),
        in_specs=[pl.BlockSpec((tm, D_MODEL), lambda i: (jnp.minimum(i, blocks_a - 1), 0)),
                  pl.BlockSpec((tm, D_MODEL), lambda i: (jnp.maximum(i - blocks_a, 0), 0)),
                  _const_spec((1, D_MODEL)),
                  _const_spec((D_MODEL, D_FF)), _const_spec((D_MODEL, D_FF)), _const_spec((D_FF, D_MODEL)),
                  _const_spec((1, D_MODEL)), _const_spec(w_pa.shape), _const_spec(w_qa.shape), _const_spec(w_zg.shape)],
        out_specs=[pl.BlockSpec((tm, w), lambda i: (i, 0)) for w in widths],
        out_shape=[jax.ShapeDtypeStruct((n, w), F32) for w in widths],
        compiler_params=_cparams(1),
        name="head",
    )(xa, xb, nw1, wg, wu, wd, nw2, w_pa, w_qa, w_zg)


TAIL_FF_SLABS = (6 * MXU_DIM, 5 * MXU_DIM)


def _tail_body(h_ref, oa_ref, ob_ref, gates_ref, pa_ref, pb_ref, wo_ref, nw_ref, wg_ref, wu_ref, wd_ref,
               fn_ref, o_ref):
    ma = _dot(oa_ref[...].astype(BF16), pa_ref[...])
    mb = _dot(ob_ref[...].astype(BF16), pb_ref[...])
    merged = _sigmoid(gates_ref[:, :D_MODEL]) * ma + _sigmoid(gates_ref[:, D_MODEL:]) * mb
    h = h_ref[...] + _dot(merged.astype(BF16), wo_ref[...])
    h = _swiglu_half_step(h, nw_ref[...], wg_ref, wu_ref, wd_ref, slabs=TAIL_FF_SLABS)
    o_ref[...] = _rms(h, fn_ref[...])


def _tail(h, oa, ob, gates, proj_a, proj_b, w_out, nw, wg, wu, wd, fn, tm, first_row):
    n = oa.shape[0]
    assert n % tm == 0 and first_row % tm == 0
    row = lambda w: pl.BlockSpec((tm, w), lambda i: (i, 0))
    stream = lambda w: pl.BlockSpec((tm, w), lambda i: (i + first_row // tm, 0))
    return pl.pallas_call(
        _tail_body,
        grid=(n // tm,),
        in_specs=[stream(D_MODEL), row(A_WIDTH), row(B_WIDTH), stream(GATE_COLS),
                  _const_spec((A_WIDTH, D_MODEL)), _const_spec((B_WIDTH, D_MODEL)),
                  _const_spec((D_MODEL, D_MODEL)), _const_spec((1, D_MODEL)),
                  _const_spec((D_MODEL, D_FF)), _const_spec((D_MODEL, D_FF)), _const_spec((D_FF, D_MODEL)),
                  _const_spec((1, D_MODEL))],
        out_specs=row(D_MODEL),
        out_shape=jax.ShapeDtypeStruct((n, D_MODEL), F32),
        compiler_params=_cparams(1),
        name="tail",
    )(h, oa, ob, gates, proj_a, proj_b, w_out, nw, wg, wu, wd, fn)


def _rwkv_token_math(x, prev, mu, w0, a0, k_k, k_a, w2a, g2):
    pm = x + (prev - x) * mu
    r = pm[:, PA_R:PA_R + A_WIDTH]
    k = pm[:, PA_K:PA_K + A_WIDTH]
    v = pm[:, PA_V:PA_V + A_WIDTH]
    wa = pm[:, PA_WA:PA_WA + LANES]
    gd = pm[:, PA_G:PA_G + A_RANK_G]
    lane = _iota((1, LANES), 1)
    lora_in = jnp.where(lane < A_RANK_W, jnp.tanh(wa), wa)
    lora = _mm(lora_in, w2a)
    g = _mm(_sigmoid(gd), g2)
    yield
    w_log = -_softplus(-(w0 + lora[:, :A_WIDTH])) - 0.5
    log_decay = -jnp.exp(w_log)
    yield
    a = _sigmoid(a0 + lora[:, A_WIDTH:])
    kk_raw = k * k_k
    k_mod = k * (1.0 + (a - 1.0) * k_a)
    return r, k_mod, v, kk_raw, a, log_decay, g


def _pair_mask(rows_per_head):
    shape = (2 * rows_per_head, LANES)
    return _group(_iota(shape, 0), rows_per_head) == _group(_iota(shape, 1), A_HEAD)


def _rwkv_prompt_part(pa_ref, mu_ref, w0_ref, a0_ref, kk_ref, ka_ref, rk_ref, lnw_ref, lnb_ref, w2a_ref, g2_ref,
                      o_ref, carry_ref, state_ref, r_s, k_s, v_s, kkraw_s, a_s, cum_s, ld_s, g_s):
    tt = pa_ref.shape[0]
    C = CHUNK
    lower = _one_hot(_iota((C, C), 1) <= _iota((C, C), 0))

    def token_chain(r0):
        rows = slice(r0, r0 + C)
        x = pa_ref[rows, :]
        before = carry_ref[SUBLANES - 1:SUBLANES, :] if r0 == 0 else pa_ref[r0 - 1:r0, :]
        prev = jnp.where(_iota((C, 1), 0) == 0, before, pltpu.roll(x, 1, axis=0))
        if r0 + C == tt:
            carry_ref[...] = x[C - SUBLANES:, :]
        r, k_mod, v, kk_raw, a, log_decay, g = yield from _rwkv_token_math(
            x, prev, mu_ref[...], w0_ref[...], a0_ref[...], kk_ref[...], ka_ref[...], w2a_ref[...], g2_ref[...])
        r_s[rows, :] = r
        k_s[rows, :] = k_mod
        v_s[rows, :] = v
        kkraw_s[rows, :] = kk_raw
        a_s[rows, :] = a
        g_s[rows, :] = g
        ld_s[rows, :] = log_decay
        yield
        cum_s[rows, :] = sum(_dot(lower, piece) for piece in _split3(log_decay))
        yield

    mask = _pair_mask(C)
    i2, j2 = _iota((2 * C, 2 * C), 0), _iota((2 * C, 2 * C), 1)
    strict = i2 > j2
    incl = i2 >= j2
    eye = jnp.where(i2 == j2, 1.0, 0.0)
    dup = lambda m: jnp.concatenate([m, m], axis=0)
    stack = lambda m: jnp.where(mask, dup(m), 0.0)
    lnb_stacked = [jnp.where(mask, lnb_ref[:, p * LANES:(p + 1) * LANES], 0.0) for p in range(A_PAIRS)]

    def solve_chain(p, r0, stash):
        sl = slice(p * LANES, (p + 1) * LANES)
        ld = lambda ref: ref[pl.ds(r0, C), sl]
        r_p, k_p, v_p, a_p, cum, ldec = ld(r_s), ld(k_s), ld(v_s), ld(a_s), ld(cum_s), ld(ld_s)
        einc = jnp.exp(cum)
        eex = jnp.exp(cum - ldec)
        einv = jnp.exp(-cum)
        etail = jnp.exp(cum[C - 1:C, :] - cum)
        kks = stack(ld(kkraw_s))
        kks = kks * jnp.minimum(lax.rsqrt(jnp.sum(kks * kks, axis=-1, keepdims=True)), 1e12)
        As = kks * dup(-eex)
        Bs = kks * dup(a_p * einv)
        Bh = kks * dup(a_p * etail)
        Ks = stack(k_p * einv)
        Kh = stack(k_p * etail)
        Rs = stack(r_p * einc)
        Vs = stack(v_p)
        AR = jnp.concatenate([As, Rs], axis=0).astype(BF16)
        Vb = Vs.astype(BF16)
        G = _mm_nt(AR, jnp.concatenate([Bs, Ks], axis=0))
        yield
        Aab = jnp.where(strict, G[:2 * C, :2 * C], 0.0)
        Aak = jnp.where(strict, G[:2 * C, 2 * C:], 0.0)
        Arb = jnp.where(incl, G[2 * C:, :2 * C], 0.0)
        Ark = jnp.where(incl, G[2 * C:, 2 * C:], 0.0)
        Y = _mm(Aak, Vb)
        yield
        T = yield from _nilpotent_inverse(Aab, eye)
        WU = _mm(T, jnp.concatenate([AR[:2 * C], Y.astype(BF16)], axis=1))
        yield
        bonus = jnp.sum(stack(r_p * k_p * rk_ref[:, sl]), axis=-1, keepdims=True) * Vs
        stash[p] = dict(WU=WU, R=AR[2 * C:], Vs=Vs, bonus=bonus,
                        Aro=jnp.concatenate([Arb, Ark], axis=1).astype(BF16),
                        BKh=jnp.concatenate([Bh, Kh], axis=0).astype(BF16), decay=einc[C - 1:C, :])

    def state_chain(p, r0, stash):
        sl = slice(p * LANES, (p + 1) * LANES)
        s = stash[p]
        S = state_ref[p]
        Sb = S.astype(BF16)
        W = _mm_nt(s["WU"][:, :LANES], Sb) + s["WU"][:, LANES:]
        yield
        WV = jnp.concatenate([W, s["Vs"]], axis=0)
        O = _dot_nt(s["R"], Sb) + _mm(s["Aro"], WV)
        state_ref[p] = S * s["decay"] + _mm(WV.T, s["BKh"])
        yield
        mean = jnp.sum(O, axis=-1, keepdims=True) * (1.0 / A_HEAD)
        cen = jnp.where(mask, O - mean, 0.0)
        var = jnp.sum(cen * cen, axis=-1, keepdims=True) * (1.0 / A_HEAD)
        normed = cen * lax.rsqrt(var + A_LNX_EPS) * lnw_ref[:, sl] + lnb_stacked[p]
        full = normed + s["bonus"]
        o_ref[pl.ds(r0, C), sl] = (full[:C] + full[C:]) * g_s[pl.ds(r0, C), sl]

    return (lambda r0: [token_chain(r0)],
            lambda r0, stash: [solve_chain(p, r0, stash) for p in range(A_PAIRS)],
            lambda r0, stash: [state_chain(p, r0, stash) for p in range(A_PAIRS)])


CHUNKS_IN_FLIGHT = 2
N_RWKV_PARAMS = 10
N_GDN_PARAMS = 4
N_RWKV_SCRATCH = 8
N_GDN_SCRATCH = 5


def _mix_prompt_body(pa_ref, qkv_ref, ab_ref, z_ref, *refs):
    refs = list(refs)
    take = lambda n: [refs.pop(0) for _ in range(n)]
    rwkv_prm, gdn_prm = take(N_RWKV_PARAMS), take(N_GDN_PARAMS)
    oa_ref, ob_ref, sfa_ref, sfb_ref, tail_a_ref, tail_b_ref = take(6)
    carry_a, state_a, carry_b, state_b = take(4)
    rwkv_scr, gdn_scr = take(N_RWKV_SCRATCH), take(N_GDN_SCRATCH)
    t = pl.program_id(1)
    tt = pa_ref.shape[0]

    @pl.when(t == 0)
    def _():
        for ref in (carry_a, state_a, carry_b, state_b):
            ref[...] = jnp.zeros_like(ref)

    rwkv_token, rwkv_solve, rwkv_state = _rwkv_prompt_part(pa_ref, *rwkv_prm, oa_ref, carry_a, state_a, *rwkv_scr)
    gdn_token, gdn_solve, gdn_state = _gdn_prompt_part(qkv_ref, ab_ref, z_ref, *gdn_prm, ob_ref, carry_b, state_b,
                                                       *gdn_scr)

    n_chunks = tt // CHUNK
    group = min(CHUNKS_IN_FLIGHT, n_chunks)
    n_groups = n_chunks // group
    stashes = [({}, {}) for _ in range(n_chunks)]
    chunks_of = lambda gi: range(gi * group, (gi + 1) * group) if 0 <= gi < n_groups else ()

    def in_sequence(per_chunk_chains):
        for chains in zip(*per_chunk_chains):
            for chain in chains:
                yield from chain

    for gi in range(n_groups + 2):
        chains = []
        for c in chunks_of(gi):
            chains += rwkv_token(c * CHUNK) + gdn_token(c * CHUNK)
        for c in chunks_of(gi - 1):
            chains += rwkv_solve(c * CHUNK, stashes[c][0]) + gdn_solve(c * CHUNK, stashes[c][1])
        state_chains = [rwkv_state(c * CHUNK, stashes[c][0]) + gdn_state(c * CHUNK, stashes[c][1])
                        for c in chunks_of(gi - 2)]
        if state_chains:
            chains += [in_sequence([per_chunk[i:i + 1] for per_chunk in state_chains])
                       for i in range(len(state_chains[0]))]
        _round_robin(chains)

    @pl.when(t == pl.num_programs(1) - 1)
    def _():
        sfa_ref[0] = state_a[...]
        sfb_ref[0] = state_b[...]
        tail_a_ref[0] = pa_ref[tt - SUBLANES:, :]
        tail_b_ref[0] = qkv_ref[tt - SUBLANES:, :]


def _mix_prompt(pa, qkv, ab, z, B, T, rwkv_params, gdn_params, tt):
    n = B * T
    nt = T // tt
    assert len(rwkv_params) == N_RWKV_PARAMS and len(gdn_params) == N_GDN_PARAMS
    rows = lambda w: pl.BlockSpec((tt, w), lambda b, t: (b * nt + t, 0))
    assert A_PAIRS == B_HEADS and B_HEAD == LANES
    state = lambda: pl.BlockSpec((1, A_PAIRS, LANES, LANES), lambda b, t: (b, 0, 0, 0))
    big = lambda: pltpu.VMEM((tt, A_WIDTH), F32)
    return pl.pallas_call(
        _mix_prompt_body,
        grid=(B, nt),
        in_specs=[rows(A_PROJ), rows(CONV_CH), rows(LANES), rows(B_WIDTH)]
                 + [_const_spec(p.shape) for p in rwkv_params + gdn_params],
        out_specs=[rows(A_WIDTH), rows(B_WIDTH), state(), state(),
                   pl.BlockSpec((1, SUBLANES, A_PROJ), lambda b, t: (b, 0, 0)),
                   pl.BlockSpec((1, SUBLANES, CONV_CH), lambda b, t: (b, 0, 0))],
        out_shape=[jax.ShapeDtypeStruct((n, A_WIDTH), F32), jax.ShapeDtypeStruct((n, B_WIDTH), F32),
                   jax.ShapeDtypeStruct((B, A_PAIRS, LANES, LANES), F32),
                   jax.ShapeDtypeStruct((B, B_HEADS, B_HEAD, B_HEAD), F32),
                   jax.ShapeDtypeStruct((B, SUBLANES, A_PROJ), F32),
                   jax.ShapeDtypeStruct((B, SUBLANES, CONV_CH), F32)],
        scratch_shapes=[pltpu.VMEM((SUBLANES, A_PROJ), F32), pltpu.VMEM((A_PAIRS, LANES, LANES), F32),
                        pltpu.VMEM((SUBLANES, CONV_CH), F32), pltpu.VMEM((B_HEADS, B_HEAD, B_HEAD), F32)]
                       + [big() for _ in range(N_RWKV_SCRATCH + N_GDN_SCRATCH)],
        compiler_params=_cparams(2),
        name="mix_prompt",
    )(pa, qkv, ab, z, *rwkv_params, *gdn_params)


def _first_step_rows(rows, seqs, steps, state_rows, offset=0):
    hist = state_rows.shape[0] // seqs
    r, c = _iota((rows, seqs * hist), 0), _iota((rows, seqs * hist), 1)
    t = r & (steps - 1)
    sel = _one_hot((c == _group(r, steps) * hist + offset + t) & (t < hist - offset))
    return sum(_dot(sel, piece) for piece in _split3(state_rows))


def _last_step_rows(x, seqs, steps, keep):
    r, c = _iota((seqs * keep, seqs * steps), 0), _iota((seqs * keep, seqs * steps), 1)
    i = (c & (steps - 1)) - (steps - keep)
    sel = _one_hot((i >= 0) & (r == _group(c, steps) * keep + i))
    return sum(_dot(sel, piece) for piece in _split3(x))


ROWP = dict(mu_r=0, mu_k=1, mu_v=2, w0=3, a0=4, k_k=5, k_a=6, r_k=7, lnw=8, lnb=9)
ROWP_ROWS = 16
VALUE_GROUP = SUBLANES


def _rwkv_lanes_body(par_ref, pak_ref, pav_ref, paw_ref, pag_ref, shr_ref, shk_ref, shv_ref, shw_ref, shg_ref,
                     pa_ref, s_ref, rowp_ref, shared_ref, w2a_ref, g2_ref,
                     o_ref, sout_ref, shift_out_ref,
                     tr_s, ot_s, *, steps):
    rows = par_ref.shape[0]
    B = rows // steps
    p = pl.program_id(0)
    rp = lambda name: rowp_ref[ROWP[name]:ROWP[name] + 1, :]

    @pl.when(p == 0)
    def _():
        shift_out_ref[...] = _last_step_rows(pa_ref[...], B, steps, 1)

    def lerp(x_ref, first_ref, mu):
        per_step = [x_ref[pl.ds(t, B, stride=steps), :] for t in range(steps)]
        x = jnp.concatenate(per_step, axis=0)
        prev = jnp.concatenate([first_ref[...]] + per_step[:-1], axis=0)
        return x + (prev - x) * mu

    r = lerp(par_ref, shr_ref, rp("mu_r"))
    k = lerp(pak_ref, shk_ref, rp("mu_k"))
    v = lerp(pav_ref, shv_ref, rp("mu_v"))
    wa = lerp(paw_ref, shw_ref, shared_ref[0:1, :])
    gd = lerp(pag_ref, shg_ref, shared_ref[1:2, :])
    lane = _iota((1, LANES), 1)
    lora = _mm(jnp.where(lane < A_RANK_W, jnp.tanh(wa), wa), w2a_ref[...])
    g = _mm(_sigmoid(gd), g2_ref[...])
    w_log = -_softplus(-(rp("w0") + lora[:, :LANES])) - 0.5
    decay = jnp.exp(-jnp.exp(w_log))
    a = _sigmoid(rp("a0") + lora[:, LANES:])
    k_mod = k * (1.0 + (a - 1.0) * rp("k_a"))
    hi, hj = _iota((LANES, LANES), 0), _iota((LANES, LANES), 1)
    pair_ones = _one_hot(_group(hi, A_HEAD) == _group(hj, A_HEAD))
    head_sum = lambda m: _mm_sel(m, pair_ones)
    kk_raw = k * rp("k_k")
    kk = kk_raw * jnp.minimum(lax.rsqrt(head_sum(kk_raw * kk_raw)), 1e12)
    names = ("nkk", "beta", "decay", "k", "r", "v")
    for idx, m in enumerate((-kk, kk * a, decay, k_mod, r, v)):
        for t in range(steps):
            tr_s[idx, t] = m[t * B:(t + 1) * B, :].T
    at = lambda name, t: tr_s.at[names.index(name), t]

    def group(gi, carry):
        j = gi // (A_HEAD // VALUE_GROUP)
        v0 = (gi % (A_HEAD // VALUE_GROUP)) * VALUE_GROUP
        keys = lambda name, t: at(name, t)[pl.ds(pl.multiple_of(j * A_HEAD, A_HEAD), A_HEAD), :]
        v_rows = [at("v", t)[pl.ds(pl.multiple_of(gi * VALUE_GROUP, VALUE_GROUP), VALUE_GROUP), :]
                  for t in range(steps)]
        outs = [[] for _ in range(steps)]
        for i in range(VALUE_GROUP):
            S = s_ref[j, v0 + i]
            for t in range(steps):
                sa = jnp.sum(S * keys("nkk", t), axis=0, keepdims=True)
                S = S * keys("decay", t) + sa * keys("beta", t) + v_rows[t][i:i + 1, :] * keys("k", t)
                outs[t].append(jnp.sum(S * keys("r", t), axis=0, keepdims=True))
            sout_ref[j, v0 + i] = S
        for t in range(steps):
            ot_s[t, pl.ds(pl.multiple_of(gi * VALUE_GROUP, VALUE_GROUP), VALUE_GROUP), :] = _rows_to_tile(outs[t])
        return carry

    lax.fori_loop(0, 2 * A_HEAD // VALUE_GROUP, group, 0)
    o = jnp.concatenate([ot_s[t].T for t in range(steps)], axis=0)
    mean = head_sum(o) * (1.0 / A_HEAD)
    cen = o - mean
    var = head_sum(cen * cen) * (1.0 / A_HEAD)
    o = cen * lax.rsqrt(var + A_LNX_EPS) * rp("lnw") + rp("lnb")
    o = (o + head_sum(r * k_mod * rp("r_k")) * v) * g
    for t in range(steps):
        o_ref[pl.ds(t, B, stride=steps), :] = o[t * B:(t + 1) * B, :]


def _rwkv_lanes(pa, first_row, shift, state_t, layer, steps, tables):
    rowp, shared, w2a_p, g2_p = tables
    B = state_t.shape[-1]
    n = B * steps
    assert first_row % n == 0 and B == LANES and steps & (steps - 1) == 0
    rb = first_row // n
    col = lambda rows_, j, r: pl.BlockSpec((rows_, LANES), lambda p: (r, j(p)))
    groups = [lambda p: p, lambda p: A_PAIRS + p, lambda p: 2 * A_PAIRS + p,
              lambda p: 3 * A_PAIRS, lambda p: 3 * A_PAIRS + 1]
    block = (None, 2, A_HEAD, A_HEAD, B)
    return pl.pallas_call(
        functools.partial(_rwkv_lanes_body, steps=steps),
        grid=(A_PAIRS,),
        in_specs=[col(n, j, rb) for j in groups] + [col(B, j, 0) for j in groups]
                 + [pl.BlockSpec((n, A_PROJ), lambda p: (rb, 0), pipeline_mode=pl.Buffered(1)),
                    pl.BlockSpec(block, lambda p: (layer, p, 0, 0, 0)),
                    pl.BlockSpec((None, ROWP_ROWS, LANES), lambda p: (p, 0, 0)), _const_spec(shared.shape),
                    pl.BlockSpec((None, LANES, 2 * LANES), lambda p: (p, 0, 0)),
                    pl.BlockSpec((None, LANES, LANES), lambda p: (p, 0, 0))],
        out_specs=[pl.BlockSpec((n, LANES), lambda p: (0, p)), pl.BlockSpec(block, lambda p: (0, p, 0, 0, 0)),
                   pl.BlockSpec((B, A_PROJ), lambda p: (0, 0))],
        out_shape=[jax.ShapeDtypeStruct((n, A_WIDTH), F32), jax.ShapeDtypeStruct((1,) + state_t.shape[1:], F32),
                   jax.ShapeDtypeStruct((B, A_PROJ), F32)],
        scratch_shapes=[pltpu.VMEM((6, steps, LANES, B), F32), pltpu.VMEM((steps, LANES, B), F32)],
        compiler_params=_cparams(1),
        name="rwkv_sample",
    )(*([pa] * 5), *([shift] * 5), pa, state_t, rowp, shared, w2a_p, g2_p)


def _rwkv_pair_tables(mu, w0, a0, k_k, k_a, r_k, lnw, lnb, w2, a2, g2):
    per_pair = lambda a: a.reshape(A_PAIRS, 1, LANES)
    rows = {"mu_r": mu[:A_WIDTH], "mu_k": mu[A_WIDTH:2 * A_WIDTH], "mu_v": mu[2 * A_WIDTH:3 * A_WIDTH],
            "w0": w0, "a0": a0, "k_k": k_k, "k_a": k_a, "r_k": r_k.reshape(-1), "lnw": lnw, "lnb": lnb}
    table = jnp.concatenate([per_pair(rows[name].astype(F32)) for name in sorted(ROWP, key=ROWP.get)]
                            + [jnp.zeros((A_PAIRS, ROWP_ROWS - len(ROWP), LANES), F32)], axis=1)
    shared = jnp.concatenate([mu[3 * A_WIDTH:3 * A_WIDTH + LANES].reshape(1, LANES),
                              mu[3 * A_WIDTH + LANES:].reshape(1, LANES),
                              jnp.zeros((SUBLANES - 2, LANES), F32)], axis=0).astype(F32)
    by_pair = lambda w: jnp.transpose(w.astype(F32).reshape(w.shape[0], A_PAIRS, LANES), (1, 0, 2))
    zeros = jnp.zeros((A_PAIRS, A_RANK_W, LANES), F32)
    w2a_p = jnp.concatenate([jnp.concatenate([by_pair(w2), zeros], axis=2),
                             jnp.concatenate([zeros, by_pair(a2)], axis=2)], axis=1)
    return table, shared, w2a_p, by_pair(g2)


def _gdn_qkv(conv):
    c = _silu(conv)
    qs, ks = [], []
    for h in range(B_HEADS):
        q = c[:, h * B_HEAD:(h + 1) * B_HEAD]
        k = c[:, B_WIDTH + h * B_HEAD:B_WIDTH + (h + 1) * B_HEAD]
        qs.append(q * (lax.rsqrt(jnp.sum(q * q, axis=-1, keepdims=True) + 1e-6) * (B_HEAD ** -0.5)))
        ks.append(k * lax.rsqrt(jnp.sum(k * k, axis=-1, keepdims=True) + 1e-6))
    q = jnp.concatenate(qs, axis=1)
    k = jnp.concatenate(ks, axis=1)
    v = c[:, 2 * B_WIDTH:]
    return q, k, v


def _gdn_gates(ab, alog, dtb):
    lane = _iota((1, LANES), 1)
    g = -jnp.exp(alog) * _softplus(ab + dtb)
    beta = _sigmoid(ab)
    gb = jnp.where(lane < B_HEADS, g, beta)
    si, sj = _iota((LANES, 2 * B_WIDTH), 0), _iota((LANES, 2 * B_WIDTH), 1)
    spread = _mm_sel(gb, _one_hot(si == _group(sj, B_HEAD)))
    return spread[:, :B_WIDTH], spread[:, B_WIDTH:]


def _gdn_out(o, norm_w, z):
    return o * lax.rsqrt(jnp.mean(o * o, axis=-1, keepdims=True) + RMS_EPS) * norm_w * _silu(z)


def _gdn_prompt_part(qkv_ref, ab_ref, z_ref, cw_ref, alog_ref, dtb_ref, nw_ref, o_ref,
                     carry_ref, state_ref, q_s, k_s, v_s, gc_s, beta_s):
    tt = qkv_ref.shape[0]
    C = CHUNK
    g, beta = _gdn_gates(ab_ref[...], alog_ref[...], dtb_ref[...])
    ri, ci = _iota((tt, tt), 0), _iota((tt, tt), 1)
    beta_s[...] = beta
    gc_s[...] = _sel_mm(_one_hot((_group(ri, C) == _group(ci, C)) & (ci <= ri)), g)

    def token_chain(r0):
        rows = slice(r0, r0 + C)
        x = qkv_ref[rows, :]
        before = carry_ref[...] if r0 == 0 else qkv_ref[r0 - SUBLANES:r0, :]
        if r0 + C == tt:
            carry_ref[...] = x[C - SUBLANES:, :]
        row8 = _iota((SUBLANES, 1), 0)

        def shift_rows(cur, halo, i):
            down = pltpu.roll(cur, i, axis=0)
            top = jnp.where(row8 < i, pltpu.roll(halo, i, axis=0), down[:SUBLANES])
            return jnp.concatenate([top, down[SUBLANES:]], axis=0)

        assert CONV_K == 4
        c0, c1, c2, c3 = (cw_ref[i:i + 1, :] for i in range(CONV_K))
        x1 = shift_rows(x, before, 1)
        yield
        older = x * c1 + x1 * c0
        older_halo = before * c1 + pltpu.roll(before, 1, axis=0) * c0
        conv = x * c3 + x1 * c2 + shift_rows(older, older_halo, 2)
        yield
        q, k, v = _gdn_qkv(conv)
        q_s[rows, :] = q
        k_s[rows, :] = k
        v_s[rows, :] = v
        yield

    i2, j2 = _iota((2 * C, 2 * C), 0), _iota((2 * C, 2 * C), 1)
    same_head = _group(i2, C) == _group(j2, C)
    strict = same_head & (i2 > j2)
    incl = same_head & (i2 >= j2)
    eye = jnp.where(i2 == j2, 1.0, 0.0)
    first = _iota((2 * C, 1), 0) < C

    def solve_chain(pr, r0, stash):
        sls = [slice(h * B_HEAD, (h + 1) * B_HEAD) for h in (2 * pr, 2 * pr + 1)]
        ld = lambda ref: jnp.concatenate([ref[pl.ds(r0, C), sl] for sl in sls], axis=0)
        q_h, k_h, v_h, gc_h, beta_h = ld(q_s), ld(k_s), ld(v_s), ld(gc_s), ld(beta_s)
        diff = gc_h - gc_h.T
        dm = jnp.where(incl, jnp.exp(jnp.where(incl, diff, 0.0)), 0.0)
        kb = k_h * beta_h
        QK = _mm_nt(jnp.concatenate([kb, q_h], axis=0), k_h)
        yield
        N = -jnp.where(strict, QK[:2 * C] * dm, 0.0)
        qk = QK[2 * C:] * dm
        egc = jnp.exp(gc_h)
        X = jnp.concatenate([v_h * beta_h, kb * egc], axis=1)
        T = yield from _nilpotent_inverse(N, eye)
        UW = _mm(T, X)
        yield
        g_last = jnp.where(first, gc_h[C - 1:C, :], gc_h[2 * C - 1:2 * C, :])
        stash[pr] = dict(u=UW[:, :B_HEAD], w=UW[:, B_HEAD:].astype(BF16), qd=(q_h * egc).astype(BF16),
                         qk=qk.astype(BF16), k_dec_t=(k_h * jnp.exp(g_last - gc_h)).T.astype(BF16),
                         decay=[jnp.exp(gc_h[(j + 1) * C - 1:(j + 1) * C, :]) for j in range(2)])

    def state_chain(pr, r0, stash):
        heads = (2 * pr, 2 * pr + 1)
        sls = [slice(h * B_HEAD, (h + 1) * B_HEAD) for h in heads]
        s = stash[pr]
        wS, qS, S_old = [], [], []
        for j, h in enumerate(heads):
            S = state_ref[h]
            rows = slice(j * C, (j + 1) * C)
            wq = _dot(jnp.concatenate([s["w"][rows], s["qd"][rows]], axis=0), S.astype(BF16))
            wS.append(wq[:C])
            qS.append(wq[C:])
            S_old.append(S)
        yield
        v_new = s["u"] - jnp.concatenate(wS, axis=0)
        o = jnp.concatenate(qS, axis=0) + _dot(s["qk"], v_new.astype(BF16))
        for j, h in enumerate(heads):
            mine = first if j == 0 else jnp.logical_not(first)
            rows = slice(j * C, (j + 1) * C)
            state_ref[h] = S_old[j] * s["decay"][j] + _mm(s["k_dec_t"], jnp.where(mine, v_new, 0.0))
            o_ref[pl.ds(r0, C), sls[j]] = _gdn_out(o[rows], nw_ref[...], z_ref[pl.ds(r0, C), sls[j]])
        yield

    return (lambda r0: [token_chain(r0)],
            lambda r0, stash: [solve_chain(pr, r0, stash) for pr in range(B_HEADS // 2)],
            lambda r0, stash: [state_chain(pr, r0, stash) for pr in range(B_HEADS // 2)])


def _gdn_chunk_body(qkv_ref, hist_ref, ab_ref, z_ref, s_ref, cw_ref, alog_ref, dtb_ref, nw_ref,
                    o_ref, sout_ref, hist_out_ref, *, steps):
    rows = ab_ref.shape[0]
    seqs = rows // steps
    x = qkv_ref[...]
    hist_out_ref[...] = _last_step_rows(x, seqs, steps, CONV_K - 1)
    hist = hist_ref[...]
    t_idx = _iota((rows, 1), 0) & (steps - 1)
    conv = x * cw_ref[CONV_K - 1:CONV_K, :]
    for i in range(1, CONV_K):
        tap = jnp.where(t_idx >= i, pltpu.roll(x, i, axis=0),
                        _first_step_rows(rows, seqs, steps, hist, offset=CONV_K - 1 - i))
        conv = conv + tap * cw_ref[CONV_K - 1 - i:CONV_K - i, :]
    q, k, v = _gdn_qkv(conv)
    g, beta = _gdn_gates(ab_ref[...], alog_ref[...], dtb_ref[...])
    i2, j2 = _iota((rows, rows), 0), _iota((rows, rows), 1)
    same = _group(i2, steps) == _group(j2, steps)
    strict = same & (i2 > j2)
    incl = same & (i2 >= j2)
    eye = jnp.where(i2 == j2, 1.0, 0.0)
    gc = _sel_mm(_one_hot(incl), g)
    g_end = _sel_mm(_one_hot(j2 == (i2 | (steps - 1))), gc)
    row = _iota((rows, 1), 0)
    pair_rows = 2 * SUBLANES
    first_half = (_iota((pair_rows, 1), 0) & (SUBLANES - 1)) < steps
    assert 2 * steps == SUBLANES

    def head_chain(h):
        sl = slice(h * B_HEAD, (h + 1) * B_HEAD)
        q_h, k_h, v_h, gc_h, beta_h, ge_h = q[:, sl], k[:, sl], v[:, sl], gc[:, sl], beta[:, sl], g_end[:, sl]
        dm = jnp.where(incl, jnp.exp(jnp.where(incl, gc_h - gc_h.T, 0.0)), 0.0)
        kb = k_h * beta_h
        QK = _mm_nt(jnp.concatenate([kb, q_h], axis=0), k_h)
        yield
        N = -jnp.where(strict, QK[:rows] * dm, 0.0)
        qk = QK[rows:] * dm
        assert steps == 4
        N2 = _mm(N, N)
        yield
        T = eye + N
        T = T + _mm(T, N2)
        yield
        egc = jnp.exp(gc_h)
        UW = _mm(T, jnp.concatenate([v_h * beta_h, kb * egc], axis=1))
        yield
        w = UW[:, B_HEAD:].astype(BF16)
        qd = (q_h * egc).astype(BF16)
        k_dec_t = (k_h * jnp.exp(ge_h - gc_h)).T.astype(BF16)
        decay = jnp.exp(ge_h)
        wS, qS = [], []
        for m in range(rows // SUBLANES):
            tile = slice(m * SUBLANES, (m + 1) * SUBLANES)
            lhs = jnp.concatenate([w[tile], qd[tile]], axis=0)
            res = jnp.where(first_half, _dot(lhs, s_ref[2 * m, h].astype(BF16)),
                            _dot(lhs, s_ref[2 * m + 1, h].astype(BF16)))
            wS.append(res[:SUBLANES])
            qS.append(res[SUBLANES:])
            yield
        v_new = UW[:, :B_HEAD] - jnp.concatenate(wS, axis=0)
        o = jnp.concatenate(qS, axis=0) + _mm(qk, v_new)
        o_ref[:, sl] = _gdn_out(o, nw_ref[...], z_ref[:, sl])
        yield
        for b in range(seqs):
            mine = _group(row, steps) == b
            sout_ref[b, h] = (s_ref[b, h] * decay[b * steps:b * steps + 1, :]
                              + _mm(k_dec_t, jnp.where(mine, v_new, 0.0)))
            yield

    _round_robin(head_chain(h) for h in range(B_HEADS))


def _gdn_sample(qkv, ab, z, first_row, hist, state, layer, steps, params, seqs):
    nb = state.shape[1]
    n = nb * steps
    rows = seqs * steps
    hist_rows = seqs * (CONV_K - 1)
    assert SUBLANES % steps == 0 and rows % SUBLANES == 0 and hist_rows % SUBLANES == 0 and first_row % rows == 0
    row = lambda w: pl.BlockSpec((rows, w), lambda i: (i + first_row // rows, 0))
    block = (None, seqs, B_HEADS, B_HEAD, B_HEAD)
    sspec = pl.BlockSpec(block, lambda i: (layer, i, 0, 0, 0))
    ospec = pl.BlockSpec(block, lambda i: (0, i, 0, 0, 0))
    return pl.pallas_call(
        functools.partial(_gdn_chunk_body, steps=steps),
        grid=(nb // seqs,),
        in_specs=[row(CONV_CH), pl.BlockSpec((hist_rows, CONV_CH), lambda i: (i, 0)), row(LANES), row(B_WIDTH), sspec]
                 + [_const_spec(p.shape) for p in params],
        out_specs=[pl.BlockSpec((rows, B_WIDTH), lambda i: (i, 0)), ospec,
                   pl.BlockSpec((hist_rows, CONV_CH), lambda i: (i, 0))],
        out_shape=[jax.ShapeDtypeStruct((n, B_WIDTH), F32), jax.ShapeDtypeStruct((1,) + state.shape[1:], F32),
                   jax.ShapeDtypeStruct(hist.shape, F32)],
        compiler_params=_cparams(1),
        name="gdn_sample",
    )(qkv, hist, ab, z, state, *params)


def _cuts(widths):
    edges, total = [], 0
    for w in widths[:-1]:
        total += w
        edges.append(total)
    return edges


def _regroup_pa(a, axis=-1):
    r, wd, k, v, ad, gd = jnp.split(a, _cuts((A_WIDTH, A_RANK_W, A_WIDTH, A_WIDTH, A_RANK_A, A_RANK_G)), axis=axis)
    return jnp.concatenate([r, k, v, wd, ad, gd], axis=axis)


def _ungroup_pa(a):
    r, k, v, wd, ad, gd = jnp.split(a, _cuts((A_WIDTH, A_WIDTH, A_WIDTH, A_RANK_W, A_RANK_A, A_RANK_G)), axis=-1)
    return jnp.concatenate([r, wd, k, v, ad, gd], axis=-1)


def _token_tile(n, want):
    tm = want
    while n % tm:
        tm //= 2
    return tm


def kernel(x_prompt, x_sample, state_rwkv, state_rwkv_shift, state_delta, state_conv, ffn1_norm, ffn1_w_gate, ffn1_w_up, ffn1_w_down, mix_norm, w_in, rwkv_mu, rwkv_w0, rwkv_w2, rwkv_a0, rwkv_a2, rwkv_g2, rwkv_k_k, rwkv_k_a, rwkv_r_k, rwkv_lnx_w, rwkv_lnx_b, gdn_conv_w, gdn_A_log, gdn_dt_bias, gdn_norm_w, proj_a, proj_b, w_out, ffn2_norm, ffn2_w_gate, ffn2_w_up, ffn2_w_down, final_norm):
    depth = ffn1_norm.shape[0]
    assert depth == 1, "single-layer trunk"
    Bp, Tp, _ = x_prompt.shape
    Bs, Ts, _ = x_sample.shape
    l = 0
    row = lambda a: a.reshape(1, -1).astype(F32)

    wi = jnp.transpose(w_in[l]).astype(BF16)
    o_b = A_PROJ
    proj_w = (_regroup_pa(wi[:A_PROJ], axis=0),
              wi[o_b:o_b + CONV_CH + LANES],
              wi[o_b + CONV_CH + 2 * B_HEADS:])
    ffn1 = (row(ffn1_norm[l]), ffn1_w_gate[l].astype(BF16), ffn1_w_up[l].astype(BF16), ffn1_w_down[l].astype(BF16))
    ffn2 = (row(ffn2_norm[l]), ffn2_w_gate[l].astype(BF16), ffn2_w_up[l].astype(BF16), ffn2_w_down[l].astype(BF16))
    merge_w = (proj_a[l].astype(BF16), proj_b[l].astype(BF16), w_out[l].astype(BF16))
    zw = jnp.zeros((A_RANK_W, A_WIDTH), F32)
    w2a = jnp.concatenate([jnp.concatenate([rwkv_w2[l], zw], axis=1),
                           jnp.concatenate([zw, rwkv_a2[l]], axis=1)], axis=0)
    rwkv_params = (row(_regroup_pa(rwkv_mu[l])), row(rwkv_w0[l]), row(rwkv_a0[l]), row(rwkv_k_k[l]), row(rwkv_k_a[l]),
                   row(rwkv_r_k[l]), row(rwkv_lnx_w[l]), row(rwkv_lnx_b[l]), w2a, rwkv_g2[l].astype(F32))
    pad_lane = lambda a: jnp.pad(a.reshape(1, -1).astype(F32), ((0, 0), (0, LANES - a.size)))
    gdn_params = (gdn_conv_w[l].astype(F32), pad_lane(gdn_A_log[l]), pad_lane(gdn_dt_bias[l]), row(gdn_norm_w[l]))

    n_p, n_s = Bp * Tp, Bs * Ts
    tm = _token_tile(n_s, _token_tile(n_p, 512))
    h, pa, qkv, z, gates, ab = _head(x_prompt.reshape(n_p, D_MODEL), x_sample.reshape(n_s, D_MODEL), *ffn1,
                                     row(mix_norm[l]), *proj_w, tm=_token_tile(n_s, _token_tile(n_p, 256)))

    def trunk_back(oa, ob, first_row):
        return _tail(h, oa, ob, gates, *merge_w, *ffn2, row(final_norm), tm=tm, first_row=first_row)

    tt = _token_tile(Tp, 256)
    oa, ob, s_pairs, delta_p, pa_tail, qkv_tail = _mix_prompt(pa, qkv, ab, z, Bp, Tp, rwkv_params, gdn_params, tt)
    y_prompt = trunk_back(oa, ob, 0).reshape(Bp, Tp, D_MODEL)
    sp = s_pairs.reshape(Bp, A_PAIRS, 2, A_HEAD, 2, A_HEAD)
    rwkv_p = jnp.stack([sp[:, :, 0, :, 0], sp[:, :, 1, :, 1]], axis=2).reshape(Bp, A_HEADS, A_HEAD, A_HEAD)
    shift_p = _ungroup_pa(pa_tail[:, -1])
    conv_p = qkv_tail[:, SUBLANES - (CONV_K - 1):]

    assert Ts & (Ts - 1) == 0 and Ts >= CONV_K - 1, "sample steps: power of two covering the conv history"
    tables = _rwkv_pair_tables(_regroup_pa(rwkv_mu[l]), rwkv_w0[l], rwkv_a0[l], rwkv_k_k[l], rwkv_k_a[l], rwkv_r_k[l],
                               rwkv_lnx_w[l], rwkv_lnx_b[l], rwkv_w2[l], rwkv_a2[l], rwkv_g2[l])
    oa, s_lanes, last_pa = _rwkv_lanes(pa, n_p, _regroup_pa(state_rwkv_shift[l].astype(F32)),
                                       jnp.transpose(state_rwkv.astype(F32), (0, 2, 3, 4, 1)), l, Ts, tables)
    rwkv_s = jnp.transpose(s_lanes, (0, 4, 1, 2, 3))
    shift_s = _ungroup_pa(last_pa)
    hist = state_conv[l].astype(F32).reshape(Bs * (CONV_K - 1), CONV_CH)
    ob, delta_s, new_hist = _gdn_sample(qkv, ab, z, n_p, hist, state_delta.astype(F32), l, Ts, gdn_params,
                                        seqs=_token_tile(Bs, 32))
    conv_s = new_hist.reshape(Bs, CONV_K - 1, CONV_CH)
    y_sample = trunk_back(oa, ob, n_p).reshape(Bs, Ts, D_MODEL)

    add_depth = lambda a: a[None]
    return (y_prompt, y_sample,
            add_depth(rwkv_p), add_depth(shift_p), add_depth(delta_p), add_depth(conv_p),
            rwkv_s, add_depth(shift_s), delta_s, add_depth(conv_s))
```
